```python
import jax, jax.numpy as jnp
from jax import lax
import numpy as np

D_MODEL = 1024
BATCH = 8
SEQ = 4096
DEPTH = 1

D_MIX = D_MODEL
D_MLSTM = D_MIX // 2
N_MLSTM_HEADS = 4
MLSTM_HEAD_DIM = D_MLSTM // N_MLSTM_HEADS
MLSTM_CHUNK = 128
CONV_W = 3
N_MLA_HEADS = 4
QK_NOPE_DIM = 128
QK_ROPE_DIM = 64
V_HEAD_DIM = 128
D_MLA = N_MLA_HEADS * V_HEAD_DIM
Q_LORA = 256
KV_LORA = 128
ROPE_THETA = 10000.0
Q_BLOCK = 128
N_EXPERTS = 32
TOP_K = 4
D_EXPERT = D_MODEL
SWIGLU_LIMIT = 7.0
SWIGLU_ALPHA = 1.702
D_PLE = 256
EPS = 1e-6

N_GATES = 4 * N_MLSTM_HEADS
OFF_Q = 0
OFF_K = OFF_Q + D_MLSTM
OFF_V = OFF_K + D_MLSTM
OFF_O = OFF_V + D_MLSTM
OFF_G = OFF_O + D_MLSTM
OFF_CQ = OFF_G + N_GATES
OFF_CKV = OFF_CQ + Q_LORA
OFF_KR = OFF_CKV + KV_LORA
D_IN = OFF_KR + QK_ROPE_DIM

kernel_name = "hybrid_mlstm_mla_moe_encoder"


def rms_norm(x, g):
    xf = x.astype(jnp.float32)
    y = xf * lax.rsqrt(jnp.mean(xf * xf, axis=-1, keepdims=True) + EPS)
    return (y * g.astype(jnp.float32)).astype(x.dtype)


def _depthwise_conv(x, w):
    c = x.shape[-1]
    return lax.conv_general_dilated(x, w[:, None, :].astype(x.dtype), window_strides=(1,), padding='SAME',
                                    dimension_numbers=('NWC', 'WIO', 'NWC'), feature_group_count=c)


def _mlstm_state_step(carry, xs):
    c_st, n_st, m_st = carry
    loc_c, loc_n, m_w, b_tot = xs
    m_new = jnp.maximum(b_tot + m_st, m_w)
    a = jnp.exp(b_tot + m_st - m_new)
    c = jnp.exp(m_w - m_new)
    c_new = a[..., None, None] * c_st + c[..., None, None] * loc_c
    n_new = a[..., None] * n_st + c[..., None] * loc_n
    return (c_new, n_new, m_new), (c_st, n_st, m_st)


def _mlstm_direction(q, k, v, ig, fg):
    b_, h_, s_, dh = q.shape
    L = MLSTM_CHUNK
    nc = s_ // L
    q = q.reshape(b_, h_, nc, L, dh)
    k = k.reshape(b_, h_, nc, L, dh)
    v = v.reshape(b_, h_, nc, L, dh)
    ig = ig.reshape(b_, h_, nc, L)
    b = jnp.cumsum(jax.nn.log_sigmoid(fg).reshape(b_, h_, nc, L), axis=-1)
    b_tot = b[..., -1]
    w = b_tot[..., None] - b + ig
    m_w = jnp.max(w, axis=-1)
    kw = k * jnp.exp(w - m_w[..., None])[..., None]
    loc_c = jnp.einsum('bhcsd,bhcse->bhcde', kw, v)
    loc_n = jnp.sum(kw, axis=-2)
    front = lambda a: jnp.moveaxis(a, 2, 0)
    back = lambda a: jnp.moveaxis(a, 0, 2)
    init = (jnp.zeros((b_, h_, dh, dh), jnp.float32), jnp.zeros((b_, h_, dh), jnp.float32),
            jnp.zeros((b_, h_), jnp.float32))
    _, (c_prev, n_prev, m_prev) = lax.scan(_mlstm_state_step, init,
                                           (front(loc_c), front(loc_n), front(m_w), front(b_tot)))
    c_prev, n_prev, m_prev = back(c_prev), back(n_prev), back(m_prev)
    lower = jnp.tril(jnp.ones((L, L), dtype=bool))
    dmat = jnp.where(lower, b[..., :, None] - b[..., None, :] + ig[..., None, :], -jnp.inf)
    inter = b + m_prev[..., None]
    m_t = jnp.maximum(inter, jnp.max(dmat, axis=-1))
    sc = jnp.einsum('bhcld,bhcsd->bhcls', q, k) * jnp.exp(dmat - m_t[..., None])
    a = jnp.exp(inter - m_t)
    num = jnp.einsum('bhcls,bhcse->bhcle', sc, v) + a[..., None] * jnp.einsum('bhcld,bhcde->bhcle', q, c_prev)
    den = jnp.sum(sc, axis=-1) + a * jnp.einsum('bhcld,bhcd->bhcl', q, n_prev)
    h = num / jnp.maximum(jnp.abs(den), jnp.exp(-m_t))[..., None]
    return h.reshape(b_, h_, s_, dh)


def mlstm_group(u, conv_w, conv_b, b_gates, mlstm_norm):
    b_, s_, _ = u.shape
    H, dh = N_MLSTM_HEADS, MLSTM_HEAD_DIM
    qk = jax.nn.silu(_depthwise_conv(u[..., OFF_Q:OFF_V], conv_w) + conv_b)
    q, k = qk[..., :D_MLSTM], qk[..., D_MLSTM:]
    v = u[..., OFF_V:OFF_O]
    o = u[..., OFF_O:OFF_G]
    g = (u[..., OFF_G:OFF_CQ] + b_gates).astype(jnp.float32).reshape(b_, s_, 4, H).transpose(2, 0, 3, 1)
    heads = lambda t: t.reshape(b_, s_, H, dh).transpose(0, 2, 1, 3).astype(jnp.float32)
    q, k, v = heads(q), heads(k) * (dh ** -0.5), heads(v)
    flip = lambda t: jnp.flip(t, axis=2)
    h_f = _mlstm_direction(q, k, v, g[0], g[1])
    h_b = flip(_mlstm_direction(flip(q), flip(k), flip(v), flip(g[2]), flip(g[3])))
    h = (h_f + h_b).transpose(0, 2, 1, 3)
    h = rms_norm(h, mlstm_norm.reshape(H, dh)).reshape(b_, s_, D_MLSTM).astype(u.dtype)
    return h * jax.nn.sigmoid(o)


def _rope(x, cos, sin):
    half = x.shape[-1] // 2
    xf = x.astype(jnp.float32)
    x1, x2 = xf[..., :half], xf[..., half:]
    return jnp.concatenate([x1 * cos - x2 * sin, x2 * cos + x1 * sin], axis=-1).astype(x.dtype)


def mla_group(u, positions, q_norm, w_q_up, kv_norm, w_kv_up, mla_norm):
    b_, s_, _ = u.shape
    H = N_MLA_HEADS
    c_q = rms_norm(u[..., OFF_CQ:OFF_CKV], q_norm)
    q = (c_q @ w_q_up).reshape(b_, s_, H, QK_NOPE_DIM + QK_ROPE_DIM)
    q_nope, q_pe = q[..., :QK_NOPE_DIM], q[..., QK_NOPE_DIM:]
    c_kv = rms_norm(u[..., OFF_CKV:OFF_KR], kv_norm)
    kv = (c_kv @ w_kv_up).reshape(b_, s_, H, QK_NOPE_DIM + V_HEAD_DIM)
    k_nope, v = kv[..., :QK_NOPE_DIM], kv[..., QK_NOPE_DIM:]
    k_pe = u[..., OFF_KR:D_IN]
    freqs = ROPE_THETA ** (-jnp.arange(0, QK_ROPE_DIM, 2, dtype=jnp.float32) / QK_ROPE_DIM)
    ang = positions.astype(jnp.float32)[..., None] * freqs
    cos, sin = jnp.cos(ang), jnp.sin(ang)
    q_pe = _rope(q_pe, cos[:, :, None, :], sin[:, :, None, :])
    k_pe = _rope(k_pe, cos, sin)
    scale = (QK_NOPE_DIM + QK_ROPE_DIM) ** -0.5
    nb = s_ // Q_BLOCK
    qblk = lambda t: t.reshape(b_, nb, Q_BLOCK, H, t.shape[-1]).transpose(1, 0, 3, 2, 4)
    qn_b, qp_b = qblk(q_nope * scale), qblk(q_pe * scale)
    k_nope = k_nope.transpose(0, 2, 1, 3)
    v = v.transpose(0, 2, 1, 3)

    def attend(qs):
        qn, qp = qs
        s = jnp.einsum('bhqd,bhkd->bhqk', qn, k_nope) + jnp.einsum('bhqd,bkd->bhqk', qp, k_pe)
        pr = jax.nn.softmax(s.astype(jnp.float32), axis=-1).astype(v.dtype)
        return jnp.einsum('bhqk,bhkd->bhqd', pr, v)

    o = lax.map(attend, (qn_b, qp_b))
    o = o.transpose(1, 0, 3, 2, 4).reshape(b_, s_, D_MLA)
    return rms_norm(o, mla_norm)


def moe(xn, w_router, b_router, w_gate_up, b_gate_up, w_down, b_down):
    b_, s_, d = xn.shape
    n = b_ * s_
    t = xn.reshape(n, d)
    logits = (t @ w_router + b_router).astype(jnp.float32)
    top_vals, top_idx = lax.top_k(logits, TOP_K)
    gates = jax.nn.softmax(top_vals, axis=-1)
    flat_e = top_idx.reshape(-1)
    flat_tok = jnp.repeat(jnp.arange(n, dtype=jnp.int32), TOP_K)
    order = jnp.argsort(flat_e)
    e_sorted = flat_e[order]
    tok_sorted = flat_tok[order]
    gate_sorted = gates.reshape(-1)[order]
    group_sizes = jnp.bincount(flat_e, length=N_EXPERTS).astype(jnp.int32)
    xs = t[tok_sorted]
    gu = lax.ragged_dot(xs, w_gate_up, group_sizes) + b_gate_up[e_sorted]
    g, up = gu[:, :D_EXPERT], gu[:, D_EXPERT:]
    g = jnp.minimum(g, SWIGLU_LIMIT)
    up = jnp.clip(up, -SWIGLU_LIMIT, SWIGLU_LIMIT)
    hmid = (up + 1) * (g * jax.nn.sigmoid(g * SWIGLU_ALPHA))
    out = lax.ragged_dot(hmid, w_down, group_sizes) + b_down[e_sorted]
    out = out * gate_sorted[:, None].astype(out.dtype)
    y = jax.ops.segment_sum(out, tok_sorted, num_segments=n)
    return y.reshape(b_, s_, d)


def setup_inputs(seed: int = 0) -> dict:
    key = jax.random.key(seed)
    ks = jax.random.split(key, 32)
    nrm = lambda k, shape, s: jax.random.normal(k, shape, jnp.float32) * s
    gain = lambda k, shape: 1.0 + 0.02 * jax.random.normal(k, shape, jnp.float32)
    L = DEPTH
    fb = jnp.linspace(3.0, 6.0, N_MLSTM_HEADS, dtype=jnp.float32)
    zb = jnp.zeros((N_MLSTM_HEADS,), jnp.float32)
    b_gates = jnp.concatenate([zb, fb, zb, fb])[None, :] + nrm(ks[3], (L, N_GATES), 0.1)
    positions = jnp.broadcast_to(jnp.arange(SEQ, dtype=jnp.int32), (BATCH, SEQ))
    return {
        "x": nrm(ks[0], (BATCH, SEQ, D_MODEL), 1.0),
        "p": nrm(ks[1], (DEPTH, BATCH, SEQ, D_PLE), 1.0),
        "positions": positions,
        "attn_norm": gain(ks[2], (L, D_MODEL)),
        "w_in": nrm(ks[4], (L, D_MODEL, D_IN), D_MODEL ** -0.5),
        "b_gates": b_gates,
        "conv_w": nrm(ks[5], (L, CONV_W, 2 * D_MLSTM), CONV_W ** -0.5),
        "conv_b": nrm(ks[6], (L, 2 * D_MLSTM), 0.01),
        "mlstm_norm": gain(ks[7], (L, D_MLSTM)),
        "q_norm": gain(ks[8], (L, Q_LORA)),
        "w_q_up": nrm(ks[9], (L, Q_LORA, N_MLA_HEADS * (QK_NOPE_DIM + QK_ROPE_DIM)), Q_LORA ** -0.5),
        "kv_norm": gain(ks[10], (L, KV_LORA)),
        "w_kv_up": nrm(ks[11], (L, KV_LORA, N_MLA_HEADS * (QK_NOPE_DIM + V_HEAD_DIM)), KV_LORA ** -0.5),
        "mla_norm": gain(ks[12], (L, D_MLA)),
        "w_out": nrm(ks[13], (L, D_MIX, D_MODEL), D_MIX ** -0.5),
        "ffn_norm": gain(ks[14], (L, D_MODEL)),
        "w_router": nrm(ks[15], (L, D_MODEL, N_EXPERTS), D_MODEL ** -0.5),
        "b_router": nrm(ks[16], (L, N_EXPERTS), 0.01),
        "w_gate_up": nrm(ks[17], (L, N_EXPERTS, D_MODEL, 2 * D_EXPERT), D_MODEL ** -0.5),
        "b_gate_up": nrm(ks[18], (L, N_EXPERTS, 2 * D_EXPERT), 0.01),
        "w_down": nrm(ks[19], (L, N_EXPERTS, D_EXPERT, D_MODEL), D_EXPERT ** -0.5),
        "b_down": nrm(ks[20], (L, N_EXPERTS, D_MODEL), 0.01),
        "ple_norm": gain(ks[21], (L, D_MODEL)),
        "w_ple_gate": nrm(ks[22], (L, D_MODEL, D_MODEL), D_MODEL ** -0.5),
        "w_ple_proj": nrm(ks[23], (L, D_PLE, D_MODEL), D_PLE ** -0.5),
        "final_norm": gain(ks[24], (D_MODEL,)),
    }


def reference(x, p, positions, attn_norm, w_in, b_gates, conv_w, conv_b, mlstm_norm, q_norm, w_q_up,
              kv_norm, w_kv_up, mla_norm, w_out, ffn_norm, w_router, b_router, w_gate_up, b_gate_up,
              w_down, b_down, ple_norm, w_ple_gate, w_ple_proj, final_norm):
    h = x
    for i in range(DEPTH):
        a = rms_norm(h, attn_norm[i])
        u = a @ w_in[i]
        y_m = mlstm_group(u, conv_w[i], conv_b[i], b_gates[i], mlstm_norm[i])
        y_a = mla_group(u, positions, q_norm[i], w_q_up[i], kv_norm[i], w_kv_up[i], mla_norm[i])
        h = h + jnp.concatenate([y_m, y_a], axis=-1) @ w_out[i]
        h = h + moe(rms_norm(h, ffn_norm[i]), w_router[i], b_router[i], w_gate_up[i], b_gate_up[i],
                    w_down[i], b_down[i])
        gate = jax.nn.sigmoid(rms_norm(h, ple_norm[i]) @ w_ple_gate[i])
        h = h + gate * (p[i] @ w_ple_proj[i])
    return rms_norm(h, final_norm)
```

```python
import functools

import jax
import jax.numpy as jnp
from jax import lax
from jax.experimental import pallas as pl
from jax.experimental.pallas import tpu as pltpu

F32 = jnp.float32
BF16 = jnp.bfloat16
I32 = jnp.int32

N_MLSTM_HEADS = 4
MLSTM_HEAD_DIM = 128
D_MLSTM = N_MLSTM_HEADS * MLSTM_HEAD_DIM
MLSTM_CHUNK = 128
N_MLA_HEADS = 4
QK_NOPE_DIM = 128
QK_ROPE_DIM = 64
V_HEAD_DIM = 128
D_MLA = N_MLA_HEADS * V_HEAD_DIM
Q_LORA = 256
KV_LORA = 128
ROPE_THETA = 10000.0
N_EXPERTS = 32
TOP_K = 4
SWIGLU_LIMIT = 7.0
SWIGLU_ALPHA = 1.702
EPS = 1e-6
N_GATES = 4 * N_MLSTM_HEADS
OFF_G = 4 * D_MLSTM
OFF_CQ = OFF_G + N_GATES
OFF_CKV = OFF_CQ + Q_LORA
OFF_KR = OFF_CKV + KV_LORA

LANES = 128
QK_SLAB = 2 * LANES
VMEM_LIMIT_BYTES = 56 * 1024 * 1024

ROW_TILE = 512
Q_TILE = 256
EXPERT_TILE = 512


def _dot(a, b):
    return jnp.dot(a, b, preferred_element_type=F32)


def _dot_nt(a, b):
    return lax.dot_general(a, b, (((1,), (1,)), ((), ())), preferred_element_type=F32)


def _rms(x, g):
    return x * lax.rsqrt(jnp.mean(x * x, axis=-1, keepdims=True) + EPS) * g


def _log_sigmoid(x):
    return jnp.minimum(x, 0.0) - jnp.log(1.0 + jnp.exp(-jnp.abs(x)))


def _params(*sem):
    return pltpu.CompilerParams(dimension_semantics=sem, vmem_limit_bytes=VMEM_LIMIT_BYTES)


def _in_proj_kernel(x_ref, pos_ref, an_ref, wm_ref, wr_ref, wgt_ref, bgt_ref, qn_ref, wq_ref, kvn_ref,
                    wk_ref, wv_ref, freq_ref, sgn_ref,
                    um_ref, gt_ref, q_ref, k_ref, v_ref):
    a = _rms(x_ref[...], an_ref[...]).astype(BF16)
    um_ref[...] = _dot(a, wm_ref[...])
    gt_ref[...] = _dot_nt(wgt_ref[...], a) + bgt_ref[...]
    rest = _dot(a, wr_ref[...])
    cq = rest[:, :Q_LORA]
    ckv = rest[:, Q_LORA:Q_LORA + KV_LORA]
    kr2 = rest[:, Q_LORA + KV_LORA:Q_LORA + KV_LORA + LANES]
    krs2 = rest[:, Q_LORA + KV_LORA + LANES:]
    ang = pos_ref[...].astype(F32) * freq_ref[...]
    cos_a = jnp.cos(ang)
    sin_a = jnp.sin(ang) * sgn_ref[...]
    scale = (QK_NOPE_DIM + QK_ROPE_DIM) ** -0.5
    lane = lax.broadcasted_iota(I32, ang.shape, 1)
    rope_mul = jnp.where(lane < QK_ROPE_DIM, cos_a, sin_a) * scale
    qf = _dot(_rms(cq, qn_ref[...]).astype(BF16), wq_ref[...])
    ckvn = _rms(ckv, kvn_ref[...]).astype(BF16)
    kn = _dot(ckvn, wk_ref[...])
    v_ref[...] = _dot(ckvn, wv_ref[...]).astype(BF16)
    k_rope = (kr2 * cos_a + krs2 * sin_a).astype(BF16)
    for h in range(N_MLA_HEADS):
        o = h * QK_SLAB
        q_ref[:, o:o + LANES] = (qf[:, o:o + LANES] * scale).astype(BF16)
        q_ref[:, o + LANES:o + QK_SLAB] = (qf[:, o + LANES:o + QK_SLAB] * rope_mul).astype(BF16)
        k_ref[:, o:o + LANES] = kn[:, h * LANES:(h + 1) * LANES].astype(BF16)
        k_ref[:, o + LANES:o + QK_SLAB] = k_rope


def _in_proj(x2, pos2, attn_norm, w_in, b_gates, q_norm, w_q_up, kv_norm, w_kv_up):
    n, d = x2.shape
    tm = ROW_TILE
    half = QK_ROPE_DIM // 2
    swap = jnp.concatenate([jnp.arange(half, QK_ROPE_DIM), jnp.arange(0, half)])
    w_kr = w_in[:, OFF_KR:OFF_KR + QK_ROPE_DIM]
    w_krs = w_kr[:, swap]
    wm = w_in[:, :OFF_G].astype(BF16)
    wr = jnp.concatenate([w_in[:, OFF_CQ:OFF_KR], w_kr, w_kr, w_krs, w_krs], axis=1).astype(BF16)
    wgt = w_in[:, OFF_G:OFF_CQ].T.astype(BF16)
    bgt = b_gates.reshape(N_GATES, 1)
    wq4 = w_q_up.reshape(Q_LORA, N_MLA_HEADS, QK_NOPE_DIM + QK_ROPE_DIM)
    wq_pe = wq4[:, :, QK_NOPE_DIM:]
    wq = jnp.concatenate([wq4, wq_pe[:, :, swap]], axis=2).reshape(Q_LORA, N_MLA_HEADS * QK_SLAB).astype(BF16)
    wkv4 = w_kv_up.reshape(KV_LORA, N_MLA_HEADS, QK_NOPE_DIM + V_HEAD_DIM)
    wk = wkv4[:, :, :QK_NOPE_DIM].reshape(KV_LORA, N_MLA_HEADS * QK_NOPE_DIM).astype(BF16)
    wv = wkv4[:, :, QK_NOPE_DIM:].reshape(KV_LORA, D_MLA).astype(BF16)
    freqs = ROPE_THETA ** (-jnp.arange(0, QK_ROPE_DIM, 2, dtype=F32) / QK_ROPE_DIM)
    freq_l = jnp.tile(freqs, LANES // half).reshape(1, LANES)
    sgn_l = jnp.tile(jnp.concatenate([-jnp.ones((half,), F32), jnp.ones((half,), F32)]),
                     LANES // QK_ROPE_DIM).reshape(1, LANES)
    full = lambda arr: pl.BlockSpec(arr.shape, lambda i: (0,) * arr.ndim)
    rows = lambda w: pl.BlockSpec((tm, w), lambda i: (i, 0))
    consts = [attn_norm.reshape(1, d), wm, wr, wgt, bgt, q_norm.reshape(1, Q_LORA), wq,
              kv_norm.reshape(1, KV_LORA), wk, wv, freq_l, sgn_l]
    return pl.pallas_call(
        _in_proj_kernel,
        grid=(n // tm,),
        in_specs=[rows(d), rows(1)] + [full(c) for c in consts],
        out_specs=[rows(OFF_G), pl.BlockSpec((N_GATES, tm), lambda i: (0, i)),
                   rows(N_MLA_HEADS * QK_SLAB), rows(N_MLA_HEADS * QK_SLAB), rows(D_MLA)],
        out_shape=[jax.ShapeDtypeStruct((n, OFF_G), F32), jax.ShapeDtypeStruct((N_GATES, n), F32),
                   jax.ShapeDtypeStruct((n, N_MLA_HEADS * QK_SLAB), BF16),
                   jax.ShapeDtypeStruct((n, N_MLA_HEADS * QK_SLAB), BF16),
                   jax.ShapeDtypeStruct((n, D_MLA), BF16)],
        compiler_params=_params("parallel"),
        name="in_proj",
    )(x2, pos2, *consts)


def _cumsum_lanes(x, reverse):
    n = x.shape[-1]
    lane = lax.broadcasted_iota(I32, x.shape, x.ndim - 1)
    sh = 1
    while sh < n:
        if reverse:
            x = x + jnp.where(lane < n - sh, pltpu.roll(x, n - sh, x.ndim - 1), 0.0)
        else:
            x = x + jnp.where(lane >= sh, pltpu.roll(x, sh, x.ndim - 1), 0.0)
        sh *= 2
    return x


def _mlstm_kernel(q_ref, k_ref, v_ref, o_ref, g_ref, cwq_ref, cwk_ref, cbq_ref, cbk_ref, nrm_ref,
                  y_ref,
                  qc_ref, kc_ref, va_ref, cst_ref, b_ref, r_ref, ew_ref, mw_ref, bt_ref, mp_ref):
    L = MLSTM_CHUNK
    dh = MLSTM_HEAD_DIM
    nc = q_ref.shape[1] // L
    s_len = q_ref.shape[1]
    h = pl.program_id(1)

    for d in range(2):
        ig = g_ref[2 * d * N_MLSTM_HEADS + h, 0]
        fg = g_ref[(2 * d + 1) * N_MLSTM_HEADS + h, 0]
        b = _cumsum_lanes(_log_sigmoid(fg), reverse=(d == 1))
        btot = b[:, L - 1:L] if d == 0 else b[:, 0:1]
        r = ig - b
        w = btot + r
        mw = jnp.max(w, axis=-1, keepdims=True)
        b_ref[d] = b
        r_ref[d] = r
        ew_ref[d] = jnp.exp(w - mw)
        mw_ref[d] = jnp.broadcast_to(mw, (nc, L))
        bt_ref[d] = jnp.broadcast_to(btot, (nc, L))

    row = lax.broadcasted_iota(I32, (L, dh), 0)

    def conv_silu(ref, cw_ref, cb_ref, c):
        start = pl.multiple_of(c * L, L)
        x = ref[0, pl.ds(start, L), :]
        prev_row = jnp.where(c > 0, ref[0, pl.ds(jnp.maximum(start - 1, 0), 1), :], 0.0)
        next_row = jnp.where(c < nc - 1, ref[0, pl.ds(jnp.minimum(start + L, s_len - 1), 1), :], 0.0)
        x_prev = jnp.where(row == 0, prev_row, pltpu.roll(x, 1, 0))
        x_next = jnp.where(row == L - 1, next_row, pltpu.roll(x, L - 1, 0))
        y = cw_ref[0:1, :] * x_prev + cw_ref[1:2, :] * x + cw_ref[2:3, :] * x_next + cb_ref[...]
        return y * jax.nn.sigmoid(y)

    ones_col = (lax.broadcasted_iota(I32, (L, dh), 1) == 0).astype(BF16)

    def pass1(c, carry):
        start = pl.multiple_of(c * L, L)
        qc_ref[pl.ds(start, L), :] = conv_silu(q_ref, cwq_ref, cbq_ref, c).astype(BF16)
        kk = conv_silu(k_ref, cwk_ref, cbk_ref, c) * (dh ** -0.5)
        kc_ref[pl.ds(start, L), :] = kk.astype(BF16)
        va = jnp.concatenate([v_ref[0, pl.ds(start, L), :].astype(BF16), ones_col], axis=1)
        va_ref[pl.ds(start, L), :] = va
        kt = kk.T
        for d in range(2):
            kw_t = (kt * ew_ref[d, pl.ds(c, 1), :]).astype(BF16)
            cst_ref[d, c] = _dot(kw_t, va)
        return carry

    lax.fori_loop(0, nc, pass1, 0)

    for d in range(2):
        def scan(i, carry):
            st, m = carry
            c = i if d == 0 else nc - 1 - i
            mw = mw_ref[d, pl.ds(c, 1), :]
            bt = bt_ref[d, pl.ds(c, 1), :]
            m_new = jnp.maximum(bt + m, mw)
            a = jnp.exp(bt + m - m_new)[:, 0:1]
            cc = jnp.exp(mw - m_new)[:, 0:1]
            loc = cst_ref[d, c]
            cst_ref[d, c] = st
            mp_ref[d, pl.ds(c, 1), :] = m
            return a * st + cc * loc, m_new

        lax.fori_loop(0, nc, scan, (jnp.zeros((dh, 2 * dh), F32), jnp.zeros((1, L), F32)))

    ti = lax.broadcasted_iota(I32, (L, L), 0)
    si = lax.broadcasted_iota(I32, (L, L), 1)
    masks = (si <= ti, si >= ti)

    def pass3(c, carry):
        start = pl.multiple_of(c * L, L)
        q = qc_ref[pl.ds(start, L), :]
        k = kc_ref[pl.ds(start, L), :]
        va = va_ref[pl.ds(start, L), :]
        qk = _dot_nt(q, k)
        hsum = jnp.zeros((L, dh), F32)
        for d in range(2):
            bmat = jnp.broadcast_to(b_ref[d, pl.ds(c, 1), :], (L, L)).T
            dmat = jnp.where(masks[d], bmat + r_ref[d, pl.ds(c, 1), :], -jnp.inf)
            inter = bmat[:, 0:1] + mp_ref[d, pl.ds(c, 1), :][:, 0:1]
            m_t = jnp.maximum(inter, jnp.max(dmat, axis=-1, keepdims=True))
            sc = qk * jnp.exp(dmat - m_t)
            a = jnp.exp(inter - m_t)
            tot = _dot(sc.astype(BF16), va) + a * _dot(q, cst_ref[d, c].astype(BF16))
            den = jnp.maximum(jnp.abs(tot[:, dh:dh + 1]), jnp.exp(-m_t))
            hsum = hsum + tot[:, :dh] / den
        hn = _rms(hsum, nrm_ref[...])
        y_ref[0, pl.ds(start, L), :] = hn * jax.nn.sigmoid(o_ref[0, pl.ds(start, L), :])
        return carry

    lax.fori_loop(0, nc, pass3, 0)


def _mlstm(um3, gt4, conv_w, conv_b, mlstm_norm):
    bsz, s, _ = um3.shape
    H, dh, L = N_MLSTM_HEADS, MLSTM_HEAD_DIM, MLSTM_CHUNK
    nc = s // L
    col = lambda off: pl.BlockSpec((1, s, dh), lambda b, h: (b, 0, off + h))
    vec = lambda rows, off: pl.BlockSpec((rows, dh), lambda b, h: (0, off + h))
    cb = conv_b.reshape(1, 2 * D_MLSTM)
    return pl.pallas_call(
        _mlstm_kernel,
        grid=(bsz, H),
        in_specs=[col(0), col(H), col(2 * H), col(3 * H),
                  pl.BlockSpec((N_GATES, 1, nc, L), lambda b, h: (0, b, 0, 0)),
                  vec(3, 0), vec(3, H), vec(1, 0), vec(1, H), vec(1, 0)],
        out_specs=pl.BlockSpec((1, s, dh), lambda b, h: (b, 0, h)),
        out_shape=jax.ShapeDtypeStruct((bsz, s, D_MLSTM), F32),
        scratch_shapes=[pltpu.VMEM((s, dh), BF16), pltpu.VMEM((s, dh), BF16), pltpu.VMEM((s, 2 * dh), BF16),
                        pltpu.VMEM((2, nc, dh, 2 * dh), F32)]
                       + [pltpu.VMEM((2, nc, L), F32) for _ in range(6)],
        compiler_params=_params("parallel", "parallel"),
        name="mlstm",
    )(um3, um3, um3, um3, gt4, conv_w, conv_w, cb, cb, mlstm_norm.reshape(1, D_MLSTM))


def _attn_kernel(q_ref, k_ref, v_ref, o_ref):
    s = _dot_nt(q_ref[0], k_ref[0])
    m = jnp.max(s, axis=-1, keepdims=True)
    p = jnp.exp(s - m)
    l = jnp.sum(p, axis=-1, keepdims=True)
    o_ref[0] = _dot(p.astype(BF16), v_ref[0]) / l


def _attention(q3, k3, v3):
    bsz, s, _ = q3.shape
    tq = Q_TILE
    return pl.pallas_call(
        _attn_kernel,
        grid=(bsz, N_MLA_HEADS, s // tq),
        in_specs=[pl.BlockSpec((1, tq, QK_SLAB), lambda b, h, i: (b, i, h)),
                  pl.BlockSpec((1, s, QK_SLAB), lambda b, h, i: (b, 0, h)),
                  pl.BlockSpec((1, s, V_HEAD_DIM), lambda b, h, i: (b, 0, h))],
        out_specs=pl.BlockSpec((1, tq, V_HEAD_DIM), lambda b, h, i: (b, i, h)),
        out_shape=jax.ShapeDtypeStruct((bsz, s, D_MLA), F32),
        compiler_params=_params("parallel", "parallel", "parallel"),
        name="attention",
    )(q3, k3, v3)


def _out_route_kernel(ym_ref, ya_ref, x_ref, mn_ref, wom_ref, woa_ref, fn_ref, wr_ref, br_ref,
                      h_ref, xn_ref, eidx_ref, gate_ref, rank_ref, cnt_ref,
                      carry_ref):
    i = pl.program_id(0)
    tm = x_ref.shape[0]

    @pl.when(i == 0)
    def _():
        carry_ref[...] = jnp.zeros_like(carry_ref)

    ya = _rms(ya_ref[...], mn_ref[...])
    h1 = x_ref[...] + _dot(ym_ref[...].astype(BF16), wom_ref[...]) + _dot(ya.astype(BF16), woa_ref[...])
    h_ref[...] = h1
    xn = _rms(h1, fn_ref[...])
    xn_ref[...] = xn
    logits = jnp.dot(xn, wr_ref[...], preferred_element_type=F32, precision=lax.Precision.HIGHEST) + br_ref[...]
    lane = lax.broadcasted_iota(I32, logits.shape, 1)
    work = jnp.where(lane < N_EXPERTS, logits, -jnp.inf)
    vals, hots = [], []
    eidx = jnp.zeros(logits.shape, I32)
    for k in range(TOP_K):
        mx = jnp.max(work, axis=-1, keepdims=True)
        idx = jnp.min(jnp.where(work == mx, lane, LANES), axis=-1, keepdims=True)
        hot = lane == idx
        work = jnp.where(hot, -jnp.inf, work)
        vals.append(mx)
        hots.append(hot)
        eidx = jnp.where(lane == k, idx, eidx)
    exps = [jnp.exp(v - vals[0]) for v in vals]
    tot = exps[0] + exps[1] + exps[2] + exps[3]
    multi = (hots[0] | hots[1] | hots[2] | hots[3]).astype(BF16)
    ri = lax.broadcasted_iota(I32, (tm, tm), 0)
    ci = lax.broadcasted_iota(I32, (tm, tm), 1)
    before = (ci < ri).astype(BF16)
    rank_all = _dot(before, multi) + carry_ref[...]
    gate = jnp.zeros(logits.shape, F32)
    rank = jnp.zeros(logits.shape, F32)
    for k in range(TOP_K):
        gate = jnp.where(lane == k, exps[k] / tot, gate)
        rk = jnp.sum(jnp.where(hots[k], rank_all, 0.0), axis=-1, keepdims=True)
        rank = jnp.where(lane == k, rk, rank)
    eidx_ref[...] = eidx[:, :TOP_K]
    gate_ref[...] = gate[:, :TOP_K]
    rank_ref[...] = rank[:, :TOP_K].astype(I32)
    carry_ref[...] += jnp.sum(multi.astype(F32), axis=0, keepdims=True)
    cnt_ref[...] = carry_ref[...].astype(I32)


def _out_route(ym2, ya2, x2, mla_norm, w_out, ffn_norm, w_router, b_router):
    n, d = x2.shape
    tm = ROW_TILE
    wom = w_out[:D_MLSTM].astype(BF16)
    woa = w_out[D_MLSTM:].astype(BF16)
    wr = jnp.pad(w_router, ((0, 0), (0, LANES - N_EXPERTS)))
    br = jnp.pad(b_router.reshape(1, N_EXPERTS), ((0, 0), (0, LANES - N_EXPERTS)))
    full = lambda arr: pl.BlockSpec(arr.shape, lambda i: (0,) * arr.ndim)
    rows = lambda w: pl.BlockSpec((tm, w), lambda i: (i, 0))
    consts = [mla_norm.reshape(1, D_MLA), wom, woa, ffn_norm.reshape(1, d), wr, br]
    return pl.pallas_call(
        _out_route_kernel,
        grid=(n // tm,),
        in_specs=[rows(D_MLSTM), rows(D_MLA), rows(d)] + [full(c) for c in consts],
        out_specs=[rows(d), rows(d), rows(TOP_K), rows(TOP_K), rows(TOP_K),
                   pl.BlockSpec((1, LANES), lambda i: (0, 0))],
        out_shape=[jax.ShapeDtypeStruct((n, d), F32), jax.ShapeDtypeStruct((n, d), F32),
                   jax.ShapeDtypeStruct((n, TOP_K), I32), jax.ShapeDtypeStruct((n, TOP_K), F32),
                   jax.ShapeDtypeStruct((n, TOP_K), I32), jax.ShapeDtypeStruct((1, LANES), I32)],
        scratch_shapes=[pltpu.VMEM((1, LANES), F32)],
        compiler_params=_params("arbitrary"),
        name="out_route",
    )(ym2, ya2, x2, *consts)


def _dispatch_kernel(starts_ref, eidx_ref, rank_ref, xn_ref, zero_ref,
                     pos_ref, xs_ref, sem, zsem):
    i = pl.program_id(0)
    per = eidx_ref.shape[2]
    tm = per // TOP_K
    tmx = zero_ref.shape[0]

    @pl.when(i == 0)
    def _():
        def clear_tile(row):
            cp = pltpu.make_async_copy(zero_ref, xs_ref.at[pl.ds(pl.multiple_of(row, tmx), tmx)], zsem)
            cp.start()
            cp.wait()

        def clear_group(e, carry):
            hi = starts_ref[e + 1]

            @pl.when(hi > starts_ref[e])
            def _():
                clear_tile(hi - tmx)

            return carry

        def clear_tail(t, carry):
            clear_tile(starts_ref[N_EXPERTS] + t * tmx)
            return carry

        lax.fori_loop(0, N_EXPERTS, clear_group, 0)
        lax.fori_loop(0, (xs_ref.shape[0] - starts_ref[N_EXPERTS]) // tmx, clear_tail, 0)

    def token_copy(t, carry):
        src = xn_ref.at[pl.ds(i * tm + t, 1)]
        for k in range(TOP_K):
            j = t * TOP_K + k
            pos = starts_ref[eidx_ref[0, 0, j]] + rank_ref[0, 0, j]
            pos_ref[0, 0, j] = pos
            pltpu.make_async_copy(src, xs_ref.at[pl.ds(pos, 1)], sem).start()
        return carry

    lax.fori_loop(0, tm, token_copy, 0, unroll=2)
    pltpu.make_async_copy(xn_ref.at[pl.ds(0, per)], xs_ref.at[pl.ds(0, per)], sem).wait()


def _dispatch(starts, eidx, rank, xn, n_rows):
    n, d = xn.shape
    tm = ROW_TILE
    nt = n // tm
    per = tm * TOP_K
    tiled = lambda a: a.reshape(nt, 1, per)
    smem_tile = pl.BlockSpec((1, 1, per), lambda i, *_: (i, 0, 0), memory_space=pltpu.SMEM)
    any_spec = pl.BlockSpec(memory_space=pl.ANY)
    zero_tile = jnp.zeros((EXPERT_TILE, d), xn.dtype)
    pos, xs = pl.pallas_call(
        _dispatch_kernel,
        grid_spec=pltpu.PrefetchScalarGridSpec(
            num_scalar_prefetch=1,
            grid=(nt,),
            in_specs=[smem_tile, smem_tile, any_spec, any_spec],
            out_specs=[smem_tile, any_spec],
            scratch_shapes=[pltpu.SemaphoreType.DMA, pltpu.SemaphoreType.DMA],
        ),
        out_shape=[jax.ShapeDtypeStruct((nt, 1, per), I32), jax.ShapeDtypeStruct((n_rows, d), xn.dtype)],
        compiler_params=pltpu.CompilerParams(dimension_semantics=("arbitrary",)),
        name="dispatch",
    )(starts, tiled(eidx), tiled(rank), xn, zero_tile)
    return pos, xs


def _experts_kernel(te_ref, tb_ref, tv_ref, xs_ref, wgu_ref, bgu_ref, wd_ref, bd_ref, out_ref,
                    wgu_bf, wd_bf):
    i = pl.program_id(0)
    de = wd_ref.shape[1]
    prev = te_ref[jnp.maximum(i - 1, 0)]

    @pl.when(jnp.logical_or(i == 0, te_ref[i] != prev))
    def _():
        wgu_bf[...] = wgu_ref[0].astype(BF16)
        wd_bf[...] = wd_ref[0].astype(BF16)

    @pl.when(tv_ref[i] == 1)
    def _():
        x = xs_ref[...].astype(BF16)
        ch = 512
        acc = jnp.zeros(out_ref.shape, F32)
        for j in range(de // ch):
            g = _dot(x, wgu_bf[:, j * ch:(j + 1) * ch]) + bgu_ref[0, :, j * ch:(j + 1) * ch]
            u = _dot(x, wgu_bf[:, de + j * ch:de + (j + 1) * ch]) + bgu_ref[0, :, de + j * ch:de + (j + 1) * ch]
            g = jnp.minimum(g, SWIGLU_LIMIT)
            u = jnp.clip(u, -SWIGLU_LIMIT, SWIGLU_LIMIT)
            hm = (u + 1.0) * (g * jax.nn.sigmoid(g * SWIGLU_ALPHA))
            acc = acc + _dot(hm.astype(BF16), wd_bf[j * ch:(j + 1) * ch, :])
        out_ref[...] = acc + bd_ref[0]

    @pl.when(tv_ref[i] == 0)
    def _():
        out_ref[...] = jnp.zeros_like(out_ref)


def _experts(tile_e, tile_b, tile_v, xs, w_gate_up, b_gate_up, w_down, b_down):
    n_rows, d = xs.shape
    tmx = EXPERT_TILE
    ne, _, de2 = w_gate_up.shape
    de = de2 // 2
    return pl.pallas_call(
        _experts_kernel,
        grid_spec=pltpu.PrefetchScalarGridSpec(
            num_scalar_prefetch=3,
            grid=(n_rows // tmx,),
            in_specs=[pl.BlockSpec((tmx, d), lambda i, te, tb, tv: (tb[i], 0)),
                      pl.BlockSpec((1, d, de2), lambda i, te, tb, tv: (te[i], 0, 0)),
                      pl.BlockSpec((1, 1, de2), lambda i, te, tb, tv: (te[i], 0, 0)),
                      pl.BlockSpec((1, de, d), lambda i, te, tb, tv: (te[i], 0, 0)),
                      pl.BlockSpec((1, 1, d), lambda i, te, tb, tv: (te[i], 0, 0))],
            out_specs=pl.BlockSpec((tmx, d), lambda i, te, tb, tv: (i, 0)),
            scratch_shapes=[pltpu.VMEM((d, de2), BF16), pltpu.VMEM((de, d), BF16)],
        ),
        out_shape=jax.ShapeDtypeStruct((n_rows, d), F32),
        compiler_params=_params("arbitrary"),
        name="experts",
    )(tile_e, tile_b, tile_v, xs, w_gate_up, b_gate_up.reshape(ne, 1, de2), w_down, b_down.reshape(ne, 1, d))


def _combine_kernel(pos_ref, posn_ref, h_ref, gate_ref, p_ref, ys_ref, pn_ref, wg_ref, wp_ref, fn_ref,
                    out_ref, ybuf, sem, *, final):
    i = pl.program_id(0)
    nt = pl.num_programs(0)
    tm = h_ref.shape[0]

    def issue(ref, slot):
        def token_copy(t, carry):
            for k in range(TOP_K):
                pos = ref[0, 0, t * TOP_K + k]
                pltpu.make_async_copy(ys_ref.at[pl.ds(pos, 1)], ybuf.at[slot, k, pl.ds(t, 1)],
                                      sem.at[slot]).start()
            return carry

        lax.fori_loop(0, tm, token_copy, 0, unroll=2)

    @pl.when(i == 0)
    def _():
        issue(pos_ref, 0)

    @pl.when(i + 1 < nt)
    def _():
        issue(posn_ref, (i + 1) % 2)

    slot = i % 2
    for k in range(TOP_K):
        pltpu.make_async_copy(ys_ref.at[pl.ds(0, tm)], ybuf.at[slot, k], sem.at[slot]).wait()

    gate = gate_ref[...]
    h2 = h_ref[...]
    for k in range(TOP_K):
        h2 = h2 + gate[:, k:k + 1] * ybuf[slot, k]
    hn = _rms(h2, pn_ref[...]).astype(BF16)
    sg = jax.nn.sigmoid(_dot(hn, wg_ref[...]))
    h3 = h2 + sg * _dot(p_ref[...].astype(BF16), wp_ref[...])
    out_ref[...] = _rms(h3, fn_ref[...]) if final else h3


def _combine(pos, h1, gate, p2, ys, ple_norm, w_ple_gate, w_ple_proj, final_norm, final):
    n, d = h1.shape
    tm = ROW_TILE
    nt = n // tm
    per = tm * TOP_K
    wg = w_ple_gate.astype(BF16)
    wp = w_ple_proj.astype(BF16)
    full = lambda arr: pl.BlockSpec(arr.shape, lambda i: (0,) * arr.ndim)
    rows = lambda w: pl.BlockSpec((tm, w), lambda i: (i, 0))
    consts = [ple_norm.reshape(1, d), wg, wp, final_norm.reshape(1, d)]
    return pl.pallas_call(
        functools.partial(_combine_kernel, final=final),
        grid=(nt,),
        in_specs=[pl.BlockSpec((1, 1, per), lambda i: (i, 0, 0), memory_space=pltpu.SMEM),
                  pl.BlockSpec((1, 1, per), lambda i: (jnp.minimum(i + 1, nt - 1), 0, 0), memory_space=pltpu.SMEM),
                  rows(d), rows(TOP_K), rows(p2.shape[1]), pl.BlockSpec(memory_space=pl.ANY)]
                 + [full(c) for c in consts],
        out_specs=rows(d),
        out_shape=jax.ShapeDtypeStruct((n, d), F32),
        scratch_shapes=[pltpu.VMEM((2, TOP_K, tm, d), F32), pltpu.SemaphoreType.DMA((2,))],
        compiler_params=_params("arbitrary"),
        name="combine",
    )(pos, pos, h1, gate, p2, ys, *consts)


def _route_tables(counts, n_tiles):
    tmx = EXPERT_TILE
    tile_end = jnp.cumsum((counts + tmx - 1) // tmx)
    starts = jnp.concatenate([jnp.zeros((1,), I32), tile_end * tmx]).astype(I32)
    n_valid = tile_end[-1]
    t = jnp.arange(n_tiles, dtype=I32)
    tb = jnp.minimum(t, n_valid - 1).astype(I32)
    te = jnp.searchsorted(tile_end, tb, side="right").astype(I32)
    tv = (t < n_valid).astype(I32)
    return starts, te, tb, tv


def kernel(x, p, positions, attn_norm, w_in, b_gates, conv_w, conv_b, mlstm_norm, q_norm, w_q_up, kv_norm, w_kv_up, mla_norm, w_out, ffn_norm, w_router, b_router, w_gate_up, b_gate_up, w_down, b_down, ple_norm, w_ple_gate, w_ple_proj, final_norm):
    bsz, s, d = x.shape
    n = bsz * s
    depth = p.shape[0]
    nc = s // MLSTM_CHUNK
    n_tiles = n * TOP_K // EXPERT_TILE + N_EXPERTS
    pos2 = positions.reshape(n, 1)
    h = x.reshape(n, d)
    for i in range(depth):
        um, gt, q, k, v = _in_proj(h, pos2, attn_norm[i], w_in[i], b_gates[i], q_norm[i], w_q_up[i],
                                   kv_norm[i], w_kv_up[i])
        ym = _mlstm(um.reshape(bsz, s, -1), gt.reshape(N_GATES, bsz, nc, MLSTM_CHUNK), conv_w[i], conv_b[i],
                    mlstm_norm[i])
        ya = _attention(q.reshape(bsz, s, -1), k.reshape(bsz, s, -1), v.reshape(bsz, s, -1))
        h1, xn, eidx, gate, rank, counts = _out_route(ym.reshape(n, -1), ya.reshape(n, -1), h, mla_norm[i],
                                                      w_out[i], ffn_norm[i], w_router[i], b_router[i])
        starts, te, tb, tv = _route_tables(counts[0, :N_EXPERTS], n_tiles)
        pos, xs = _dispatch(starts, eidx, rank, xn, n_tiles * EXPERT_TILE)
        ys = _experts(te, tb, tv, xs, w_gate_up[i], b_gate_up[i], w_down[i], b_down[i])
        h = _combine(pos, h1, gate, p[i].reshape(n, -1), ys, ple_norm[i], w_ple_gate[i], w_ple_proj[i],
                     final_norm, final=(i == depth - 1))
    return h.reshape(bsz, s, d)
```

```python
import functools

import jax
import jax.numpy as jnp
from jax import lax
from jax.experimental import pallas as pl
from jax.experimental.pallas import tpu as pltpu

F32 = jnp.float32
BF16 = jnp.bfloat16
I32 = jnp.int32

N_MLSTM_HEADS = 4
MLSTM_HEAD_DIM = 128
D_MLSTM = N_MLSTM_HEADS * MLSTM_HEAD_DIM
MLSTM_CHUNK = 128
N_MLA_HEADS = 4
QK_NOPE_DIM = 128
QK_ROPE_DIM = 64
V_HEAD_DIM = 128
D_MLA = N_MLA_HEADS * V_HEAD_DIM
Q_LORA = 256
KV_LORA = 128
ROPE_THETA = 10000.0
N_EXPERTS = 32
TOP_K = 4
SWIGLU_LIMIT = 7.0
SWIGLU_ALPHA = 1.702
EPS = 1e-6
N_GATES = 4 * N_MLSTM_HEADS
OFF_G = 4 * D_MLSTM
OFF_CQ = OFF_G + N_GATES
OFF_CKV = OFF_CQ + Q_LORA
OFF_KR = OFF_CKV + KV_LORA

LANES = 128
QK_SLAB = 2 * LANES
VMEM_LIMIT_BYTES = 56 * 1024 * 1024

ROW_TILE = 512
Q_TILE = 256
EXPERT_TILE = 512


def _dot(a, b):
    return jnp.dot(a, b, preferred_element_type=F32)


def _dot_nt(a, b):
    return lax.dot_general(a, b, (((1,), (1,)), ((), ())), preferred_element_type=F32)


def _rms(x, g):
    return x * lax.rsqrt(jnp.mean(x * x, axis=-1, keepdims=True) + EPS) * g


def _log_sigmoid(x):
    return jnp.minimum(x, 0.0) - jnp.log(1.0 + jnp.exp(-jnp.abs(x)))


def _params(*sem):
    return pltpu.CompilerParams(dimension_semantics=sem, vmem_limit_bytes=VMEM_LIMIT_BYTES)


def _in_proj_kernel(x_ref, pos_ref, an_ref, wm_ref, wr_ref, wgt_ref, bgt_ref, qn_ref, wq_ref, kvn_ref,
                    wk_ref, wv_ref, freq_ref, sgn_ref,
                    um_ref, gt_ref, q_ref, k_ref, v_ref):
    a = _rms(x_ref[...], an_ref[...]).astype(BF16)
    um_ref[...] = _dot(a, wm_ref[...])
    gt_ref[...] = _dot_nt(wgt_ref[...], a) + bgt_ref[...]
    rest = _dot(a, wr_ref[...])
    cq = rest[:, :Q_LORA]
    ckv = rest[:, Q_LORA:Q_LORA + KV_LORA]
    kr2 = rest[:, Q_LORA + KV_LORA:Q_LORA + KV_LORA + LANES]
    krs2 = rest[:, Q_LORA + KV_LORA + LANES:]
    ang = pos_ref[...].astype(F32) * freq_ref[...]
    cos_a = jnp.cos(ang)
    sin_a = jnp.sin(ang) * sgn_ref[...]
    scale = (QK_NOPE_DIM + QK_ROPE_DIM) ** -0.5
    lane = lax.broadcasted_iota(I32, ang.shape, 1)
    rope_mul = jnp.where(lane < QK_ROPE_DIM, cos_a, sin_a) * scale
    qf = _dot(_rms(cq, qn_ref[...]).astype(BF16), wq_ref[...])
    ckvn = _rms(ckv, kvn_ref[...]).astype(BF16)
    kn = _dot(ckvn, wk_ref[...])
    v_ref[...] = _dot(ckvn, wv_ref[...]).astype(BF16)
    k_rope = (kr2 * cos_a + krs2 * sin_a).astype(BF16)
    for h in range(N_MLA_HEADS):
        o = h * QK_SLAB
        q_ref[:, o:o + LANES] = (qf[:, o:o + LANES] * scale).astype(BF16)
        q_ref[:, o + LANES:o + QK_SLAB] = (qf[:, o + LANES:o + QK_SLAB] * rope_mul).astype(BF16)
        k_ref[:, o:o + LANES] = kn[:, h * LANES:(h + 1) * LANES].astype(BF16)
        k_ref[:, o + LANES:o + QK_SLAB] = k_rope


def _in_proj(x2, pos2, attn_norm, w_in, b_gates, q_norm, w_q_up, kv_norm, w_kv_up):
    n, d = x2.shape
    tm = ROW_TILE
    half = QK_ROPE_DIM // 2
    swap = jnp.concatenate([jnp.arange(half, QK_ROPE_DIM), jnp.arange(0, half)])
    w_kr = w_in[:, OFF_KR:OFF_KR + QK_ROPE_DIM]
    w_krs = w_kr[:, swap]
    wm = w_in[:, :OFF_G].astype(BF16)
    wr = jnp.concatenate([w_in[:, OFF_CQ:OFF_KR], w_kr, w_kr, w_krs, w_krs], axis=1).astype(BF16)
    wgt = w_in[:, OFF_G:OFF_CQ].T.astype(BF16)
    bgt = b_gates.reshape(N_GATES, 1)
    wq4 = w_q_up.reshape(Q_LORA, N_MLA_HEADS, QK_NOPE_DIM + QK_ROPE_DIM)
    wq_pe = wq4[:, :, QK_NOPE_DIM:]
    wq = jnp.concatenate([wq4, wq_pe[:, :, swap]], axis=2).reshape(Q_LORA, N_MLA_HEADS * QK_SLAB).astype(BF16)
    wkv4 = w_kv_up.reshape(KV_LORA, N_MLA_HEADS, QK_NOPE_DIM + V_HEAD_DIM)
    wk = wkv4[:, :, :QK_NOPE_DIM].reshape(KV_LORA, N_MLA_HEADS * QK_NOPE_DIM).astype(BF16)
    wv = wkv4[:, :, QK_NOPE_DIM:].reshape(KV_LORA, D_MLA).astype(BF16)
    freqs = ROPE_THETA ** (-jnp.arange(0, QK_ROPE_DIM, 2, dtype=F32) / QK_ROPE_DIM)
    freq_l = jnp.tile(freqs, LANES // half).reshape(1, LANES)
    sgn_l = jnp.tile(jnp.concatenate([-jnp.ones((half,), F32), jnp.ones((half,), F32)]),
                     LANES // QK_ROPE_DIM).reshape(1, LANES)
    full = lambda arr: pl.BlockSpec(arr.shape, lambda i: (0,) * arr.ndim)
    rows = lambda w: pl.BlockSpec((tm, w), lambda i: (i, 0))
    consts = [attn_norm.reshape(1, d), wm, wr, wgt, bgt, q_norm.reshape(1, Q_LORA), wq,
              kv_norm.reshape(1, KV_LORA), wk, wv, freq_l, sgn_l]
    return pl.pallas_call(
        _in_proj_kernel,
        grid=(n // tm,),
        in_specs=[rows(d), rows(1)] + [full(c) for c in consts],
        out_specs=[rows(OFF_G), pl.BlockSpec((N_GATES, tm), lambda i: (0, i)),
                   rows(N_MLA_HEADS * QK_SLAB), rows(N_MLA_HEADS * QK_SLAB), rows(D_MLA)],
        out_shape=[jax.ShapeDtypeStruct((n, OFF_G), F32), jax.ShapeDtypeStruct((N_GATES, n), F32),
                   jax.ShapeDtypeStruct((n, N_MLA_HEADS * QK_SLAB), BF16),
                   jax.ShapeDtypeStruct((n, N_MLA_HEADS * QK_SLAB), BF16),
                   jax.ShapeDtypeStruct((n, D_MLA), BF16)],
        compiler_params=_params("parallel"),
        name="in_proj",
    )(x2, pos2, *consts)


def _cumsum_lanes(x, reverse):
    n = x.shape[-1]
    lane = lax.broadcasted_iota(I32, x.shape, x.ndim - 1)
    sh = 1
    while sh < n:
        if reverse:
            x = x + jnp.where(lane < n - sh, pltpu.roll(x, n - sh, x.ndim - 1), 0.0)
        else:
            x = x + jnp.where(lane >= sh, pltpu.roll(x, sh, x.ndim - 1), 0.0)
        sh *= 2
    return x


def _mlstm_kernel(q_ref, k_ref, v_ref, o_ref, g_ref, cwq_ref, cwk_ref, cbq_ref, cbk_ref, nrm_ref,
                  y_ref,
                  qc_ref, kc_ref, va_ref, cst_ref, b_ref, r_ref, ew_ref, mw_ref, bt_ref, mp_ref):
    L = MLSTM_CHUNK
    dh = MLSTM_HEAD_DIM
    nc = q_ref.shape[1] // L
    s_len = q_ref.shape[1]
    h = pl.program_id(1)

    for d in range(2):
        ig = g_ref[2 * d * N_MLSTM_HEADS + h, 0]
        fg = g_ref[(2 * d + 1) * N_MLSTM_HEADS + h, 0]
        b = _cumsum_lanes(_log_sigmoid(fg), reverse=(d == 1))
        btot = b[:, L - 1:L] if d == 0 else b[:, 0:1]
        r = ig - b
        w = btot + r
        mw = jnp.max(w, axis=-1, keepdims=True)
        b_ref[d] = b
        r_ref[d] = r
        ew_ref[d] = jnp.exp(w - mw)
        mw_ref[d] = jnp.broadcast_to(mw, (nc, L))
        bt_ref[d] = jnp.broadcast_to(btot, (nc, L))

    row = lax.broadcasted_iota(I32, (L, dh), 0)

    def conv_silu(ref, cw_ref, cb_ref, c):
        start = pl.multiple_of(c * L, L)
        x = ref[0, pl.ds(start, L), :]
        prev_row = jnp.where(c > 0, ref[0, pl.ds(jnp.maximum(start - 1, 0), 1), :], 0.0)
        next_row = jnp.where(c < nc - 1, ref[0, pl.ds(jnp.minimum(start + L, s_len - 1), 1), :], 0.0)
        x_prev = jnp.where(row == 0, prev_row, pltpu.roll(x, 1, 0))
        x_next = jnp.where(row == L - 1, next_row, pltpu.roll(x, L - 1, 0))
        y = cw_ref[0:1, :] * x_prev + cw_ref[1:2, :] * x + cw_ref[2:3, :] * x_next + cb_ref[...]
        return y * jax.nn.sigmoid(y)

    ones_col = (lax.broadcasted_iota(I32, (L, dh), 1) == 0).astype(BF16)

    def pass1(c, carry):
        start = pl.multiple_of(c * L, L)
        qc_ref[pl.ds(start, L), :] = conv_silu(q_ref, cwq_ref, cbq_ref, c).astype(BF16)
        kk = conv_silu(k_ref, cwk_ref, cbk_ref, c) * (dh ** -0.5)
        kc_ref[pl.ds(start, L), :] = kk.astype(BF16)
        va = jnp.concatenate([v_ref[0, pl.ds(start, L), :].astype(BF16), ones_col], axis=1)
        va_ref[pl.ds(start, L), :] = va
        kt = kk.T
        for d in range(2):
            kw_t = (kt * ew_ref[d, pl.ds(c, 1), :]).astype(BF16)
            cst_ref[d, c] = _dot(kw_t, va)
        return carry

    lax.fori_loop(0, nc, pass1, 0)

    for d in range(2):
        def scan(i, carry):
            st, m = carry
            c = i if d == 0 else nc - 1 - i
            mw = mw_ref[d, pl.ds(c, 1), :]
            bt = bt_ref[d, pl.ds(c, 1), :]
            m_new = jnp.maximum(bt + m, mw)
            a = jnp.exp(bt + m - m_new)[:, 0:1]
            cc = jnp.exp(mw - m_new)[:, 0:1]
            loc = cst_ref[d, c]
            cst_ref[d, c] = st
            mp_ref[d, pl.ds(c, 1), :] = m
            return a * st + cc * loc, m_new

        lax.fori_loop(0, nc, scan, (jnp.zeros((dh, 2 * dh), F32), jnp.zeros((1, L), F32)))

    ti = lax.broadcasted_iota(I32, (L, L), 0)
    si = lax.broadcasted_iota(I32, (L, L), 1)
    masks = (si <= ti, si >= ti)

    def pass3(c, carry):
        start = pl.multiple_of(c * L, L)
        q = qc_ref[pl.ds(start, L), :]
        k = kc_ref[pl.ds(start, L), :]
        va = va_ref[pl.ds(start, L), :]
        qk = _dot_nt(q, k)
        hsum = jnp.zeros((L, dh), F32)
        for d in range(2):
            bmat = jnp.broadcast_to(b_ref[d, pl.ds(c, 1), :], (L, L)).T
            dmat = jnp.where(masks[d], bmat + r_ref[d, pl.ds(c, 1), :], -jnp.inf)
            inter = bmat[:, 0:1] + mp_ref[d, pl.ds(c, 1), :][:, 0:1]
            m_t = jnp.maximum(inter, jnp.max(dmat, axis=-1, keepdims=True))
            sc = qk * jnp.exp(dmat - m_t)
            a = jnp.exp(inter - m_t)
            tot = _dot(sc.astype(BF16), va) + a * _dot(q, cst_ref[d, c].astype(BF16))
            den = jnp.maximum(jnp.abs(tot[:, dh:dh + 1]), jnp.exp(-m_t))
            hsum = hsum + tot[:, :dh] / den
        hn = _rms(hsum, nrm_ref[...])
        y_ref[0, pl.ds(start, L), :] = hn * jax.nn.sigmoid(o_ref[0, pl.ds(start, L), :])
        return carry

    lax.fori_loop(0, nc, pass3, 0)


def _mlstm(um3, gt4, conv_w, conv_b, mlstm_norm):
    bsz, s, _ = um3.shape
    H, dh, L = N_MLSTM_HEADS, MLSTM_HEAD_DIM, MLSTM_CHUNK
    nc = s // L
    col = lambda off: pl.BlockSpec((1, s, dh), lambda b, h: (b, 0, off + h))
    vec = lambda rows, off: pl.BlockSpec((rows, dh), lambda b, h: (0, off + h))
    cb = conv_b.reshape(1, 2 * D_MLSTM)
    return pl.pallas_call(
        _mlstm_kernel,
        grid=(bsz, H),
        in_specs=[col(0), col(H), col(2 * H), col(3 * H),
                  pl.BlockSpec((N_GATES, 1, nc, L), lambda b, h: (0, b, 0, 0)),
                  vec(3, 0), vec(3, H), vec(1, 0), vec(1, H), vec(1, 0)],
        out_specs=pl.BlockSpec((1, s, dh), lambda b, h: (b, 0, h)),
        out_shape=jax.ShapeDtypeStruct((bsz, s, D_MLSTM), F32),
        scratch_shapes=[pltpu.VMEM((s, dh), BF16), pltpu.VMEM((s, dh), BF16), pltpu.VMEM((s, 2 * dh), BF16),
                        pltpu.VMEM((2, nc, dh, 2 * dh), F32)]
                       + [pltpu.VMEM((2, nc, L), F32) for _ in range(6)],
        compiler_params=_params("parallel", "parallel"),
        name="mlstm",
    )(um3, um3, um3, um3, gt4, conv_w, conv_w, cb, cb, mlstm_norm.reshape(1, D_MLSTM))


def _attn_kernel(q_ref, k_ref, v_ref, o_ref):
    s = _dot_nt(q_ref[0], k_ref[0])
    m = jnp.max(s, axis=-1, keepdims=True)
    p = jnp.exp(s - m)
    l = jnp.sum(p, axis=-1, keepdims=True)
    o_ref[0] = _dot(p.astype(BF16), v_ref[0]) / l


def _attention(q3, k3, v3):
    bsz, s, _ = q3.shape
    tq = Q_TILE
    return pl.pallas_call(
        _attn_kernel,
        grid=(bsz, N_MLA_HEADS, s // tq),
        in_specs=[pl.BlockSpec((1, tq, QK_SLAB), lambda b, h, i: (b, i, h)),
                  pl.BlockSpec((1, s, QK_SLAB), lambda b, h, i: (b, 0, h)),
                  pl.BlockSpec((1, s, V_HEAD_DIM), lambda b, h, i: (b, 0, h))],
        out_specs=pl.BlockSpec((1, tq, V_HEAD_DIM), lambda b, h, i: (b, i, h)),
        out_shape=jax.ShapeDtypeStruct((bsz, s, D_MLA), F32),
        compiler_params=_params("parallel", "parallel", "parallel"),
        name="attention",
    )(q3, k3, v3)


def _out_route_kernel(ym_ref, ya_ref, x_ref, mn_ref, wom_ref, woa_ref, fn_ref, wr_ref, br_ref,
                      h_ref, xn_ref, eidx_ref, gate_ref, rank_ref, cnt_ref,
                      carry_ref):
    i = pl.program_id(0)
    tm = x_ref.shape[0]

    @pl.when(i == 0)
    def _():
        carry_ref[...] = jnp.zeros_like(carry_ref)

    ya = _rms(ya_ref[...], mn_ref[...])
    h1 = x_ref[...] + _dot(ym_ref[...].astype(BF16), wom_ref[...]) + _dot(ya.astype(BF16), woa_ref[...])
    h_ref[...] = h1
    xn = _rms(h1, fn_ref[...])
    xn_ref[...] = xn
    logits = jnp.dot(xn, wr_ref[...], preferred_element_type=F32, precision=lax.Precision.HIGHEST) + br_ref[...]
    lane = lax.broadcasted_iota(I32, logits.shape, 1)
    work = jnp.where(lane < N_EXPERTS, logits, -jnp.inf)
    vals, hots = [], []
    eidx = jnp.zeros(logits.shape, I32)
    for k in range(TOP_K):
        mx = jnp.max(work, axis=-1, keepdims=True)
        idx = jnp.min(jnp.where(work == mx, lane, LANES), axis=-1, keepdims=True)
        hot = lane == idx
        work = jnp.where(hot, -jnp.inf, work)
        vals.append(mx)
        hots.append(hot)
        eidx = jnp.where(lane == k, idx, eidx)
    exps = [jnp.exp(v - vals[0]) for v in vals]
    tot = exps[0] + exps[1] + exps[2] + exps[3]
    multi = (hots[0] | hots[1] | hots[2] | hots[3]).astype(BF16)
    ri = lax.broadcasted_iota(I32, (tm, tm), 0)
    ci = lax.broadcasted_iota(I32, (tm, tm), 1)
    before = (ci < ri).astype(BF16)
    rank_all = _dot(before, multi) + carry_ref[...]
    gate = jnp.zeros(logits.shape, F32)
    rank = jnp.zeros(logits.shape, F32)
    for k in range(TOP_K):
        gate = jnp.where(lane == k, exps[k] / tot, gate)
        rk = jnp.sum(jnp.where(hots[k], rank_all, 0.0), axis=-1, keepdims=True)
        rank = jnp.where(lane == k, rk, rank)
    eidx_ref[...] = eidx[:, :TOP_K]
    gate_ref[...] = gate[:, :TOP_K]
    rank_ref[...] = rank[:, :TOP_K].astype(I32)
    carry_ref[...] += jnp.sum(multi.astype(F32), axis=0, keepdims=True)
    cnt_ref[...] = carry_ref[...].astype(I32)


def _out_route(ym2, ya2, x2, mla_norm, w_out, ffn_norm, w_router, b_router):
    n, d = x2.shape
    tm = ROW_TILE
    wom = w_out[:D_MLSTM].astype(BF16)
    woa = w_out[D_MLSTM:].astype(BF16)
    wr = jnp.pad(w_router, ((0, 0), (0, LANES - N_EXPERTS)))
    br = jnp.pad(b_router.reshape(1, N_EXPERTS), ((0, 0), (0, LANES - N_EXPERTS)))
    full = lambda arr: pl.BlockSpec(arr.shape, lambda i: (0,) * arr.ndim)
    rows = lambda w: pl.BlockSpec((tm, w), lambda i: (i, 0))
    consts = [mla_norm.reshape(1, D_MLA), wom, woa, ffn_norm.reshape(1, d), wr, br]
    return pl.pallas_call(
        _out_route_kernel,
        grid=(n // tm,),
        in_specs=[rows(D_MLSTM), rows(D_MLA), rows(d)] + [full(c) for c in consts],
        out_specs=[rows(d), rows(d), rows(TOP_K), rows(TOP_K), rows(TOP_K),
                   pl.BlockSpec((1, LANES), lambda i: (0, 0))],
        out_shape=[jax.ShapeDtypeStruct((n, d), F32), jax.ShapeDtypeStruct((n, d), F32),
                   jax.ShapeDtypeStruct((n, TOP_K), I32), jax.ShapeDtypeStruct((n, TOP_K), F32),
                   jax.ShapeDtypeStruct((n, TOP_K), I32), jax.ShapeDtypeStruct((1, LANES), I32)],
        scratch_shapes=[pltpu.VMEM((1, LANES), F32)],
        compiler_params=_params("arbitrary"),
        name="out_route",
    )(ym2, ya2, x2, *consts)


def _dispatch_kernel(starts_ref, eidx_ref, rank_ref, xn_ref, zero_ref,
                     pos_ref, xs_ref, sem, zsem):
    i = pl.program_id(0)
    per = eidx_ref.shape[2]
    tm = per // TOP_K
    tmx = zero_ref.shape[0]

    @pl.when(i == 0)
    def _():
        def clear_tile(row):
            cp = pltpu.make_async_copy(zero_ref, xs_ref.at[pl.ds(pl.multiple_of(row, tmx), tmx)], zsem)
            cp.start()
            cp.wait()

        def clear_group(e, carry):
            hi = starts_ref[e + 1]

            @pl.when(hi > starts_ref[e])
            def _():
                clear_tile(hi - tmx)

            return carry

        def clear_tail(t, carry):
            clear_tile(starts_ref[N_EXPERTS] + t * tmx)
            return carry

        lax.fori_loop(0, N_EXPERTS, clear_group, 0)
        lax.fori_loop(0, (xs_ref.shape[0] - starts_ref[N_EXPERTS]) // tmx, clear_tail, 0)

    def token_copy(t, carry):
        src = xn_ref.at[pl.ds(t, 1)]
        for k in range(TOP_K):
            j = t * TOP_K + k
            pos = starts_ref[eidx_ref[0, 0, j]] + rank_ref[0, 0, j]
            pos_ref[0, 0, j] = pos
            pltpu.make_async_copy(src, xs_ref.at[pl.ds(pos, 1)], sem).start()
        return carry

    lax.fori_loop(0, tm, token_copy, 0, unroll=2)
    for k in range(TOP_K):
        pltpu.make_async_copy(xn_ref, xs_ref.at[pl.ds(0, tm)], sem).wait()


def _dispatch(starts, eidx, rank, xn, n_rows):
    n, d = xn.shape
    tm = ROW_TILE
    nt = n // tm
    per = tm * TOP_K
    tiled = lambda a: a.reshape(nt, 1, per)
    smem_tile = pl.BlockSpec((1, 1, per), lambda i, *_: (i, 0, 0), memory_space=pltpu.SMEM)
    any_spec = pl.BlockSpec(memory_space=pl.ANY)
    zero_tile = jnp.zeros((EXPERT_TILE, d), xn.dtype)
    pos, xs = pl.pallas_call(
        _dispatch_kernel,
        grid_spec=pltpu.PrefetchScalarGridSpec(
            num_scalar_prefetch=1,
            grid=(nt,),
            in_specs=[smem_tile, smem_tile, pl.BlockSpec((tm, d), lambda i, *_: (i, 0)), any_spec],
            out_specs=[smem_tile, any_spec],
            scratch_shapes=[pltpu.SemaphoreType.DMA, pltpu.SemaphoreType.DMA],
        ),
        out_shape=[jax.ShapeDtypeStruct((nt, 1, per), I32), jax.ShapeDtypeStruct((n_rows, d), xn.dtype)],
        compiler_params=pltpu.CompilerParams(dimension_semantics=("arbitrary",)),
        name="dispatch",
    )(starts, tiled(eidx), tiled(rank), xn, zero_tile)
    return pos, xs


def _experts_kernel(te_ref, tb_ref, tv_ref, xs_ref, wgu_ref, bgu_ref, wd_ref, bd_ref, out_ref,
                    wgu_bf, wd_bf):
    i = pl.program_id(0)
    de = wd_ref.shape[1]
    prev = te_ref[jnp.maximum(i - 1, 0)]

    @pl.when(jnp.logical_or(i == 0, te_ref[i] != prev))
    def _():
        wgu_bf[...] = wgu_ref[0].astype(BF16)
        wd_bf[...] = wd_ref[0].astype(BF16)

    @pl.when(tv_ref[i] == 1)
    def _():
        x = xs_ref[...].astype(BF16)
        ch = 512
        acc = jnp.zeros(out_ref.shape, F32)
        for j in range(de // ch):
            g = _dot(x, wgu_bf[:, j * ch:(j + 1) * ch]) + bgu_ref[0, :, j * ch:(j + 1) * ch]
            u = _dot(x, wgu_bf[:, de + j * ch:de + (j + 1) * ch]) + bgu_ref[0, :, de + j * ch:de + (j + 1) * ch]
            g = jnp.minimum(g, SWIGLU_LIMIT)
            u = jnp.clip(u, -SWIGLU_LIMIT, SWIGLU_LIMIT)
            hm = (u + 1.0) * (g * jax.nn.sigmoid(g * SWIGLU_ALPHA))
            acc = acc + _dot(hm.astype(BF16), wd_bf[j * ch:(j + 1) * ch, :])
        out_ref[...] = acc + bd_ref[0]

    @pl.when(tv_ref[i] == 0)
    def _():
        out_ref[...] = jnp.zeros_like(out_ref)


def _experts(tile_e, tile_b, tile_v, xs, w_gate_up, b_gate_up, w_down, b_down):
    n_rows, d = xs.shape
    tmx = EXPERT_TILE
    ne, _, de2 = w_gate_up.shape
    de = de2 // 2
    return pl.pallas_call(
        _experts_kernel,
        grid_spec=pltpu.PrefetchScalarGridSpec(
            num_scalar_prefetch=3,
            grid=(n_rows // tmx,),
            in_specs=[pl.BlockSpec((tmx, d), lambda i, te, tb, tv: (tb[i], 0)),
                      pl.BlockSpec((1, d, de2), lambda i, te, tb, tv: (te[i], 0, 0)),
                      pl.BlockSpec((1, 1, de2), lambda i, te, tb, tv: (te[i], 0, 0)),
                      pl.BlockSpec((1, de, d), lambda i, te, tb, tv: (te[i], 0, 0)),
                      pl.BlockSpec((1, 1, d), lambda i, te, tb, tv: (te[i], 0, 0))],
            out_specs=pl.BlockSpec((tmx, d), lambda i, te, tb, tv: (i, 0)),
            scratch_shapes=[pltpu.VMEM((d, de2), BF16), pltpu.VMEM((de, d), BF16)],
        ),
        out_shape=jax.ShapeDtypeStruct((n_rows, d), F32),
        compiler_params=_params("arbitrary"),
        name="experts",
    )(tile_e, tile_b, tile_v, xs, w_gate_up, b_gate_up.reshape(ne, 1, de2), w_down, b_down.reshape(ne, 1, d))


def _combine_kernel(pos_ref, posn_ref, h_ref, gate_ref, p_ref, ys_ref, pn_ref, wg_ref, wp_ref, fn_ref,
                    out_ref, ybuf, sem, *, final):
    i = pl.program_id(0)
    nt = pl.num_programs(0)
    tm = h_ref.shape[0]

    def issue(ref, slot):
        def token_copy(t, carry):
            for k in range(TOP_K):
                pos = ref[0, 0, t * TOP_K + k]
                pltpu.make_async_copy(ys_ref.at[pl.ds(pos, 1)], ybuf.at[slot, k, pl.ds(t, 1)],
                                      sem.at[slot]).start()
            return carry

        lax.fori_loop(0, tm, token_copy, 0, unroll=2)

    @pl.when(i == 0)
    def _():
        issue(pos_ref, 0)

    @pl.when(i + 1 < nt)
    def _():
        issue(posn_ref, (i + 1) % 2)

    slot = i % 2
    for k in range(TOP_K):
        pltpu.make_async_copy(ys_ref.at[pl.ds(0, tm)], ybuf.at[slot, k], sem.at[slot]).wait()

    gate = gate_ref[...]
    h2 = h_ref[...]
    for k in range(TOP_K):
        h2 = h2 + gate[:, k:k + 1] * ybuf[slot, k]
    hn = _rms(h2, pn_ref[...]).astype(BF16)
    sg = jax.nn.sigmoid(_dot(hn, wg_ref[...]))
    h3 = h2 + sg * _dot(p_ref[...].astype(BF16), wp_ref[...])
    out_ref[...] = _rms(h3, fn_ref[...]) if final else h3


def _combine(pos, h1, gate, p2, ys, ple_norm, w_ple_gate, w_ple_proj, final_norm, final):
    n, d = h1.shape
    tm = ROW_TILE
    nt = n // tm
    per = tm * TOP_K
    wg = w_ple_gate.astype(BF16)
    wp = w_ple_proj.astype(BF16)
    full = lambda arr: pl.BlockSpec(arr.shape, lambda i: (0,) * arr.ndim)
    rows = lambda w: pl.BlockSpec((tm, w), lambda i: (i, 0))
    consts = [ple_norm.reshape(1, d), wg, wp, final_norm.reshape(1, d)]
    return pl.pallas_call(
        functools.partial(_combine_kernel, final=final),
        grid=(nt,),
        in_specs=[pl.BlockSpec((1, 1, per), lambda i: (i, 0, 0), memory_space=pltpu.SMEM),
                  pl.BlockSpec((1, 1, per), lambda i: (jnp.minimum(i + 1, nt - 1), 0, 0), memory_space=pltpu.SMEM),
                  rows(d), rows(TOP_K), rows(p2.shape[1]), pl.BlockSpec(memory_space=pl.ANY)]
                 + [full(c) for c in consts],
        out_specs=rows(d),
        out_shape=jax.ShapeDtypeStruct((n, d), F32),
        scratch_shapes=[pltpu.VMEM((2, TOP_K, tm, d), F32), pltpu.SemaphoreType.DMA((2,))],
        compiler_params=_params("arbitrary"),
        name="combine",
    )(pos, pos, h1, gate, p2, ys, *consts)


def _route_tables(counts, n_tiles):
    tmx = EXPERT_TILE
    tile_end = jnp.cumsum((counts + tmx - 1) // tmx)
    starts = jnp.concatenate([jnp.zeros((1,), I32), tile_end * tmx]).astype(I32)
    n_valid = tile_end[-1]
    t = jnp.arange(n_tiles, dtype=I32)
    tb = jnp.minimum(t, n_valid - 1).astype(I32)
    te = jnp.sum(tile_end[None, :] <= tb[:, None], axis=1).astype(I32)
    tv = (t < n_valid).astype(I32)
    return starts, te, tb, tv


def kernel(x, p, positions, attn_norm, w_in, b_gates, conv_w, conv_b, mlstm_norm, q_norm, w_q_up, kv_norm, w_kv_up, mla_norm, w_out, ffn_norm, w_router, b_router, w_gate_up, b_gate_up, w_down, b_down, ple_norm, w_ple_gate, w_ple_proj, final_norm):
    bsz, s, d = x.shape
    n = bsz * s
    depth = p.shape[0]
    nc = s // MLSTM_CHUNK
    n_tiles = n * TOP_K // EXPERT_TILE + N_EXPERTS
    pos2 = positions.reshape(n, 1)
    h = x.reshape(n, d)
    for i in range(depth):
        um, gt, q, k, v = _in_proj(h, pos2, attn_norm[i], w_in[i], b_gates[i], q_norm[i], w_q_up[i],
                                   kv_norm[i], w_kv_up[i])
        ym = _mlstm(um.reshape(bsz, s, -1), gt.reshape(N_GATES, bsz, nc, MLSTM_CHUNK), conv_w[i], conv_b[i],
                    mlstm_norm[i])
        ya = _attention(q.reshape(bsz, s, -1), k.reshape(bsz, s, -1), v.reshape(bsz, s, -1))
        h1, xn, eidx, gate, rank, counts = _out_route(ym.reshape(n, -1), ya.reshape(n, -1), h, mla_norm[i],
                                                      w_out[i], ffn_norm[i], w_router[i], b_router[i])
        starts, te, tb, tv = _route_tables(counts[0, :N_EXPERTS], n_tiles)
        pos, xs = _dispatch(starts, eidx, rank, xn, n_tiles * EXPERT_TILE)
        ys = _experts(te, tb, tv, xs, w_gate_up[i], b_gate_up[i], w_down[i], b_down[i])
        h = _combine(pos, h1, gate, p[i].reshape(n, -1), ys, ple_norm[i], w_ple_gate[i], w_ple_proj[i],
                     final_norm, final=(i == depth - 1))
    return h.reshape(bsz, s, d)
```

```python
import functools

import jax
import jax.numpy as jnp
from jax import lax
from jax.experimental import pallas as pl
from jax.experimental.pallas import tpu as pltpu

F32 = jnp.float32
BF16 = jnp.bfloat16
I32 = jnp.int32

N_MLSTM_HEADS = 4
MLSTM_HEAD_DIM = 128
D_MLSTM = N_MLSTM_HEADS * MLSTM_HEAD_DIM
MLSTM_CHUNK = 128
N_MLA_HEADS = 4
QK_NOPE_DIM = 128
QK_ROPE_DIM = 64
V_HEAD_DIM = 128
D_MLA = N_MLA_HEADS * V_HEAD_DIM
Q_LORA = 256
KV_LORA = 128
ROPE_THETA = 10000.0
N_EXPERTS = 32
TOP_K = 4
SWIGLU_LIMIT = 7.0
SWIGLU_ALPHA = 1.702
EPS = 1e-6
N_GATES = 4 * N_MLSTM_HEADS
OFF_G = 4 * D_MLSTM
OFF_CQ = OFF_G + N_GATES
OFF_CKV = OFF_CQ + Q_LORA
OFF_KR = OFF_CKV + KV_LORA

LANES = 128
QK_SLAB = 2 * LANES
VMEM_LIMIT_BYTES = 56 * 1024 * 1024

ROW_TILE = 512
Q_TILE = 256
EXPERT_TILE = 512


def _dot(a, b):
    return jnp.dot(a, b, preferred_element_type=F32)


def _dot_nt(a, b):
    return lax.dot_general(a, b, (((1,), (1,)), ((), ())), preferred_element_type=F32)


def _rms(x, g):
    return x * lax.rsqrt(jnp.mean(x * x, axis=-1, keepdims=True) + EPS) * g


def _log_sigmoid(x):
    return jnp.minimum(x, 0.0) - jnp.log(1.0 + jnp.exp(-jnp.abs(x)))


def _to_row_tiles(ref, x):
    for j in range(ref.shape[1]):
        ref[:, j, :] = x[:, j * LANES:(j + 1) * LANES]


def _from_row_tiles(x3):
    return jnp.concatenate([x3[:, j, :] for j in range(x3.shape[1])], axis=1)


def _params(*sem):
    return pltpu.CompilerParams(dimension_semantics=sem, vmem_limit_bytes=VMEM_LIMIT_BYTES)


def _in_proj_kernel(x_ref, pos_ref, an_ref, wm_ref, wr_ref, wgt_ref, bgt_ref, qn_ref, wq_ref, kvn_ref,
                    wk_ref, wv_ref, freq_ref, sgn_ref,
                    um_ref, gt_ref, q_ref, k_ref, v_ref):
    a = _rms(x_ref[...], an_ref[...]).astype(BF16)
    um_ref[...] = _dot(a, wm_ref[...])
    gt_ref[...] = _dot_nt(wgt_ref[...], a) + bgt_ref[...]
    rest = _dot(a, wr_ref[...])
    cq = rest[:, :Q_LORA]
    ckv = rest[:, Q_LORA:Q_LORA + KV_LORA]
    kr2 = rest[:, Q_LORA + KV_LORA:Q_LORA + KV_LORA + LANES]
    krs2 = rest[:, Q_LORA + KV_LORA + LANES:]
    ang = pos_ref[...].astype(F32) * freq_ref[...]
    cos_a = jnp.cos(ang)
    sin_a = jnp.sin(ang) * sgn_ref[...]
    scale = (QK_NOPE_DIM + QK_ROPE_DIM) ** -0.5
    lane = lax.broadcasted_iota(I32, ang.shape, 1)
    rope_mul = jnp.where(lane < QK_ROPE_DIM, cos_a, sin_a) * scale
    qf = _dot(_rms(cq, qn_ref[...]).astype(BF16), wq_ref[...])
    ckvn = _rms(ckv, kvn_ref[...]).astype(BF16)
    kn = _dot(ckvn, wk_ref[...])
    v_ref[...] = _dot(ckvn, wv_ref[...]).astype(BF16)
    k_rope = (kr2 * cos_a + krs2 * sin_a).astype(BF16)
    for h in range(N_MLA_HEADS):
        o = h * QK_SLAB
        q_ref[:, o:o + LANES] = (qf[:, o:o + LANES] * scale).astype(BF16)
        q_ref[:, o + LANES:o + QK_SLAB] = (qf[:, o + LANES:o + QK_SLAB] * rope_mul).astype(BF16)
        k_ref[:, o:o + LANES] = kn[:, h * LANES:(h + 1) * LANES].astype(BF16)
        k_ref[:, o + LANES:o + QK_SLAB] = k_rope


def _in_proj(x2, pos2, attn_norm, w_in, b_gates, q_norm, w_q_up, kv_norm, w_kv_up):
    n, d = x2.shape
    tm = ROW_TILE
    half = QK_ROPE_DIM // 2
    swap = jnp.concatenate([jnp.arange(half, QK_ROPE_DIM), jnp.arange(0, half)])
    w_kr = w_in[:, OFF_KR:OFF_KR + QK_ROPE_DIM]
    w_krs = w_kr[:, swap]
    wm = w_in[:, :OFF_G].astype(BF16)
    wr = jnp.concatenate([w_in[:, OFF_CQ:OFF_KR], w_kr, w_kr, w_krs, w_krs], axis=1).astype(BF16)
    wgt = w_in[:, OFF_G:OFF_CQ].T.astype(BF16)
    bgt = b_gates.reshape(N_GATES, 1)
    wq4 = w_q_up.reshape(Q_LORA, N_MLA_HEADS, QK_NOPE_DIM + QK_ROPE_DIM)
    wq_pe = wq4[:, :, QK_NOPE_DIM:]
    wq = jnp.concatenate([wq4, wq_pe[:, :, swap]], axis=2).reshape(Q_LORA, N_MLA_HEADS * QK_SLAB).astype(BF16)
    wkv4 = w_kv_up.reshape(KV_LORA, N_MLA_HEADS, QK_NOPE_DIM + V_HEAD_DIM)
    wk = wkv4[:, :, :QK_NOPE_DIM].reshape(KV_LORA, N_MLA_HEADS * QK_NOPE_DIM).astype(BF16)
    wv = wkv4[:, :, QK_NOPE_DIM:].reshape(KV_LORA, D_MLA).astype(BF16)
    freqs = ROPE_THETA ** (-jnp.arange(0, QK_ROPE_DIM, 2, dtype=F32) / QK_ROPE_DIM)
    freq_l = jnp.tile(freqs, LANES // half).reshape(1, LANES)
    sgn_l = jnp.tile(jnp.concatenate([-jnp.ones((half,), F32), jnp.ones((half,), F32)]),
                     LANES // QK_ROPE_DIM).reshape(1, LANES)
    full = lambda arr: pl.BlockSpec(arr.shape, lambda i: (0,) * arr.ndim)
    rows = lambda w: pl.BlockSpec((tm, w), lambda i: (i, 0))
    consts = [attn_norm.reshape(1, d), wm, wr, wgt, bgt, q_norm.reshape(1, Q_LORA), wq,
              kv_norm.reshape(1, KV_LORA), wk, wv, freq_l, sgn_l]
    return pl.pallas_call(
        _in_proj_kernel,
        grid=(n // tm,),
        in_specs=[rows(d), rows(1)] + [full(c) for c in consts],
        out_specs=[rows(OFF_G), pl.BlockSpec((N_GATES, tm), lambda i: (0, i)),
                   rows(N_MLA_HEADS * QK_SLAB), rows(N_MLA_HEADS * QK_SLAB), rows(D_MLA)],
        out_shape=[jax.ShapeDtypeStruct((n, OFF_G), F32), jax.ShapeDtypeStruct((N_GATES, n), F32),
                   jax.ShapeDtypeStruct((n, N_MLA_HEADS * QK_SLAB), BF16),
                   jax.ShapeDtypeStruct((n, N_MLA_HEADS * QK_SLAB), BF16),
                   jax.ShapeDtypeStruct((n, D_MLA), BF16)],
        compiler_params=_params("parallel"),
        name="in_proj",
    )(x2, pos2, *consts)


def _cumsum_lanes(x, reverse):
    n = x.shape[-1]
    lane = lax.broadcasted_iota(I32, x.shape, x.ndim - 1)
    sh = 1
    while sh < n:
        if reverse:
            x = x + jnp.where(lane < n - sh, pltpu.roll(x, n - sh, x.ndim - 1), 0.0)
        else:
            x = x + jnp.where(lane >= sh, pltpu.roll(x, sh, x.ndim - 1), 0.0)
        sh *= 2
    return x


def _mlstm_kernel(q_ref, k_ref, v_ref, o_ref, g_ref, cwq_ref, cwk_ref, cbq_ref, cbk_ref, nrm_ref,
                  y_ref,
                  qc_ref, kc_ref, va_ref, cst_ref, b_ref, r_ref, ew_ref, mw_ref, bt_ref, mp_ref):
    L = MLSTM_CHUNK
    dh = MLSTM_HEAD_DIM
    nc = q_ref.shape[1] // L
    s_len = q_ref.shape[1]
    h = pl.program_id(1)

    for d in range(2):
        ig = g_ref[2 * d * N_MLSTM_HEADS + h, 0]
        fg = g_ref[(2 * d + 1) * N_MLSTM_HEADS + h, 0]
        b = _cumsum_lanes(_log_sigmoid(fg), reverse=(d == 1))
        btot = b[:, L - 1:L] if d == 0 else b[:, 0:1]
        r = ig - b
        w = btot + r
        mw = jnp.max(w, axis=-1, keepdims=True)
        b_ref[d] = b
        r_ref[d] = r
        ew_ref[d] = jnp.exp(w - mw)
        mw_ref[d] = jnp.broadcast_to(mw, (nc, L))
        bt_ref[d] = jnp.broadcast_to(btot, (nc, L))

    row = lax.broadcasted_iota(I32, (L, dh), 0)

    def conv_silu(ref, cw_ref, cb_ref, c):
        start = pl.multiple_of(c * L, L)
        x = ref[0, pl.ds(start, L), :]
        prev_row = jnp.where(c > 0, ref[0, pl.ds(jnp.maximum(start - 1, 0), 1), :], 0.0)
        next_row = jnp.where(c < nc - 1, ref[0, pl.ds(jnp.minimum(start + L, s_len - 1), 1), :], 0.0)
        x_prev = jnp.where(row == 0, prev_row, pltpu.roll(x, 1, 0))
        x_next = jnp.where(row == L - 1, next_row, pltpu.roll(x, L - 1, 0))
        y = cw_ref[0:1, :] * x_prev + cw_ref[1:2, :] * x + cw_ref[2:3, :] * x_next + cb_ref[...]
        return y * jax.nn.sigmoid(y)

    ones_col = (lax.broadcasted_iota(I32, (L, dh), 1) == 0).astype(BF16)

    def pass1(c, carry):
        start = pl.multiple_of(c * L, L)
        qc_ref[pl.ds(start, L), :] = conv_silu(q_ref, cwq_ref, cbq_ref, c).astype(BF16)
        kk = conv_silu(k_ref, cwk_ref, cbk_ref, c) * (dh ** -0.5)
        kc_ref[pl.ds(start, L), :] = kk.astype(BF16)
        va = jnp.concatenate([v_ref[0, pl.ds(start, L), :].astype(BF16), ones_col], axis=1)
        va_ref[pl.ds(start, L), :] = va
        kt = kk.T
        for d in range(2):
            kw_t = (kt * ew_ref[d, pl.ds(c, 1), :]).astype(BF16)
            cst_ref[d, c] = _dot(kw_t, va)
        return carry

    lax.fori_loop(0, nc, pass1, 0)

    for d in range(2):
        def scan(i, carry):
            st, m = carry
            c = i if d == 0 else nc - 1 - i
            mw = mw_ref[d, pl.ds(c, 1), :]
            bt = bt_ref[d, pl.ds(c, 1), :]
            m_new = jnp.maximum(bt + m, mw)
            a = jnp.exp(bt + m - m_new)[:, 0:1]
            cc = jnp.exp(mw - m_new)[:, 0:1]
            loc = cst_ref[d, c]
            cst_ref[d, c] = st
            mp_ref[d, pl.ds(c, 1), :] = m
            return a * st + cc * loc, m_new

        lax.fori_loop(0, nc, scan, (jnp.zeros((dh, 2 * dh), F32), jnp.zeros((1, L), F32)))

    ti = lax.broadcasted_iota(I32, (L, L), 0)
    si = lax.broadcasted_iota(I32, (L, L), 1)
    masks = (si <= ti, si >= ti)

    def pass3(c, carry):
        start = pl.multiple_of(c * L, L)
        q = qc_ref[pl.ds(start, L), :]
        k = kc_ref[pl.ds(start, L), :]
        va = va_ref[pl.ds(start, L), :]
        qk = _dot_nt(q, k)
        hsum = jnp.zeros((L, dh), F32)
        for d in range(2):
            bmat = jnp.broadcast_to(b_ref[d, pl.ds(c, 1), :], (L, L)).T
            dmat = jnp.where(masks[d], bmat + r_ref[d, pl.ds(c, 1), :], -jnp.inf)
            inter = bmat[:, 0:1] + mp_ref[d, pl.ds(c, 1), :][:, 0:1]
            m_t = jnp.maximum(inter, jnp.max(dmat, axis=-1, keepdims=True))
            sc = qk * jnp.exp(dmat - m_t)
            a = jnp.exp(inter - m_t)
            tot = _dot(sc.astype(BF16), va) + a * _dot(q, cst_ref[d, c].astype(BF16))
            den = jnp.maximum(jnp.abs(tot[:, dh:dh + 1]), jnp.exp(-m_t))
            hsum = hsum + tot[:, :dh] / den
        hn = _rms(hsum, nrm_ref[...])
        y_ref[0, pl.ds(start, L), :] = hn * jax.nn.sigmoid(o_ref[0, pl.ds(start, L), :])
        return carry

    lax.fori_loop(0, nc, pass3, 0)


def _mlstm(um3, gt4, conv_w, conv_b, mlstm_norm):
    bsz, s, _ = um3.shape
    H, dh, L = N_MLSTM_HEADS, MLSTM_HEAD_DIM, MLSTM_CHUNK
    nc = s // L
    col = lambda off: pl.BlockSpec((1, s, dh), lambda b, h: (b, 0, off + h))
    vec = lambda rows, off: pl.BlockSpec((rows, dh), lambda b, h: (0, off + h))
    cb = conv_b.reshape(1, 2 * D_MLSTM)
    return pl.pallas_call(
        _mlstm_kernel,
        grid=(bsz, H),
        in_specs=[col(0), col(H), col(2 * H), col(3 * H),
                  pl.BlockSpec((N_GATES, 1, nc, L), lambda b, h: (0, b, 0, 0)),
                  vec(3, 0), vec(3, H), vec(1, 0), vec(1, H), vec(1, 0)],
        out_specs=pl.BlockSpec((1, s, dh), lambda b, h: (b, 0, h)),
        out_shape=jax.ShapeDtypeStruct((bsz, s, D_MLSTM), F32),
        scratch_shapes=[pltpu.VMEM((s, dh), BF16), pltpu.VMEM((s, dh), BF16), pltpu.VMEM((s, 2 * dh), BF16),
                        pltpu.VMEM((2, nc, dh, 2 * dh), F32)]
                       + [pltpu.VMEM((2, nc, L), F32) for _ in range(6)],
        compiler_params=_params("parallel", "parallel"),
        name="mlstm",
    )(um3, um3, um3, um3, gt4, conv_w, conv_w, cb, cb, mlstm_norm.reshape(1, D_MLSTM))


def _attn_kernel(q_ref, k_ref, v_ref, o_ref):
    s = _dot_nt(q_ref[0], k_ref[0])
    m = jnp.max(s, axis=-1, keepdims=True)
    p = jnp.exp(s - m)
    l = jnp.sum(p, axis=-1, keepdims=True)
    o_ref[0] = _dot(p.astype(BF16), v_ref[0]) / l


def _attention(q3, k3, v3):
    bsz, s, _ = q3.shape
    tq = Q_TILE
    return pl.pallas_call(
        _attn_kernel,
        grid=(bsz, N_MLA_HEADS, s // tq),
        in_specs=[pl.BlockSpec((1, tq, QK_SLAB), lambda b, h, i: (b, i, h)),
                  pl.BlockSpec((1, s, QK_SLAB), lambda b, h, i: (b, 0, h)),
                  pl.BlockSpec((1, s, V_HEAD_DIM), lambda b, h, i: (b, 0, h))],
        out_specs=pl.BlockSpec((1, tq, V_HEAD_DIM), lambda b, h, i: (b, i, h)),
        out_shape=jax.ShapeDtypeStruct((bsz, s, D_MLA), F32),
        compiler_params=_params("parallel", "parallel", "parallel"),
        name="attention",
    )(q3, k3, v3)


def _out_route_kernel(ym_ref, ya_ref, x_ref, mn_ref, wom_ref, woa_ref, fn_ref, wr_ref, br_ref,
                      h_ref, xn_ref, eidx_ref, gate_ref, rank_ref, cnt_ref,
                      carry_ref):
    i = pl.program_id(0)
    tm = x_ref.shape[0]

    @pl.when(i == 0)
    def _():
        carry_ref[...] = jnp.zeros_like(carry_ref)

    ya = _rms(ya_ref[...], mn_ref[...])
    h1 = x_ref[...] + _dot(ym_ref[...].astype(BF16), wom_ref[...]) + _dot(ya.astype(BF16), woa_ref[...])
    h_ref[...] = h1
    xn = _rms(h1, fn_ref[...])
    _to_row_tiles(xn_ref, xn)
    logits = jnp.dot(xn, wr_ref[...], preferred_element_type=F32, precision=lax.Precision.HIGHEST) + br_ref[...]
    lane = lax.broadcasted_iota(I32, logits.shape, 1)
    work = jnp.where(lane < N_EXPERTS, logits, -jnp.inf)
    vals, hots = [], []
    eidx = jnp.zeros(logits.shape, I32)
    for k in range(TOP_K):
        mx = jnp.max(work, axis=-1, keepdims=True)
        idx = jnp.min(jnp.where(work == mx, lane, LANES), axis=-1, keepdims=True)
        hot = lane == idx
        work = jnp.where(hot, -jnp.inf, work)
        vals.append(mx)
        hots.append(hot)
        eidx = jnp.where(lane == k, idx, eidx)
    exps = [jnp.exp(v - vals[0]) for v in vals]
    tot = exps[0] + exps[1] + exps[2] + exps[3]
    multi = (hots[0] | hots[1] | hots[2] | hots[3]).astype(BF16)
    ri = lax.broadcasted_iota(I32, (tm, tm), 0)
    ci = lax.broadcasted_iota(I32, (tm, tm), 1)
    before = (ci < ri).astype(BF16)
    rank_all = _dot(before, multi) + carry_ref[...]
    gate = jnp.zeros(logits.shape, F32)
    rank = jnp.zeros(logits.shape, F32)
    for k in range(TOP_K):
        gate = jnp.where(lane == k, exps[k] / tot, gate)
        rk = jnp.sum(jnp.where(hots[k], rank_all, 0.0), axis=-1, keepdims=True)
        rank = jnp.where(lane == k, rk, rank)
    eidx_ref[...] = eidx[:, :TOP_K]
    gate_ref[...] = gate[:, :TOP_K]
    rank_ref[...] = rank[:, :TOP_K].astype(I32)
    carry_ref[...] += jnp.sum(multi.astype(F32), axis=0, keepdims=True)
    cnt_ref[...] = carry_ref[...].astype(I32)


def _out_route(ym2, ya2, x2, mla_norm, w_out, ffn_norm, w_router, b_router):
    n, d = x2.shape
    tm = ROW_TILE
    wom = w_out[:D_MLSTM].astype(BF16)
    woa = w_out[D_MLSTM:].astype(BF16)
    wr = jnp.pad(w_router, ((0, 0), (0, LANES - N_EXPERTS)))
    br = jnp.pad(b_router.reshape(1, N_EXPERTS), ((0, 0), (0, LANES - N_EXPERTS)))
    full = lambda arr: pl.BlockSpec(arr.shape, lambda i: (0,) * arr.ndim)
    rows = lambda w: pl.BlockSpec((tm, w), lambda i: (i, 0))
    consts = [mla_norm.reshape(1, D_MLA), wom, woa, ffn_norm.reshape(1, d), wr, br]
    return pl.pallas_call(
        _out_route_kernel,
        grid=(n // tm,),
        in_specs=[rows(D_MLSTM), rows(D_MLA), rows(d)] + [full(c) for c in consts],
        out_specs=[rows(d), pl.BlockSpec((tm, d // LANES, LANES), lambda i: (i, 0, 0)),
                   rows(TOP_K), rows(TOP_K), rows(TOP_K), pl.BlockSpec((1, LANES), lambda i: (0, 0))],
        out_shape=[jax.ShapeDtypeStruct((n, d), F32), jax.ShapeDtypeStruct((n, d // LANES, LANES), F32),
                   jax.ShapeDtypeStruct((n, TOP_K), I32), jax.ShapeDtypeStruct((n, TOP_K), F32),
                   jax.ShapeDtypeStruct((n, TOP_K), I32), jax.ShapeDtypeStruct((1, LANES), I32)],
        scratch_shapes=[pltpu.VMEM((1, LANES), F32)],
        compiler_params=_params("arbitrary"),
        name="out_route",
    )(ym2, ya2, x2, *consts)


def _dispatch_kernel(starts_ref, eidx_ref, rank_ref, xn_ref, zero_ref,
                     pos_ref, xs_ref, sem, zsem):
    i = pl.program_id(0)
    per = eidx_ref.shape[2]
    tm = per // TOP_K
    tmx = zero_ref.shape[0]

    @pl.when(i == 0)
    def _():
        def clear_tile(row):
            cp = pltpu.make_async_copy(zero_ref, xs_ref.at[pl.ds(pl.multiple_of(row, tmx), tmx)], zsem)
            cp.start()
            cp.wait()

        def clear_group(e, carry):
            hi = starts_ref[e + 1]

            @pl.when(hi > starts_ref[e])
            def _():
                clear_tile(hi - tmx)

            return carry

        def clear_tail(t, carry):
            clear_tile(starts_ref[N_EXPERTS] + t * tmx)
            return carry

        lax.fori_loop(0, N_EXPERTS, clear_group, 0)
        lax.fori_loop(0, (xs_ref.shape[0] - starts_ref[N_EXPERTS]) // tmx, clear_tail, 0)

    def token_copy(t, carry):
        src = xn_ref.at[pl.ds(t, 1)]
        for k in range(TOP_K):
            j = t * TOP_K + k
            pos = starts_ref[eidx_ref[0, 0, j]] + rank_ref[0, 0, j]
            pos_ref[0, 0, j] = pos
            pltpu.make_async_copy(src, xs_ref.at[pl.ds(pos, 1)], sem).start()
        return carry

    lax.fori_loop(0, tm, token_copy, 0, unroll=2)
    for k in range(TOP_K):
        pltpu.make_async_copy(xn_ref, xs_ref.at[pl.ds(0, tm)], sem).wait()


def _dispatch(starts, eidx, rank, xn, n_rows):
    n = xn.shape[0]
    tm = ROW_TILE
    nt = n // tm
    per = tm * TOP_K
    tiled = lambda a: a.reshape(nt, 1, per)
    smem_tile = pl.BlockSpec((1, 1, per), lambda i, *_: (i, 0, 0), memory_space=pltpu.SMEM)
    any_spec = pl.BlockSpec(memory_space=pl.ANY)
    zero_tile = jnp.zeros((EXPERT_TILE,) + xn.shape[1:], xn.dtype)
    pos, xs = pl.pallas_call(
        _dispatch_kernel,
        grid_spec=pltpu.PrefetchScalarGridSpec(
            num_scalar_prefetch=1,
            grid=(nt,),
            in_specs=[smem_tile, smem_tile, pl.BlockSpec((tm,) + xn.shape[1:], lambda i, *_: (i, 0, 0)), any_spec],
            out_specs=[smem_tile, any_spec],
            scratch_shapes=[pltpu.SemaphoreType.DMA, pltpu.SemaphoreType.DMA],
        ),
        out_shape=[jax.ShapeDtypeStruct((nt, 1, per), I32), jax.ShapeDtypeStruct((n_rows,) + xn.shape[1:], xn.dtype)],
        compiler_params=pltpu.CompilerParams(dimension_semantics=("arbitrary",)),
        name="dispatch",
    )(starts, tiled(eidx), tiled(rank), xn, zero_tile)
    return pos, xs


def _experts_kernel(te_ref, tb_ref, tv_ref, xs_ref, wgu_ref, bgu_ref, wd_ref, bd_ref, out_ref,
                    wgu_bf, wd_bf):
    i = pl.program_id(0)
    de = wd_ref.shape[1]
    prev = te_ref[jnp.maximum(i - 1, 0)]

    @pl.when(jnp.logical_or(i == 0, te_ref[i] != prev))
    def _():
        wgu_bf[...] = wgu_ref[0].astype(BF16)
        wd_bf[...] = wd_ref[0].astype(BF16)

    @pl.when(tv_ref[i] == 1)
    def _():
        x = _from_row_tiles(xs_ref).astype(BF16)
        ch = 512
        acc = jnp.zeros((xs_ref.shape[0], bd_ref.shape[2]), F32)
        for j in range(de // ch):
            g = _dot(x, wgu_bf[:, j * ch:(j + 1) * ch]) + bgu_ref[0, :, j * ch:(j + 1) * ch]
            u = _dot(x, wgu_bf[:, de + j * ch:de + (j + 1) * ch]) + bgu_ref[0, :, de + j * ch:de + (j + 1) * ch]
            g = jnp.minimum(g, SWIGLU_LIMIT)
            u = jnp.clip(u, -SWIGLU_LIMIT, SWIGLU_LIMIT)
            hm = (u + 1.0) * (g * jax.nn.sigmoid(g * SWIGLU_ALPHA))
            acc = acc + _dot(hm.astype(BF16), wd_bf[j * ch:(j + 1) * ch, :])
        _to_row_tiles(out_ref, acc + bd_ref[0])

    @pl.when(tv_ref[i] == 0)
    def _():
        out_ref[...] = jnp.zeros_like(out_ref)


def _experts(tile_e, tile_b, tile_v, xs, w_gate_up, b_gate_up, w_down, b_down):
    n_rows = xs.shape[0]
    d = xs.shape[1] * xs.shape[2]
    tmx = EXPERT_TILE
    ne, _, de2 = w_gate_up.shape
    de = de2 // 2
    return pl.pallas_call(
        _experts_kernel,
        grid_spec=pltpu.PrefetchScalarGridSpec(
            num_scalar_prefetch=3,
            grid=(n_rows // tmx,),
            in_specs=[pl.BlockSpec((tmx,) + xs.shape[1:], lambda i, te, tb, tv: (tb[i], 0, 0)),
                      pl.BlockSpec((1, d, de2), lambda i, te, tb, tv: (te[i], 0, 0)),
                      pl.BlockSpec((1, 1, de2), lambda i, te, tb, tv: (te[i], 0, 0)),
                      pl.BlockSpec((1, de, d), lambda i, te, tb, tv: (te[i], 0, 0)),
                      pl.BlockSpec((1, 1, d), lambda i, te, tb, tv: (te[i], 0, 0))],
            out_specs=pl.BlockSpec((tmx,) + xs.shape[1:], lambda i, te, tb, tv: (i, 0, 0)),
            scratch_shapes=[pltpu.VMEM((d, de2), BF16), pltpu.VMEM((de, d), BF16)],
        ),
        out_shape=jax.ShapeDtypeStruct(xs.shape, F32),
        compiler_params=_params("arbitrary"),
        name="experts",
    )(tile_e, tile_b, tile_v, xs, w_gate_up, b_gate_up.reshape(ne, 1, de2), w_down, b_down.reshape(ne, 1, d))


def _combine_kernel(pos_ref, posn_ref, h_ref, gate_ref, p_ref, ys_ref, pn_ref, wg_ref, wp_ref, fn_ref,
                    out_ref, ybuf, sem, *, final):
    i = pl.program_id(0)
    nt = pl.num_programs(0)
    tm = h_ref.shape[0]

    def issue(ref, slot):
        def token_copy(t, carry):
            for k in range(TOP_K):
                pos = ref[0, 0, t * TOP_K + k]
                pltpu.make_async_copy(ys_ref.at[pl.ds(pos, 1)], ybuf.at[slot, k, pl.ds(t, 1)],
                                      sem.at[slot]).start()
            return carry

        lax.fori_loop(0, tm, token_copy, 0, unroll=2)

    @pl.when(i == 0)
    def _():
        issue(pos_ref, 0)

    @pl.when(i + 1 < nt)
    def _():
        issue(posn_ref, (i + 1) % 2)

    slot = i % 2
    for k in range(TOP_K):
        pltpu.make_async_copy(ys_ref.at[pl.ds(0, tm)], ybuf.at[slot, k], sem.at[slot]).wait()

    gate = gate_ref[...]
    h2 = h_ref[...]
    for k in range(TOP_K):
        h2 = h2 + gate[:, k:k + 1] * _from_row_tiles(ybuf.at[slot, k])
    hn = _rms(h2, pn_ref[...]).astype(BF16)
    sg = jax.nn.sigmoid(_dot(hn, wg_ref[...]))
    h3 = h2 + sg * _dot(p_ref[...].astype(BF16), wp_ref[...])
    out_ref[...] = _rms(h3, fn_ref[...]) if final else h3


def _combine(pos, h1, gate, p2, ys, ple_norm, w_ple_gate, w_ple_proj, final_norm, final):
    n, d = h1.shape
    tm = ROW_TILE
    nt = n // tm
    per = tm * TOP_K
    wg = w_ple_gate.astype(BF16)
    wp = w_ple_proj.astype(BF16)
    full = lambda arr: pl.BlockSpec(arr.shape, lambda i: (0,) * arr.ndim)
    rows = lambda w: pl.BlockSpec((tm, w), lambda i: (i, 0))
    consts = [ple_norm.reshape(1, d), wg, wp, final_norm.reshape(1, d)]
    return pl.pallas_call(
        functools.partial(_combine_kernel, final=final),
        grid=(nt,),
        in_specs=[pl.BlockSpec((1, 1, per), lambda i: (i, 0, 0), memory_space=pltpu.SMEM),
                  pl.BlockSpec((1, 1, per), lambda i: (jnp.minimum(i + 1, nt - 1), 0, 0), memory_space=pltpu.SMEM),
                  rows(d), rows(TOP_K), rows(p2.shape[1]), pl.BlockSpec(memory_space=pl.ANY)]
                 + [full(c) for c in consts],
        out_specs=rows(d),
        out_shape=jax.ShapeDtypeStruct((n, d), F32),
        scratch_shapes=[pltpu.VMEM((2, TOP_K, tm) + ys.shape[1:], F32), pltpu.SemaphoreType.DMA((2,))],
        compiler_params=_params("arbitrary"),
        name="combine",
    )(pos, pos, h1, gate, p2, ys, *consts)


def _route_tables(counts, n_tiles):
    tmx = EXPERT_TILE
    tile_end = jnp.cumsum((counts + tmx - 1) // tmx)
    starts = jnp.concatenate([jnp.zeros((1,), I32), tile_end * tmx]).astype(I32)
    n_valid = tile_end[-1]
    t = jnp.arange(n_tiles, dtype=I32)
    tb = jnp.minimum(t, n_valid - 1).astype(I32)
    te = jnp.sum(tile_end[None, :] <= tb[:, None], axis=1).astype(I32)
    tv = (t < n_valid).astype(I32)
    return starts, te, tb, tv


def kernel(x, p, positions, attn_norm, w_in, b_gates, conv_w, conv_b, mlstm_norm, q_norm, w_q_up, kv_norm, w_kv_up, mla_norm, w_out, ffn_norm, w_router, b_router, w_gate_up, b_gate_up, w_down, b_down, ple_norm, w_ple_gate, w_ple_proj, final_norm):
    bsz, s, d = x.shape
    n = bsz * s
    depth = p.shape[0]
    nc = s // MLSTM_CHUNK
    n_tiles = n * TOP_K // EXPERT_TILE + N_EXPERTS
    pos2 = positions.reshape(n, 1)
    h = x.reshape(n, d)
    for i in range(depth):
        um, gt, q, k, v = _in_proj(h, pos2, attn_norm[i], w_in[i], b_gates[i], q_norm[i], w_q_up[i],
                                   kv_norm[i], w_kv_up[i])
        ym = _mlstm(um.reshape(bsz, s, -1), gt.reshape(N_GATES, bsz, nc, MLSTM_CHUNK), conv_w[i], conv_b[i],
                    mlstm_norm[i])
        ya = _attention(q.reshape(bsz, s, -1), k.reshape(bsz, s, -1), v.reshape(bsz, s, -1))
        h1, xn, eidx, gate, rank, counts = _out_route(ym.reshape(n, -1), ya.reshape(n, -1), h, mla_norm[i],
                                                      w_out[i], ffn_norm[i], w_router[i], b_router[i])
        starts, te, tb, tv = _route_tables(counts[0, :N_EXPERTS], n_tiles)
        pos, xs = _dispatch(starts, eidx, rank, xn, n_tiles * EXPERT_TILE)
        ys = _experts(te, tb, tv, xs, w_gate_up[i], b_gate_up[i], w_down[i], b_down[i])
        h = _combine(pos, h1, gate, p[i].reshape(n, -1), ys, ple_norm[i], w_ple_gate[i], w_ple_proj[i],
                     final_norm, final=(i == depth - 1))
    return h.reshape(bsz, s, d)
```

```python
import functools

import jax
import jax.numpy as jnp
from jax import lax
from jax.experimental import pallas as pl
from jax.experimental.pallas import tpu as pltpu

F32 = jnp.float32
BF16 = jnp.bfloat16
I32 = jnp.int32

N_MLSTM_HEADS = 4
MLSTM_HEAD_DIM = 128
D_MLSTM = N_MLSTM_HEADS * MLSTM_HEAD_DIM
MLSTM_CHUNK = 128
N_MLA_HEADS = 4
QK_NOPE_DIM = 128
QK_ROPE_DIM = 64
V_HEAD_DIM = 128
D_MLA = N_MLA_HEADS * V_HEAD_DIM
Q_LORA = 256
KV_LORA = 128
ROPE_THETA = 10000.0
N_EXPERTS = 32
TOP_K = 4
SWIGLU_LIMIT = 7.0
SWIGLU_ALPHA = 1.702
EPS = 1e-6
N_GATES = 4 * N_MLSTM_HEADS
OFF_G = 4 * D_MLSTM
OFF_CQ = OFF_G + N_GATES
OFF_CKV = OFF_CQ + Q_LORA
OFF_KR = OFF_CKV + KV_LORA

LANES = 128
SUBLANES = 8
BF16_SUBLANES = 16
QK_SLAB = 2 * LANES
VMEM_LIMIT_BYTES = 56 * 1024 * 1024
LOG2_E = 1.4426950408889634

ROW_TILE = 512
Q_TILE = 256
EXPERT_TILE = 512
SEG_ALIGN = BF16_SUBLANES
SORT_ROWS = ROW_TILE * TOP_K + N_EXPERTS * SEG_ALIGN


def _dot(a, b):
    return jnp.dot(a, b, preferred_element_type=F32)


def _dot_nt(a, b):
    return lax.dot_general(a, b, (((1,), (1,)), ((), ())), preferred_element_type=F32)


def _rms(x, g):
    return x * lax.rsqrt(jnp.mean(x * x, axis=-1, keepdims=True) + EPS) * g


def _log_sigmoid(x):
    return jnp.minimum(x, 0.0) - jnp.log(1.0 + jnp.exp(-jnp.abs(x)))


def _cumsum_lanes(x, reverse):
    n = x.shape[-1]
    lane = lax.broadcasted_iota(I32, x.shape, x.ndim - 1)
    sh = 1
    while sh < n:
        if reverse:
            x = x + jnp.where(lane < n - sh, pltpu.roll(x, n - sh, x.ndim - 1), 0.0)
        else:
            x = x + jnp.where(lane >= sh, pltpu.roll(x, sh, x.ndim - 1), 0.0)
        sh *= 2
    return x


def _params(*sem):
    return pltpu.CompilerParams(dimension_semantics=sem, vmem_limit_bytes=VMEM_LIMIT_BYTES)


def _in_proj_kernel(x_ref, pos_ref, an_ref, wm_ref, wr_ref, wgt_ref, bgt_ref, qn_ref, wq_ref, kvn_ref,
                    wk_ref, wv_ref, freq_ref, sgn_ref,
                    um_ref, gt_ref, q_ref, k_ref, v_ref):
    a = _rms(x_ref[...], an_ref[...]).astype(BF16)
    um_ref[...] = _dot(a, wm_ref[...])
    gt_ref[...] = _dot_nt(wgt_ref[...], a) + bgt_ref[...]
    rest = _dot(a, wr_ref[...])
    cq = rest[:, :Q_LORA]
    ckv = rest[:, Q_LORA:Q_LORA + KV_LORA]
    kr2 = rest[:, Q_LORA + KV_LORA:Q_LORA + KV_LORA + LANES]
    krs2 = rest[:, Q_LORA + KV_LORA + LANES:]
    ang = pos_ref[...].astype(F32) * freq_ref[...]
    cos_a = jnp.cos(ang)
    sin_a = jnp.sin(ang) * sgn_ref[...]
    scale = (QK_NOPE_DIM + QK_ROPE_DIM) ** -0.5 * LOG2_E
    lane = lax.broadcasted_iota(I32, ang.shape, 1)
    rope_mul = jnp.where(lane < QK_ROPE_DIM, cos_a, sin_a) * scale
    qf = _dot(_rms(cq, qn_ref[...]).astype(BF16), wq_ref[...])
    ckvn = _rms(ckv, kvn_ref[...]).astype(BF16)
    kn = _dot(ckvn, wk_ref[...])
    vv = _dot(ckvn, wv_ref[...])
    k_rope = (kr2 * cos_a + krs2 * sin_a).astype(BF16)
    ones_col = (lane == 0).astype(BF16)
    for h in range(N_MLA_HEADS):
        o = h * QK_SLAB
        q_ref[:, o:o + LANES] = (qf[:, o:o + LANES] * scale).astype(BF16)
        q_ref[:, o + LANES:o + QK_SLAB] = (qf[:, o + LANES:o + QK_SLAB] * rope_mul).astype(BF16)
        k_ref[:, o:o + LANES] = kn[:, h * LANES:(h + 1) * LANES].astype(BF16)
        k_ref[:, o + LANES:o + QK_SLAB] = k_rope
        v_ref[:, o:o + LANES] = vv[:, h * LANES:(h + 1) * LANES].astype(BF16)
        v_ref[:, o + LANES:o + QK_SLAB] = ones_col


def _in_proj(x2, pos2, attn_norm, w_in, b_gates, q_norm, w_q_up, kv_norm, w_kv_up):
    n, d = x2.shape
    tm = ROW_TILE
    half = QK_ROPE_DIM // 2
    swap = jnp.concatenate([jnp.arange(half, QK_ROPE_DIM), jnp.arange(0, half)])
    w_kr = w_in[:, OFF_KR:OFF_KR + QK_ROPE_DIM]
    w_krs = w_kr[:, swap]
    wm = w_in[:, :OFF_G].astype(BF16)
    wr = jnp.concatenate([w_in[:, OFF_CQ:OFF_KR], w_kr, w_kr, w_krs, w_krs], axis=1).astype(BF16)
    wgt = w_in[:, OFF_G:OFF_CQ].T.astype(BF16)
    bgt = b_gates.reshape(N_GATES, 1)
    wq4 = w_q_up.reshape(Q_LORA, N_MLA_HEADS, QK_NOPE_DIM + QK_ROPE_DIM)
    wq_pe = wq4[:, :, QK_NOPE_DIM:]
    wq = jnp.concatenate([wq4, wq_pe[:, :, swap]], axis=2).reshape(Q_LORA, N_MLA_HEADS * QK_SLAB).astype(BF16)
    wkv4 = w_kv_up.reshape(KV_LORA, N_MLA_HEADS, QK_NOPE_DIM + V_HEAD_DIM)
    wk = wkv4[:, :, :QK_NOPE_DIM].reshape(KV_LORA, N_MLA_HEADS * QK_NOPE_DIM).astype(BF16)
    wv = wkv4[:, :, QK_NOPE_DIM:].reshape(KV_LORA, D_MLA).astype(BF16)
    freqs = ROPE_THETA ** (-jnp.arange(0, QK_ROPE_DIM, 2, dtype=F32) / QK_ROPE_DIM)
    freq_l = jnp.tile(freqs, LANES // half).reshape(1, LANES)
    sgn_l = jnp.tile(jnp.concatenate([-jnp.ones((half,), F32), jnp.ones((half,), F32)]),
                     LANES // QK_ROPE_DIM).reshape(1, LANES)
    full = lambda arr: pl.BlockSpec(arr.shape, lambda i: (0,) * arr.ndim)
    rows = lambda w: pl.BlockSpec((tm, w), lambda i: (i, 0))
    consts = [attn_norm.reshape(1, d), wm, wr, wgt, bgt, q_norm.reshape(1, Q_LORA), wq,
              kv_norm.reshape(1, KV_LORA), wk, wv, freq_l, sgn_l]
    return pl.pallas_call(
        _in_proj_kernel,
        grid=(n // tm,),
        in_specs=[rows(d), rows(1)] + [full(c) for c in consts],
        out_specs=[rows(OFF_G), pl.BlockSpec((N_GATES, tm), lambda i: (0, i)),
                   rows(N_MLA_HEADS * QK_SLAB), rows(N_MLA_HEADS * QK_SLAB), rows(N_MLA_HEADS * QK_SLAB)],
        out_shape=[jax.ShapeDtypeStruct((n, OFF_G), F32), jax.ShapeDtypeStruct((N_GATES, n), F32),
                   jax.ShapeDtypeStruct((n, N_MLA_HEADS * QK_SLAB), BF16),
                   jax.ShapeDtypeStruct((n, N_MLA_HEADS * QK_SLAB), BF16),
                   jax.ShapeDtypeStruct((n, N_MLA_HEADS * QK_SLAB), BF16)],
        compiler_params=_params("parallel"),
        name="in_proj",
    )(x2, pos2, *consts)


def _mlstm_kernel(q_ref, k_ref, v_ref, o_ref, g_ref, cwq_ref, cwk_ref, cbq_ref, cbk_ref, nrm_ref,
                  y_ref,
                  qc_ref, kc_ref, va_ref, cst_ref, b_ref, r_ref, ew_ref, mw_ref, bt_ref, mp_ref):
    L = MLSTM_CHUNK
    dh = MLSTM_HEAD_DIM
    nc = q_ref.shape[1] // L
    s_len = q_ref.shape[1]
    h = pl.program_id(1)

    for d in range(2):
        ig = g_ref[2 * d * N_MLSTM_HEADS + h, 0]
        fg = g_ref[(2 * d + 1) * N_MLSTM_HEADS + h, 0]
        b = _cumsum_lanes(_log_sigmoid(fg), reverse=(d == 1))
        btot = b[:, L - 1:L] if d == 0 else b[:, 0:1]
        r = ig - b
        w = btot + r
        mw = jnp.max(w, axis=-1, keepdims=True)
        b_ref[d] = b
        r_ref[d] = r
        ew_ref[d] = jnp.exp(w - mw)
        mw_ref[d] = jnp.broadcast_to(mw, (nc, L))
        bt_ref[d] = jnp.broadcast_to(btot, (nc, L))

    row = lax.broadcasted_iota(I32, (L, dh), 0)

    def conv_silu(ref, cw_ref, cb_ref, c):
        start = pl.multiple_of(c * L, L)
        x = ref[0, pl.ds(start, L), :]
        prev_row = jnp.where(c > 0, ref[0, pl.ds(jnp.maximum(start - 1, 0), 1), :], 0.0)
        next_row = jnp.where(c < nc - 1, ref[0, pl.ds(jnp.minimum(start + L, s_len - 1), 1), :], 0.0)
        x_prev = jnp.where(row == 0, prev_row, pltpu.roll(x, 1, 0))
        x_next = jnp.where(row == L - 1, next_row, pltpu.roll(x, L - 1, 0))
        y = cw_ref[0:1, :] * x_prev + cw_ref[1:2, :] * x + cw_ref[2:3, :] * x_next + cb_ref[...]
        return y * jax.nn.sigmoid(y)

    ones_col = (lax.broadcasted_iota(I32, (L, dh), 1) == 0).astype(BF16)

    def pass1(c, carry):
        start = pl.multiple_of(c * L, L)
        qc_ref[pl.ds(start, L), :] = conv_silu(q_ref, cwq_ref, cbq_ref, c).astype(BF16)
        kk = conv_silu(k_ref, cwk_ref, cbk_ref, c) * (dh ** -0.5)
        kc_ref[pl.ds(start, L), :] = kk.astype(BF16)
        va = jnp.concatenate([v_ref[0, pl.ds(start, L), :].astype(BF16), ones_col], axis=1)
        va_ref[pl.ds(start, L), :] = va
        kt = kk.T
        for d in range(2):
            kw_t = (kt * ew_ref[d, pl.ds(c, 1), :]).astype(BF16)
            cst_ref[d, c] = _dot(kw_t, va)
        return carry

    lax.fori_loop(0, nc, pass1, 0)

    for d in range(2):
        def scan(i, carry):
            st, m = carry
            c = i if d == 0 else nc - 1 - i
            mw = mw_ref[d, pl.ds(c, 1), :]
            bt = bt_ref[d, pl.ds(c, 1), :]
            m_new = jnp.maximum(bt + m, mw)
            a = jnp.exp(bt + m - m_new)[:, 0:1]
            cc = jnp.exp(mw - m_new)[:, 0:1]
            loc = cst_ref[d, c]
            cst_ref[d, c] = st
            mp_ref[d, pl.ds(c, 1), :] = m
            return a * st + cc * loc, m_new

        lax.fori_loop(0, nc, scan, (jnp.zeros((dh, 2 * dh), F32), jnp.zeros((1, L), F32)))

    ti = lax.broadcasted_iota(I32, (L, L), 0)
    si = lax.broadcasted_iota(I32, (L, L), 1)
    masks = (si <= ti, si >= ti)

    def pass3(c, carry):
        start = pl.multiple_of(c * L, L)
        q = qc_ref[pl.ds(start, L), :]
        k = kc_ref[pl.ds(start, L), :]
        va = va_ref[pl.ds(start, L), :]
        qk = _dot_nt(q, k)
        hsum = jnp.zeros((L, dh), F32)
        for d in range(2):
            bmat = jnp.broadcast_to(b_ref[d, pl.ds(c, 1), :], (L, L)).T
            dmat = jnp.where(masks[d], bmat + r_ref[d, pl.ds(c, 1), :], -jnp.inf)
            inter = bmat[:, 0:1] + mp_ref[d, pl.ds(c, 1), :][:, 0:1]
            m_t = jnp.maximum(inter, jnp.max(dmat, axis=-1, keepdims=True))
            sc = qk * jnp.exp(dmat - m_t)
            a = jnp.exp(inter - m_t)
            tot = _dot(sc.astype(BF16), va) + a * _dot(q, cst_ref[d, c].astype(BF16))
            den = jnp.maximum(jnp.abs(tot[:, dh:dh + 1]), jnp.exp(-m_t))
            hsum = hsum + tot[:, :dh] / den
        hn = _rms(hsum, nrm_ref[...])
        y_ref[0, pl.ds(start, L), :] = hn * jax.nn.sigmoid(o_ref[0, pl.ds(start, L), :])
        return carry

    lax.fori_loop(0, nc, pass3, 0)


def _mlstm(um3, gt4, conv_w, conv_b, mlstm_norm):
    bsz, s, _ = um3.shape
    H, dh, L = N_MLSTM_HEADS, MLSTM_HEAD_DIM, MLSTM_CHUNK
    nc = s // L
    col = lambda off: pl.BlockSpec((1, s, dh), lambda b, h: (b, 0, off + h))
    vec = lambda rows, off: pl.BlockSpec((rows, dh), lambda b, h: (0, off + h))
    cb = conv_b.reshape(1, 2 * D_MLSTM)
    return pl.pallas_call(
        _mlstm_kernel,
        grid=(bsz, H),
        in_specs=[col(0), col(H), col(2 * H), col(3 * H),
                  pl.BlockSpec((N_GATES, 1, nc, L), lambda b, h: (0, b, 0, 0)),
                  vec(3, 0), vec(3, H), vec(1, 0), vec(1, H), vec(1, 0)],
        out_specs=pl.BlockSpec((1, s, dh), lambda b, h: (b, 0, h)),
        out_shape=jax.ShapeDtypeStruct((bsz, s, D_MLSTM), F32),
        scratch_shapes=[pltpu.VMEM((s, dh), BF16), pltpu.VMEM((s, dh), BF16), pltpu.VMEM((s, 2 * dh), BF16),
                        pltpu.VMEM((2, nc, dh, 2 * dh), F32)]
                       + [pltpu.VMEM((2, nc, L), F32) for _ in range(6)],
        compiler_params=_params("parallel", "parallel"),
        name="mlstm",
    )(um3, um3, um3, um3, gt4, conv_w, conv_w, cb, cb, mlstm_norm.reshape(1, D_MLSTM))


def _attn_kernel(q_ref, k_ref, v_ref, o_ref):
    s = _dot_nt(q_ref[0], k_ref[0])
    m = jnp.max(s, axis=-1, keepdims=True)
    p = jnp.exp2((s - m).astype(BF16))
    acc = _dot(p, v_ref[0])
    o_ref[0] = acc[:, :V_HEAD_DIM] / acc[:, V_HEAD_DIM:V_HEAD_DIM + 1]


def _attention(q3, k3, v3):
    bsz, s, _ = q3.shape
    tq = Q_TILE
    return pl.pallas_call(
        _attn_kernel,
        grid=(bsz, N_MLA_HEADS, s // tq),
        in_specs=[pl.BlockSpec((1, tq, QK_SLAB), lambda b, h, i: (b, i, h)),
                  pl.BlockSpec((1, s, QK_SLAB), lambda b, h, i: (b, 0, h)),
                  pl.BlockSpec((1, s, QK_SLAB), lambda b, h, i: (b, 0, h))],
        out_specs=pl.BlockSpec((1, tq, V_HEAD_DIM), lambda b, h, i: (b, i, h)),
        out_shape=jax.ShapeDtypeStruct((bsz, s, D_MLA), F32),
        compiler_params=_params("parallel", "parallel", "parallel"),
        name="attention",
    )(q3, k3, v3)


def _out_route_kernel(ym_ref, ya_ref, x_ref, mn_ref, wom_ref, woa_ref, fn_ref, wr_ref, br_ref,
                      h_ref, xn_ref, gate_ref, slot_ref, slot_t_ref, seg_ref, size_ref,
                      carry_ref):
    i = pl.program_id(0)
    tm = x_ref.shape[0]

    @pl.when(i == 0)
    def _():
        carry_ref[...] = jnp.zeros_like(carry_ref)

    ya = _rms(ya_ref[...], mn_ref[...])
    h1 = x_ref[...] + _dot(ym_ref[...].astype(BF16), wom_ref[...]) + _dot(ya.astype(BF16), woa_ref[...])
    h_ref[...] = h1
    xn = _rms(h1, fn_ref[...])
    xn_ref[...] = xn.astype(BF16)
    logits = jnp.dot(xn, wr_ref[...], preferred_element_type=F32, precision=lax.Precision.HIGHEST) + br_ref[...]
    lane = lax.broadcasted_iota(I32, logits.shape, 1)
    work = jnp.where(lane < N_EXPERTS, logits, -jnp.inf)
    vals, hots = [], []
    for k in range(TOP_K):
        mx = jnp.max(work, axis=-1, keepdims=True)
        idx = jnp.min(jnp.where(work == mx, lane, LANES), axis=-1, keepdims=True)
        hot = lane == idx
        work = jnp.where(hot, -jnp.inf, work)
        vals.append(mx)
        hots.append(hot)
    exps = [jnp.exp(v - vals[0]) for v in vals]
    tot = exps[0] + exps[1] + exps[2] + exps[3]
    multi = (hots[0] | hots[1] | hots[2] | hots[3]).astype(BF16)
    ri = lax.broadcasted_iota(I32, (tm, tm), 0)
    ci = lax.broadcasted_iota(I32, (tm, tm), 1)
    before = (ci < ri).astype(BF16)
    local_rank = _dot(before, multi)
    count = jnp.sum(multi.astype(F32), axis=0, keepdims=True)
    padded = jnp.ceil(count * (1.0 / SEG_ALIGN)) * SEG_ALIGN
    local_start = _cumsum_lanes(padded, reverse=False) - padded
    slot_all = local_rank + local_start
    gate = jnp.zeros(logits.shape, F32)
    slot = jnp.zeros(logits.shape, F32)
    for k in range(TOP_K):
        gate = jnp.where(lane == k, exps[k] / tot, gate)
        sk = jnp.sum(jnp.where(hots[k], slot_all, 0.0), axis=-1, keepdims=True)
        slot = jnp.where(lane == k, sk, slot)
    gate_ref[...] = gate[:, :TOP_K]
    slot_ref[...] = slot[:, :TOP_K].astype(I32)
    slot_t_ref[0] = slot.T[:SUBLANES, :].astype(I32)
    srow = lax.broadcasted_iota(I32, (SUBLANES, LANES), 0)
    seg = jnp.where(srow == 0, padded, jnp.where(srow == 1, local_start, jnp.where(srow == 2, carry_ref[...], 0.0)))
    seg_ref[0] = seg.astype(I32)
    carry_ref[...] += padded
    size_ref[...] = carry_ref[...].astype(I32)


def _out_route(ym2, ya2, x2, mla_norm, w_out, ffn_norm, w_router, b_router):
    n, d = x2.shape
    tm = ROW_TILE
    nt = n // tm
    wom = w_out[:D_MLSTM].astype(BF16)
    woa = w_out[D_MLSTM:].astype(BF16)
    wr = jnp.pad(w_router, ((0, 0), (0, LANES - N_EXPERTS)))
    br = jnp.pad(b_router.reshape(1, N_EXPERTS), ((0, 0), (0, LANES - N_EXPERTS)))
    full = lambda arr: pl.BlockSpec(arr.shape, lambda i: (0,) * arr.ndim)
    rows = lambda w: pl.BlockSpec((tm, w), lambda i: (i, 0))
    consts = [mla_norm.reshape(1, D_MLA), wom, woa, ffn_norm.reshape(1, d), wr, br]
    return pl.pallas_call(
        _out_route_kernel,
        grid=(nt,),
        in_specs=[rows(D_MLSTM), rows(D_MLA), rows(d)] + [full(c) for c in consts],
        out_specs=[rows(d), rows(d), rows(TOP_K), rows(TOP_K),
                   pl.BlockSpec((1, SUBLANES, tm), lambda i: (i, 0, 0)),
                   pl.BlockSpec((1, SUBLANES, LANES), lambda i: (i, 0, 0)),
                   pl.BlockSpec((1, LANES), lambda i: (0, 0))],
        out_shape=[jax.ShapeDtypeStruct((n, d), F32), jax.ShapeDtypeStruct((n, d), BF16),
                   jax.ShapeDtypeStruct((n, TOP_K), F32), jax.ShapeDtypeStruct((n, TOP_K), I32),
                   jax.ShapeDtypeStruct((nt, SUBLANES, tm), I32), jax.ShapeDtypeStruct((nt, SUBLANES, LANES), I32),
                   jax.ShapeDtypeStruct((1, LANES), I32)],
        scratch_shapes=[pltpu.VMEM((1, LANES), F32)],
        compiler_params=_params("arbitrary"),
        name="out_route",
    )(ym2, ya2, x2, *consts)


def _segment_copies(seg_ref, starts_ref, local_ref, global_ref, sem, to_global, wait):
    def per_expert(e, carry):
        size = seg_ref[0, 0, e]
        src = seg_ref[0, 1, e]
        dst = starts_ref[e] + seg_ref[0, 2, e]
        chunk = ROW_TILE
        while chunk >= SEG_ALIGN:
            @pl.when((size & chunk) != 0)
            def _(src=src, dst=dst, chunk=chunk):
                loc = local_ref.at[pl.ds(pl.multiple_of(src, SEG_ALIGN), chunk)]
                glo = global_ref.at[pl.ds(pl.multiple_of(dst, SEG_ALIGN), chunk)]
                cp = pltpu.make_async_copy(loc, glo, sem) if to_global else pltpu.make_async_copy(glo, loc, sem)
                if wait:
                    cp.wait()
                else:
                    cp.start()

            step = size & chunk
            src = src + step
            dst = dst + step
            chunk //= 2
        return carry

    lax.fori_loop(0, N_EXPERTS, per_expert, 0)


def _dispatch_kernel(starts_ref, seg_ref, segp_ref, slot_t_ref, xn_ref, zero_ref, xs_ref, sort_ref, sem, zsem):
    i = pl.program_id(0)
    tm = xn_ref.shape[0]
    tmx = zero_ref.shape[0]
    cap = sort_ref.shape[1]

    @pl.when(i == 0)
    def _():
        def clear_tile(row):
            cp = pltpu.make_async_copy(zero_ref, xs_ref.at[pl.ds(pl.multiple_of(row, tmx), tmx)], zsem)
            cp.start()
            cp.wait()

        def clear_group(e, carry):
            hi = starts_ref[e + 1]

            @pl.when(hi > starts_ref[e])
            def _():
                clear_tile(hi - tmx)

            return carry

        def clear_tail(t, carry):
            clear_tile(starts_ref[N_EXPERTS] + t * tmx)
            return carry

        lax.fori_loop(0, N_EXPERTS, clear_group, 0)
        lax.fori_loop(0, (xs_ref.shape[0] - starts_ref[N_EXPERTS]) // tmx, clear_tail, 0)

    slot = i % 2
    pos = lax.broadcasted_iota(I32, (cap, tm), 0)
    hit = pos == slot_t_ref[0, 0:1, :]
    for k in range(1, TOP_K):
        hit = hit | (pos == slot_t_ref[0, k:k + 1, :])
    sort_ref[slot] = _dot(hit.astype(BF16), xn_ref[...]).astype(BF16)
    _segment_copies(seg_ref, starts_ref, sort_ref.at[slot], xs_ref, sem.at[slot], True, False)

    @pl.when(i >= 1)
    def _():
        _segment_copies(segp_ref, starts_ref, sort_ref.at[1 - slot], xs_ref, sem.at[1 - slot], True, True)

    @pl.when(i == pl.num_programs(0) - 1)
    def _():
        _segment_copies(seg_ref, starts_ref, sort_ref.at[slot], xs_ref, sem.at[slot], True, True)


def _dispatch(starts, seg, slot_t, xn, n_rows):
    n, d = xn.shape
    tm = ROW_TILE
    nt = n // tm
    smem = lambda f: pl.BlockSpec((1, SUBLANES, LANES), f, memory_space=pltpu.SMEM)
    any_spec = pl.BlockSpec(memory_space=pl.ANY)
    zero_tile = jnp.zeros((EXPERT_TILE, d), xn.dtype)
    return pl.pallas_call(
        _dispatch_kernel,
        grid_spec=pltpu.PrefetchScalarGridSpec(
            num_scalar_prefetch=1,
            grid=(nt,),
            in_specs=[smem(lambda i, *_: (i, 0, 0)), smem(lambda i, *_: (jnp.maximum(i - 1, 0), 0, 0)),
                      pl.BlockSpec((1, SUBLANES, tm), lambda i, *_: (i, 0, 0)),
                      pl.BlockSpec((tm, d), lambda i, *_: (i, 0)), any_spec],
            out_specs=any_spec,
            scratch_shapes=[pltpu.VMEM((2, SORT_ROWS, d), xn.dtype), pltpu.SemaphoreType.DMA((2,)),
                            pltpu.SemaphoreType.DMA],
        ),
        out_shape=jax.ShapeDtypeStruct((n_rows, d), xn.dtype),
        compiler_params=_params("arbitrary"),
        name="dispatch",
    )(starts, seg, seg, slot_t, xn, zero_tile)


def _experts_kernel(te_ref, tb_ref, tv_ref, xs_ref, wgu_ref, bgu_ref, wd_ref, bd_ref, out_ref,
                    wgu_bf, wd_bf):
    i = pl.program_id(0)
    de = wd_ref.shape[1]
    prev = te_ref[jnp.maximum(i - 1, 0)]

    @pl.when(jnp.logical_or(i == 0, te_ref[i] != prev))
    def _():
        wgu_bf[...] = wgu_ref[0].astype(BF16)
        wd_bf[...] = wd_ref[0].astype(BF16)

    @pl.when(tv_ref[i] == 1)
    def _():
        x = xs_ref[...]
        ch = 512
        acc = jnp.zeros(out_ref.shape, F32)
        for j in range(de // ch):
            g = _dot(x, wgu_bf[:, j * ch:(j + 1) * ch]) + bgu_ref[0, :, j * ch:(j + 1) * ch]
            u = _dot(x, wgu_bf[:, de + j * ch:de + (j + 1) * ch]) + bgu_ref[0, :, de + j * ch:de + (j + 1) * ch]
            g = jnp.minimum(g, SWIGLU_LIMIT)
            u = jnp.clip(u, -SWIGLU_LIMIT, SWIGLU_LIMIT)
            hm = (u + 1.0) * (g * jax.nn.sigmoid(g * SWIGLU_ALPHA))
            acc = acc + _dot(hm.astype(BF16), wd_bf[j * ch:(j + 1) * ch, :])
        out_ref[...] = (acc + bd_ref[0]).astype(out_ref.dtype)

    @pl.when(tv_ref[i] == 0)
    def _():
        out_ref[...] = jnp.zeros_like(out_ref)


def _experts(tile_e, tile_b, tile_v, xs, w_gate_up, b_gate_up, w_down, b_down):
    n_rows, d = xs.shape
    tmx = EXPERT_TILE
    ne, _, de2 = w_gate_up.shape
    de = de2 // 2
    return pl.pallas_call(
        _experts_kernel,
        grid_spec=pltpu.PrefetchScalarGridSpec(
            num_scalar_prefetch=3,
            grid=(n_rows // tmx,),
            in_specs=[pl.BlockSpec((tmx, d), lambda i, te, tb, tv: (tb[i], 0)),
                      pl.BlockSpec((1, d, de2), lambda i, te, tb, tv: (te[i], 0, 0)),
                      pl.BlockSpec((1, 1, de2), lambda i, te, tb, tv: (te[i], 0, 0)),
                      pl.BlockSpec((1, de, d), lambda i, te, tb, tv: (te[i], 0, 0)),
                      pl.BlockSpec((1, 1, d), lambda i, te, tb, tv: (te[i], 0, 0))],
            out_specs=pl.BlockSpec((tmx, d), lambda i, te, tb, tv: (i, 0)),
            scratch_shapes=[pltpu.VMEM((d, de2), BF16), pltpu.VMEM((de, d), BF16)],
        ),
        out_shape=jax.ShapeDtypeStruct((n_rows, d), BF16),
        compiler_params=_params("arbitrary"),
        name="experts",
    )(tile_e, tile_b, tile_v, xs, w_gate_up, b_gate_up.reshape(ne, 1, de2), w_down, b_down.reshape(ne, 1, d))


def _combine_kernel(starts_ref, seg_ref, segn_ref, h_ref, gate_ref, slot_ref, p_ref, ys_ref, pn_ref, wg_ref,
                    wp_ref, fn_ref, out_ref, ybuf, sem, *, final):
    i = pl.program_id(0)
    nt = pl.num_programs(0)
    tm = h_ref.shape[0]
    cap = ybuf.shape[1]

    @pl.when(i == 0)
    def _():
        ybuf[...] = jnp.zeros_like(ybuf)
        _segment_copies(seg_ref, starts_ref, ybuf.at[0], ys_ref, sem.at[0], False, False)

    slot = i % 2

    @pl.when(i + 1 < nt)
    def _():
        _segment_copies(segn_ref, starts_ref, ybuf.at[1 - slot], ys_ref, sem.at[1 - slot], False, False)

    _segment_copies(seg_ref, starts_ref, ybuf.at[slot], ys_ref, sem.at[slot], False, True)

    pos = lax.broadcasted_iota(I32, (tm, cap), 1)
    gate = gate_ref[...]
    weights = jnp.zeros((tm, cap), F32)
    for k in range(TOP_K):
        weights = jnp.where(pos == slot_ref[:, k:k + 1], gate[:, k:k + 1], weights)
    h2 = h_ref[...] + _dot(weights.astype(BF16), ybuf[slot])
    hn = _rms(h2, pn_ref[...]).astype(BF16)
    sg = jax.nn.sigmoid(_dot(hn, wg_ref[...]))
    h3 = h2 + sg * _dot(p_ref[...].astype(BF16), wp_ref[...])
    out_ref[...] = _rms(h3, fn_ref[...]) if final else h3


def _combine(starts, seg, h1, gate, slot, p2, ys, ple_norm, w_ple_gate, w_ple_proj, final_norm, final):
    n, d = h1.shape
    tm = ROW_TILE
    nt = n // tm
    wg = w_ple_gate.astype(BF16)
    wp = w_ple_proj.astype(BF16)
    smem = lambda f: pl.BlockSpec((1, SUBLANES, LANES), f, memory_space=pltpu.SMEM)
    full = lambda arr: pl.BlockSpec(arr.shape, lambda i, *_: (0,) * arr.ndim)
    rows = lambda w: pl.BlockSpec((tm, w), lambda i, *_: (i, 0))
    consts = [ple_norm.reshape(1, d), wg, wp, final_norm.reshape(1, d)]
    return pl.pallas_call(
        functools.partial(_combine_kernel, final=final),
        grid_spec=pltpu.PrefetchScalarGridSpec(
            num_scalar_prefetch=1,
            grid=(nt,),
            in_specs=[smem(lambda i, *_: (i, 0, 0)), smem(lambda i, *_: (jnp.minimum(i + 1, nt - 1), 0, 0)),
                      rows(d), rows(TOP_K), rows(TOP_K), rows(p2.shape[1]), pl.BlockSpec(memory_space=pl.ANY)]
                     + [full(c) for c in consts],
            out_specs=rows(d),
            scratch_shapes=[pltpu.VMEM((2, SORT_ROWS, d), ys.dtype), pltpu.SemaphoreType.DMA((2,))],
        ),
        out_shape=jax.ShapeDtypeStruct((n, d), F32),
        compiler_params=_params("arbitrary"),
        name="combine",
    )(starts, seg, seg, h1, gate, slot, p2, ys, *consts)


def _route_tables(sizes, n_tiles):
    tmx = EXPERT_TILE
    tile_end = jnp.cumsum((sizes + tmx - 1) // tmx)
    starts = jnp.concatenate([jnp.zeros((1,), I32), tile_end * tmx]).astype(I32)
    n_valid = tile_end[-1]
    t = jnp.arange(n_tiles, dtype=I32)
    tb = jnp.minimum(t, n_valid - 1).astype(I32)
    te = jnp.sum(tile_end[None, :] <= tb[:, None], axis=1).astype(I32)
    tv = (t < n_valid).astype(I32)
    return starts, te, tb, tv


def kernel(x, p, positions, attn_norm, w_in, b_gates, conv_w, conv_b, mlstm_norm, q_norm, w_q_up, kv_norm, w_kv_up, mla_norm, w_out, ffn_norm, w_router, b_router, w_gate_up, b_gate_up, w_down, b_down, ple_norm, w_ple_gate, w_ple_proj, final_norm):
    bsz, s, d = x.shape
    n = bsz * s
    depth = p.shape[0]
    nc = s // MLSTM_CHUNK
    max_rows = n * TOP_K + (n // ROW_TILE) * N_EXPERTS * (SEG_ALIGN - 1) + N_EXPERTS * (EXPERT_TILE - 1)
    n_tiles = max_rows // EXPERT_TILE
    pos2 = positions.reshape(n, 1)
    h = x.reshape(n, d)
    for i in range(depth):
        um, gt, q, k, v = _in_proj(h, pos2, attn_norm[i], w_in[i], b_gates[i], q_norm[i], w_q_up[i],
                                   kv_norm[i], w_kv_up[i])
        ym = _mlstm(um.reshape(bsz, s, -1), gt.reshape(N_GATES, bsz, nc, MLSTM_CHUNK), conv_w[i], conv_b[i],
                    mlstm_norm[i])
        ya = _attention(q.reshape(bsz, s, -1), k.reshape(bsz, s, -1), v.reshape(bsz, s, -1))
        h1, xn, gate, slot, slot_t, seg, sizes = _out_route(ym.reshape(n, -1), ya.reshape(n, -1), h, mla_norm[i],
                                                            w_out[i], ffn_norm[i], w_router[i], b_router[i])
        starts, te, tb, tv = _route_tables(sizes[0, :N_EXPERTS], n_tiles)
        xs = _dispatch(starts, seg, slot_t, xn, n_tiles * EXPERT_TILE)
        ys = _experts(te, tb, tv, xs, w_gate_up[i], b_gate_up[i], w_down[i], b_down[i])
        h = _combine(starts, seg, h1, gate, slot, p[i].reshape(n, -1), ys, ple_norm[i], w_ple_gate[i],
                     w_ple_proj[i], final_norm, final=(i == depth - 1))
    return h.reshape(bsz, s, d)
```

```python
import functools

import jax
import jax.numpy as jnp
from jax import lax
from jax.experimental import pallas as pl
from jax.experimental.pallas import tpu as pltpu

F32 = jnp.float32
BF16 = jnp.bfloat16
I32 = jnp.int32

N_MLSTM_HEADS = 4
MLSTM_HEAD_DIM = 128
D_MLSTM = N_MLSTM_HEADS * MLSTM_HEAD_DIM
MLSTM_CHUNK = 128
N_MLA_HEADS = 4
QK_NOPE_DIM = 128
QK_ROPE_DIM = 64
V_HEAD_DIM = 128
D_MLA = N_MLA_HEADS * V_HEAD_DIM
Q_LORA = 256
KV_LORA = 128
ROPE_THETA = 10000.0
N_EXPERTS = 32
TOP_K = 4
SWIGLU_LIMIT = 7.0
SWIGLU_ALPHA = 1.702
EPS = 1e-6
N_GATES = 4 * N_MLSTM_HEADS
OFF_G = 4 * D_MLSTM
OFF_CQ = OFF_G + N_GATES
OFF_CKV = OFF_CQ + Q_LORA
OFF_KR = OFF_CKV + KV_LORA

LANES = 128
SUBLANES = 8
BF16_SUBLANES = 16
QK_SLAB = 2 * LANES
VMEM_LIMIT_BYTES = 56 * 1024 * 1024
LOG2_E = 1.4426950408889634

ROW_TILE = 512
Q_TILE = 256
EXPERT_TILE = 512
CHUNK_UNROLL = 2
SEG_ALIGN = BF16_SUBLANES
SORT_ROWS = ROW_TILE * TOP_K + N_EXPERTS * SEG_ALIGN


def _dot(a, b):
    return jnp.dot(a, b, preferred_element_type=F32)


def _dot_nt(a, b):
    return lax.dot_general(a, b, (((1,), (1,)), ((), ())), preferred_element_type=F32)


def _rms(x, g):
    return x * lax.rsqrt(jnp.mean(x * x, axis=-1, keepdims=True) + EPS) * g


def _log_sigmoid(x):
    return jnp.minimum(x, 0.0) - jnp.log(1.0 + jnp.exp(-jnp.abs(x)))


def _cumsum_lanes(x, reverse):
    n = x.shape[-1]
    lane = lax.broadcasted_iota(I32, x.shape, x.ndim - 1)
    sh = 1
    while sh < n:
        if reverse:
            x = x + jnp.where(lane < n - sh, pltpu.roll(x, n - sh, x.ndim - 1), 0.0)
        else:
            x = x + jnp.where(lane >= sh, pltpu.roll(x, sh, x.ndim - 1), 0.0)
        sh *= 2
    return x


def _params(*sem):
    return pltpu.CompilerParams(dimension_semantics=sem, vmem_limit_bytes=VMEM_LIMIT_BYTES)


def _in_proj_kernel(x_ref, pos_ref, an_ref, wm_ref, wr_ref, wgt_ref, bgt_ref, qn_ref, wq_ref, kvn_ref,
                    wk_ref, wv_ref, freq_ref, sgn_ref,
                    um_ref, gt_ref, q_ref, k_ref, v_ref):
    a = _rms(x_ref[...], an_ref[...]).astype(BF16)
    um_ref[...] = _dot(a, wm_ref[...])
    gt_ref[...] = _dot_nt(wgt_ref[...], a) + bgt_ref[...]
    rest = _dot(a, wr_ref[...])
    cq = rest[:, :Q_LORA]
    ckv = rest[:, Q_LORA:Q_LORA + KV_LORA]
    kr2 = rest[:, Q_LORA + KV_LORA:Q_LORA + KV_LORA + LANES]
    krs2 = rest[:, Q_LORA + KV_LORA + LANES:]
    ang = pos_ref[...].astype(F32) * freq_ref[...]
    cos_a = jnp.cos(ang)
    sin_a = jnp.sin(ang) * sgn_ref[...]
    scale = (QK_NOPE_DIM + QK_ROPE_DIM) ** -0.5 * LOG2_E
    lane = lax.broadcasted_iota(I32, ang.shape, 1)
    rope_mul = jnp.where(lane < QK_ROPE_DIM, cos_a, sin_a) * scale
    qf = _dot(_rms(cq, qn_ref[...]).astype(BF16), wq_ref[...])
    ckvn = _rms(ckv, kvn_ref[...]).astype(BF16)
    kn = _dot(ckvn, wk_ref[...])
    v_ref[...] = _dot(ckvn, wv_ref[...]).astype(BF16)
    k_rope = (kr2 * cos_a + krs2 * sin_a).astype(BF16)
    for h in range(N_MLA_HEADS):
        o = h * QK_SLAB
        q_ref[:, o:o + LANES] = (qf[:, o:o + LANES] * scale).astype(BF16)
        q_ref[:, o + LANES:o + QK_SLAB] = (qf[:, o + LANES:o + QK_SLAB] * rope_mul).astype(BF16)
        k_ref[:, o:o + LANES] = kn[:, h * LANES:(h + 1) * LANES].astype(BF16)
        k_ref[:, o + LANES:o + QK_SLAB] = k_rope


def _in_proj(x2, pos2, attn_norm, w_in, b_gates, q_norm, w_q_up, kv_norm, w_kv_up):
    n, d = x2.shape
    tm = ROW_TILE
    half = QK_ROPE_DIM // 2
    swap = jnp.concatenate([jnp.arange(half, QK_ROPE_DIM), jnp.arange(0, half)])
    w_kr = w_in[:, OFF_KR:OFF_KR + QK_ROPE_DIM]
    w_krs = w_kr[:, swap]
    wm = w_in[:, :OFF_G].astype(BF16)
    wr = jnp.concatenate([w_in[:, OFF_CQ:OFF_KR], w_kr, w_kr, w_krs, w_krs], axis=1).astype(BF16)
    wgt = w_in[:, OFF_G:OFF_CQ].T.astype(BF16)
    bgt = b_gates.reshape(N_GATES, 1)
    wq4 = w_q_up.reshape(Q_LORA, N_MLA_HEADS, QK_NOPE_DIM + QK_ROPE_DIM)
    wq_pe = wq4[:, :, QK_NOPE_DIM:]
    wq = jnp.concatenate([wq4, wq_pe[:, :, swap]], axis=2).reshape(Q_LORA, N_MLA_HEADS * QK_SLAB).astype(BF16)
    wkv4 = w_kv_up.reshape(KV_LORA, N_MLA_HEADS, QK_NOPE_DIM + V_HEAD_DIM)
    wk = wkv4[:, :, :QK_NOPE_DIM].reshape(KV_LORA, N_MLA_HEADS * QK_NOPE_DIM).astype(BF16)
    wv = wkv4[:, :, QK_NOPE_DIM:].reshape(KV_LORA, D_MLA).astype(BF16)
    freqs = ROPE_THETA ** (-jnp.arange(0, QK_ROPE_DIM, 2, dtype=F32) / QK_ROPE_DIM)
    freq_l = jnp.tile(freqs, LANES // half).reshape(1, LANES)
    sgn_l = jnp.tile(jnp.concatenate([-jnp.ones((half,), F32), jnp.ones((half,), F32)]),
                     LANES // QK_ROPE_DIM).reshape(1, LANES)
    full = lambda arr: pl.BlockSpec(arr.shape, lambda i: (0,) * arr.ndim)
    rows = lambda w: pl.BlockSpec((tm, w), lambda i: (i, 0))
    consts = [attn_norm.reshape(1, d), wm, wr, wgt, bgt, q_norm.reshape(1, Q_LORA), wq,
              kv_norm.reshape(1, KV_LORA), wk, wv, freq_l, sgn_l]
    return pl.pallas_call(
        _in_proj_kernel,
        grid=(n // tm,),
        in_specs=[rows(d), rows(1)] + [full(c) for c in consts],
        out_specs=[rows(OFF_G), pl.BlockSpec((N_GATES, tm), lambda i: (0, i)),
                   rows(N_MLA_HEADS * QK_SLAB), rows(N_MLA_HEADS * QK_SLAB), rows(D_MLA)],
        out_shape=[jax.ShapeDtypeStruct((n, OFF_G), F32), jax.ShapeDtypeStruct((N_GATES, n), F32),
                   jax.ShapeDtypeStruct((n, N_MLA_HEADS * QK_SLAB), BF16),
                   jax.ShapeDtypeStruct((n, N_MLA_HEADS * QK_SLAB), BF16),
                   jax.ShapeDtypeStruct((n, D_MLA), BF16)],
        compiler_params=_params("parallel"),
        name="in_proj",
    )(x2, pos2, *consts)


def _mlstm_kernel(q_ref, k_ref, v_ref, o_ref, g_ref, cwq_ref, cwk_ref, cbq_ref, cbk_ref, nrm_ref,
                  y_ref,
                  qc_ref, kc_ref, va_ref, cst_ref, b_ref, r_ref, ew_ref, mw_ref, bt_ref, mp_ref):
    L = MLSTM_CHUNK
    dh = MLSTM_HEAD_DIM
    nc = q_ref.shape[1] // L
    s_len = q_ref.shape[1]
    h = pl.program_id(1)

    for d in range(2):
        ig = g_ref[2 * d * N_MLSTM_HEADS + h, 0]
        fg = g_ref[(2 * d + 1) * N_MLSTM_HEADS + h, 0]
        b = _cumsum_lanes(_log_sigmoid(fg), reverse=(d == 1))
        btot = b[:, L - 1:L] if d == 0 else b[:, 0:1]
        r = ig - b
        w = btot + r
        mw = jnp.max(w, axis=-1, keepdims=True)
        b_ref[d] = b
        r_ref[d] = r
        ew_ref[d] = jnp.exp(w - mw)
        mw_ref[d] = jnp.broadcast_to(mw, (nc, L))
        bt_ref[d] = jnp.broadcast_to(btot, (nc, L))

    row = lax.broadcasted_iota(I32, (L, dh), 0)

    def conv_silu(ref, cw_ref, cb_ref, c):
        start = pl.multiple_of(c * L, L)
        x = ref[0, pl.ds(start, L), :]
        prev_row = jnp.where(c > 0, ref[0, pl.ds(jnp.maximum(start - 1, 0), 1), :], 0.0)
        next_row = jnp.where(c < nc - 1, ref[0, pl.ds(jnp.minimum(start + L, s_len - 1), 1), :], 0.0)
        x_prev = jnp.where(row == 0, prev_row, pltpu.roll(x, 1, 0))
        x_next = jnp.where(row == L - 1, next_row, pltpu.roll(x, L - 1, 0))
        y = cw_ref[0:1, :] * x_prev + cw_ref[1:2, :] * x + cw_ref[2:3, :] * x_next + cb_ref[...]
        return y * jax.nn.sigmoid(y)

    ones_col = (lax.broadcasted_iota(I32, (L, dh), 1) == 0).astype(BF16)

    def pass1(c, carry):
        start = pl.multiple_of(c * L, L)
        qc_ref[pl.ds(start, L), :] = conv_silu(q_ref, cwq_ref, cbq_ref, c).astype(BF16)
        kk = conv_silu(k_ref, cwk_ref, cbk_ref, c) * (dh ** -0.5)
        kc_ref[pl.ds(start, L), :] = kk.astype(BF16)
        va = jnp.concatenate([v_ref[0, pl.ds(start, L), :].astype(BF16), ones_col], axis=1)
        va_ref[pl.ds(start, L), :] = va
        kt = kk.T
        for d in range(2):
            kw_t = (kt * ew_ref[d, pl.ds(c, 1), :]).astype(BF16)
            cst_ref[d, c] = _dot(kw_t, va)
        return carry

    lax.fori_loop(0, nc, pass1, 0, unroll=CHUNK_UNROLL)

    for d in range(2):
        def scan(i, carry):
            st, m = carry
            c = i if d == 0 else nc - 1 - i
            mw = mw_ref[d, pl.ds(c, 1), :]
            bt = bt_ref[d, pl.ds(c, 1), :]
            m_new = jnp.maximum(bt + m, mw)
            a = jnp.exp(bt + m - m_new)[:, 0:1]
            cc = jnp.exp(mw - m_new)[:, 0:1]
            loc = cst_ref[d, c]
            cst_ref[d, c] = st
            mp_ref[d, pl.ds(c, 1), :] = m
            return a * st + cc * loc, m_new

        lax.fori_loop(0, nc, scan, (jnp.zeros((dh, 2 * dh), F32), jnp.zeros((1, L), F32)))

    ti = lax.broadcasted_iota(I32, (L, L), 0)
    si = lax.broadcasted_iota(I32, (L, L), 1)
    masks = (si <= ti, si >= ti)

    def pass3(c, carry):
        start = pl.multiple_of(c * L, L)
        q = qc_ref[pl.ds(start, L), :]
        k = kc_ref[pl.ds(start, L), :]
        va = va_ref[pl.ds(start, L), :]
        qk = _dot_nt(q, k)
        hsum = jnp.zeros((L, dh), F32)
        for d in range(2):
            bmat = jnp.broadcast_to(b_ref[d, pl.ds(c, 1), :], (L, L)).T
            dmat = jnp.where(masks[d], bmat + r_ref[d, pl.ds(c, 1), :], -jnp.inf)
            inter = bmat[:, 0:1] + mp_ref[d, pl.ds(c, 1), :][:, 0:1]
            m_t = jnp.maximum(inter, jnp.max(dmat, axis=-1, keepdims=True))
            sc = qk * jnp.exp(dmat - m_t)
            a = jnp.exp(inter - m_t)
            tot = _dot(sc.astype(BF16), va) + a * _dot(q, cst_ref[d, c].astype(BF16))
            den = jnp.maximum(jnp.abs(tot[:, dh:dh + 1]), jnp.exp(-m_t))
            hsum = hsum + tot[:, :dh] / den
        hn = _rms(hsum, nrm_ref[...])
        y_ref[0, pl.ds(start, L), :] = hn * jax.nn.sigmoid(o_ref[0, pl.ds(start, L), :])
        return carry

    lax.fori_loop(0, nc, pass3, 0, unroll=CHUNK_UNROLL)


def _mlstm(um3, gt4, conv_w, conv_b, mlstm_norm):
    bsz, s, _ = um3.shape
    H, dh, L = N_MLSTM_HEADS, MLSTM_HEAD_DIM, MLSTM_CHUNK
    nc = s // L
    col = lambda off: pl.BlockSpec((1, s, dh), lambda b, h: (b, 0, off + h))
    vec = lambda rows, off: pl.BlockSpec((rows, dh), lambda b, h: (0, off + h))
    cb = conv_b.reshape(1, 2 * D_MLSTM)
    return pl.pallas_call(
        _mlstm_kernel,
        grid=(bsz, H),
        in_specs=[col(0), col(H), col(2 * H), col(3 * H),
                  pl.BlockSpec((N_GATES, 1, nc, L), lambda b, h: (0, b, 0, 0)),
                  vec(3, 0), vec(3, H), vec(1, 0), vec(1, H), vec(1, 0)],
        out_specs=pl.BlockSpec((1, s, dh), lambda b, h: (b, 0, h)),
        out_shape=jax.ShapeDtypeStruct((bsz, s, D_MLSTM), F32),
        scratch_shapes=[pltpu.VMEM((s, dh), BF16), pltpu.VMEM((s, dh), BF16), pltpu.VMEM((s, 2 * dh), BF16),
                        pltpu.VMEM((2, nc, dh, 2 * dh), F32)]
                       + [pltpu.VMEM((2, nc, L), F32) for _ in range(6)],
        compiler_params=_params("parallel", "parallel"),
        name="mlstm",
    )(um3, um3, um3, um3, gt4, conv_w, conv_w, cb, cb, mlstm_norm.reshape(1, D_MLSTM))


def _attn_kernel(q_ref, k_ref, v_ref, o_ref):
    s = _dot_nt(q_ref[0], k_ref[0])
    m = jnp.max(s, axis=-1, keepdims=True)
    p = jnp.exp2(s - m)
    l = jnp.sum(p, axis=-1, keepdims=True)
    o_ref[0] = _dot(p.astype(BF16), v_ref[0]) / l


def _attention(q3, k3, v3):
    bsz, s, _ = q3.shape
    tq = Q_TILE
    return pl.pallas_call(
        _attn_kernel,
        grid=(bsz, N_MLA_HEADS, s // tq),
        in_specs=[pl.BlockSpec((1, tq, QK_SLAB), lambda b, h, i: (b, i, h)),
                  pl.BlockSpec((1, s, QK_SLAB), lambda b, h, i: (b, 0, h)),
                  pl.BlockSpec((1, s, V_HEAD_DIM), lambda b, h, i: (b, 0, h))],
        out_specs=pl.BlockSpec((1, tq, V_HEAD_DIM), lambda b, h, i: (b, i, h)),
        out_shape=jax.ShapeDtypeStruct((bsz, s, D_MLA), F32),
        compiler_params=_params("parallel", "parallel", "parallel"),
        name="attention",
    )(q3, k3, v3)


def _out_route_kernel(ym_ref, ya_ref, x_ref, mn_ref, wom_ref, woa_ref, fn_ref, wr_ref, br_ref,
                      h_ref, xn_ref, gate_ref, slot_ref, slot_t_ref, seg_ref, size_ref,
                      carry_ref):
    i = pl.program_id(0)
    tm = x_ref.shape[0]

    @pl.when(i == 0)
    def _():
        carry_ref[...] = jnp.zeros_like(carry_ref)

    ya = _rms(ya_ref[...], mn_ref[...])
    h1 = x_ref[...] + _dot(ym_ref[...].astype(BF16), wom_ref[...]) + _dot(ya.astype(BF16), woa_ref[...])
    h_ref[...] = h1
    xn = _rms(h1, fn_ref[...])
    xn_ref[...] = xn.astype(BF16)
    logits = jnp.dot(xn, wr_ref[...], preferred_element_type=F32, precision=lax.Precision.HIGHEST) + br_ref[...]
    lane = lax.broadcasted_iota(I32, logits.shape, 1)
    work = jnp.where(lane < N_EXPERTS, logits, -jnp.inf)
    vals, hots = [], []
    for k in range(TOP_K):
        mx = jnp.max(work, axis=-1, keepdims=True)
        idx = jnp.min(jnp.where(work == mx, lane, LANES), axis=-1, keepdims=True)
        hot = lane == idx
        work = jnp.where(hot, -jnp.inf, work)
        vals.append(mx)
        hots.append(hot)
    exps = [jnp.exp(v - vals[0]) for v in vals]
    tot = exps[0] + exps[1] + exps[2] + exps[3]
    multi = (hots[0] | hots[1] | hots[2] | hots[3]).astype(BF16)
    ri = lax.broadcasted_iota(I32, (tm, tm), 0)
    ci = lax.broadcasted_iota(I32, (tm, tm), 1)
    before = (ci < ri).astype(BF16)
    local_rank = _dot(before, multi)
    count = jnp.sum(multi.astype(F32), axis=0, keepdims=True)
    padded = jnp.ceil(count * (1.0 / SEG_ALIGN)) * SEG_ALIGN
    local_start = _cumsum_lanes(padded, reverse=False) - padded
    slot_all = local_rank + local_start
    gate = jnp.zeros(logits.shape, F32)
    slot = jnp.zeros(logits.shape, F32)
    for k in range(TOP_K):
        gate = jnp.where(lane == k, exps[k] / tot, gate)
        sk = jnp.sum(jnp.where(hots[k], slot_all, 0.0), axis=-1, keepdims=True)
        slot = jnp.where(lane == k, sk, slot)
    gate_ref[...] = gate[:, :TOP_K]
    slot_ref[...] = slot[:, :TOP_K].astype(I32)
    slot_t_ref[0] = slot.T[:SUBLANES, :].astype(I32)
    srow = lax.broadcasted_iota(I32, (SUBLANES, LANES), 0)
    seg = jnp.where(srow == 0, padded, jnp.where(srow == 1, local_start, jnp.where(srow == 2, carry_ref[...], 0.0)))
    seg_ref[0] = seg.astype(I32)
    carry_ref[...] += padded
    size_ref[...] = carry_ref[...].astype(I32)


def _out_route(ym2, ya2, x2, mla_norm, w_out, ffn_norm, w_router, b_router):
    n, d = x2.shape
    tm = ROW_TILE
    nt = n // tm
    wom = w_out[:D_MLSTM].astype(BF16)
    woa = w_out[D_MLSTM:].astype(BF16)
    wr = jnp.pad(w_router, ((0, 0), (0, LANES - N_EXPERTS)))
    br = jnp.pad(b_router.reshape(1, N_EXPERTS), ((0, 0), (0, LANES - N_EXPERTS)))
    full = lambda arr: pl.BlockSpec(arr.shape, lambda i: (0,) * arr.ndim)
    rows = lambda w: pl.BlockSpec((tm, w), lambda i: (i, 0))
    consts = [mla_norm.reshape(1, D_MLA), wom, woa, ffn_norm.reshape(1, d), wr, br]
    return pl.pallas_call(
        _out_route_kernel,
        grid=(nt,),
        in_specs=[rows(D_MLSTM), rows(D_MLA), rows(d)] + [full(c) for c in consts],
        out_specs=[rows(d), rows(d), rows(TOP_K), rows(TOP_K),
                   pl.BlockSpec((1, SUBLANES, tm), lambda i: (i, 0, 0)),
                   pl.BlockSpec((1, SUBLANES, LANES), lambda i: (i, 0, 0)),
                   pl.BlockSpec((1, LANES), lambda i: (0, 0))],
        out_shape=[jax.ShapeDtypeStruct((n, d), F32), jax.ShapeDtypeStruct((n, d), BF16),
                   jax.ShapeDtypeStruct((n, TOP_K), F32), jax.ShapeDtypeStruct((n, TOP_K), I32),
                   jax.ShapeDtypeStruct((nt, SUBLANES, tm), I32), jax.ShapeDtypeStruct((nt, SUBLANES, LANES), I32),
                   jax.ShapeDtypeStruct((1, LANES), I32)],
        scratch_shapes=[pltpu.VMEM((1, LANES), F32)],
        compiler_params=_params("arbitrary"),
        name="out_route",
    )(ym2, ya2, x2, *consts)


def _segment_copies(seg_ref, starts_ref, local_ref, global_ref, sem, to_global, wait):
    def per_expert(e, carry):
        size = seg_ref[0, 0, e]
        src = seg_ref[0, 1, e]
        dst = starts_ref[e] + seg_ref[0, 2, e]
        chunk = ROW_TILE
        while chunk >= SEG_ALIGN:
            @pl.when((size & chunk) != 0)
            def _(src=src, dst=dst, chunk=chunk):
                loc = local_ref.at[pl.ds(pl.multiple_of(src, SEG_ALIGN), chunk)]
                glo = global_ref.at[pl.ds(pl.multiple_of(dst, SEG_ALIGN), chunk)]
                cp = pltpu.make_async_copy(loc, glo, sem) if to_global else pltpu.make_async_copy(glo, loc, sem)
                if wait:
                    cp.wait()
                else:
                    cp.start()

            step = size & chunk
            src = src + step
            dst = dst + step
            chunk //= 2
        return carry

    lax.fori_loop(0, N_EXPERTS, per_expert, 0)


def _dispatch_kernel(starts_ref, seg_ref, segp_ref, slot_t_ref, xn_ref, xs_ref, sort_ref, zero_ref, sem, zsem):
    i = pl.program_id(0)
    tm = xn_ref.shape[0]
    tmx = zero_ref.shape[0]
    cap = sort_ref.shape[1]

    @pl.when(i == 0)
    def _():
        zero_ref[...] = jnp.zeros_like(zero_ref)
        n_tail = (xs_ref.shape[0] - starts_ref[N_EXPERTS]) // tmx

        def clear_tile(row, wait):
            cp = pltpu.make_async_copy(zero_ref, xs_ref.at[pl.ds(pl.multiple_of(row, tmx), tmx)], zsem)
            if wait:
                cp.wait()
            else:
                cp.start()

        for wait in (False, True):
            def clear_group(e, carry, wait=wait):
                hi = starts_ref[e + 1]

                @pl.when(hi > starts_ref[e])
                def _():
                    clear_tile(hi - tmx, wait)

                return carry

            def clear_tail(t, carry, wait=wait):
                clear_tile(starts_ref[N_EXPERTS] + t * tmx, wait)
                return carry

            lax.fori_loop(0, N_EXPERTS, clear_group, 0)
            lax.fori_loop(0, n_tail, clear_tail, 0)

    slot = i % 2
    pos = lax.broadcasted_iota(I32, (cap, tm), 0)
    hit = pos == slot_t_ref[0, 0:1, :]
    for k in range(1, TOP_K):
        hit = hit | (pos == slot_t_ref[0, k:k + 1, :])
    sort_ref[slot] = _dot(hit.astype(BF16), xn_ref[...]).astype(BF16)
    _segment_copies(seg_ref, starts_ref, sort_ref.at[slot], xs_ref, sem.at[slot], True, False)

    @pl.when(i >= 1)
    def _():
        _segment_copies(segp_ref, starts_ref, sort_ref.at[1 - slot], xs_ref, sem.at[1 - slot], True, True)

    @pl.when(i == pl.num_programs(0) - 1)
    def _():
        _segment_copies(seg_ref, starts_ref, sort_ref.at[slot], xs_ref, sem.at[slot], True, True)


def _dispatch(starts, seg, slot_t, xn, n_rows):
    n, d = xn.shape
    tm = ROW_TILE
    nt = n // tm
    smem = lambda f: pl.BlockSpec((1, SUBLANES, LANES), f, memory_space=pltpu.SMEM)
    any_spec = pl.BlockSpec(memory_space=pl.ANY)
    return pl.pallas_call(
        _dispatch_kernel,
        grid_spec=pltpu.PrefetchScalarGridSpec(
            num_scalar_prefetch=1,
            grid=(nt,),
            in_specs=[smem(lambda i, *_: (i, 0, 0)), smem(lambda i, *_: (jnp.maximum(i - 1, 0), 0, 0)),
                      pl.BlockSpec((1, SUBLANES, tm), lambda i, *_: (i, 0, 0)),
                      pl.BlockSpec((tm, d), lambda i, *_: (i, 0))],
            out_specs=any_spec,
            scratch_shapes=[pltpu.VMEM((2, SORT_ROWS, d), xn.dtype), pltpu.VMEM((EXPERT_TILE, d), xn.dtype),
                            pltpu.SemaphoreType.DMA((2,)), pltpu.SemaphoreType.DMA],
        ),
        out_shape=jax.ShapeDtypeStruct((n_rows, d), xn.dtype),
        compiler_params=_params("arbitrary"),
        name="dispatch",
    )(starts, seg, seg, slot_t, xn)


def _experts_kernel(te_ref, tb_ref, tv_ref, xs_ref, wgu_ref, bgu_ref, wd_ref, bd_ref, out_ref,
                    wgu_bf, wd_bf):
    i = pl.program_id(0)
    de = wd_ref.shape[1]
    prev = te_ref[jnp.maximum(i - 1, 0)]

    @pl.when(jnp.logical_or(i == 0, te_ref[i] != prev))
    def _():
        wgu_bf[...] = wgu_ref[0].astype(BF16)
        wd_bf[...] = wd_ref[0].astype(BF16)

    @pl.when(tv_ref[i] == 1)
    def _():
        x = xs_ref[...]
        ch = 512
        acc = jnp.zeros(out_ref.shape, F32)
        for j in range(de // ch):
            g = _dot(x, wgu_bf[:, j * ch:(j + 1) * ch]) + bgu_ref[0, :, j * ch:(j + 1) * ch]
            u = _dot(x, wgu_bf[:, de + j * ch:de + (j + 1) * ch]) + bgu_ref[0, :, de + j * ch:de + (j + 1) * ch]
            g = jnp.minimum(g, SWIGLU_LIMIT)
            u = jnp.clip(u, -SWIGLU_LIMIT, SWIGLU_LIMIT)
            hm = (u + 1.0) * (g * jax.nn.sigmoid(g * SWIGLU_ALPHA))
            acc = acc + _dot(hm.astype(BF16), wd_bf[j * ch:(j + 1) * ch, :])
        out_ref[...] = (acc + bd_ref[0]).astype(out_ref.dtype)

    @pl.when(tv_ref[i] == 0)
    def _():
        out_ref[...] = jnp.zeros_like(out_ref)


def _experts(tile_e, tile_b, tile_v, xs, w_gate_up, b_gate_up, w_down, b_down):
    n_rows, d = xs.shape
    tmx = EXPERT_TILE
    ne, _, de2 = w_gate_up.shape
    de = de2 // 2
    return pl.pallas_call(
        _experts_kernel,
        grid_spec=pltpu.PrefetchScalarGridSpec(
            num_scalar_prefetch=3,
            grid=(n_rows // tmx,),
            in_specs=[pl.BlockSpec((tmx, d), lambda i, te, tb, tv: (tb[i], 0)),
                      pl.BlockSpec((1, d, de2), lambda i, te, tb, tv: (te[i], 0, 0)),
                      pl.BlockSpec((1, 1, de2), lambda i, te, tb, tv: (te[i], 0, 0)),
                      pl.BlockSpec((1, de, d), lambda i, te, tb, tv: (te[i], 0, 0)),
                      pl.BlockSpec((1, 1, d), lambda i, te, tb, tv: (te[i], 0, 0))],
            out_specs=pl.BlockSpec((tmx, d), lambda i, te, tb, tv: (i, 0)),
            scratch_shapes=[pltpu.VMEM((d, de2), BF16), pltpu.VMEM((de, d), BF16)],
        ),
        out_shape=jax.ShapeDtypeStruct((n_rows, d), BF16),
        compiler_params=_params("arbitrary"),
        name="experts",
    )(tile_e, tile_b, tile_v, xs, w_gate_up, b_gate_up.reshape(ne, 1, de2), w_down, b_down.reshape(ne, 1, d))


def _combine_kernel(starts_ref, seg_ref, segn_ref, h_ref, gate_ref, slot_ref, p_ref, ys_ref, pn_ref, wg_ref,
                    wp_ref, fn_ref, out_ref, ybuf, sem, *, final):
    i = pl.program_id(0)
    nt = pl.num_programs(0)
    tm = h_ref.shape[0]
    cap = ybuf.shape[1]

    @pl.when(i == 0)
    def _():
        ybuf[...] = jnp.zeros_like(ybuf)
        _segment_copies(seg_ref, starts_ref, ybuf.at[0], ys_ref, sem.at[0], False, False)

    slot = i % 2

    @pl.when(i + 1 < nt)
    def _():
        _segment_copies(segn_ref, starts_ref, ybuf.at[1 - slot], ys_ref, sem.at[1 - slot], False, False)

    _segment_copies(seg_ref, starts_ref, ybuf.at[slot], ys_ref, sem.at[slot], False, True)

    pos = lax.broadcasted_iota(I32, (tm, cap), 1)
    gate = gate_ref[...]
    weights = jnp.zeros((tm, cap), F32)
    for k in range(TOP_K):
        weights = jnp.where(pos == slot_ref[:, k:k + 1], gate[:, k:k + 1], weights)
    h2 = h_ref[...] + _dot(weights.astype(BF16), ybuf[slot])
    hn = _rms(h2, pn_ref[...]).astype(BF16)
    sg = jax.nn.sigmoid(_dot(hn, wg_ref[...]))
    h3 = h2 + sg * _dot(p_ref[...].astype(BF16), wp_ref[...])
    out_ref[...] = _rms(h3, fn_ref[...]) if final else h3


def _combine(starts, seg, h1, gate, slot, p2, ys, ple_norm, w_ple_gate, w_ple_proj, final_norm, final):
    n, d = h1.shape
    tm = ROW_TILE
    nt = n // tm
    wg = w_ple_gate.astype(BF16)
    wp = w_ple_proj.astype(BF16)
    smem = lambda f: pl.BlockSpec((1, SUBLANES, LANES), f, memory_space=pltpu.SMEM)
    full = lambda arr: pl.BlockSpec(arr.shape, lambda i, *_: (0,) * arr.ndim)
    rows = lambda w: pl.BlockSpec((tm, w), lambda i, *_: (i, 0))
    consts = [ple_norm.reshape(1, d), wg, wp, final_norm.reshape(1, d)]
    return pl.pallas_call(
        functools.partial(_combine_kernel, final=final),
        grid_spec=pltpu.PrefetchScalarGridSpec(
            num_scalar_prefetch=1,
            grid=(nt,),
            in_specs=[smem(lambda i, *_: (i, 0, 0)), smem(lambda i, *_: (jnp.minimum(i + 1, nt - 1), 0, 0)),
                      rows(d), rows(TOP_K), rows(TOP_K), rows(p2.shape[1]), pl.BlockSpec(memory_space=pl.ANY)]
                     + [full(c) for c in consts],
            out_specs=rows(d),
            scratch_shapes=[pltpu.VMEM((2, SORT_ROWS, d), ys.dtype), pltpu.SemaphoreType.DMA((2,))],
        ),
        out_shape=jax.ShapeDtypeStruct((n, d), F32),
        compiler_params=_params("arbitrary"),
        name="combine",
    )(starts, seg, seg, h1, gate, slot, p2, ys, *consts)


def _route_tables(sizes, n_tiles):
    tmx = EXPERT_TILE
    tile_end = jnp.cumsum((sizes + tmx - 1) // tmx)
    starts = jnp.concatenate([jnp.zeros((1,), I32), tile_end * tmx]).astype(I32)
    n_valid = tile_end[-1]
    t = jnp.arange(n_tiles, dtype=I32)
    tb = jnp.minimum(t, n_valid - 1).astype(I32)
    te = jnp.sum(tile_end[None, :] <= tb[:, None], axis=1).astype(I32)
    tv = (t < n_valid).astype(I32)
    return starts, te, tb, tv


def kernel(x, p, positions, attn_norm, w_in, b_gates, conv_w, conv_b, mlstm_norm, q_norm, w_q_up, kv_norm, w_kv_up, mla_norm, w_out, ffn_norm, w_router, b_router, w_gate_up, b_gate_up, w_down, b_down, ple_norm, w_ple_gate, w_ple_proj, final_norm):
    bsz, s, d = x.shape
    n = bsz * s
    depth = p.shape[0]
    nc = s // MLSTM_CHUNK
    max_rows = n * TOP_K + (n // ROW_TILE) * N_EXPERTS * (SEG_ALIGN - 1) + N_EXPERTS * (EXPERT_TILE - 1)
    n_tiles = max_rows // EXPERT_TILE
    pos2 = positions.reshape(n, 1)
    h = x.reshape(n, d)
    for i in range(depth):
        um, gt, q, k, v = _in_proj(h, pos2, attn_norm[i], w_in[i], b_gates[i], q_norm[i], w_q_up[i],
                                   kv_norm[i], w_kv_up[i])
        ym = _mlstm(um.reshape(bsz, s, -1), gt.reshape(N_GATES, bsz, nc, MLSTM_CHUNK), conv_w[i], conv_b[i],
                    mlstm_norm[i])
        ya = _attention(q.reshape(bsz, s, -1), k.reshape(bsz, s, -1), v.reshape(bsz, s, -1))
        h1, xn, gate, slot, slot_t, seg, sizes = _out_route(ym.reshape(n, -1), ya.reshape(n, -1), h, mla_norm[i],
                                                            w_out[i], ffn_norm[i], w_router[i], b_router[i])
        starts, te, tb, tv = _route_tables(sizes[0, :N_EXPERTS], n_tiles)
        xs = _dispatch(starts, seg, slot_t, xn, n_tiles * EXPERT_TILE)
        ys = _experts(te, tb, tv, xs, w_gate_up[i], b_gate_up[i], w_down[i], b_down[i])
        h = _combine(starts, seg, h1, gate, slot, p[i].reshape(n, -1), ys, ple_norm[i], w_ple_gate[i],
                     w_ple_proj[i], final_norm, final=(i == depth - 1))
    return h.reshape(bsz, s, d)
```

```python
import functools

import jax
import jax.numpy as jnp
from jax import lax
from jax.experimental import pallas as pl
from jax.experimental.pallas import tpu as pltpu

F32 = jnp.float32
BF16 = jnp.bfloat16
I32 = jnp.int32

N_MLSTM_HEADS = 4
MLSTM_HEAD_DIM = 128
D_MLSTM = N_MLSTM_HEADS * MLSTM_HEAD_DIM
MLSTM_CHUNK = 128
N_MLA_HEADS = 4
QK_NOPE_DIM = 128
QK_ROPE_DIM = 64
V_HEAD_DIM = 128
D_MLA = N_MLA_HEADS * V_HEAD_DIM
Q_LORA = 256
KV_LORA = 128
ROPE_THETA = 10000.0
N_EXPERTS = 32
TOP_K = 4
SWIGLU_LIMIT = 7.0
SWIGLU_ALPHA = 1.702
EPS = 1e-6
N_GATES = 4 * N_MLSTM_HEADS
OFF_G = 4 * D_MLSTM
OFF_CQ = OFF_G + N_GATES
OFF_CKV = OFF_CQ + Q_LORA
OFF_KR = OFF_CKV + KV_LORA

LANES = 128
SUBLANES = 8
BF16_SUBLANES = 16
QK_SLAB = 2 * LANES
VMEM_LIMIT_BYTES = 56 * 1024 * 1024
LOG2_E = 1.4426950408889634

ROW_TILE = 512
Q_TILE = 1024
KV_CHUNK = 1024
EXPERT_TILE = 512
CHUNK_UNROLL = 2
SEG_ALIGN = BF16_SUBLANES
SORT_ROWS = ROW_TILE * TOP_K + N_EXPERTS * SEG_ALIGN


def _dot(a, b):
    return jnp.dot(a, b, preferred_element_type=F32)


def _dot_nt(a, b):
    return lax.dot_general(a, b, (((1,), (1,)), ((), ())), preferred_element_type=F32)


def _rms(x, g):
    return x * lax.rsqrt(jnp.mean(x * x, axis=-1, keepdims=True) + EPS) * g


def _log_sigmoid(x):
    return jnp.minimum(x, 0.0) - jnp.log(1.0 + jnp.exp(-jnp.abs(x)))


def _cumsum_lanes(x, reverse):
    n = x.shape[-1]
    lane = lax.broadcasted_iota(I32, x.shape, x.ndim - 1)
    sh = 1
    while sh < n:
        if reverse:
            x = x + jnp.where(lane < n - sh, pltpu.roll(x, n - sh, x.ndim - 1), 0.0)
        else:
            x = x + jnp.where(lane >= sh, pltpu.roll(x, sh, x.ndim - 1), 0.0)
        sh *= 2
    return x


def _params(*sem):
    return pltpu.CompilerParams(dimension_semantics=sem, vmem_limit_bytes=VMEM_LIMIT_BYTES)


def _in_proj_kernel(x_ref, pos_ref, an_ref, wm_ref, wr_ref, wgt_ref, bgt_ref, qn_ref, wq_ref, kvn_ref,
                    wk_ref, wv_ref, freq_ref, sgn_ref,
                    um_ref, gt_ref, q_ref, k_ref, v_ref):
    a = _rms(x_ref[...], an_ref[...]).astype(BF16)
    um_ref[...] = _dot(a, wm_ref[...])
    gt_ref[...] = _dot_nt(wgt_ref[...], a) + bgt_ref[...]
    rest = _dot(a, wr_ref[...])
    cq = rest[:, :Q_LORA]
    ckv = rest[:, Q_LORA:Q_LORA + KV_LORA]
    kr2 = rest[:, Q_LORA + KV_LORA:Q_LORA + KV_LORA + LANES]
    krs2 = rest[:, Q_LORA + KV_LORA + LANES:]
    ang = pos_ref[...].astype(F32) * freq_ref[...]
    cos_a = jnp.cos(ang)
    sin_a = jnp.sin(ang) * sgn_ref[...]
    scale = (QK_NOPE_DIM + QK_ROPE_DIM) ** -0.5 * LOG2_E
    lane = lax.broadcasted_iota(I32, ang.shape, 1)
    rope_mul = jnp.where(lane < QK_ROPE_DIM, cos_a, sin_a) * scale
    qf = _dot(_rms(cq, qn_ref[...]).astype(BF16), wq_ref[...])
    ckvn = _rms(ckv, kvn_ref[...]).astype(BF16)
    kn = _dot(ckvn, wk_ref[...])
    v_ref[...] = _dot(ckvn, wv_ref[...]).astype(BF16)
    k_rope = (kr2 * cos_a + krs2 * sin_a).astype(BF16)
    for h in range(N_MLA_HEADS):
        o = h * QK_SLAB
        q_ref[:, o:o + LANES] = (qf[:, o:o + LANES] * scale).astype(BF16)
        q_ref[:, o + LANES:o + QK_SLAB] = (qf[:, o + LANES:o + QK_SLAB] * rope_mul).astype(BF16)
        k_ref[:, o:o + LANES] = kn[:, h * LANES:(h + 1) * LANES].astype(BF16)
        k_ref[:, o + LANES:o + QK_SLAB] = k_rope


def _in_proj(x2, pos2, attn_norm, w_in, b_gates, q_norm, w_q_up, kv_norm, w_kv_up):
    n, d = x2.shape
    tm = ROW_TILE
    half = QK_ROPE_DIM // 2
    swap = jnp.concatenate([jnp.arange(half, QK_ROPE_DIM), jnp.arange(0, half)])
    w_kr = w_in[:, OFF_KR:OFF_KR + QK_ROPE_DIM]
    w_krs = w_kr[:, swap]
    wm = w_in[:, :OFF_G].astype(BF16)
    wr = jnp.concatenate([w_in[:, OFF_CQ:OFF_KR], w_kr, w_kr, w_krs, w_krs], axis=1).astype(BF16)
    wgt = w_in[:, OFF_G:OFF_CQ].T.astype(BF16)
    bgt = b_gates.reshape(N_GATES, 1)
    wq4 = w_q_up.reshape(Q_LORA, N_MLA_HEADS, QK_NOPE_DIM + QK_ROPE_DIM)
    wq_pe = wq4[:, :, QK_NOPE_DIM:]
    wq = jnp.concatenate([wq4, wq_pe[:, :, swap]], axis=2).reshape(Q_LORA, N_MLA_HEADS * QK_SLAB).astype(BF16)
    wkv4 = w_kv_up.reshape(KV_LORA, N_MLA_HEADS, QK_NOPE_DIM + V_HEAD_DIM)
    wk = wkv4[:, :, :QK_NOPE_DIM].reshape(KV_LORA, N_MLA_HEADS * QK_NOPE_DIM).astype(BF16)
    wv = wkv4[:, :, QK_NOPE_DIM:].reshape(KV_LORA, D_MLA).astype(BF16)
    freqs = ROPE_THETA ** (-jnp.arange(0, QK_ROPE_DIM, 2, dtype=F32) / QK_ROPE_DIM)
    freq_l = jnp.tile(freqs, LANES // half).reshape(1, LANES)
    sgn_l = jnp.tile(jnp.concatenate([-jnp.ones((half,), F32), jnp.ones((half,), F32)]),
                     LANES // QK_ROPE_DIM).reshape(1, LANES)
    full = lambda arr: pl.BlockSpec(arr.shape, lambda i: (0,) * arr.ndim)
    rows = lambda w: pl.BlockSpec((tm, w), lambda i: (i, 0))
    consts = [attn_norm.reshape(1, d), wm, wr, wgt, bgt, q_norm.reshape(1, Q_LORA), wq,
              kv_norm.reshape(1, KV_LORA), wk, wv, freq_l, sgn_l]
    return pl.pallas_call(
        _in_proj_kernel,
        grid=(n // tm,),
        in_specs=[rows(d), rows(1)] + [full(c) for c in consts],
        out_specs=[rows(OFF_G), pl.BlockSpec((N_GATES, tm), lambda i: (0, i)),
                   rows(N_MLA_HEADS * QK_SLAB), rows(N_MLA_HEADS * QK_SLAB), rows(D_MLA)],
        out_shape=[jax.ShapeDtypeStruct((n, OFF_G), F32), jax.ShapeDtypeStruct((N_GATES, n), F32),
                   jax.ShapeDtypeStruct((n, N_MLA_HEADS * QK_SLAB), BF16),
                   jax.ShapeDtypeStruct((n, N_MLA_HEADS * QK_SLAB), BF16),
                   jax.ShapeDtypeStruct((n, D_MLA), BF16)],
        compiler_params=_params("parallel"),
        name="in_proj",
    )(x2, pos2, *consts)


def _mlstm_kernel(q_ref, k_ref, v_ref, o_ref, g_ref, cwq_ref, cwk_ref, cbq_ref, cbk_ref, nrm_ref,
                  y_ref,
                  qc_ref, kc_ref, va_ref, cst_ref, b_ref, r_ref, ew_ref, mw_ref, bt_ref, mp_ref):
    L = MLSTM_CHUNK
    dh = MLSTM_HEAD_DIM
    nc = q_ref.shape[1] // L
    s_len = q_ref.shape[1]
    h = pl.program_id(1)

    for d in range(2):
        ig = g_ref[2 * d * N_MLSTM_HEADS + h, 0]
        fg = g_ref[(2 * d + 1) * N_MLSTM_HEADS + h, 0]
        b = _cumsum_lanes(_log_sigmoid(fg), reverse=(d == 1))
        btot = b[:, L - 1:L] if d == 0 else b[:, 0:1]
        r = ig - b
        w = btot + r
        mw = jnp.max(w, axis=-1, keepdims=True)
        b_ref[d] = b
        r_ref[d] = r
        ew_ref[d] = jnp.exp(w - mw)
        mw_ref[d] = jnp.broadcast_to(mw, (nc, L))
        bt_ref[d] = jnp.broadcast_to(btot, (nc, L))

    row = lax.broadcasted_iota(I32, (L, dh), 0)

    def conv_silu(ref, cw_ref, cb_ref, c):
        start = pl.multiple_of(c * L, L)
        x = ref[0, pl.ds(start, L), :]
        prev_row = jnp.where(c > 0, ref[0, pl.ds(jnp.maximum(start - 1, 0), 1), :], 0.0)
        next_row = jnp.where(c < nc - 1, ref[0, pl.ds(jnp.minimum(start + L, s_len - 1), 1), :], 0.0)
        x_prev = jnp.where(row == 0, prev_row, pltpu.roll(x, 1, 0))
        x_next = jnp.where(row == L - 1, next_row, pltpu.roll(x, L - 1, 0))
        y = cw_ref[0:1, :] * x_prev + cw_ref[1:2, :] * x + cw_ref[2:3, :] * x_next + cb_ref[...]
        return y * jax.nn.sigmoid(y)

    ones_col = (lax.broadcasted_iota(I32, (L, dh), 1) == 0).astype(BF16)

    def pass1(c, carry):
        start = pl.multiple_of(c * L, L)
        qc_ref[pl.ds(start, L), :] = conv_silu(q_ref, cwq_ref, cbq_ref, c).astype(BF16)
        kk = conv_silu(k_ref, cwk_ref, cbk_ref, c) * (dh ** -0.5)
        kc_ref[pl.ds(start, L), :] = kk.astype(BF16)
        va = jnp.concatenate([v_ref[0, pl.ds(start, L), :].astype(BF16), ones_col], axis=1)
        va_ref[pl.ds(start, L), :] = va
        kt = kk.T
        for d in range(2):
            kw_t = (kt * ew_ref[d, pl.ds(c, 1), :]).astype(BF16)
            cst_ref[d, c] = _dot(kw_t, va)
        return carry

    lax.fori_loop(0, nc, pass1, 0, unroll=CHUNK_UNROLL)

    for d in range(2):
        def scan(i, carry):
            st, m = carry
            c = i if d == 0 else nc - 1 - i
            mw = mw_ref[d, pl.ds(c, 1), :]
            bt = bt_ref[d, pl.ds(c, 1), :]
            m_new = jnp.maximum(bt + m, mw)
            a = jnp.exp(bt + m - m_new)[:, 0:1]
            cc = jnp.exp(mw - m_new)[:, 0:1]
            loc = cst_ref[d, c]
            cst_ref[d, c] = st
            mp_ref[d, pl.ds(c, 1), :] = m
            return a * st + cc * loc, m_new

        lax.fori_loop(0, nc, scan, (jnp.zeros((dh, 2 * dh), F32), jnp.zeros((1, L), F32)))

    ti = lax.broadcasted_iota(I32, (L, L), 0)
    si = lax.broadcasted_iota(I32, (L, L), 1)
    masks = (si <= ti, si >= ti)

    def pass3(c, carry):
        start = pl.multiple_of(c * L, L)
        q = qc_ref[pl.ds(start, L), :]
        k = kc_ref[pl.ds(start, L), :]
        va = va_ref[pl.ds(start, L), :]
        qk = _dot_nt(q, k)
        hsum = jnp.zeros((L, dh), F32)
        for d in range(2):
            bmat = jnp.broadcast_to(b_ref[d, pl.ds(c, 1), :], (L, L)).T
            dmat = jnp.where(masks[d], bmat + r_ref[d, pl.ds(c, 1), :], -jnp.inf)
            inter = bmat[:, 0:1] + mp_ref[d, pl.ds(c, 1), :][:, 0:1]
            m_t = jnp.maximum(inter, jnp.max(dmat, axis=-1, keepdims=True))
            sc = qk * jnp.exp(dmat - m_t)
            a = jnp.exp(inter - m_t)
            tot = _dot(sc.astype(BF16), va) + a * _dot(q, cst_ref[d, c].astype(BF16))
            den = jnp.maximum(jnp.abs(tot[:, dh:dh + 1]), jnp.exp(-m_t))
            hsum = hsum + tot[:, :dh] / den
        hn = _rms(hsum, nrm_ref[...])
        y_ref[0, pl.ds(start, L), :] = hn * jax.nn.sigmoid(o_ref[0, pl.ds(start, L), :])
        return carry

    lax.fori_loop(0, nc, pass3, 0, unroll=CHUNK_UNROLL)


def _mlstm(um3, gt4, conv_w, conv_b, mlstm_norm):
    bsz, s, _ = um3.shape
    H, dh, L = N_MLSTM_HEADS, MLSTM_HEAD_DIM, MLSTM_CHUNK
    nc = s // L
    col = lambda off: pl.BlockSpec((1, s, dh), lambda b, h: (b, 0, off + h))
    vec = lambda rows, off: pl.BlockSpec((rows, dh), lambda b, h: (0, off + h))
    cb = conv_b.reshape(1, 2 * D_MLSTM)
    return pl.pallas_call(
        _mlstm_kernel,
        grid=(bsz, H),
        in_specs=[col(0), col(H), col(2 * H), col(3 * H),
                  pl.BlockSpec((N_GATES, 1, nc, L), lambda b, h: (0, b, 0, 0)),
                  vec(3, 0), vec(3, H), vec(1, 0), vec(1, H), vec(1, 0)],
        out_specs=pl.BlockSpec((1, s, dh), lambda b, h: (b, 0, h)),
        out_shape=jax.ShapeDtypeStruct((bsz, s, D_MLSTM), F32),
        scratch_shapes=[pltpu.VMEM((s, dh), BF16), pltpu.VMEM((s, dh), BF16), pltpu.VMEM((s, 2 * dh), BF16),
                        pltpu.VMEM((2, nc, dh, 2 * dh), F32)]
                       + [pltpu.VMEM((2, nc, L), F32) for _ in range(6)],
        compiler_params=_params("parallel", "parallel"),
        name="mlstm",
    )(um3, um3, um3, um3, gt4, conv_w, conv_w, cb, cb, mlstm_norm.reshape(1, D_MLSTM))


def _attn_kernel(q_ref, k_ref, v_ref, o_ref):
    tq = q_ref.shape[1]
    q = q_ref[0]
    m = jnp.full((tq, 1), -jnp.inf, F32)
    l = jnp.zeros((tq, 1), F32)
    acc = jnp.zeros((tq, V_HEAD_DIM), F32)
    chunk = min(KV_CHUNK, k_ref.shape[1])
    for c in range(k_ref.shape[1] // chunk):
        keys = slice(c * chunk, (c + 1) * chunk)
        s = _dot_nt(q, k_ref[0, keys, :])
        m_new = jnp.maximum(m, jnp.max(s, axis=-1, keepdims=True))
        alpha = jnp.exp2(m - m_new)
        p = jnp.exp2(s - m_new)
        l = alpha * l + jnp.sum(p, axis=-1, keepdims=True)
        acc = alpha * acc + _dot(p.astype(BF16), v_ref[0, keys, :])
        m = m_new
    o_ref[0] = acc / l


def _attention(q3, k3, v3):
    bsz, s, _ = q3.shape
    tq = min(Q_TILE, s)
    return pl.pallas_call(
        _attn_kernel,
        grid=(bsz, N_MLA_HEADS, s // tq),
        in_specs=[pl.BlockSpec((1, tq, QK_SLAB), lambda b, h, i: (b, i, h)),
                  pl.BlockSpec((1, s, QK_SLAB), lambda b, h, i: (b, 0, h)),
                  pl.BlockSpec((1, s, V_HEAD_DIM), lambda b, h, i: (b, 0, h))],
        out_specs=pl.BlockSpec((1, tq, V_HEAD_DIM), lambda b, h, i: (b, i, h)),
        out_shape=jax.ShapeDtypeStruct((bsz, s, D_MLA), F32),
        compiler_params=_params("parallel", "parallel", "parallel"),
        name="attention",
    )(q3, k3, v3)


def _out_route_kernel(ym_ref, ya_ref, x_ref, mn_ref, wom_ref, woa_ref, fn_ref, wr_ref, br_ref,
                      h_ref, xn_ref, gate_ref, slot_ref, slot_t_ref, seg_ref, size_ref,
                      carry_ref):
    i = pl.program_id(0)
    tm = x_ref.shape[0]

    @pl.when(i == 0)
    def _():
        carry_ref[...] = jnp.zeros_like(carry_ref)

    ya = _rms(ya_ref[...], mn_ref[...])
    h1 = x_ref[...] + _dot(ym_ref[...].astype(BF16), wom_ref[...]) + _dot(ya.astype(BF16), woa_ref[...])
    h_ref[...] = h1
    xn = _rms(h1, fn_ref[...])
    xn_ref[...] = xn.astype(BF16)
    logits = lax.dot_general(wr_ref[...], xn, (((1,), (1,)), ((), ())), preferred_element_type=F32,
                             precision=lax.Precision.HIGHEST) + br_ref[...]
    erow = lax.broadcasted_iota(I32, logits.shape, 0)
    work = logits
    vals, hots = [], []
    for k in range(TOP_K):
        mx = jnp.max(work, axis=0, keepdims=True)
        idx = jnp.min(jnp.where(work == mx, erow, N_EXPERTS), axis=0, keepdims=True)
        hot = erow == idx
        work = jnp.where(hot, -jnp.inf, work)
        vals.append(mx)
        hots.append(hot)
    exps = [jnp.exp(v - vals[0]) for v in vals]
    tot = exps[0] + exps[1] + exps[2] + exps[3]
    multi = (hots[0] | hots[1] | hots[2] | hots[3]).astype(BF16)
    ti = lax.broadcasted_iota(I32, (tm, tm), 0)
    tj = lax.broadcasted_iota(I32, (tm, tm), 1)
    local_rank = _dot(multi, (ti < tj).astype(BF16))
    multi_rows = jnp.concatenate([multi, jnp.zeros((LANES - N_EXPERTS, tm), BF16)], axis=0)
    count = _dot_nt(jnp.ones((SUBLANES, tm), BF16), multi_rows)[0:1]
    padded = jnp.ceil(count * (1.0 / SEG_ALIGN)) * SEG_ALIGN
    local_start = _cumsum_lanes(padded, reverse=False) - padded
    starts8 = jnp.broadcast_to(local_start, (SUBLANES, LANES))[:, :N_EXPERTS].astype(BF16)
    krow = lax.broadcasted_iota(I32, (SUBLANES, tm), 0)
    packed = jnp.zeros((SUBLANES, tm), F32)
    for k in range(TOP_K):
        sk = (jnp.sum(jnp.where(hots[k], local_rank, 0.0), axis=0, keepdims=True)
              + _dot(starts8, hots[k].astype(BF16))[0:1])
        packed = jnp.where(krow == k, exps[k] / tot, jnp.where(krow == TOP_K + k, sk, packed))
    slot_t_ref[0] = jnp.where(krow < TOP_K, pltpu.roll(packed, TOP_K, 0), 0.0).astype(I32)
    cols = packed.T
    gate_ref[...] = cols[:, :TOP_K]
    slot_ref[...] = cols[:, TOP_K:2 * TOP_K].astype(I32)
    srow = lax.broadcasted_iota(I32, (SUBLANES, LANES), 0)
    seg = jnp.where(srow == 0, padded, jnp.where(srow == 1, local_start, jnp.where(srow == 2, carry_ref[...], 0.0)))
    seg_ref[0] = seg.astype(I32)
    carry_ref[...] += padded
    size_ref[...] = carry_ref[...].astype(I32)


def _out_route(ym2, ya2, x2, mla_norm, w_out, ffn_norm, w_router, b_router):
    n, d = x2.shape
    tm = ROW_TILE
    nt = n // tm
    wom = w_out[:D_MLSTM].astype(BF16)
    woa = w_out[D_MLSTM:].astype(BF16)
    wr = w_router.T
    br = b_router.reshape(N_EXPERTS, 1)
    full = lambda arr: pl.BlockSpec(arr.shape, lambda i: (0,) * arr.ndim)
    rows = lambda w: pl.BlockSpec((tm, w), lambda i: (i, 0))
    consts = [mla_norm.reshape(1, D_MLA), wom, woa, ffn_norm.reshape(1, d), wr, br]
    return pl.pallas_call(
        _out_route_kernel,
        grid=(nt,),
        in_specs=[rows(D_MLSTM), rows(D_MLA), rows(d)] + [full(c) for c in consts],
        out_specs=[rows(d), rows(d), rows(TOP_K), rows(TOP_K),
                   pl.BlockSpec((1, SUBLANES, tm), lambda i: (i, 0, 0)),
                   pl.BlockSpec((1, SUBLANES, LANES), lambda i: (i, 0, 0)),
                   pl.BlockSpec((1, LANES), lambda i: (0, 0))],
        out_shape=[jax.ShapeDtypeStruct((n, d), F32), jax.ShapeDtypeStruct((n, d), BF16),
                   jax.ShapeDtypeStruct((n, TOP_K), F32), jax.ShapeDtypeStruct((n, TOP_K), I32),
                   jax.ShapeDtypeStruct((nt, SUBLANES, tm), I32), jax.ShapeDtypeStruct((nt, SUBLANES, LANES), I32),
                   jax.ShapeDtypeStruct((1, LANES), I32)],
        scratch_shapes=[pltpu.VMEM((1, LANES), F32)],
        compiler_params=_params("arbitrary"),
        name="out_route",
    )(ym2, ya2, x2, *consts)


def _segment_copies(seg_ref, starts_ref, local_ref, global_ref, sem, to_global, wait):
    def per_expert(e, carry):
        size = seg_ref[0, 0, e]
        src = seg_ref[0, 1, e]
        dst = starts_ref[e] + seg_ref[0, 2, e]
        chunk = ROW_TILE
        while chunk >= SEG_ALIGN:
            @pl.when((size & chunk) != 0)
            def _(src=src, dst=dst, chunk=chunk):
                loc = local_ref.at[pl.ds(pl.multiple_of(src, SEG_ALIGN), chunk)]
                glo = global_ref.at[pl.ds(pl.multiple_of(dst, SEG_ALIGN), chunk)]
                cp = pltpu.make_async_copy(loc, glo, sem) if to_global else pltpu.make_async_copy(glo, loc, sem)
                if wait:
                    cp.wait()
                else:
                    cp.start()

            step = size & chunk
            src = src + step
            dst = dst + step
            chunk //= 2
        return carry

    lax.fori_loop(0, N_EXPERTS, per_expert, 0)


def _dispatch_kernel(starts_ref, seg_ref, segp_ref, slot_t_ref, xn_ref, xs_ref, sort_ref, zero_ref, sem, zsem):
    i = pl.program_id(0)
    tm = xn_ref.shape[0]
    tmx = zero_ref.shape[0]
    cap = sort_ref.shape[1]

    @pl.when(i == 0)
    def _():
        zero_ref[...] = jnp.zeros_like(zero_ref)
        n_tail = (xs_ref.shape[0] - starts_ref[N_EXPERTS]) // tmx

        def clear_tile(row, wait):
            cp = pltpu.make_async_copy(zero_ref, xs_ref.at[pl.ds(pl.multiple_of(row, tmx), tmx)], zsem)
            if wait:
                cp.wait()
            else:
                cp.start()

        for wait in (False, True):
            def clear_group(e, carry, wait=wait):
                hi = starts_ref[e + 1]

                @pl.when(hi > starts_ref[e])
                def _():
                    clear_tile(hi - tmx, wait)

                return carry

            def clear_tail(t, carry, wait=wait):
                clear_tile(starts_ref[N_EXPERTS] + t * tmx, wait)
                return carry

            lax.fori_loop(0, N_EXPERTS, clear_group, 0)
            lax.fori_loop(0, n_tail, clear_tail, 0)

    slot = i % 2
    pos = lax.broadcasted_iota(I32, (cap, tm), 0)
    hit = pos == slot_t_ref[0, 0:1, :]
    for k in range(1, TOP_K):
        hit = hit | (pos == slot_t_ref[0, k:k + 1, :])
    sort_ref[slot] = _dot(hit.astype(BF16), xn_ref[...]).astype(BF16)
    _segment_copies(seg_ref, starts_ref, sort_ref.at[slot], xs_ref, sem.at[slot], True, False)

    @pl.when(i >= 1)
    def _():
        _segment_copies(segp_ref, starts_ref, sort_ref.at[1 - slot], xs_ref, sem.at[1 - slot], True, True)

    @pl.when(i == pl.num_programs(0) - 1)
    def _():
        _segment_copies(seg_ref, starts_ref, sort_ref.at[slot], xs_ref, sem.at[slot], True, True)


def _dispatch(starts, seg, slot_t, xn, n_rows):
    n, d = xn.shape
    tm = ROW_TILE
    nt = n // tm
    smem = lambda f: pl.BlockSpec((1, SUBLANES, LANES), f, memory_space=pltpu.SMEM)
    any_spec = pl.BlockSpec(memory_space=pl.ANY)
    return pl.pallas_call(
        _dispatch_kernel,
        grid_spec=pltpu.PrefetchScalarGridSpec(
            num_scalar_prefetch=1,
            grid=(nt,),
            in_specs=[smem(lambda i, *_: (i, 0, 0)), smem(lambda i, *_: (jnp.maximum(i - 1, 0), 0, 0)),
                      pl.BlockSpec((1, SUBLANES, tm), lambda i, *_: (i, 0, 0)),
                      pl.BlockSpec((tm, d), lambda i, *_: (i, 0))],
            out_specs=any_spec,
            scratch_shapes=[pltpu.VMEM((2, SORT_ROWS, d), xn.dtype), pltpu.VMEM((EXPERT_TILE, d), xn.dtype),
                            pltpu.SemaphoreType.DMA((2,)), pltpu.SemaphoreType.DMA],
        ),
        out_shape=jax.ShapeDtypeStruct((n_rows, d), xn.dtype),
        compiler_params=_params("arbitrary"),
        name="dispatch",
    )(starts, seg, seg, slot_t, xn)


def _experts_kernel(te_ref, tb_ref, tv_ref, xs_ref, wgu_ref, bgu_ref, wd_ref, bd_ref, out_ref,
                    wgu_bf, wd_bf):
    i = pl.program_id(0)
    de = wd_ref.shape[1]
    prev = te_ref[jnp.maximum(i - 1, 0)]

    @pl.when(jnp.logical_or(i == 0, te_ref[i] != prev))
    def _():
        wgu_bf[...] = wgu_ref[0].astype(BF16)
        wd_bf[...] = wd_ref[0].astype(BF16)

    @pl.when(tv_ref[i] == 1)
    def _():
        x = xs_ref[...]
        ch = 512
        acc = jnp.zeros(out_ref.shape, F32)
        for j in range(de // ch):
            g = _dot(x, wgu_bf[:, j * ch:(j + 1) * ch]) + bgu_ref[0, :, j * ch:(j + 1) * ch]
            u = _dot(x, wgu_bf[:, de + j * ch:de + (j + 1) * ch]) + bgu_ref[0, :, de + j * ch:de + (j + 1) * ch]
            g = jnp.minimum(g, SWIGLU_LIMIT)
            u = jnp.clip(u, -SWIGLU_LIMIT, SWIGLU_LIMIT)
            hm = (u + 1.0) * (g * jax.nn.sigmoid(g * SWIGLU_ALPHA))
            acc = acc + _dot(hm.astype(BF16), wd_bf[j * ch:(j + 1) * ch, :])
        out_ref[...] = (acc + bd_ref[0]).astype(out_ref.dtype)

    @pl.when(tv_ref[i] == 0)
    def _():
        out_ref[...] = jnp.zeros_like(out_ref)


def _experts(tile_e, tile_b, tile_v, xs, w_gate_up, b_gate_up, w_down, b_down):
    n_rows, d = xs.shape
    tmx = EXPERT_TILE
    ne, _, de2 = w_gate_up.shape
    de = de2 // 2
    return pl.pallas_call(
        _experts_kernel,
        grid_spec=pltpu.PrefetchScalarGridSpec(
            num_scalar_prefetch=3,
            grid=(n_rows // tmx,),
            in_specs=[pl.BlockSpec((tmx, d), lambda i, te, tb, tv: (tb[i], 0)),
                      pl.BlockSpec((1, d, de2), lambda i, te, tb, tv: (te[i], 0, 0)),
                      pl.BlockSpec((1, 1, de2), lambda i, te, tb, tv: (te[i], 0, 0)),
                      pl.BlockSpec((1, de, d), lambda i, te, tb, tv: (te[i], 0, 0)),
                      pl.BlockSpec((1, 1, d), lambda i, te, tb, tv: (te[i], 0, 0))],
            out_specs=pl.BlockSpec((tmx, d), lambda i, te, tb, tv: (i, 0)),
            scratch_shapes=[pltpu.VMEM((d, de2), BF16), pltpu.VMEM((de, d), BF16)],
        ),
        out_shape=jax.ShapeDtypeStruct((n_rows, d), BF16),
        compiler_params=_params("arbitrary"),
        name="experts",
    )(tile_e, tile_b, tile_v, xs, w_gate_up, b_gate_up.reshape(ne, 1, de2), w_down, b_down.reshape(ne, 1, d))


def _combine_kernel(starts_ref, seg_ref, segn_ref, h_ref, gate_ref, slot_ref, p_ref, ys_ref, pn_ref, wg_ref,
                    wp_ref, fn_ref, out_ref, ybuf, sem, *, final):
    i = pl.program_id(0)
    nt = pl.num_programs(0)
    tm = h_ref.shape[0]
    cap = ybuf.shape[1]

    @pl.when(i == 0)
    def _():
        ybuf[...] = jnp.zeros_like(ybuf)
        _segment_copies(seg_ref, starts_ref, ybuf.at[0], ys_ref, sem.at[0], False, False)

    slot = i % 2

    @pl.when(i + 1 < nt)
    def _():
        _segment_copies(segn_ref, starts_ref, ybuf.at[1 - slot], ys_ref, sem.at[1 - slot], False, False)

    _segment_copies(seg_ref, starts_ref, ybuf.at[slot], ys_ref, sem.at[slot], False, True)

    pos = lax.broadcasted_iota(I32, (tm, cap), 1)
    gate = gate_ref[...]
    weights = jnp.zeros((tm, cap), F32)
    for k in range(TOP_K):
        weights = jnp.where(pos == slot_ref[:, k:k + 1], gate[:, k:k + 1], weights)
    h2 = h_ref[...] + _dot(weights.astype(BF16), ybuf[slot])
    hn = _rms(h2, pn_ref[...]).astype(BF16)
    sg = jax.nn.sigmoid(_dot(hn, wg_ref[...]))
    h3 = h2 + sg * _dot(p_ref[...].astype(BF16), wp_ref[...])
    out_ref[...] = _rms(h3, fn_ref[...]) if final else h3


def _combine(starts, seg, h1, gate, slot, p2, ys, ple_norm, w_ple_gate, w_ple_proj, final_norm, final):
    n, d = h1.shape
    tm = ROW_TILE
    nt = n // tm
    wg = w_ple_gate.astype(BF16)
    wp = w_ple_proj.astype(BF16)
    smem = lambda f: pl.BlockSpec((1, SUBLANES, LANES), f, memory_space=pltpu.SMEM)
    full = lambda arr: pl.BlockSpec(arr.shape, lambda i, *_: (0,) * arr.ndim)
    rows = lambda w: pl.BlockSpec((tm, w), lambda i, *_: (i, 0))
    consts = [ple_norm.reshape(1, d), wg, wp, final_norm.reshape(1, d)]
    return pl.pallas_call(
        functools.partial(_combine_kernel, final=final),
        grid_spec=pltpu.PrefetchScalarGridSpec(
            num_scalar_prefetch=1,
            grid=(nt,),
            in_specs=[smem(lambda i, *_: (i, 0, 0)), smem(lambda i, *_: (jnp.minimum(i + 1, nt - 1), 0, 0)),
                      rows(d), rows(TOP_K), rows(TOP_K), rows(p2.shape[1]), pl.BlockSpec(memory_space=pl.ANY)]
                     + [full(c) for c in consts],
            out_specs=rows(d),
            scratch_shapes=[pltpu.VMEM((2, SORT_ROWS, d), ys.dtype), pltpu.SemaphoreType.DMA((2,))],
        ),
        out_shape=jax.ShapeDtypeStruct((n, d), F32),
        compiler_params=_params("arbitrary"),
        name="combine",
    )(starts, seg, seg, h1, gate, slot, p2, ys, *consts)


def _route_tables(sizes, n_tiles):
    tmx = EXPERT_TILE
    tile_end = jnp.cumsum((sizes + tmx - 1) // tmx)
    starts = jnp.concatenate([jnp.zeros((1,), I32), tile_end * tmx]).astype(I32)
    n_valid = tile_end[-1]
    t = jnp.arange(n_tiles, dtype=I32)
    tb = jnp.minimum(t, n_valid - 1).astype(I32)
    te = jnp.sum(tile_end[None, :] <= tb[:, None], axis=1).astype(I32)
    tv = (t < n_valid).astype(I32)
    return starts, te, tb, tv


def kernel(x, p, positions, attn_norm, w_in, b_gates, conv_w, conv_b, mlstm_norm, q_norm, w_q_up, kv_norm, w_kv_up, mla_norm, w_out, ffn_norm, w_router, b_router, w_gate_up, b_gate_up, w_down, b_down, ple_norm, w_ple_gate, w_ple_proj, final_norm):
    bsz, s, d = x.shape
    n = bsz * s
    depth = p.shape[0]
    nc = s // MLSTM_CHUNK
    max_rows = n * TOP_K + (n // ROW_TILE) * N_EXPERTS * (SEG_ALIGN - 1) + N_EXPERTS * (EXPERT_TILE - 1)
    n_tiles = max_rows // EXPERT_TILE
    pos2 = positions.reshape(n, 1)
    h = x.reshape(n, d)
    for i in range(depth):
        um, gt, q, k, v = _in_proj(h, pos2, attn_norm[i], w_in[i], b_gates[i], q_norm[i], w_q_up[i],
                                   kv_norm[i], w_kv_up[i])
        ym = _mlstm(um.reshape(bsz, s, -1), gt.reshape(N_GATES, bsz, nc, MLSTM_CHUNK), conv_w[i], conv_b[i],
                    mlstm_norm[i])
        ya = _attention(q.reshape(bsz, s, -1), k.reshape(bsz, s, -1), v.reshape(bsz, s, -1))
        h1, xn, gate, slot, slot_t, seg, sizes = _out_route(ym.reshape(n, -1), ya.reshape(n, -1), h, mla_norm[i],
                                                            w_out[i], ffn_norm[i], w_router[i], b_router[i])
        starts, te, tb, tv = _route_tables(sizes[0, :N_EXPERTS], n_tiles)
        xs = _dispatch(starts, seg, slot_t, xn, n_tiles * EXPERT_TILE)
        ys = _experts(te, tb, tv, xs, w_gate_up[i], b_gate_up[i], w_down[i], b_down[i])
        h = _combine(starts, seg, h1, gate, slot, p[i].reshape(n, -1), ys, ple_norm[i], w_ple_gate[i],
                     w_ple_proj[i], final_norm, final=(i == depth - 1))
    return h.reshape(bsz, s, d)
```

```python
import functools

import jax
import jax.numpy as jnp
from jax import lax
from jax.experimental import pallas as pl
from jax.experimental.pallas import tpu as pltpu

F32 = jnp.float32
BF16 = jnp.bfloat16
I32 = jnp.int32

N_MLSTM_HEADS = 4
MLSTM_HEAD_DIM = 128
D_MLSTM = N_MLSTM_HEADS * MLSTM_HEAD_DIM
MLSTM_CHUNK = 128
N_MLA_HEADS = 4
QK_NOPE_DIM = 128
QK_ROPE_DIM = 64
V_HEAD_DIM = 128
D_MLA = N_MLA_HEADS * V_HEAD_DIM
Q_LORA = 256
KV_LORA = 128
ROPE_THETA = 10000.0
N_EXPERTS = 32
TOP_K = 4
SWIGLU_LIMIT = 7.0
SWIGLU_ALPHA = 1.702
EPS = 1e-6
N_GATES = 4 * N_MLSTM_HEADS
OFF_G = 4 * D_MLSTM
OFF_CQ = OFF_G + N_GATES
OFF_CKV = OFF_CQ + Q_LORA
OFF_KR = OFF_CKV + KV_LORA

LANES = 128
SUBLANES = 8
BF16_SUBLANES = 16
QK_SLAB = 2 * LANES
VMEM_LIMIT_BYTES = 56 * 1024 * 1024
LOG2_E = 1.4426950408889634

ROW_TILE = 512
Q_TILE = 1024
KV_CHUNK = 1024
EXPERT_TILE = 512
CHUNK_UNROLL = 4
SEG_ALIGN = BF16_SUBLANES
SEG_CHUNK_SHIFT = 5
SEG_CHUNK = 1 << SEG_CHUNK_SHIFT
SORT_ROWS = ROW_TILE * TOP_K + N_EXPERTS * SEG_ALIGN


def _dot(a, b):
    return jnp.dot(a, b, preferred_element_type=F32)


def _dot_nt(a, b):
    return lax.dot_general(a, b, (((1,), (1,)), ((), ())), preferred_element_type=F32)


def _rms(x, g):
    return x * lax.rsqrt(jnp.mean(x * x, axis=-1, keepdims=True) + EPS) * g


def _log_sigmoid(x):
    return jnp.minimum(x, 0.0) - jnp.log(1.0 + jnp.exp(-jnp.abs(x)))


def _scan_lanes(x, reverse, op, identity):
    n = x.shape[-1]
    lane = lax.broadcasted_iota(I32, x.shape, x.ndim - 1)
    sh = 1
    while sh < n:
        if reverse:
            x = op(x, jnp.where(lane < n - sh, pltpu.roll(x, n - sh, x.ndim - 1), identity))
        else:
            x = op(x, jnp.where(lane >= sh, pltpu.roll(x, sh, x.ndim - 1), identity))
        sh *= 2
    return x


def _cumsum_lanes(x, reverse):
    return _scan_lanes(x, reverse, jnp.add, 0.0)


def _cummax_lanes(x, reverse):
    return _scan_lanes(x, reverse, jnp.maximum, -jnp.inf)


def _params(*sem):
    return pltpu.CompilerParams(dimension_semantics=sem, vmem_limit_bytes=VMEM_LIMIT_BYTES)


def _in_proj_kernel(x_ref, pos_ref, an_ref, wm_ref, wr_ref, wgt_ref, bgt_ref, qn_ref, wq_ref, kvn_ref,
                    wk_ref, wv_ref, freq_ref, sgn_ref,
                    um_ref, gt_ref, q_ref, k_ref, v_ref):
    a = _rms(x_ref[...], an_ref[...]).astype(BF16)
    um_ref[...] = _dot(a, wm_ref[...])
    gt_ref[...] = _dot_nt(wgt_ref[...], a) + bgt_ref[...]
    rest = _dot(a, wr_ref[...])
    cq = rest[:, :Q_LORA]
    ckv = rest[:, Q_LORA:Q_LORA + KV_LORA]
    kr2 = rest[:, Q_LORA + KV_LORA:Q_LORA + KV_LORA + LANES]
    krs2 = rest[:, Q_LORA + KV_LORA + LANES:]
    ang = pos_ref[...].astype(F32) * freq_ref[...]
    cos_a = jnp.cos(ang)
    sin_a = jnp.sin(ang) * sgn_ref[...]
    scale = (QK_NOPE_DIM + QK_ROPE_DIM) ** -0.5 * LOG2_E
    lane = lax.broadcasted_iota(I32, ang.shape, 1)
    rope_mul = jnp.where(lane < QK_ROPE_DIM, cos_a, sin_a) * scale
    qf = _dot(_rms(cq, qn_ref[...]).astype(BF16), wq_ref[...])
    ckvn = _rms(ckv, kvn_ref[...]).astype(BF16)
    kn = _dot(ckvn, wk_ref[...])
    v_ref[...] = _dot(ckvn, wv_ref[...]).astype(BF16)
    k_rope = (kr2 * cos_a + krs2 * sin_a).astype(BF16)
    for h in range(N_MLA_HEADS):
        o = h * QK_SLAB
        q_ref[:, o:o + LANES] = (qf[:, o:o + LANES] * scale).astype(BF16)
        q_ref[:, o + LANES:o + QK_SLAB] = (qf[:, o + LANES:o + QK_SLAB] * rope_mul).astype(BF16)
        k_ref[:, o:o + LANES] = kn[:, h * LANES:(h + 1) * LANES].astype(BF16)
        k_ref[:, o + LANES:o + QK_SLAB] = k_rope


def _in_proj(x2, pos2, attn_norm, w_in, b_gates, q_norm, w_q_up, kv_norm, w_kv_up):
    n, d = x2.shape
    tm = ROW_TILE
    half = QK_ROPE_DIM // 2
    swap = jnp.concatenate([jnp.arange(half, QK_ROPE_DIM), jnp.arange(0, half)])
    w_kr = w_in[:, OFF_KR:OFF_KR + QK_ROPE_DIM]
    w_krs = w_kr[:, swap]
    wm = w_in[:, :OFF_G].astype(BF16)
    wr = jnp.concatenate([w_in[:, OFF_CQ:OFF_KR], w_kr, w_kr, w_krs, w_krs], axis=1).astype(BF16)
    wgt = w_in[:, OFF_G:OFF_CQ].T.astype(BF16)
    bgt = b_gates.reshape(N_GATES, 1)
    wq4 = w_q_up.reshape(Q_LORA, N_MLA_HEADS, QK_NOPE_DIM + QK_ROPE_DIM)
    wq_pe = wq4[:, :, QK_NOPE_DIM:]
    wq = jnp.concatenate([wq4, wq_pe[:, :, swap]], axis=2).reshape(Q_LORA, N_MLA_HEADS * QK_SLAB).astype(BF16)
    wkv4 = w_kv_up.reshape(KV_LORA, N_MLA_HEADS, QK_NOPE_DIM + V_HEAD_DIM)
    wk = wkv4[:, :, :QK_NOPE_DIM].reshape(KV_LORA, N_MLA_HEADS * QK_NOPE_DIM).astype(BF16)
    wv = wkv4[:, :, QK_NOPE_DIM:].reshape(KV_LORA, D_MLA).astype(BF16)
    freqs = ROPE_THETA ** (-jnp.arange(0, QK_ROPE_DIM, 2, dtype=F32) / QK_ROPE_DIM)
    freq_l = jnp.tile(freqs, LANES // half).reshape(1, LANES)
    sgn_l = jnp.tile(jnp.concatenate([-jnp.ones((half,), F32), jnp.ones((half,), F32)]),
                     LANES // QK_ROPE_DIM).reshape(1, LANES)
    full = lambda arr: pl.BlockSpec(arr.shape, lambda i: (0,) * arr.ndim)
    rows = lambda w: pl.BlockSpec((tm, w), lambda i: (i, 0))
    consts = [attn_norm.reshape(1, d), wm, wr, wgt, bgt, q_norm.reshape(1, Q_LORA), wq,
              kv_norm.reshape(1, KV_LORA), wk, wv, freq_l, sgn_l]
    return pl.pallas_call(
        _in_proj_kernel,
        grid=(n // tm,),
        in_specs=[rows(d), rows(1)] + [full(c) for c in consts],
        out_specs=[rows(OFF_G), pl.BlockSpec((N_GATES, tm), lambda i: (0, i)),
                   rows(N_MLA_HEADS * QK_SLAB), rows(N_MLA_HEADS * QK_SLAB), rows(D_MLA)],
        out_shape=[jax.ShapeDtypeStruct((n, OFF_G), F32), jax.ShapeDtypeStruct((N_GATES, n), F32),
                   jax.ShapeDtypeStruct((n, N_MLA_HEADS * QK_SLAB), BF16),
                   jax.ShapeDtypeStruct((n, N_MLA_HEADS * QK_SLAB), BF16),
                   jax.ShapeDtypeStruct((n, D_MLA), BF16)],
        compiler_params=_params("parallel"),
        name="in_proj",
    )(x2, pos2, *consts)


def _mlstm_kernel(q_ref, k_ref, v_ref, o_ref, g_ref, cwq_ref, cwk_ref, cbq_ref, cbk_ref, nrm_ref,
                  y_ref,
                  qc_ref, kc_ref, va_ref, cst_ref, b_ref, e_ref, r_ref, ew_ref, mw_ref, bt_ref, mp_ref):
    L = MLSTM_CHUNK
    dh = MLSTM_HEAD_DIM
    nc = q_ref.shape[1] // L
    s_len = q_ref.shape[1]
    h = pl.program_id(1)

    for d in range(2):
        ig = g_ref[2 * d * N_MLSTM_HEADS + h, 0]
        fg = g_ref[(2 * d + 1) * N_MLSTM_HEADS + h, 0]
        b = _cumsum_lanes(_log_sigmoid(fg), reverse=(d == 1))
        btot = b[:, L - 1:L] if d == 0 else b[:, 0:1]
        r = ig - b
        w = btot + r
        mw = jnp.max(w, axis=-1, keepdims=True)
        b_ref[d] = b
        e_ref[d] = b + _cummax_lanes(r, reverse=(d == 1))
        r_ref[d] = r
        ew_ref[d] = jnp.exp(w - mw)
        mw_ref[d] = jnp.broadcast_to(mw, (nc, L))
        bt_ref[d] = jnp.broadcast_to(btot, (nc, L))

    row = lax.broadcasted_iota(I32, (L, dh), 0)

    def conv_silu(ref, cw_ref, cb_ref, c):
        start = pl.multiple_of(c * L, L)
        x = ref[0, pl.ds(start, L), :]
        prev_row = jnp.where(c > 0, ref[0, pl.ds(jnp.maximum(start - 1, 0), 1), :], 0.0)
        next_row = jnp.where(c < nc - 1, ref[0, pl.ds(jnp.minimum(start + L, s_len - 1), 1), :], 0.0)
        x_prev = jnp.where(row == 0, prev_row, pltpu.roll(x, 1, 0))
        x_next = jnp.where(row == L - 1, next_row, pltpu.roll(x, L - 1, 0))
        y = cw_ref[0:1, :] * x_prev + cw_ref[1:2, :] * x + cw_ref[2:3, :] * x_next + cb_ref[...]
        return y * jax.nn.sigmoid(y)

    ones_blk = jnp.ones((L, dh), BF16)

    def pass1(c, carry):
        start = pl.multiple_of(c * L, L)
        qc_ref[pl.ds(start, L), :] = conv_silu(q_ref, cwq_ref, cbq_ref, c).astype(BF16)
        kk = conv_silu(k_ref, cwk_ref, cbk_ref, c) * (dh ** -0.5)
        kc_ref[pl.ds(start, L), :] = kk.astype(BF16)
        va = jnp.concatenate([v_ref[0, pl.ds(start, L), :].astype(BF16), ones_blk], axis=1)
        va_ref[pl.ds(start, L), :] = va
        kt = kk.T
        for d in range(2):
            kw_t = (kt * ew_ref[d, pl.ds(c, 1), :]).astype(BF16)
            cst_ref[d, c] = _dot(kw_t, va)
        return carry

    lax.fori_loop(0, nc, pass1, 0, unroll=CHUNK_UNROLL)

    def scan(i, carry):
        out = []
        for d in range(2):
            st, m = carry[d]
            c = i if d == 0 else nc - 1 - i
            mw = mw_ref[d, pl.ds(c, 1), :]
            bt = bt_ref[d, pl.ds(c, 1), :]
            m_new = jnp.maximum(bt + m, mw)
            a = jnp.exp(bt + m - m_new)[:, 0:1]
            cc = jnp.exp(mw - m_new)[:, 0:1]
            loc = cst_ref[d, c]
            cst_ref[d, c] = st
            mp_ref[d, pl.ds(c, 1), :] = m
            out.append((a * st + cc * loc, m_new))
        return tuple(out)

    init = (jnp.zeros((dh, 2 * dh), F32), jnp.zeros((1, L), F32))
    lax.fori_loop(0, nc, scan, (init, init))

    ti = lax.broadcasted_iota(I32, (L, L), 0)
    si = lax.broadcasted_iota(I32, (L, L), 1)
    masks = (si <= ti, si >= ti)

    def pass3(c, carry):
        start = pl.multiple_of(c * L, L)
        q = qc_ref[pl.ds(start, L), :]
        k = kc_ref[pl.ds(start, L), :]
        va = va_ref[pl.ds(start, L), :]
        qk = _dot_nt(q, k)
        hsum = jnp.zeros((L, dh), F32)
        for d in range(2):
            bmat = jnp.broadcast_to(b_ref[d, pl.ds(c, 1), :], (L, L)).T
            emat = jnp.broadcast_to(e_ref[d, pl.ds(c, 1), :], (L, L)).T
            dmat = jnp.where(masks[d], bmat + r_ref[d, pl.ds(c, 1), :], -jnp.inf)
            inter = bmat + mp_ref[d, pl.ds(c, 1), :]
            m_t = jnp.maximum(inter, emat)
            sc = qk * jnp.exp(dmat - m_t)
            a = jnp.exp(inter - m_t)
            intra = _dot(sc.astype(BF16), va)
            cross = _dot(q, cst_ref[d, c].astype(BF16))
            num = intra[:, :dh] + a * cross[:, :dh]
            den = intra[:, dh:] + a * cross[:, dh:]
            hsum = hsum + num / jnp.maximum(jnp.abs(den), jnp.exp(-m_t))
        hn = _rms(hsum, nrm_ref[...])
        y_ref[0, pl.ds(start, L), :] = hn * jax.nn.sigmoid(o_ref[0, pl.ds(start, L), :])
        return carry

    lax.fori_loop(0, nc, pass3, 0, unroll=CHUNK_UNROLL)


def _mlstm(um3, gt4, conv_w, conv_b, mlstm_norm):
    bsz, s, _ = um3.shape
    H, dh, L = N_MLSTM_HEADS, MLSTM_HEAD_DIM, MLSTM_CHUNK
    nc = s // L
    col = lambda off: pl.BlockSpec((1, s, dh), lambda b, h: (b, 0, off + h))
    vec = lambda rows, off: pl.BlockSpec((rows, dh), lambda b, h: (0, off + h))
    cb = conv_b.reshape(1, 2 * D_MLSTM)
    return pl.pallas_call(
        _mlstm_kernel,
        grid=(bsz, H),
        in_specs=[col(0), col(H), col(2 * H), col(3 * H),
                  pl.BlockSpec((N_GATES, 1, nc, L), lambda b, h: (0, b, 0, 0)),
                  vec(3, 0), vec(3, H), vec(1, 0), vec(1, H), vec(1, 0)],
        out_specs=pl.BlockSpec((1, s, dh), lambda b, h: (b, 0, h)),
        out_shape=jax.ShapeDtypeStruct((bsz, s, D_MLSTM), F32),
        scratch_shapes=[pltpu.VMEM((s, dh), BF16), pltpu.VMEM((s, dh), BF16), pltpu.VMEM((s, 2 * dh), BF16),
                        pltpu.VMEM((2, nc, dh, 2 * dh), F32)]
                       + [pltpu.VMEM((2, nc, L), F32) for _ in range(7)],
        compiler_params=_params("parallel", "parallel"),
        name="mlstm",
    )(um3, um3, um3, um3, gt4, conv_w, conv_w, cb, cb, mlstm_norm.reshape(1, D_MLSTM))


def _attn_kernel(q_ref, k_ref, v_ref, o_ref):
    tq = q_ref.shape[1]
    q = q_ref[0]
    m = jnp.full((tq, 1), -jnp.inf, F32)
    l = jnp.zeros((tq, 1), F32)
    acc = jnp.zeros((tq, V_HEAD_DIM), F32)
    chunk = min(KV_CHUNK, k_ref.shape[1])
    for c in range(k_ref.shape[1] // chunk):
        keys = slice(c * chunk, (c + 1) * chunk)
        s = _dot_nt(q, k_ref[0, keys, :])
        m_new = jnp.maximum(m, jnp.max(s, axis=-1, keepdims=True))
        alpha = jnp.exp2(m - m_new)
        p = jnp.exp2(s - m_new)
        l = alpha * l + jnp.sum(p, axis=-1, keepdims=True)
        acc = alpha * acc + _dot(p.astype(BF16), v_ref[0, keys, :])
        m = m_new
    o_ref[0] = acc / l


def _attention(q3, k3, v3):
    bsz, s, _ = q3.shape
    tq = min(Q_TILE, s)
    return pl.pallas_call(
        _attn_kernel,
        grid=(bsz, N_MLA_HEADS, s // tq),
        in_specs=[pl.BlockSpec((1, tq, QK_SLAB), lambda b, h, i: (b, i, h)),
                  pl.BlockSpec((1, s, QK_SLAB), lambda b, h, i: (b, 0, h)),
                  pl.BlockSpec((1, s, V_HEAD_DIM), lambda b, h, i: (b, 0, h))],
        out_specs=pl.BlockSpec((1, tq, V_HEAD_DIM), lambda b, h, i: (b, i, h)),
        out_shape=jax.ShapeDtypeStruct((bsz, s, D_MLA), F32),
        compiler_params=_params("parallel", "parallel", "parallel"),
        name="attention",
    )(q3, k3, v3)


def _out_route_kernel(ym_ref, ya_ref, x_ref, mn_ref, wom_ref, woa_ref, fn_ref, wr_ref, br_ref,
                      h_ref, xn_ref, gate_ref, slot_ref, slot_t_ref, seg_ref, size_ref,
                      carry_ref):
    i = pl.program_id(0)
    tm = x_ref.shape[0]

    @pl.when(i == 0)
    def _():
        carry_ref[...] = jnp.zeros_like(carry_ref)

    ya = _rms(ya_ref[...], mn_ref[...])
    h1 = x_ref[...] + _dot(ym_ref[...].astype(BF16), wom_ref[...]) + _dot(ya.astype(BF16), woa_ref[...])
    h_ref[...] = h1
    xn = _rms(h1, fn_ref[...])
    xn_ref[...] = xn.astype(BF16)
    logits = lax.dot_general(wr_ref[...], xn, (((1,), (1,)), ((), ())), preferred_element_type=F32,
                             precision=lax.Precision.HIGHEST) + br_ref[...]
    erow = lax.broadcasted_iota(I32, logits.shape, 0)
    work = logits
    vals, hots = [], []
    for k in range(TOP_K):
        mx = jnp.max(work, axis=0, keepdims=True)
        idx = jnp.min(jnp.where(work == mx, erow, N_EXPERTS), axis=0, keepdims=True)
        hot = erow == idx
        work = jnp.where(hot, -jnp.inf, work)
        vals.append(mx)
        hots.append(hot)
    exps = [jnp.exp(v - vals[0]) for v in vals]
    tot = exps[0] + exps[1] + exps[2] + exps[3]
    multi = (hots[0] | hots[1] | hots[2] | hots[3]).astype(BF16)
    ti = lax.broadcasted_iota(I32, (tm, tm), 0)
    tj = lax.broadcasted_iota(I32, (tm, tm), 1)
    local_rank = _dot(multi, (ti < tj).astype(BF16))
    multi_rows = jnp.concatenate([multi, jnp.zeros((LANES - N_EXPERTS, tm), BF16)], axis=0)
    count = _dot_nt(jnp.ones((SUBLANES, tm), BF16), multi_rows)[0:1]
    padded = jnp.ceil(count * (1.0 / SEG_ALIGN)) * SEG_ALIGN
    local_start = _cumsum_lanes(padded, reverse=False) - padded
    starts8 = jnp.broadcast_to(local_start, (SUBLANES, LANES))[:, :N_EXPERTS].astype(BF16)
    krow = lax.broadcasted_iota(I32, (SUBLANES, tm), 0)
    packed = jnp.zeros((SUBLANES, tm), F32)
    for k in range(TOP_K):
        sk = (jnp.sum(jnp.where(hots[k], local_rank, 0.0), axis=0, keepdims=True)
              + _dot(starts8, hots[k].astype(BF16))[0:1])
        packed = jnp.where(krow == k, exps[k] / tot, jnp.where(krow == TOP_K + k, sk, packed))
    slot_t_ref[0] = jnp.where(krow < TOP_K, pltpu.roll(packed, TOP_K, 0), 0.0).astype(I32)
    cols = packed.T
    gate_ref[...] = cols[:, :TOP_K]
    slot_ref[...] = cols[:, TOP_K:2 * TOP_K].astype(I32)
    srow = lax.broadcasted_iota(I32, (SUBLANES, LANES), 0)
    seg = jnp.where(srow == 0, padded, jnp.where(srow == 1, local_start, jnp.where(srow == 2, carry_ref[...], 0.0)))
    seg_ref[0] = seg.astype(I32)
    carry_ref[...] += padded
    size_ref[...] = carry_ref[...].astype(I32)


def _out_route(ym2, ya2, x2, mla_norm, w_out, ffn_norm, w_router, b_router):
    n, d = x2.shape
    tm = ROW_TILE
    nt = n // tm
    wom = w_out[:D_MLSTM].astype(BF16)
    woa = w_out[D_MLSTM:].astype(BF16)
    wr = w_router.T
    br = b_router.reshape(N_EXPERTS, 1)
    full = lambda arr: pl.BlockSpec(arr.shape, lambda i: (0,) * arr.ndim)
    rows = lambda w: pl.BlockSpec((tm, w), lambda i: (i, 0))
    consts = [mla_norm.reshape(1, D_MLA), wom, woa, ffn_norm.reshape(1, d), wr, br]
    return pl.pallas_call(
        _out_route_kernel,
        grid=(nt,),
        in_specs=[rows(D_MLSTM), rows(D_MLA), rows(d)] + [full(c) for c in consts],
        out_specs=[rows(d), rows(d), rows(TOP_K), rows(TOP_K),
                   pl.BlockSpec((1, SUBLANES, tm), lambda i: (i, 0, 0)),
                   pl.BlockSpec((1, SUBLANES, LANES), lambda i: (i, 0, 0)),
                   pl.BlockSpec((1, LANES), lambda i: (0, 0))],
        out_shape=[jax.ShapeDtypeStruct((n, d), F32), jax.ShapeDtypeStruct((n, d), BF16),
                   jax.ShapeDtypeStruct((n, TOP_K), F32), jax.ShapeDtypeStruct((n, TOP_K), I32),
                   jax.ShapeDtypeStruct((nt, SUBLANES, tm), I32), jax.ShapeDtypeStruct((nt, SUBLANES, LANES), I32),
                   jax.ShapeDtypeStruct((1, LANES), I32)],
        scratch_shapes=[pltpu.VMEM((1, LANES), F32)],
        compiler_params=_params("arbitrary"),
        name="out_route",
    )(ym2, ya2, x2, *consts)


def _segment_copies(seg_ref, starts_ref, local_ref, global_ref, sem, to_global, wait):
    def copy(src, dst, rows):
        loc = local_ref.at[pl.ds(pl.multiple_of(src, SEG_ALIGN), rows)]
        glo = global_ref.at[pl.ds(pl.multiple_of(dst, SEG_ALIGN), rows)]
        cp = pltpu.make_async_copy(loc, glo, sem) if to_global else pltpu.make_async_copy(glo, loc, sem)
        if wait:
            cp.wait()
        else:
            cp.start()

    def per_expert(e, carry):
        size = seg_ref[0, 0, e]
        src = seg_ref[0, 1, e]
        dst = starts_ref[e] + seg_ref[0, 2, e]
        n_chunks = lax.shift_right_logical(size, SEG_CHUNK_SHIFT)

        def whole_chunk(j, c2):
            off = lax.shift_left(j, SEG_CHUNK_SHIFT)
            copy(src + off, dst + off, SEG_CHUNK)
            return c2

        lax.fori_loop(0, n_chunks, whole_chunk, 0)
        off = lax.shift_left(n_chunks, SEG_CHUNK_SHIFT)
        rows = SEG_CHUNK // 2
        while rows >= SEG_ALIGN:
            @pl.when((size & rows) != 0)
            def _(off=off, rows=rows):
                copy(src + off, dst + off, rows)

            off = off + (size & rows)
            rows //= 2
        return carry

    lax.fori_loop(0, N_EXPERTS, per_expert, 0)


def _dispatch_kernel(starts_ref, seg_ref, segp_ref, slot_t_ref, xn_ref, xs_ref, sort_ref, zero_ref, sem, zsem):
    i = pl.program_id(0)
    tm = xn_ref.shape[0]
    tmx = zero_ref.shape[0]
    cap = sort_ref.shape[1]

    @pl.when(i == 0)
    def _():
        zero_ref[...] = jnp.zeros_like(zero_ref)
        n_tail = (xs_ref.shape[0] - starts_ref[N_EXPERTS]) // tmx

        def clear_tile(row, wait):
            cp = pltpu.make_async_copy(zero_ref, xs_ref.at[pl.ds(pl.multiple_of(row, tmx), tmx)], zsem)
            if wait:
                cp.wait()
            else:
                cp.start()

        for wait in (False, True):
            def clear_group(e, carry, wait=wait):
                hi = starts_ref[e + 1]

                @pl.when(hi > starts_ref[e])
                def _():
                    clear_tile(hi - tmx, wait)

                return carry

            def clear_tail(t, carry, wait=wait):
                clear_tile(starts_ref[N_EXPERTS] + t * tmx, wait)
                return carry

            lax.fori_loop(0, N_EXPERTS, clear_group, 0)
            lax.fori_loop(0, n_tail, clear_tail, 0)

    slot = i % 2
    pos = lax.broadcasted_iota(I32, (cap, tm), 0)
    hit = pos == slot_t_ref[0, 0:1, :]
    for k in range(1, TOP_K):
        hit = hit | (pos == slot_t_ref[0, k:k + 1, :])
    sort_ref[slot] = _dot(hit.astype(BF16), xn_ref[...]).astype(BF16)
    _segment_copies(seg_ref, starts_ref, sort_ref.at[slot], xs_ref, sem.at[slot], True, False)

    @pl.when(i >= 1)
    def _():
        _segment_copies(segp_ref, starts_ref, sort_ref.at[1 - slot], xs_ref, sem.at[1 - slot], True, True)

    @pl.when(i == pl.num_programs(0) - 1)
    def _():
        _segment_copies(seg_ref, starts_ref, sort_ref.at[slot], xs_ref, sem.at[slot], True, True)


def _dispatch(starts, seg, slot_t, xn, n_rows):
    n, d = xn.shape
    tm = ROW_TILE
    nt = n // tm
    smem = lambda f: pl.BlockSpec((1, SUBLANES, LANES), f, memory_space=pltpu.SMEM)
    any_spec = pl.BlockSpec(memory_space=pl.ANY)
    return pl.pallas_call(
        _dispatch_kernel,
        grid_spec=pltpu.PrefetchScalarGridSpec(
            num_scalar_prefetch=1,
            grid=(nt,),
            in_specs=[smem(lambda i, *_: (i, 0, 0)), smem(lambda i, *_: (jnp.maximum(i - 1, 0), 0, 0)),
                      pl.BlockSpec((1, SUBLANES, tm), lambda i, *_: (i, 0, 0)),
                      pl.BlockSpec((tm, d), lambda i, *_: (i, 0))],
            out_specs=any_spec,
            scratch_shapes=[pltpu.VMEM((2, SORT_ROWS, d), xn.dtype), pltpu.VMEM((EXPERT_TILE, d), xn.dtype),
                            pltpu.SemaphoreType.DMA((2,)), pltpu.SemaphoreType.DMA],
        ),
        out_shape=jax.ShapeDtypeStruct((n_rows, d), xn.dtype),
        compiler_params=_params("arbitrary"),
        name="dispatch",
    )(starts, seg, seg, slot_t, xn)


def _experts_kernel(te_ref, tb_ref, tv_ref, xs_ref, wgu_ref, bgu_ref, wd_ref, bd_ref, out_ref,
                    wgu_bf, wd_bf):
    i = pl.program_id(0)
    de = wd_ref.shape[1]
    prev = te_ref[jnp.maximum(i - 1, 0)]

    @pl.when(jnp.logical_or(i == 0, te_ref[i] != prev))
    def _():
        wgu_bf[...] = wgu_ref[0].astype(BF16)
        wd_bf[...] = wd_ref[0].astype(BF16)

    @pl.when(tv_ref[i] == 1)
    def _():
        x = xs_ref[...]
        ch = 512
        acc = jnp.zeros(out_ref.shape, F32)
        for j in range(de // ch):
            g = _dot(x, wgu_bf[:, j * ch:(j + 1) * ch]) + bgu_ref[0, :, j * ch:(j + 1) * ch]
            u = _dot(x, wgu_bf[:, de + j * ch:de + (j + 1) * ch]) + bgu_ref[0, :, de + j * ch:de + (j + 1) * ch]
            g = jnp.minimum(g, SWIGLU_LIMIT)
            u = jnp.clip(u, -SWIGLU_LIMIT, SWIGLU_LIMIT)
            hm = (u + 1.0) * (g * jax.nn.sigmoid(g * SWIGLU_ALPHA))
            acc = acc + _dot(hm.astype(BF16), wd_bf[j * ch:(j + 1) * ch, :])
        out_ref[...] = (acc + bd_ref[0]).astype(out_ref.dtype)

    @pl.when(tv_ref[i] == 0)
    def _():
        out_ref[...] = jnp.zeros_like(out_ref)


def _experts(tile_e, tile_b, tile_v, xs, w_gate_up, b_gate_up, w_down, b_down):
    n_rows, d = xs.shape
    tmx = EXPERT_TILE
    ne, _, de2 = w_gate_up.shape
    de = de2 // 2
    return pl.pallas_call(
        _experts_kernel,
        grid_spec=pltpu.PrefetchScalarGridSpec(
            num_scalar_prefetch=3,
            grid=(n_rows // tmx,),
            in_specs=[pl.BlockSpec((tmx, d), lambda i, te, tb, tv: (tb[i], 0)),
                      pl.BlockSpec((1, d, de2), lambda i, te, tb, tv: (te[i], 0, 0)),
                      pl.BlockSpec((1, 1, de2), lambda i, te, tb, tv: (te[i], 0, 0)),
                      pl.BlockSpec((1, de, d), lambda i, te, tb, tv: (te[i], 0, 0)),
                      pl.BlockSpec((1, 1, d), lambda i, te, tb, tv: (te[i], 0, 0))],
            out_specs=pl.BlockSpec((tmx, d), lambda i, te, tb, tv: (i, 0)),
            scratch_shapes=[pltpu.VMEM((d, de2), BF16), pltpu.VMEM((de, d), BF16)],
        ),
        out_shape=jax.ShapeDtypeStruct((n_rows, d), BF16),
        compiler_params=_params("arbitrary"),
        name="experts",
    )(tile_e, tile_b, tile_v, xs, w_gate_up, b_gate_up.reshape(ne, 1, de2), w_down, b_down.reshape(ne, 1, d))


def _combine_kernel(starts_ref, seg_ref, segn_ref, h_ref, gate_ref, slot_ref, p_ref, ys_ref, pn_ref, wg_ref,
                    wp_ref, fn_ref, out_ref, ybuf, sem, *, final):
    i = pl.program_id(0)
    nt = pl.num_programs(0)
    tm = h_ref.shape[0]
    cap = ybuf.shape[1]

    @pl.when(i == 0)
    def _():
        ybuf[...] = jnp.zeros_like(ybuf)
        _segment_copies(seg_ref, starts_ref, ybuf.at[0], ys_ref, sem.at[0], False, False)

    slot = i % 2

    @pl.when(i + 1 < nt)
    def _():
        _segment_copies(segn_ref, starts_ref, ybuf.at[1 - slot], ys_ref, sem.at[1 - slot], False, False)

    _segment_copies(seg_ref, starts_ref, ybuf.at[slot], ys_ref, sem.at[slot], False, True)

    pos = lax.broadcasted_iota(I32, (tm, cap), 1)
    gate = gate_ref[...]
    weights = jnp.zeros((tm, cap), F32)
    for k in range(TOP_K):
        weights = jnp.where(pos == slot_ref[:, k:k + 1], gate[:, k:k + 1], weights)
    h2 = h_ref[...] + _dot(weights.astype(BF16), ybuf[slot])
    hn = _rms(h2, pn_ref[...]).astype(BF16)
    sg = jax.nn.sigmoid(_dot(hn, wg_ref[...]))
    h3 = h2 + sg * _dot(p_ref[...].astype(BF16), wp_ref[...])
    out_ref[...] = _rms(h3, fn_ref[...]) if final else h3


def _combine(starts, seg, h1, gate, slot, p2, ys, ple_norm, w_ple_gate, w_ple_proj, final_norm, final):
    n, d = h1.shape
    tm = ROW_TILE
    nt = n // tm
    wg = w_ple_gate.astype(BF16)
    wp = w_ple_proj.astype(BF16)
    smem = lambda f: pl.BlockSpec((1, SUBLANES, LANES), f, memory_space=pltpu.SMEM)
    full = lambda arr: pl.BlockSpec(arr.shape, lambda i, *_: (0,) * arr.ndim)
    rows = lambda w: pl.BlockSpec((tm, w), lambda i, *_: (i, 0))
    consts = [ple_norm.reshape(1, d), wg, wp, final_norm.reshape(1, d)]
    return pl.pallas_call(
        functools.partial(_combine_kernel, final=final),
        grid_spec=pltpu.PrefetchScalarGridSpec(
            num_scalar_prefetch=1,
            grid=(nt,),
            in_specs=[smem(lambda i, *_: (i, 0, 0)), smem(lambda i, *_: (jnp.minimum(i + 1, nt - 1), 0, 0)),
                      rows(d), rows(TOP_K), rows(TOP_K), rows(p2.shape[1]), pl.BlockSpec(memory_space=pl.ANY)]
                     + [full(c) for c in consts],
            out_specs=rows(d),
            scratch_shapes=[pltpu.VMEM((2, SORT_ROWS, d), ys.dtype), pltpu.SemaphoreType.DMA((2,))],
        ),
        out_shape=jax.ShapeDtypeStruct((n, d), F32),
        compiler_params=_params("arbitrary"),
        name="combine",
    )(starts, seg, seg, h1, gate, slot, p2, ys, *consts)


def _route_tables(sizes, n_tiles):
    tmx = EXPERT_TILE
    tile_end = jnp.cumsum((sizes + tmx - 1) // tmx)
    starts = jnp.concatenate([jnp.zeros((1,), I32), tile_end * tmx]).astype(I32)
    n_valid = tile_end[-1]
    t = jnp.arange(n_tiles, dtype=I32)
    tb = jnp.minimum(t, n_valid - 1).astype(I32)
    te = jnp.sum(tile_end[None, :] <= tb[:, None], axis=1).astype(I32)
    tv = (t < n_valid).astype(I32)
    return starts, te, tb, tv


def kernel(x, p, positions, attn_norm, w_in, b_gates, conv_w, conv_b, mlstm_norm, q_norm, w_q_up, kv_norm, w_kv_up, mla_norm, w_out, ffn_norm, w_router, b_router, w_gate_up, b_gate_up, w_down, b_down, ple_norm, w_ple_gate, w_ple_proj, final_norm):
    bsz, s, d = x.shape
    n = bsz * s
    depth = p.shape[0]
    nc = s // MLSTM_CHUNK
    max_rows = n * TOP_K + (n // ROW_TILE) * N_EXPERTS * (SEG_ALIGN - 1) + N_EXPERTS * (EXPERT_TILE - 1)
    n_tiles = max_rows // EXPERT_TILE
    pos2 = positions.reshape(n, 1)
    h = x.reshape(n, d)
    for i in range(depth):
        um, gt, q, k, v = _in_proj(h, pos2, attn_norm[i], w_in[i], b_gates[i], q_norm[i], w_q_up[i],
                                   kv_norm[i], w_kv_up[i])
        ym = _mlstm(um.reshape(bsz, s, -1), gt.reshape(N_GATES, bsz, nc, MLSTM_CHUNK), conv_w[i], conv_b[i],
                    mlstm_norm[i])
        ya = _attention(q.reshape(bsz, s, -1), k.reshape(bsz, s, -1), v.reshape(bsz, s, -1))
        h1, xn, gate, slot, slot_t, seg, sizes = _out_route(ym.reshape(n, -1), ya.reshape(n, -1), h, mla_norm[i],
                                                            w_out[i], ffn_norm[i], w_router[i], b_router[i])
        starts, te, tb, tv = _route_tables(sizes[0, :N_EXPERTS], n_tiles)
        xs = _dispatch(starts, seg, slot_t, xn, n_tiles * EXPERT_TILE)
        ys = _experts(te, tb, tv, xs, w_gate_up[i], b_gate_up[i], w_down[i], b_down[i])
        h = _combine(starts, seg, h1, gate, slot, p[i].reshape(n, -1), ys, ple_norm[i], w_ple_gate[i],
                     w_ple_proj[i], final_norm, final=(i == depth - 1))
    return h.reshape(bsz, s, d)
```

```python
import functools

import jax
import jax.numpy as jnp
from jax import lax
from jax.experimental import pallas as pl
from jax.experimental.pallas import tpu as pltpu

F32 = jnp.float32
BF16 = jnp.bfloat16
I32 = jnp.int32
U32 = jnp.uint32

N_MLSTM_HEADS = 4
MLSTM_HEAD_DIM = 128
D_MLSTM = N_MLSTM_HEADS * MLSTM_HEAD_DIM
MLSTM_CHUNK = 128
N_MLA_HEADS = 4
QK_NOPE_DIM = 128
QK_ROPE_DIM = 64
V_HEAD_DIM = 128
D_MLA = N_MLA_HEADS * V_HEAD_DIM
Q_LORA = 256
KV_LORA = 128
ROPE_THETA = 10000.0
N_EXPERTS = 32
TOP_K = 4
SWIGLU_LIMIT = 7.0
SWIGLU_ALPHA = 1.702
EPS = 1e-6
N_GATES = 4 * N_MLSTM_HEADS
OFF_G = 4 * D_MLSTM
OFF_CQ = OFF_G + N_GATES
OFF_CKV = OFF_CQ + Q_LORA
OFF_KR = OFF_CKV + KV_LORA

LANES = 128
SUBLANES = 8
QK_SLAB = 2 * LANES
VMEM_LIMIT_BYTES = 56 * 1024 * 1024
LOG2_E = 1.4426950408889634

ROW_TILE = 512
Q_TILE = 1024
KV_CHUNK = 1024
EXPERT_TILE = 512
CHUNK_UNROLL = 4
SEG_ALIGN = SUBLANES
SORT_ROWS = ROW_TILE * TOP_K + N_EXPERTS * SEG_ALIGN


def _dot(a, b):
    return jnp.dot(a, b, preferred_element_type=F32)


def _dot_nt(a, b):
    return lax.dot_general(a, b, (((1,), (1,)), ((), ())), preferred_element_type=F32)


def _rms(x, g):
    return x * lax.rsqrt(jnp.mean(x * x, axis=-1, keepdims=True) + EPS) * g


def _log_sigmoid(x):
    return jnp.minimum(x, 0.0) - jnp.log(1.0 + jnp.exp(-jnp.abs(x)))


def _scan_lanes(x, reverse, op, identity):
    n = x.shape[-1]
    lane = lax.broadcasted_iota(I32, x.shape, x.ndim - 1)
    sh = 1
    while sh < n:
        if reverse:
            x = op(x, jnp.where(lane < n - sh, pltpu.roll(x, n - sh, x.ndim - 1), identity))
        else:
            x = op(x, jnp.where(lane >= sh, pltpu.roll(x, sh, x.ndim - 1), identity))
        sh *= 2
    return x


def _cumsum_lanes(x, reverse):
    return _scan_lanes(x, reverse, jnp.add, 0.0)


def _cummax_lanes(x, reverse):
    return _scan_lanes(x, reverse, jnp.maximum, -jnp.inf)


def _pack_halves(x):
    w = x.shape[1] // 2
    lo = lax.shift_right_logical(lax.bitcast_convert_type(x[:, :w], U32), jnp.uint32(16))
    hi = lax.bitcast_convert_type(x[:, w:], U32)
    return lo | hi


def _unpack_halves(words):
    lo = lax.bitcast_convert_type(lax.shift_left(words, jnp.uint32(16)), F32)
    hi = lax.bitcast_convert_type(words & jnp.uint32(0xFFFF0000), F32)
    return jnp.concatenate([lo, hi], axis=1).astype(BF16)


def _params(*sem):
    return pltpu.CompilerParams(dimension_semantics=sem, vmem_limit_bytes=VMEM_LIMIT_BYTES)


def _in_proj_kernel(x_ref, pos_ref, an_ref, wm_ref, wr_ref, wgt_ref, bgt_ref, qn_ref, wq_ref, kvn_ref,
                    wk_ref, wv_ref, freq_ref, sgn_ref,
                    um_ref, gt_ref, q_ref, k_ref, v_ref):
    a = _rms(x_ref[...], an_ref[...]).astype(BF16)
    um_ref[...] = _dot(a, wm_ref[...])
    gt_ref[...] = _dot_nt(wgt_ref[...], a) + bgt_ref[...]
    rest = _dot(a, wr_ref[...])
    cq = rest[:, :Q_LORA]
    ckv = rest[:, Q_LORA:Q_LORA + KV_LORA]
    kr2 = rest[:, Q_LORA + KV_LORA:Q_LORA + KV_LORA + LANES]
    krs2 = rest[:, Q_LORA + KV_LORA + LANES:]
    ang = pos_ref[...].astype(F32) * freq_ref[...]
    cos_a = jnp.cos(ang)
    sin_a = jnp.sin(ang) * sgn_ref[...]
    scale = (QK_NOPE_DIM + QK_ROPE_DIM) ** -0.5 * LOG2_E
    lane = lax.broadcasted_iota(I32, ang.shape, 1)
    rope_mul = jnp.where(lane < QK_ROPE_DIM, cos_a, sin_a) * scale
    qf = _dot(_rms(cq, qn_ref[...]).astype(BF16), wq_ref[...])
    ckvn = _rms(ckv, kvn_ref[...]).astype(BF16)
    kn = _dot(ckvn, wk_ref[...])
    v_ref[...] = _dot(ckvn, wv_ref[...]).astype(BF16)
    k_rope = (kr2 * cos_a + krs2 * sin_a).astype(BF16)
    for h in range(N_MLA_HEADS):
        o = h * QK_SLAB
        q_ref[:, o:o + LANES] = (qf[:, o:o + LANES] * scale).astype(BF16)
        q_ref[:, o + LANES:o + QK_SLAB] = (qf[:, o + LANES:o + QK_SLAB] * rope_mul).astype(BF16)
        k_ref[:, o:o + LANES] = kn[:, h * LANES:(h + 1) * LANES].astype(BF16)
        k_ref[:, o + LANES:o + QK_SLAB] = k_rope


def _in_proj(x2, pos2, attn_norm, w_in, b_gates, q_norm, w_q_up, kv_norm, w_kv_up):
    n, d = x2.shape
    tm = ROW_TILE
    half = QK_ROPE_DIM // 2
    swap = jnp.concatenate([jnp.arange(half, QK_ROPE_DIM), jnp.arange(0, half)])
    w_kr = w_in[:, OFF_KR:OFF_KR + QK_ROPE_DIM]
    w_krs = w_kr[:, swap]
    wm = w_in[:, :OFF_G].astype(BF16)
    wr = jnp.concatenate([w_in[:, OFF_CQ:OFF_KR], w_kr, w_kr, w_krs, w_krs], axis=1).astype(BF16)
    wgt = w_in[:, OFF_G:OFF_CQ].T.astype(BF16)
    bgt = b_gates.reshape(N_GATES, 1)
    wq4 = w_q_up.reshape(Q_LORA, N_MLA_HEADS, QK_NOPE_DIM + QK_ROPE_DIM)
    wq_pe = wq4[:, :, QK_NOPE_DIM:]
    wq = jnp.concatenate([wq4, wq_pe[:, :, swap]], axis=2).reshape(Q_LORA, N_MLA_HEADS * QK_SLAB).astype(BF16)
    wkv4 = w_kv_up.reshape(KV_LORA, N_MLA_HEADS, QK_NOPE_DIM + V_HEAD_DIM)
    wk = wkv4[:, :, :QK_NOPE_DIM].reshape(KV_LORA, N_MLA_HEADS * QK_NOPE_DIM).astype(BF16)
    wv = wkv4[:, :, QK_NOPE_DIM:].reshape(KV_LORA, D_MLA).astype(BF16)
    freqs = ROPE_THETA ** (-jnp.arange(0, QK_ROPE_DIM, 2, dtype=F32) / QK_ROPE_DIM)
    freq_l = jnp.tile(freqs, LANES // half).reshape(1, LANES)
    sgn_l = jnp.tile(jnp.concatenate([-jnp.ones((half,), F32), jnp.ones((half,), F32)]),
                     LANES // QK_ROPE_DIM).reshape(1, LANES)
    full = lambda arr: pl.BlockSpec(arr.shape, lambda i: (0,) * arr.ndim)
    rows = lambda w: pl.BlockSpec((tm, w), lambda i: (i, 0))
    consts = [attn_norm.reshape(1, d), wm, wr, wgt, bgt, q_norm.reshape(1, Q_LORA), wq,
              kv_norm.reshape(1, KV_LORA), wk, wv, freq_l, sgn_l]
    return pl.pallas_call(
        _in_proj_kernel,
        grid=(n // tm,),
        in_specs=[rows(d), rows(1)] + [full(c) for c in consts],
        out_specs=[rows(OFF_G), pl.BlockSpec((N_GATES, tm), lambda i: (0, i)),
                   rows(N_MLA_HEADS * QK_SLAB), rows(N_MLA_HEADS * QK_SLAB), rows(D_MLA)],
        out_shape=[jax.ShapeDtypeStruct((n, OFF_G), F32), jax.ShapeDtypeStruct((N_GATES, n), F32),
                   jax.ShapeDtypeStruct((n, N_MLA_HEADS * QK_SLAB), BF16),
                   jax.ShapeDtypeStruct((n, N_MLA_HEADS * QK_SLAB), BF16),
                   jax.ShapeDtypeStruct((n, D_MLA), BF16)],
        compiler_params=_params("parallel"),
        name="in_proj",
    )(x2, pos2, *consts)


def _mlstm_kernel(q_ref, k_ref, v_ref, o_ref, g_ref, cwq_ref, cwk_ref, cbq_ref, cbk_ref, nrm_ref,
                  y_ref,
                  qc_ref, kc_ref, va_ref, cst_ref, b_ref, e_ref, r_ref, ew_ref, mw_ref, bt_ref, mp_ref):
    L = MLSTM_CHUNK
    dh = MLSTM_HEAD_DIM
    nc = q_ref.shape[1] // L
    s_len = q_ref.shape[1]
    h = pl.program_id(1)

    for d in range(2):
        ig = g_ref[2 * d * N_MLSTM_HEADS + h, 0]
        fg = g_ref[(2 * d + 1) * N_MLSTM_HEADS + h, 0]
        b = _cumsum_lanes(_log_sigmoid(fg), reverse=(d == 1))
        btot = b[:, L - 1:L] if d == 0 else b[:, 0:1]
        r = ig - b
        w = btot + r
        mw = jnp.max(w, axis=-1, keepdims=True)
        b_ref[d] = b
        e_ref[d] = b + _cummax_lanes(r, reverse=(d == 1))
        r_ref[d] = r
        ew_ref[d] = jnp.exp(w - mw)
        mw_ref[d] = jnp.broadcast_to(mw, (nc, L))
        bt_ref[d] = jnp.broadcast_to(btot, (nc, L))

    row = lax.broadcasted_iota(I32, (L, dh), 0)

    def conv_silu(ref, cw_ref, cb_ref, c):
        start = pl.multiple_of(c * L, L)
        x = ref[0, pl.ds(start, L), :]
        prev_row = jnp.where(c > 0, ref[0, pl.ds(jnp.maximum(start - 1, 0), 1), :], 0.0)
        next_row = jnp.where(c < nc - 1, ref[0, pl.ds(jnp.minimum(start + L, s_len - 1), 1), :], 0.0)
        x_prev = jnp.where(row == 0, prev_row, pltpu.roll(x, 1, 0))
        x_next = jnp.where(row == L - 1, next_row, pltpu.roll(x, L - 1, 0))
        y = cw_ref[0:1, :] * x_prev + cw_ref[1:2, :] * x + cw_ref[2:3, :] * x_next + cb_ref[...]
        return y * jax.nn.sigmoid(y)

    ones_blk = jnp.ones((L, dh), BF16)

    def pass1(c, carry):
        start = pl.multiple_of(c * L, L)
        qc_ref[pl.ds(start, L), :] = conv_silu(q_ref, cwq_ref, cbq_ref, c).astype(BF16)
        kk = conv_silu(k_ref, cwk_ref, cbk_ref, c) * (dh ** -0.5)
        kc_ref[pl.ds(start, L), :] = kk.astype(BF16)
        va = jnp.concatenate([v_ref[0, pl.ds(start, L), :].astype(BF16), ones_blk], axis=1)
        va_ref[pl.ds(start, L), :] = va
        kt = kk.T
        for d in range(2):
            kw_t = (kt * ew_ref[d, pl.ds(c, 1), :]).astype(BF16)
            cst_ref[d, c] = _dot(kw_t, va)
        return carry

    lax.fori_loop(0, nc, pass1, 0, unroll=CHUNK_UNROLL)

    def scan(i, carry):
        out = []
        for d in range(2):
            st, m = carry[d]
            c = i if d == 0 else nc - 1 - i
            mw = mw_ref[d, pl.ds(c, 1), :]
            bt = bt_ref[d, pl.ds(c, 1), :]
            m_new = jnp.maximum(bt + m, mw)
            a = jnp.exp(bt + m - m_new)[:, 0:1]
            cc = jnp.exp(mw - m_new)[:, 0:1]
            loc = cst_ref[d, c]
            cst_ref[d, c] = st
            mp_ref[d, pl.ds(c, 1), :] = m
            out.append((a * st + cc * loc, m_new))
        return tuple(out)

    init = (jnp.zeros((dh, 2 * dh), F32), jnp.zeros((1, L), F32))
    lax.fori_loop(0, nc, scan, (init, init))

    ti = lax.broadcasted_iota(I32, (L, L), 0)
    si = lax.broadcasted_iota(I32, (L, L), 1)
    masks = (si <= ti, si >= ti)

    def pass3(c, carry):
        start = pl.multiple_of(c * L, L)
        q = qc_ref[pl.ds(start, L), :]
        k = kc_ref[pl.ds(start, L), :]
        va = va_ref[pl.ds(start, L), :]
        qk = _dot_nt(q, k)
        hsum = jnp.zeros((L, dh), F32)
        for d in range(2):
            bmat = jnp.broadcast_to(b_ref[d, pl.ds(c, 1), :], (L, L)).T
            emat = jnp.broadcast_to(e_ref[d, pl.ds(c, 1), :], (L, L)).T
            dmat = jnp.where(masks[d], bmat + r_ref[d, pl.ds(c, 1), :], -jnp.inf)
            inter = bmat + mp_ref[d, pl.ds(c, 1), :]
            m_t = jnp.maximum(inter, emat)
            sc = qk * jnp.exp(dmat - m_t)
            a = jnp.exp(inter - m_t)
            intra = _dot(sc.astype(BF16), va)
            cross = _dot(q, cst_ref[d, c].astype(BF16))
            num = intra[:, :dh] + a * cross[:, :dh]
            den = intra[:, dh:] + a * cross[:, dh:]
            hsum = hsum + num / jnp.maximum(jnp.abs(den), jnp.exp(-m_t))
        hn = _rms(hsum, nrm_ref[...])
        y_ref[0, pl.ds(start, L), :] = hn * jax.nn.sigmoid(o_ref[0, pl.ds(start, L), :])
        return carry

    lax.fori_loop(0, nc, pass3, 0, unroll=CHUNK_UNROLL)


def _mlstm(um3, gt4, conv_w, conv_b, mlstm_norm):
    bsz, s, _ = um3.shape
    H, dh, L = N_MLSTM_HEADS, MLSTM_HEAD_DIM, MLSTM_CHUNK
    nc = s // L
    col = lambda off: pl.BlockSpec((1, s, dh), lambda b, h: (b, 0, off + h))
    vec = lambda rows, off: pl.BlockSpec((rows, dh), lambda b, h: (0, off + h))
    cb = conv_b.reshape(1, 2 * D_MLSTM)
    return pl.pallas_call(
        _mlstm_kernel,
        grid=(bsz, H),
        in_specs=[col(0), col(H), col(2 * H), col(3 * H),
                  pl.BlockSpec((N_GATES, 1, nc, L), lambda b, h: (0, b, 0, 0)),
                  vec(3, 0), vec(3, H), vec(1, 0), vec(1, H), vec(1, 0)],
        out_specs=pl.BlockSpec((1, s, dh), lambda b, h: (b, 0, h)),
        out_shape=jax.ShapeDtypeStruct((bsz, s, D_MLSTM), F32),
        scratch_shapes=[pltpu.VMEM((s, dh), BF16), pltpu.VMEM((s, dh), BF16), pltpu.VMEM((s, 2 * dh), BF16),
                        pltpu.VMEM((2, nc, dh, 2 * dh), F32)]
                       + [pltpu.VMEM((2, nc, L), F32) for _ in range(7)],
        compiler_params=_params("parallel", "parallel"),
        name="mlstm",
    )(um3, um3, um3, um3, gt4, conv_w, conv_w, cb, cb, mlstm_norm.reshape(1, D_MLSTM))


def _attn_kernel(q_ref, k_ref, v_ref, o_ref):
    tq = q_ref.shape[1]
    q = q_ref[0]
    m = jnp.full((tq, 1), -jnp.inf, F32)
    l = jnp.zeros((tq, 1), F32)
    acc = jnp.zeros((tq, V_HEAD_DIM), F32)
    chunk = min(KV_CHUNK, k_ref.shape[1])
    for c in range(k_ref.shape[1] // chunk):
        keys = slice(c * chunk, (c + 1) * chunk)
        s = _dot_nt(q, k_ref[0, keys, :])
        m_new = jnp.maximum(m, jnp.max(s, axis=-1, keepdims=True))
        alpha = jnp.exp2(m - m_new)
        p = jnp.exp2(s - m_new)
        l = alpha * l + jnp.sum(p, axis=-1, keepdims=True)
        acc = alpha * acc + _dot(p.astype(BF16), v_ref[0, keys, :])
        m = m_new
    o_ref[0] = acc / l


def _attention(q3, k3, v3):
    bsz, s, _ = q3.shape
    tq = min(Q_TILE, s)
    return pl.pallas_call(
        _attn_kernel,
        grid=(bsz, N_MLA_HEADS, s // tq),
        in_specs=[pl.BlockSpec((1, tq, QK_SLAB), lambda b, h, i: (b, i, h)),
                  pl.BlockSpec((1, s, QK_SLAB), lambda b, h, i: (b, 0, h)),
                  pl.BlockSpec((1, s, V_HEAD_DIM), lambda b, h, i: (b, 0, h))],
        out_specs=pl.BlockSpec((1, tq, V_HEAD_DIM), lambda b, h, i: (b, i, h)),
        out_shape=jax.ShapeDtypeStruct((bsz, s, D_MLA), F32),
        compiler_params=_params("parallel", "parallel", "parallel"),
        name="attention",
    )(q3, k3, v3)


def _out_route_kernel(ym_ref, ya_ref, x_ref, mn_ref, wom_ref, woa_ref, fn_ref, wr_ref, br_ref,
                      h_ref, xn_ref, gate_ref, slot_ref, slot_t_ref, seg_ref, size_ref,
                      carry_ref):
    i = pl.program_id(0)
    tm = x_ref.shape[0]

    @pl.when(i == 0)
    def _():
        carry_ref[...] = jnp.zeros_like(carry_ref)

    ya = _rms(ya_ref[...], mn_ref[...])
    h1 = x_ref[...] + _dot(ym_ref[...].astype(BF16), wom_ref[...]) + _dot(ya.astype(BF16), woa_ref[...])
    h_ref[...] = h1
    xn = _rms(h1, fn_ref[...])
    xn_ref[...] = xn.astype(BF16)
    logits = lax.dot_general(wr_ref[...], xn, (((1,), (1,)), ((), ())), preferred_element_type=F32,
                             precision=lax.Precision.HIGHEST) + br_ref[...]
    erow = lax.broadcasted_iota(I32, logits.shape, 0)
    work = logits
    vals, hots = [], []
    for k in range(TOP_K):
        mx = jnp.max(work, axis=0, keepdims=True)
        idx = jnp.min(jnp.where(work == mx, erow, N_EXPERTS), axis=0, keepdims=True)
        hot = erow == idx
        work = jnp.where(hot, -jnp.inf, work)
        vals.append(mx)
        hots.append(hot)
    exps = [jnp.exp(v - vals[0]) for v in vals]
    tot = exps[0] + exps[1] + exps[2] + exps[3]
    multi = (hots[0] | hots[1] | hots[2] | hots[3]).astype(BF16)
    ti = lax.broadcasted_iota(I32, (tm, tm), 0)
    tj = lax.broadcasted_iota(I32, (tm, tm), 1)
    local_rank = _dot(multi, (ti < tj).astype(BF16))
    multi_rows = jnp.concatenate([multi, jnp.zeros((LANES - N_EXPERTS, tm), BF16)], axis=0)
    count = _dot_nt(jnp.ones((SUBLANES, tm), BF16), multi_rows)[0:1]
    padded = jnp.ceil(count * (1.0 / SEG_ALIGN)) * SEG_ALIGN
    local_start = _cumsum_lanes(padded, reverse=False) - padded
    start_col = jnp.broadcast_to(local_start, (SUBLANES, LANES)).T[:N_EXPERTS, 0:1]
    slot_all = local_rank + start_col
    krow = lax.broadcasted_iota(I32, (SUBLANES, tm), 0)
    packed = jnp.zeros((SUBLANES, tm), F32)
    for k in range(TOP_K):
        sk = jnp.sum(jnp.where(hots[k], slot_all, 0.0), axis=0, keepdims=True)
        packed = jnp.where(krow == k, exps[k] / tot, jnp.where(krow == TOP_K + k, sk, packed))
    slot_t_ref[0] = jnp.where(krow < TOP_K, pltpu.roll(packed, TOP_K, 0), 0.0).astype(I32)
    cols = packed.T
    gate_ref[...] = cols[:, :TOP_K]
    slot_ref[...] = cols[:, TOP_K:2 * TOP_K].astype(I32)
    srow = lax.broadcasted_iota(I32, (SUBLANES, LANES), 0)
    seg = jnp.where(srow == 0, padded, jnp.where(srow == 1, local_start, jnp.where(srow == 2, carry_ref[...], 0.0)))
    seg_ref[0] = seg.astype(I32)
    carry_ref[...] += padded
    size_ref[...] = carry_ref[...].astype(I32)


def _out_route(ym2, ya2, x2, mla_norm, w_out, ffn_norm, w_router, b_router):
    n, d = x2.shape
    tm = ROW_TILE
    nt = n // tm
    wom = w_out[:D_MLSTM].astype(BF16)
    woa = w_out[D_MLSTM:].astype(BF16)
    wr = w_router.T
    br = b_router.reshape(N_EXPERTS, 1)
    full = lambda arr: pl.BlockSpec(arr.shape, lambda i: (0,) * arr.ndim)
    rows = lambda w: pl.BlockSpec((tm, w), lambda i: (i, 0))
    consts = [mla_norm.reshape(1, D_MLA), wom, woa, ffn_norm.reshape(1, d), wr, br]
    return pl.pallas_call(
        _out_route_kernel,
        grid=(nt,),
        in_specs=[rows(D_MLSTM), rows(D_MLA), rows(d)] + [full(c) for c in consts],
        out_specs=[rows(d), rows(d), rows(TOP_K), rows(TOP_K),
                   pl.BlockSpec((1, SUBLANES, tm), lambda i: (i, 0, 0)),
                   pl.BlockSpec((1, SUBLANES, LANES), lambda i: (i, 0, 0)),
                   pl.BlockSpec((1, LANES), lambda i: (0, 0))],
        out_shape=[jax.ShapeDtypeStruct((n, d), F32), jax.ShapeDtypeStruct((n, d), BF16),
                   jax.ShapeDtypeStruct((n, TOP_K), F32), jax.ShapeDtypeStruct((n, TOP_K), I32),
                   jax.ShapeDtypeStruct((nt, SUBLANES, tm), I32), jax.ShapeDtypeStruct((nt, SUBLANES, LANES), I32),
                   jax.ShapeDtypeStruct((1, LANES), I32)],
        scratch_shapes=[pltpu.VMEM((1, LANES), F32)],
        compiler_params=_params("arbitrary"),
        name="out_route",
    )(ym2, ya2, x2, *consts)


def _segment_copies(seg_ref, starts_ref, local_ref, global_ref, sem, to_global, wait):
    def copy(src, dst, rows):
        loc = local_ref.at[pl.ds(pl.multiple_of(src, SEG_ALIGN), rows)]
        glo = global_ref.at[pl.ds(pl.multiple_of(dst, SEG_ALIGN), rows)]
        cp = pltpu.make_async_copy(loc, glo, sem) if to_global else pltpu.make_async_copy(glo, loc, sem)
        if wait:
            cp.wait()
        else:
            cp.start()

    def per_expert(e, carry):
        size = seg_ref[0, 0, e]
        src = seg_ref[0, 1, e]
        dst = starts_ref[e] + seg_ref[0, 2, e]
        off = 0
        rows = ROW_TILE
        while rows >= SEG_ALIGN:
            @pl.when((size & rows) != 0)
            def _(off=off, rows=rows):
                copy(src + off, dst + off, rows)

            off = off + (size & rows)
            rows //= 2
        return carry

    lax.fori_loop(0, N_EXPERTS, per_expert, 0)


def _dispatch_kernel(starts_ref, seg_ref, segp_ref, slot_t_ref, xn_ref, xs_ref, sort_ref, zero_ref, sem, zsem):
    i = pl.program_id(0)
    tm = xn_ref.shape[0]
    tmx = zero_ref.shape[0]
    cap = sort_ref.shape[1]

    @pl.when(i == 0)
    def _():
        zero_ref[...] = jnp.zeros_like(zero_ref)
        n_tail = (xs_ref.shape[0] - starts_ref[N_EXPERTS]) // tmx

        def clear_tile(row, wait):
            cp = pltpu.make_async_copy(zero_ref, xs_ref.at[pl.ds(pl.multiple_of(row, tmx), tmx)], zsem)
            if wait:
                cp.wait()
            else:
                cp.start()

        for wait in (False, True):
            def clear_group(e, carry, wait=wait):
                hi = starts_ref[e + 1]

                @pl.when(hi > starts_ref[e])
                def _():
                    clear_tile(hi - tmx, wait)

                return carry

            def clear_tail(t, carry, wait=wait):
                clear_tile(starts_ref[N_EXPERTS] + t * tmx, wait)
                return carry

            lax.fori_loop(0, N_EXPERTS, clear_group, 0)
            lax.fori_loop(0, n_tail, clear_tail, 0)

    slot = i % 2
    pos = lax.broadcasted_iota(I32, (cap, tm), 0)
    hit = pos == slot_t_ref[0, 0:1, :]
    for k in range(1, TOP_K):
        hit = hit | (pos == slot_t_ref[0, k:k + 1, :])
    sort_ref[slot] = _pack_halves(_dot(hit.astype(BF16), xn_ref[...]))
    _segment_copies(seg_ref, starts_ref, sort_ref.at[slot], xs_ref, sem.at[slot], True, False)

    @pl.when(i >= 1)
    def _():
        _segment_copies(segp_ref, starts_ref, sort_ref.at[1 - slot], xs_ref, sem.at[1 - slot], True, True)

    @pl.when(i == pl.num_programs(0) - 1)
    def _():
        _segment_copies(seg_ref, starts_ref, sort_ref.at[slot], xs_ref, sem.at[slot], True, True)


def _dispatch(starts, seg, slot_t, xn, n_rows):
    n, d = xn.shape
    tm = ROW_TILE
    nt = n // tm
    smem = lambda f: pl.BlockSpec((1, SUBLANES, LANES), f, memory_space=pltpu.SMEM)
    any_spec = pl.BlockSpec(memory_space=pl.ANY)
    return pl.pallas_call(
        _dispatch_kernel,
        grid_spec=pltpu.PrefetchScalarGridSpec(
            num_scalar_prefetch=1,
            grid=(nt,),
            in_specs=[smem(lambda i, *_: (i, 0, 0)), smem(lambda i, *_: (jnp.maximum(i - 1, 0), 0, 0)),
                      pl.BlockSpec((1, SUBLANES, tm), lambda i, *_: (i, 0, 0)),
                      pl.BlockSpec((tm, d), lambda i, *_: (i, 0))],
            out_specs=any_spec,
            scratch_shapes=[pltpu.VMEM((2, SORT_ROWS, d // 2), U32), pltpu.VMEM((EXPERT_TILE, d // 2), U32),
                            pltpu.SemaphoreType.DMA((2,)), pltpu.SemaphoreType.DMA],
        ),
        out_shape=jax.ShapeDtypeStruct((n_rows, d // 2), U32),
        compiler_params=_params("arbitrary"),
        name="dispatch",
    )(starts, seg, seg, slot_t, xn)


def _experts_kernel(te_ref, tb_ref, tv_ref, xs_ref, wgu_ref, bgu_ref, wd_ref, bd_ref, out_ref,
                    wgu_bf, wd_bf):
    i = pl.program_id(0)
    de = wd_ref.shape[1]
    prev = te_ref[jnp.maximum(i - 1, 0)]

    @pl.when(jnp.logical_or(i == 0, te_ref[i] != prev))
    def _():
        wgu_bf[...] = wgu_ref[0].astype(BF16)
        wd_bf[...] = wd_ref[0].astype(BF16)

    @pl.when(tv_ref[i] == 1)
    def _():
        x = _unpack_halves(xs_ref[...])
        ch = 512
        acc = jnp.zeros((xs_ref.shape[0], bd_ref.shape[2]), F32)
        for j in range(de // ch):
            g = _dot(x, wgu_bf[:, j * ch:(j + 1) * ch]) + bgu_ref[0, :, j * ch:(j + 1) * ch]
            u = _dot(x, wgu_bf[:, de + j * ch:de + (j + 1) * ch]) + bgu_ref[0, :, de + j * ch:de + (j + 1) * ch]
            g = jnp.minimum(g, SWIGLU_LIMIT)
            u = jnp.clip(u, -SWIGLU_LIMIT, SWIGLU_LIMIT)
            hm = (u + 1.0) * (g * jax.nn.sigmoid(g * SWIGLU_ALPHA))
            acc = acc + _dot(hm.astype(BF16), wd_bf[j * ch:(j + 1) * ch, :])
        out_ref[...] = _pack_halves((acc + bd_ref[0]).astype(BF16).astype(F32))

    @pl.when(tv_ref[i] == 0)
    def _():
        out_ref[...] = jnp.zeros_like(out_ref)


def _experts(tile_e, tile_b, tile_v, xs, w_gate_up, b_gate_up, w_down, b_down):
    n_rows, dw = xs.shape
    tmx = EXPERT_TILE
    ne, d, de2 = w_gate_up.shape
    de = de2 // 2
    return pl.pallas_call(
        _experts_kernel,
        grid_spec=pltpu.PrefetchScalarGridSpec(
            num_scalar_prefetch=3,
            grid=(n_rows // tmx,),
            in_specs=[pl.BlockSpec((tmx, dw), lambda i, te, tb, tv: (tb[i], 0)),
                      pl.BlockSpec((1, d, de2), lambda i, te, tb, tv: (te[i], 0, 0)),
                      pl.BlockSpec((1, 1, de2), lambda i, te, tb, tv: (te[i], 0, 0)),
                      pl.BlockSpec((1, de, d), lambda i, te, tb, tv: (te[i], 0, 0)),
                      pl.BlockSpec((1, 1, d), lambda i, te, tb, tv: (te[i], 0, 0))],
            out_specs=pl.BlockSpec((tmx, dw), lambda i, te, tb, tv: (i, 0)),
            scratch_shapes=[pltpu.VMEM((d, de2), BF16), pltpu.VMEM((de, d), BF16)],
        ),
        out_shape=jax.ShapeDtypeStruct((n_rows, dw), U32),
        compiler_params=_params("arbitrary"),
        name="experts",
    )(tile_e, tile_b, tile_v, xs, w_gate_up, b_gate_up.reshape(ne, 1, de2), w_down, b_down.reshape(ne, 1, d))


def _combine_kernel(starts_ref, seg_ref, segn_ref, h_ref, gate_ref, slot_ref, p_ref, ys_ref, pn_ref, wg_ref,
                    wp_ref, fn_ref, out_ref, ybuf, sem, *, final):
    i = pl.program_id(0)
    nt = pl.num_programs(0)
    tm = h_ref.shape[0]
    cap = ybuf.shape[1]

    @pl.when(i == 0)
    def _():
        ybuf[...] = jnp.zeros_like(ybuf)
        _segment_copies(seg_ref, starts_ref, ybuf.at[0], ys_ref, sem.at[0], False, False)

    slot = i % 2

    @pl.when(i + 1 < nt)
    def _():
        _segment_copies(segn_ref, starts_ref, ybuf.at[1 - slot], ys_ref, sem.at[1 - slot], False, False)

    _segment_copies(seg_ref, starts_ref, ybuf.at[slot], ys_ref, sem.at[slot], False, True)

    pos = lax.broadcasted_iota(I32, (tm, cap), 1)
    gate = gate_ref[...]
    weights = jnp.zeros((tm, cap), F32)
    for k in range(TOP_K):
        weights = jnp.where(pos == slot_ref[:, k:k + 1], gate[:, k:k + 1], weights)
    h2 = h_ref[...] + _dot(weights.astype(BF16), _unpack_halves(ybuf[slot]))
    hn = _rms(h2, pn_ref[...]).astype(BF16)
    sg = jax.nn.sigmoid(_dot(hn, wg_ref[...]))
    h3 = h2 + sg * _dot(p_ref[...].astype(BF16), wp_ref[...])
    out_ref[...] = _rms(h3, fn_ref[...]) if final else h3


def _combine(starts, seg, h1, gate, slot, p2, ys, ple_norm, w_ple_gate, w_ple_proj, final_norm, final):
    n, d = h1.shape
    tm = ROW_TILE
    nt = n // tm
    wg = w_ple_gate.astype(BF16)
    wp = w_ple_proj.astype(BF16)
    smem = lambda f: pl.BlockSpec((1, SUBLANES, LANES), f, memory_space=pltpu.SMEM)
    full = lambda arr: pl.BlockSpec(arr.shape, lambda i, *_: (0,) * arr.ndim)
    rows = lambda w: pl.BlockSpec((tm, w), lambda i, *_: (i, 0))
    consts = [ple_norm.reshape(1, d), wg, wp, final_norm.reshape(1, d)]
    return pl.pallas_call(
        functools.partial(_combine_kernel, final=final),
        grid_spec=pltpu.PrefetchScalarGridSpec(
            num_scalar_prefetch=1,
            grid=(nt,),
            in_specs=[smem(lambda i, *_: (i, 0, 0)), smem(lambda i, *_: (jnp.minimum(i + 1, nt - 1), 0, 0)),
                      rows(d), rows(TOP_K), rows(TOP_K), rows(p2.shape[1]), pl.BlockSpec(memory_space=pl.ANY)]
                     + [full(c) for c in consts],
            out_specs=rows(d),
            scratch_shapes=[pltpu.VMEM((2, SORT_ROWS, d // 2), U32), pltpu.SemaphoreType.DMA((2,))],
        ),
        out_shape=jax.ShapeDtypeStruct((n, d), F32),
        compiler_params=_params("arbitrary"),
        name="combine",
    )(starts, seg, seg, h1, gate, slot, p2, ys, *consts)


def _route_tables(sizes, n_tiles):
    tmx = EXPERT_TILE
    tile_end = jnp.cumsum((sizes + tmx - 1) // tmx)
    starts = jnp.concatenate([jnp.zeros((1,), I32), tile_end * tmx]).astype(I32)
    n_valid = tile_end[-1]
    t = jnp.arange(n_tiles, dtype=I32)
    tb = jnp.minimum(t, n_valid - 1).astype(I32)
    te = jnp.sum(tile_end[None, :] <= tb[:, None], axis=1).astype(I32)
    tv = (t < n_valid).astype(I32)
    return starts, te, tb, tv


def kernel(x, p, positions, attn_norm, w_in, b_gates, conv_w, conv_b, mlstm_norm, q_norm, w_q_up, kv_norm, w_kv_up, mla_norm, w_out, ffn_norm, w_router, b_router, w_gate_up, b_gate_up, w_down, b_down, ple_norm, w_ple_gate, w_ple_proj, final_norm):
    bsz, s, d = x.shape
    n = bsz * s
    depth = p.shape[0]
    nc = s // MLSTM_CHUNK
    max_rows = n * TOP_K + (n // ROW_TILE) * N_EXPERTS * (SEG_ALIGN - 1) + N_EXPERTS * (EXPERT_TILE - 1)
    n_tiles = max_rows // EXPERT_TILE
    pos2 = positions.reshape(n, 1)
    h = x.reshape(n, d)
    for i in range(depth):
        um, gt, q, k, v = _in_proj(h, pos2, attn_norm[i], w_in[i], b_gates[i], q_norm[i], w_q_up[i],
                                   kv_norm[i], w_kv_up[i])
        ym = _mlstm(um.reshape(bsz, s, -1), gt.reshape(N_GATES, bsz, nc, MLSTM_CHUNK), conv_w[i], conv_b[i],
                    mlstm_norm[i])
        ya = _attention(q.reshape(bsz, s, -1), k.reshape(bsz, s, -1), v.reshape(bsz, s, -1))
        h1, xn, gate, slot, slot_t, seg, sizes = _out_route(ym.reshape(n, -1), ya.reshape(n, -1), h, mla_norm[i],
                                                            w_out[i], ffn_norm[i], w_router[i], b_router[i])
        starts, te, tb, tv = _route_tables(sizes[0, :N_EXPERTS], n_tiles)
        xs = _dispatch(starts, seg, slot_t, xn, n_tiles * EXPERT_TILE)
        ys = _experts(te, tb, tv, xs, w_gate_up[i], b_gate_up[i], w_down[i], b_down[i])
        h = _combine(starts, seg, h1, gate, slot, p[i].reshape(n, -1), ys, ple_norm[i], w_ple_gate[i],
                     w_ple_proj[i], final_norm, final=(i == depth - 1))
    return h.reshape(bsz, s, d)
```

```python
import functools

import jax
import jax.numpy as jnp
from jax import lax
from jax.experimental import pallas as pl
from jax.experimental.pallas import tpu as pltpu

F32 = jnp.float32
BF16 = jnp.bfloat16
I32 = jnp.int32
U32 = jnp.uint32

N_MLSTM_HEADS = 4
MLSTM_HEAD_DIM = 128
D_MLSTM = N_MLSTM_HEADS * MLSTM_HEAD_DIM
MLSTM_CHUNK = 128
N_MLA_HEADS = 4
QK_NOPE_DIM = 128
QK_ROPE_DIM = 64
V_HEAD_DIM = 128
D_MLA = N_MLA_HEADS * V_HEAD_DIM
Q_LORA = 256
KV_LORA = 128
ROPE_THETA = 10000.0
N_EXPERTS = 32
TOP_K = 4
SWIGLU_LIMIT = 7.0
SWIGLU_ALPHA = 1.702
EPS = 1e-6
N_GATES = 4 * N_MLSTM_HEADS
OFF_G = 4 * D_MLSTM
OFF_CQ = OFF_G + N_GATES
OFF_CKV = OFF_CQ + Q_LORA
OFF_KR = OFF_CKV + KV_LORA

LANES = 128
SUBLANES = 8
QK_SLAB = 2 * LANES
VMEM_LIMIT_BYTES = 56 * 1024 * 1024
LOG2_E = 1.4426950408889634

ROW_TILE = 512
Q_TILE = 1024
KV_CHUNK = 1024
EXPERT_TILE = 512
CHUNK_UNROLL = 8
SEG_ALIGN = SUBLANES
SORT_ROWS = ROW_TILE * TOP_K + N_EXPERTS * SEG_ALIGN


def _dot(a, b):
    return jnp.dot(a, b, preferred_element_type=F32)


def _dot_nt(a, b):
    return lax.dot_general(a, b, (((1,), (1,)), ((), ())), preferred_element_type=F32)


def _rms(x, g):
    return x * lax.rsqrt(jnp.mean(x * x, axis=-1, keepdims=True) + EPS) * g


def _log_sigmoid(x):
    return jnp.minimum(x, 0.0) - jnp.log(1.0 + jnp.exp(-jnp.abs(x)))


def _scan_lanes(x, reverse, op, identity):
    n = x.shape[-1]
    lane = lax.broadcasted_iota(I32, x.shape, x.ndim - 1)
    sh = 1
    while sh < n:
        if reverse:
            x = op(x, jnp.where(lane < n - sh, pltpu.roll(x, n - sh, x.ndim - 1), identity))
        else:
            x = op(x, jnp.where(lane >= sh, pltpu.roll(x, sh, x.ndim - 1), identity))
        sh *= 2
    return x


def _cumsum_lanes(x, reverse):
    return _scan_lanes(x, reverse, jnp.add, 0.0)


def _cummax_lanes(x, reverse):
    return _scan_lanes(x, reverse, jnp.maximum, -jnp.inf)


def _pack_halves(x):
    w = x.shape[1] // 2
    lo = lax.shift_right_logical(lax.bitcast_convert_type(x[:, :w], U32), jnp.uint32(16))
    hi = lax.bitcast_convert_type(x[:, w:], U32)
    return lo | hi


def _unpack_halves(words):
    lo = lax.bitcast_convert_type(lax.shift_left(words, jnp.uint32(16)), F32)
    hi = lax.bitcast_convert_type(words & jnp.uint32(0xFFFF0000), F32)
    return jnp.concatenate([lo, hi], axis=1).astype(BF16)


def _params(*sem):
    return pltpu.CompilerParams(dimension_semantics=sem, vmem_limit_bytes=VMEM_LIMIT_BYTES)


def _in_proj_kernel(x_ref, pos_ref, an_ref, wm_ref, wr_ref, bgt_ref, qn_ref, wq_ref, kvn_ref,
                    wk_ref, wv_ref, freq_ref, sgn_ref,
                    um_ref, gt_ref, q_ref, k_ref, v_ref):
    a = _rms(x_ref[...], an_ref[...]).astype(BF16)
    um_ref[...] = _dot(a, wm_ref[...])
    rest = _dot(a, wr_ref[...])
    cq = rest[:, :Q_LORA]
    ckv = rest[:, Q_LORA:Q_LORA + KV_LORA]
    kr2 = rest[:, Q_LORA + KV_LORA:Q_LORA + KV_LORA + LANES]
    krs2 = rest[:, Q_LORA + KV_LORA + LANES:Q_LORA + KV_LORA + 2 * LANES]
    gt_ref[...] = rest[:, Q_LORA + KV_LORA + 2 * LANES:].T[:N_GATES, :] + bgt_ref[...]
    ang = pos_ref[...].astype(F32) * freq_ref[...]
    cos_a = jnp.cos(ang)
    sin_a = jnp.sin(ang) * sgn_ref[...]
    scale = (QK_NOPE_DIM + QK_ROPE_DIM) ** -0.5 * LOG2_E
    lane = lax.broadcasted_iota(I32, ang.shape, 1)
    rope_mul = jnp.where(lane < QK_ROPE_DIM, cos_a, sin_a) * scale
    qf = _dot(_rms(cq, qn_ref[...]).astype(BF16), wq_ref[...])
    ckvn = _rms(ckv, kvn_ref[...]).astype(BF16)
    kn = _dot(ckvn, wk_ref[...])
    v_ref[...] = _dot(ckvn, wv_ref[...]).astype(BF16)
    k_rope = (kr2 * cos_a + krs2 * sin_a).astype(BF16)
    for h in range(N_MLA_HEADS):
        o = h * QK_SLAB
        q_ref[:, o:o + LANES] = (qf[:, o:o + LANES] * scale).astype(BF16)
        q_ref[:, o + LANES:o + QK_SLAB] = (qf[:, o + LANES:o + QK_SLAB] * rope_mul).astype(BF16)
        k_ref[:, o:o + LANES] = kn[:, h * LANES:(h + 1) * LANES].astype(BF16)
        k_ref[:, o + LANES:o + QK_SLAB] = k_rope


def _in_proj(x2, pos2, attn_norm, w_in, b_gates, q_norm, w_q_up, kv_norm, w_kv_up):
    n, d = x2.shape
    tm = ROW_TILE
    half = QK_ROPE_DIM // 2
    swap = jnp.concatenate([jnp.arange(half, QK_ROPE_DIM), jnp.arange(0, half)])
    w_kr = w_in[:, OFF_KR:OFF_KR + QK_ROPE_DIM]
    w_krs = w_kr[:, swap]
    wm = w_in[:, :OFF_G].astype(BF16)
    w_g = jnp.pad(w_in[:, OFF_G:OFF_CQ], ((0, 0), (0, LANES - N_GATES)))
    wr = jnp.concatenate([w_in[:, OFF_CQ:OFF_KR], w_kr, w_kr, w_krs, w_krs, w_g], axis=1).astype(BF16)
    bgt = b_gates.reshape(N_GATES, 1)
    wq4 = w_q_up.reshape(Q_LORA, N_MLA_HEADS, QK_NOPE_DIM + QK_ROPE_DIM)
    wq_pe = wq4[:, :, QK_NOPE_DIM:]
    wq = jnp.concatenate([wq4, wq_pe[:, :, swap]], axis=2).reshape(Q_LORA, N_MLA_HEADS * QK_SLAB).astype(BF16)
    wkv4 = w_kv_up.reshape(KV_LORA, N_MLA_HEADS, QK_NOPE_DIM + V_HEAD_DIM)
    wk = wkv4[:, :, :QK_NOPE_DIM].reshape(KV_LORA, N_MLA_HEADS * QK_NOPE_DIM).astype(BF16)
    wv = wkv4[:, :, QK_NOPE_DIM:].reshape(KV_LORA, D_MLA).astype(BF16)
    freqs = ROPE_THETA ** (-jnp.arange(0, QK_ROPE_DIM, 2, dtype=F32) / QK_ROPE_DIM)
    freq_l = jnp.tile(freqs, LANES // half).reshape(1, LANES)
    sgn_l = jnp.tile(jnp.concatenate([-jnp.ones((half,), F32), jnp.ones((half,), F32)]),
                     LANES // QK_ROPE_DIM).reshape(1, LANES)
    full = lambda arr: pl.BlockSpec(arr.shape, lambda i: (0,) * arr.ndim)
    rows = lambda w: pl.BlockSpec((tm, w), lambda i: (i, 0))
    consts = [attn_norm.reshape(1, d), wm, wr, bgt, q_norm.reshape(1, Q_LORA), wq,
              kv_norm.reshape(1, KV_LORA), wk, wv, freq_l, sgn_l]
    return pl.pallas_call(
        _in_proj_kernel,
        grid=(n // tm,),
        in_specs=[rows(d), rows(1)] + [full(c) for c in consts],
        out_specs=[rows(OFF_G), pl.BlockSpec((N_GATES, tm), lambda i: (0, i)),
                   rows(N_MLA_HEADS * QK_SLAB), rows(N_MLA_HEADS * QK_SLAB), rows(D_MLA)],
        out_shape=[jax.ShapeDtypeStruct((n, OFF_G), F32), jax.ShapeDtypeStruct((N_GATES, n), F32),
                   jax.ShapeDtypeStruct((n, N_MLA_HEADS * QK_SLAB), BF16),
                   jax.ShapeDtypeStruct((n, N_MLA_HEADS * QK_SLAB), BF16),
                   jax.ShapeDtypeStruct((n, D_MLA), BF16)],
        compiler_params=_params("parallel"),
        name="in_proj",
    )(x2, pos2, *consts)


def _mlstm_kernel(q_ref, k_ref, v_ref, o_ref, g_ref, cwq_ref, cwk_ref, cbq_ref, cbk_ref, nrm_ref,
                  y_ref,
                  qc_ref, kc_ref, va_ref, cst_ref, b_ref, e_ref, r_ref, ew_ref, mw_ref, bt_ref, mp_ref):
    L = MLSTM_CHUNK
    dh = MLSTM_HEAD_DIM
    nc = q_ref.shape[1] // L
    s_len = q_ref.shape[1]
    h = pl.program_id(1)

    for d in range(2):
        ig = g_ref[2 * d * N_MLSTM_HEADS + h, 0]
        fg = g_ref[(2 * d + 1) * N_MLSTM_HEADS + h, 0]
        b = _cumsum_lanes(_log_sigmoid(fg), reverse=(d == 1))
        btot = b[:, L - 1:L] if d == 0 else b[:, 0:1]
        r = ig - b
        w = btot + r
        mw = jnp.max(w, axis=-1, keepdims=True)
        b_ref[d] = b
        e_ref[d] = b + _cummax_lanes(r, reverse=(d == 1))
        r_ref[d] = r
        ew_ref[d] = jnp.exp(w - mw)
        mw_ref[d] = jnp.broadcast_to(mw, (nc, L))
        bt_ref[d] = jnp.broadcast_to(btot, (nc, L))

    row = lax.broadcasted_iota(I32, (L, dh), 0)

    def conv_silu(ref, cw_ref, cb_ref, c):
        start = pl.multiple_of(c * L, L)
        x = ref[0, pl.ds(start, L), :]
        prev_row = jnp.where(c > 0, ref[0, pl.ds(jnp.maximum(start - 1, 0), 1), :], 0.0)
        next_row = jnp.where(c < nc - 1, ref[0, pl.ds(jnp.minimum(start + L, s_len - 1), 1), :], 0.0)
        x_prev = jnp.where(row == 0, prev_row, pltpu.roll(x, 1, 0))
        x_next = jnp.where(row == L - 1, next_row, pltpu.roll(x, L - 1, 0))
        y = cw_ref[0:1, :] * x_prev + cw_ref[1:2, :] * x + cw_ref[2:3, :] * x_next + cb_ref[...]
        return y * jax.nn.sigmoid(y)

    ones_blk = jnp.ones((L, dh), BF16)

    def pass1(c, carry):
        start = pl.multiple_of(c * L, L)
        qc_ref[pl.ds(start, L), :] = conv_silu(q_ref, cwq_ref, cbq_ref, c).astype(BF16)
        kk = conv_silu(k_ref, cwk_ref, cbk_ref, c) * (dh ** -0.5)
        kc_ref[pl.ds(start, L), :] = kk.astype(BF16)
        va = jnp.concatenate([v_ref[0, pl.ds(start, L), :].astype(BF16), ones_blk], axis=1)
        va_ref[pl.ds(start, L), :] = va
        kt = kk.T
        for d in range(2):
            kw_t = (kt * ew_ref[d, pl.ds(c, 1), :]).astype(BF16)
            cst_ref[d, c] = _dot(kw_t, va)
        return carry

    lax.fori_loop(0, nc, pass1, 0, unroll=CHUNK_UNROLL)

    def scan(i, carry):
        out = []
        for d in range(2):
            st, m = carry[d]
            c = i if d == 0 else nc - 1 - i
            mw = mw_ref[d, pl.ds(c, 1), :]
            bt = bt_ref[d, pl.ds(c, 1), :]
            m_new = jnp.maximum(bt + m, mw)
            a = jnp.exp(bt + m - m_new)[:, 0:1]
            cc = jnp.exp(mw - m_new)[:, 0:1]
            loc = cst_ref[d, c]
            cst_ref[d, c] = st
            mp_ref[d, pl.ds(c, 1), :] = m
            out.append((a * st + cc * loc, m_new))
        return tuple(out)

    init = (jnp.zeros((dh, 2 * dh), F32), jnp.zeros((1, L), F32))
    lax.fori_loop(0, nc, scan, (init, init))

    ti = lax.broadcasted_iota(I32, (L, L), 0)
    si = lax.broadcasted_iota(I32, (L, L), 1)
    masks = (si <= ti, si >= ti)

    def pass3(c, carry):
        start = pl.multiple_of(c * L, L)
        q = qc_ref[pl.ds(start, L), :]
        k = kc_ref[pl.ds(start, L), :]
        va = va_ref[pl.ds(start, L), :]
        qk = _dot_nt(q, k)
        hsum = jnp.zeros((L, dh), F32)
        for d in range(2):
            bmat = jnp.broadcast_to(b_ref[d, pl.ds(c, 1), :], (L, L)).T
            emat = jnp.broadcast_to(e_ref[d, pl.ds(c, 1), :], (L, L)).T
            dmat = jnp.where(masks[d], bmat + r_ref[d, pl.ds(c, 1), :], -jnp.inf)
            inter = bmat + mp_ref[d, pl.ds(c, 1), :]
            m_t = jnp.maximum(inter, emat)
            sc = qk * jnp.exp(dmat - m_t)
            a = jnp.exp(inter - m_t)
            intra = _dot(sc.astype(BF16), va)
            cross = _dot(q, cst_ref[d, c].astype(BF16))
            num = intra[:, :dh] + a * cross[:, :dh]
            den = intra[:, dh:] + a * cross[:, dh:]
            hsum = hsum + num / jnp.maximum(jnp.abs(den), jnp.exp(-m_t))
        hn = _rms(hsum, nrm_ref[...])
        y_ref[0, pl.ds(start, L), :] = hn * jax.nn.sigmoid(o_ref[0, pl.ds(start, L), :])
        return carry

    lax.fori_loop(0, nc, pass3, 0, unroll=CHUNK_UNROLL)


def _mlstm(um3, gt4, conv_w, conv_b, mlstm_norm):
    bsz, s, _ = um3.shape
    H, dh, L = N_MLSTM_HEADS, MLSTM_HEAD_DIM, MLSTM_CHUNK
    nc = s // L
    col = lambda off: pl.BlockSpec((1, s, dh), lambda b, h: (b, 0, off + h))
    vec = lambda rows, off: pl.BlockSpec((rows, dh), lambda b, h: (0, off + h))
    cb = conv_b.reshape(1, 2 * D_MLSTM)
    return pl.pallas_call(
        _mlstm_kernel,
        grid=(bsz, H),
        in_specs=[col(0), col(H), col(2 * H), col(3 * H),
                  pl.BlockSpec((N_GATES, 1, nc, L), lambda b, h: (0, b, 0, 0)),
                  vec(3, 0), vec(3, H), vec(1, 0), vec(1, H), vec(1, 0)],
        out_specs=pl.BlockSpec((1, s, dh), lambda b, h: (b, 0, h)),
        out_shape=jax.ShapeDtypeStruct((bsz, s, D_MLSTM), F32),
        scratch_shapes=[pltpu.VMEM((s, dh), BF16), pltpu.VMEM((s, dh), BF16), pltpu.VMEM((s, 2 * dh), BF16),
                        pltpu.VMEM((2, nc, dh, 2 * dh), F32)]
                       + [pltpu.VMEM((2, nc, L), F32) for _ in range(7)],
        compiler_params=_params("parallel", "parallel"),
        name="mlstm",
    )(um3, um3, um3, um3, gt4, conv_w, conv_w, cb, cb, mlstm_norm.reshape(1, D_MLSTM))


def _attn_kernel(q_ref, k_ref, v_ref, o_ref):
    tq = q_ref.shape[1]
    q = q_ref[0]
    m = jnp.full((tq, 1), -jnp.inf, F32)
    l = jnp.zeros((tq, 1), F32)
    acc = jnp.zeros((tq, V_HEAD_DIM), F32)
    chunk = min(KV_CHUNK, k_ref.shape[1])
    for c in range(k_ref.shape[1] // chunk):
        keys = slice(c * chunk, (c + 1) * chunk)
        s = _dot_nt(q, k_ref[0, keys, :])
        m_new = jnp.maximum(m, jnp.max(s, axis=-1, keepdims=True))
        alpha = jnp.exp2(m - m_new)
        p = jnp.exp2(s - m_new)
        l = alpha * l + jnp.sum(p, axis=-1, keepdims=True)
        acc = alpha * acc + _dot(p.astype(BF16), v_ref[0, keys, :])
        m = m_new
    o_ref[0] = acc / l


def _attention(q3, k3, v3):
    bsz, s, _ = q3.shape
    tq = min(Q_TILE, s)
    return pl.pallas_call(
        _attn_kernel,
        grid=(bsz, N_MLA_HEADS, s // tq),
        in_specs=[pl.BlockSpec((1, tq, QK_SLAB), lambda b, h, i: (b, i, h)),
                  pl.BlockSpec((1, s, QK_SLAB), lambda b, h, i: (b, 0, h)),
                  pl.BlockSpec((1, s, V_HEAD_DIM), lambda b, h, i: (b, 0, h))],
        out_specs=pl.BlockSpec((1, tq, V_HEAD_DIM), lambda b, h, i: (b, i, h)),
        out_shape=jax.ShapeDtypeStruct((bsz, s, D_MLA), F32),
        compiler_params=_params("parallel", "parallel", "parallel"),
        name="attention",
    )(q3, k3, v3)


def _out_route_kernel(ym_ref, ya_ref, x_ref, mn_ref, wom_ref, woa_ref, fn_ref, wr_ref, br_ref,
                      h_ref, xn_ref, gate_ref, slot_ref, slot_t_ref, seg_ref, size_ref,
                      carry_ref):
    i = pl.program_id(0)
    tm = x_ref.shape[0]

    @pl.when(i == 0)
    def _():
        carry_ref[...] = jnp.zeros_like(carry_ref)

    ya = _rms(ya_ref[...], mn_ref[...])
    h1 = x_ref[...] + _dot(ym_ref[...].astype(BF16), wom_ref[...]) + _dot(ya.astype(BF16), woa_ref[...])
    h_ref[...] = h1
    xn = _rms(h1, fn_ref[...])
    xn_ref[...] = xn.astype(BF16)
    logits = lax.dot_general(wr_ref[...], xn, (((1,), (1,)), ((), ())), preferred_element_type=F32,
                             precision=lax.Precision.HIGHEST) + br_ref[...]
    erow = lax.broadcasted_iota(I32, logits.shape, 0)
    work = logits
    vals, hots = [], []
    for k in range(TOP_K):
        mx = jnp.max(work, axis=0, keepdims=True)
        idx = jnp.min(jnp.where(work == mx, erow, N_EXPERTS), axis=0, keepdims=True)
        hot = erow == idx
        work = jnp.where(hot, -jnp.inf, work)
        vals.append(mx)
        hots.append(hot)
    exps = [jnp.exp(v - vals[0]) for v in vals]
    tot = exps[0] + exps[1] + exps[2] + exps[3]
    multi = (hots[0] | hots[1] | hots[2] | hots[3]).astype(BF16)
    ti = lax.broadcasted_iota(I32, (tm, tm), 0)
    tj = lax.broadcasted_iota(I32, (tm, tm), 1)
    local_rank = _dot(multi, (ti < tj).astype(BF16))
    multi_rows = jnp.concatenate([multi, jnp.zeros((LANES - N_EXPERTS, tm), BF16)], axis=0)
    count = _dot_nt(jnp.ones((SUBLANES, tm), BF16), multi_rows)[0:1]
    padded = jnp.ceil(count * (1.0 / SEG_ALIGN)) * SEG_ALIGN
    filled = _cumsum_lanes(padded, reverse=False)
    local_start = filled - padded
    start_col = jnp.broadcast_to(local_start, (SUBLANES, LANES)).T[:N_EXPERTS, 0:1]
    slot_all = local_rank + start_col
    krow = lax.broadcasted_iota(I32, (SUBLANES, tm), 0)
    packed = jnp.zeros((SUBLANES, tm), F32)
    for k in range(TOP_K):
        sk = jnp.sum(jnp.where(hots[k], slot_all, 0.0), axis=0, keepdims=True)
        packed = jnp.where(krow == k, exps[k] / tot, jnp.where(krow == TOP_K + k, sk, packed))
    slot_t_ref[0] = jnp.where(krow < TOP_K, pltpu.roll(packed, TOP_K, 0), 0.0).astype(I32)
    cols = packed.T
    gate_ref[...] = cols[:, :TOP_K]
    slot_ref[...] = cols[:, TOP_K:2 * TOP_K].astype(I32)
    srow = lax.broadcasted_iota(I32, (SUBLANES, LANES), 0)
    seg = jnp.where(srow == 0, padded, jnp.where(srow == 1, local_start, jnp.where(srow == 2, carry_ref[...],
                                                                                  filled[:, LANES - 1:LANES])))
    seg_ref[0] = seg.astype(I32)
    carry_ref[...] += padded
    size_ref[...] = carry_ref[...].astype(I32)


def _out_route(ym2, ya2, x2, mla_norm, w_out, ffn_norm, w_router, b_router):
    n, d = x2.shape
    tm = ROW_TILE
    nt = n // tm
    wom = w_out[:D_MLSTM].astype(BF16)
    woa = w_out[D_MLSTM:].astype(BF16)
    wr = w_router.T
    br = b_router.reshape(N_EXPERTS, 1)
    full = lambda arr: pl.BlockSpec(arr.shape, lambda i: (0,) * arr.ndim)
    rows = lambda w: pl.BlockSpec((tm, w), lambda i: (i, 0))
    consts = [mla_norm.reshape(1, D_MLA), wom, woa, ffn_norm.reshape(1, d), wr, br]
    return pl.pallas_call(
        _out_route_kernel,
        grid=(nt,),
        in_specs=[rows(D_MLSTM), rows(D_MLA), rows(d)] + [full(c) for c in consts],
        out_specs=[rows(d), rows(d), rows(TOP_K), rows(TOP_K),
                   pl.BlockSpec((1, SUBLANES, tm), lambda i: (i, 0, 0)),
                   pl.BlockSpec((1, SUBLANES, LANES), lambda i: (i, 0, 0)),
                   pl.BlockSpec((1, LANES), lambda i: (0, 0))],
        out_shape=[jax.ShapeDtypeStruct((n, d), F32), jax.ShapeDtypeStruct((n, d), BF16),
                   jax.ShapeDtypeStruct((n, TOP_K), F32), jax.ShapeDtypeStruct((n, TOP_K), I32),
                   jax.ShapeDtypeStruct((nt, SUBLANES, tm), I32), jax.ShapeDtypeStruct((nt, SUBLANES, LANES), I32),
                   jax.ShapeDtypeStruct((1, LANES), I32)],
        scratch_shapes=[pltpu.VMEM((1, LANES), F32)],
        compiler_params=_params("arbitrary"),
        name="out_route",
    )(ym2, ya2, x2, *consts)


def _segment_copies(seg_ref, starts_ref, local_ref, global_ref, sem, to_global, wait):
    def copy(src, dst, rows):
        loc = local_ref.at[pl.ds(pl.multiple_of(src, SEG_ALIGN), rows)]
        glo = global_ref.at[pl.ds(pl.multiple_of(dst, SEG_ALIGN), rows)]
        return pltpu.make_async_copy(loc, glo, sem) if to_global else pltpu.make_async_copy(glo, loc, sem)

    if wait:
        copy(0, 0, pl.multiple_of(seg_ref[0, 3, 0], SEG_ALIGN)).wait()
        return

    def per_expert(e, carry):
        size = seg_ref[0, 0, e]
        src = seg_ref[0, 1, e]
        dst = starts_ref[e] + seg_ref[0, 2, e]
        off = 0
        rows = ROW_TILE
        while rows >= SEG_ALIGN:
            @pl.when((size & rows) != 0)
            def _(off=off, rows=rows):
                copy(src + off, dst + off, rows).start()

            off = off + (size & rows)
            rows //= 2
        return carry

    lax.fori_loop(0, N_EXPERTS, per_expert, 0)


def _dispatch_kernel(starts_ref, seg_ref, segp_ref, slot_t_ref, xn_ref, xs_ref, sort_ref, zero_ref, sem, zsem):
    i = pl.program_id(0)
    tm = xn_ref.shape[0]
    tmx = zero_ref.shape[0]
    cap = sort_ref.shape[1]

    @pl.when(i == 0)
    def _():
        zero_ref[...] = jnp.zeros_like(zero_ref)
        n_tail = (xs_ref.shape[0] - starts_ref[N_EXPERTS]) // tmx

        def clear_tile(row, wait):
            cp = pltpu.make_async_copy(zero_ref, xs_ref.at[pl.ds(pl.multiple_of(row, tmx), tmx)], zsem)
            if wait:
                cp.wait()
            else:
                cp.start()

        for wait in (False, True):
            def clear_group(e, carry, wait=wait):
                hi = starts_ref[e + 1]

                @pl.when(hi > starts_ref[e])
                def _():
                    clear_tile(hi - tmx, wait)

                return carry

            def clear_tail(t, carry, wait=wait):
                clear_tile(starts_ref[N_EXPERTS] + t * tmx, wait)
                return carry

            lax.fori_loop(0, N_EXPERTS, clear_group, 0)
            lax.fori_loop(0, n_tail, clear_tail, 0)

    slot = i % 2
    pos = lax.broadcasted_iota(I32, (cap, tm), 0)
    hit = pos == slot_t_ref[0, 0:1, :]
    for k in range(1, TOP_K):
        hit = hit | (pos == slot_t_ref[0, k:k + 1, :])
    sort_ref[slot] = _pack_halves(_dot(hit.astype(BF16), xn_ref[...]))
    _segment_copies(seg_ref, starts_ref, sort_ref.at[slot], xs_ref, sem.at[slot], True, False)

    @pl.when(i >= 1)
    def _():
        _segment_copies(segp_ref, starts_ref, sort_ref.at[1 - slot], xs_ref, sem.at[1 - slot], True, True)

    @pl.when(i == pl.num_programs(0) - 1)
    def _():
        _segment_copies(seg_ref, starts_ref, sort_ref.at[slot], xs_ref, sem.at[slot], True, True)


def _dispatch(starts, seg, slot_t, xn, n_rows):
    n, d = xn.shape
    tm = ROW_TILE
    nt = n // tm
    smem = lambda f: pl.BlockSpec((1, SUBLANES, LANES), f, memory_space=pltpu.SMEM)
    any_spec = pl.BlockSpec(memory_space=pl.ANY)
    return pl.pallas_call(
        _dispatch_kernel,
        grid_spec=pltpu.PrefetchScalarGridSpec(
            num_scalar_prefetch=1,
            grid=(nt,),
            in_specs=[smem(lambda i, *_: (i, 0, 0)), smem(lambda i, *_: (jnp.maximum(i - 1, 0), 0, 0)),
                      pl.BlockSpec((1, SUBLANES, tm), lambda i, *_: (i, 0, 0)),
                      pl.BlockSpec((tm, d), lambda i, *_: (i, 0))],
            out_specs=any_spec,
            scratch_shapes=[pltpu.VMEM((2, SORT_ROWS, d // 2), U32), pltpu.VMEM((EXPERT_TILE, d // 2), U32),
                            pltpu.SemaphoreType.DMA((2,)), pltpu.SemaphoreType.DMA],
        ),
        out_shape=jax.ShapeDtypeStruct((n_rows, d // 2), U32),
        compiler_params=_params("arbitrary"),
        name="dispatch",
    )(starts, seg, seg, slot_t, xn)


def _experts_kernel(te_ref, tb_ref, tv_ref, xs_ref, wgu_ref, bgu_ref, wd_ref, bd_ref, out_ref,
                    wgu_bf, wd_bf):
    i = pl.program_id(0)
    de = wd_ref.shape[1]
    prev = te_ref[jnp.maximum(i - 1, 0)]

    @pl.when(jnp.logical_or(i == 0, te_ref[i] != prev))
    def _():
        wgu_bf[...] = wgu_ref[0].astype(BF16)
        wd_bf[...] = wd_ref[0].astype(BF16)

    @pl.when(tv_ref[i] == 1)
    def _():
        x = _unpack_halves(xs_ref[...])
        ch = 512
        acc = jnp.zeros((xs_ref.shape[0], bd_ref.shape[2]), F32)
        for j in range(de // ch):
            g = _dot(x, wgu_bf[:, j * ch:(j + 1) * ch]) + bgu_ref[0, :, j * ch:(j + 1) * ch]
            u = _dot(x, wgu_bf[:, de + j * ch:de + (j + 1) * ch]) + bgu_ref[0, :, de + j * ch:de + (j + 1) * ch]
            g = jnp.minimum(g, SWIGLU_LIMIT)
            u = jnp.clip(u, -SWIGLU_LIMIT, SWIGLU_LIMIT)
            hm = (u + 1.0) * (g * jax.nn.sigmoid(g * SWIGLU_ALPHA))
            acc = acc + _dot(hm.astype(BF16), wd_bf[j * ch:(j + 1) * ch, :])
        out_ref[...] = _pack_halves((acc + bd_ref[0]).astype(BF16).astype(F32))

    @pl.when(tv_ref[i] == 0)
    def _():
        out_ref[...] = jnp.zeros_like(out_ref)


def _experts(tile_e, tile_b, tile_v, xs, w_gate_up, b_gate_up, w_down, b_down):
    n_rows, dw = xs.shape
    tmx = EXPERT_TILE
    ne, d, de2 = w_gate_up.shape
    de = de2 // 2
    return pl.pallas_call(
        _experts_kernel,
        grid_spec=pltpu.PrefetchScalarGridSpec(
            num_scalar_prefetch=3,
            grid=(n_rows // tmx,),
            in_specs=[pl.BlockSpec((tmx, dw), lambda i, te, tb, tv: (tb[i], 0)),
                      pl.BlockSpec((1, d, de2), lambda i, te, tb, tv: (te[i], 0, 0)),
                      pl.BlockSpec((1, 1, de2), lambda i, te, tb, tv: (te[i], 0, 0)),
                      pl.BlockSpec((1, de, d), lambda i, te, tb, tv: (te[i], 0, 0)),
                      pl.BlockSpec((1, 1, d), lambda i, te, tb, tv: (te[i], 0, 0))],
            out_specs=pl.BlockSpec((tmx, dw), lambda i, te, tb, tv: (i, 0)),
            scratch_shapes=[pltpu.VMEM((d, de2), BF16), pltpu.VMEM((de, d), BF16)],
        ),
        out_shape=jax.ShapeDtypeStruct((n_rows, dw), U32),
        compiler_params=_params("arbitrary"),
        name="experts",
    )(tile_e, tile_b, tile_v, xs, w_gate_up, b_gate_up.reshape(ne, 1, de2), w_down, b_down.reshape(ne, 1, d))


def _combine_kernel(starts_ref, seg_ref, segn_ref, h_ref, gate_ref, slot_ref, p_ref, ys_ref, pn_ref, wg_ref,
                    wp_ref, fn_ref, out_ref, ybuf, sem, *, final):
    i = pl.program_id(0)
    nt = pl.num_programs(0)
    tm = h_ref.shape[0]
    cap = ybuf.shape[1]

    @pl.when(i == 0)
    def _():
        ybuf[...] = jnp.zeros_like(ybuf)
        _segment_copies(seg_ref, starts_ref, ybuf.at[0], ys_ref, sem.at[0], False, False)

    slot = i % 2

    @pl.when(i + 1 < nt)
    def _():
        _segment_copies(segn_ref, starts_ref, ybuf.at[1 - slot], ys_ref, sem.at[1 - slot], False, False)

    _segment_copies(seg_ref, starts_ref, ybuf.at[slot], ys_ref, sem.at[slot], False, True)

    pos = lax.broadcasted_iota(I32, (tm, cap), 1)
    gate = gate_ref[...]
    weights = jnp.zeros((tm, cap), F32)
    for k in range(TOP_K):
        weights = jnp.where(pos == slot_ref[:, k:k + 1], gate[:, k:k + 1], weights)
    h2 = h_ref[...] + _dot(weights.astype(BF16), _unpack_halves(ybuf[slot]))
    hn = _rms(h2, pn_ref[...]).astype(BF16)
    sg = jax.nn.sigmoid(_dot(hn, wg_ref[...]))
    h3 = h2 + sg * _dot(p_ref[...].astype(BF16), wp_ref[...])
    out_ref[...] = _rms(h3, fn_ref[...]) if final else h3


def _combine(starts, seg, h1, gate, slot, p2, ys, ple_norm, w_ple_gate, w_ple_proj, final_norm, final):
    n, d = h1.shape
    tm = ROW_TILE
    nt = n // tm
    wg = w_ple_gate.astype(BF16)
    wp = w_ple_proj.astype(BF16)
    smem = lambda f: pl.BlockSpec((1, SUBLANES, LANES), f, memory_space=pltpu.SMEM)
    full = lambda arr: pl.BlockSpec(arr.shape, lambda i, *_: (0,) * arr.ndim)
    rows = lambda w: pl.BlockSpec((tm, w), lambda i, *_: (i, 0))
    consts = [ple_norm.reshape(1, d), wg, wp, final_norm.reshape(1, d)]
    return pl.pallas_call(
        functools.partial(_combine_kernel, final=final),
        grid_spec=pltpu.PrefetchScalarGridSpec(
            num_scalar_prefetch=1,
            grid=(nt,),
            in_specs=[smem(lambda i, *_: (i, 0, 0)), smem(lambda i, *_: (jnp.minimum(i + 1, nt - 1), 0, 0)),
                      rows(d), rows(TOP_K), rows(TOP_K), rows(p2.shape[1]), pl.BlockSpec(memory_space=pl.ANY)]
                     + [full(c) for c in consts],
            out_specs=rows(d),
            scratch_shapes=[pltpu.VMEM((2, SORT_ROWS, d // 2), U32), pltpu.SemaphoreType.DMA((2,))],
        ),
        out_shape=jax.ShapeDtypeStruct((n, d), F32),
        compiler_params=_params("arbitrary"),
        name="combine",
    )(starts, seg, seg, h1, gate, slot, p2, ys, *consts)


def _route_tables(sizes, n_tiles):
    tmx = EXPERT_TILE
    tile_end = jnp.cumsum((sizes + tmx - 1) // tmx)
    starts = jnp.concatenate([jnp.zeros((1,), I32), tile_end * tmx]).astype(I32)
    n_valid = tile_end[-1]
    t = jnp.arange(n_tiles, dtype=I32)
    tb = jnp.minimum(t, n_valid - 1).astype(I32)
    te = jnp.sum(tile_end[None, :] <= tb[:, None], axis=1).astype(I32)
    tv = (t < n_valid).astype(I32)
    return starts, te, tb, tv


def kernel(x, p, positions, attn_norm, w_in, b_gates, conv_w, conv_b, mlstm_norm, q_norm, w_q_up, kv_norm, w_kv_up, mla_norm, w_out, ffn_norm, w_router, b_router, w_gate_up, b_gate_up, w_down, b_down, ple_norm, w_ple_gate, w_ple_proj, final_norm):
    bsz, s, d = x.shape
    n = bsz * s
    depth = p.shape[0]
    nc = s // MLSTM_CHUNK
    max_rows = n * TOP_K + (n // ROW_TILE) * N_EXPERTS * (SEG_ALIGN - 1) + N_EXPERTS * (EXPERT_TILE - 1)
    n_tiles = max_rows // EXPERT_TILE
    pos2 = positions.reshape(n, 1)
    h = x.reshape(n, d)
    for i in range(depth):
        um, gt, q, k, v = _in_proj(h, pos2, attn_norm[i], w_in[i], b_gates[i], q_norm[i], w_q_up[i],
                                   kv_norm[i], w_kv_up[i])
        ym = _mlstm(um.reshape(bsz, s, -1), gt.reshape(N_GATES, bsz, nc, MLSTM_CHUNK), conv_w[i], conv_b[i],
                    mlstm_norm[i])
        ya = _attention(q.reshape(bsz, s, -1), k.reshape(bsz, s, -1), v.reshape(bsz, s, -1))
        h1, xn, gate, slot, slot_t, seg, sizes = _out_route(ym.reshape(n, -1), ya.reshape(n, -1), h, mla_norm[i],
                                                            w_out[i], ffn_norm[i], w_router[i], b_router[i])
        starts, te, tb, tv = _route_tables(sizes[0, :N_EXPERTS], n_tiles)
        xs = _dispatch(starts, seg, slot_t, xn, n_tiles * EXPERT_TILE)
        ys = _experts(te, tb, tv, xs, w_gate_up[i], b_gate_up[i], w_down[i], b_down[i])
        h = _combine(starts, seg, h1, gate, slot, p[i].reshape(n, -1), ys, ple_norm[i], w_ple_gate[i],
                     w_ple_proj[i], final_norm, final=(i == depth - 1))
    return h.reshape(bsz, s, d)
```

```python
import functools

import jax
import jax.numpy as jnp
from jax import lax
from jax.experimental import pallas as pl
from jax.experimental.pallas import tpu as pltpu

F32 = jnp.float32
BF16 = jnp.bfloat16
I32 = jnp.int32
U32 = jnp.uint32

N_MLSTM_HEADS = 4
MLSTM_HEAD_DIM = 128
D_MLSTM = N_MLSTM_HEADS * MLSTM_HEAD_DIM
MLSTM_CHUNK = 128
N_MLA_HEADS = 4
QK_NOPE_DIM = 128
QK_ROPE_DIM = 64
V_HEAD_DIM = 128
D_MLA = N_MLA_HEADS * V_HEAD_DIM
Q_LORA = 256
KV_LORA = 128
ROPE_THETA = 10000.0
N_EXPERTS = 32
TOP_K = 4
SWIGLU_LIMIT = 7.0
SWIGLU_ALPHA = 1.702
EPS = 1e-6
N_GATES = 4 * N_MLSTM_HEADS
OFF_G = 4 * D_MLSTM
OFF_CQ = OFF_G + N_GATES
OFF_CKV = OFF_CQ + Q_LORA
OFF_KR = OFF_CKV + KV_LORA

LANES = 128
SUBLANES = 8
QK_SLAB = 2 * LANES
VMEM_LIMIT_BYTES = 56 * 1024 * 1024
LOG2_E = 1.4426950408889634

ROW_TILE = 512
Q_TILE = 2048
KV_CHUNK = 1024
EXPERT_TILE = 512
CHUNK_UNROLL = 8
SEG_ALIGN = SUBLANES
SORT_CHUNK = 256
SORT_ROWS = ROW_TILE * TOP_K + N_EXPERTS * SEG_ALIGN


def _dot(a, b):
    return jnp.dot(a, b, preferred_element_type=F32)


def _dot_nt(a, b):
    return lax.dot_general(a, b, (((1,), (1,)), ((), ())), preferred_element_type=F32)


def _rms(x, g):
    return x * lax.rsqrt(jnp.mean(x * x, axis=-1, keepdims=True) + EPS) * g


def _log_sigmoid(x):
    return jnp.minimum(x, 0.0) - jnp.log(1.0 + jnp.exp(-jnp.abs(x)))


def _scan_lanes(x, reverse, op, identity):
    n = x.shape[-1]
    lane = lax.broadcasted_iota(I32, x.shape, x.ndim - 1)
    sh = 1
    while sh < n:
        if reverse:
            x = op(x, jnp.where(lane < n - sh, pltpu.roll(x, n - sh, x.ndim - 1), identity))
        else:
            x = op(x, jnp.where(lane >= sh, pltpu.roll(x, sh, x.ndim - 1), identity))
        sh *= 2
    return x


def _cumsum_lanes(x, reverse):
    return _scan_lanes(x, reverse, jnp.add, 0.0)


def _cummax_lanes(x, reverse):
    return _scan_lanes(x, reverse, jnp.maximum, -jnp.inf)


def _pack_halves(x):
    w = x.shape[1] // 2
    lo = lax.shift_right_logical(lax.bitcast_convert_type(x[:, :w], U32), jnp.uint32(16))
    hi = lax.bitcast_convert_type(x[:, w:], U32)
    return lo | hi


def _unpack_halves(words):
    lo = lax.bitcast_convert_type(lax.shift_left(words, jnp.uint32(16)), F32)
    hi = lax.bitcast_convert_type(words & jnp.uint32(0xFFFF0000), F32)
    return jnp.concatenate([lo, hi], axis=1).astype(BF16)


def _params(*sem):
    return pltpu.CompilerParams(dimension_semantics=sem, vmem_limit_bytes=VMEM_LIMIT_BYTES)


def _in_proj_kernel(x_ref, pos_ref, an_ref, wm_ref, wr_ref, bgt_ref, qn_ref, wq_ref, kvn_ref,
                    wk_ref, wv_ref, freq_ref, sgn_ref,
                    um_ref, gt_ref, q_ref, k_ref, v_ref):
    a = _rms(x_ref[...], an_ref[...]).astype(BF16)
    um_ref[...] = _dot(a, wm_ref[...])
    rest = _dot(a, wr_ref[...])
    cq = rest[:, :Q_LORA]
    ckv = rest[:, Q_LORA:Q_LORA + KV_LORA]
    kr2 = rest[:, Q_LORA + KV_LORA:Q_LORA + KV_LORA + LANES]
    krs2 = rest[:, Q_LORA + KV_LORA + LANES:Q_LORA + KV_LORA + 2 * LANES]
    gt_ref[...] = rest[:, Q_LORA + KV_LORA + 2 * LANES:].T[:N_GATES, :] + bgt_ref[...]
    ang = pos_ref[...].astype(F32) * freq_ref[...]
    cos_a = jnp.cos(ang)
    sin_a = jnp.sin(ang) * sgn_ref[...]
    scale = (QK_NOPE_DIM + QK_ROPE_DIM) ** -0.5 * LOG2_E
    lane = lax.broadcasted_iota(I32, ang.shape, 1)
    rope_mul = jnp.where(lane < QK_ROPE_DIM, cos_a, sin_a) * scale
    qf = _dot(_rms(cq, qn_ref[...]).astype(BF16), wq_ref[...])
    ckvn = _rms(ckv, kvn_ref[...]).astype(BF16)
    kn = _dot(ckvn, wk_ref[...])
    v_ref[...] = _dot(ckvn, wv_ref[...]).astype(BF16)
    k_rope = (kr2 * cos_a + krs2 * sin_a).astype(BF16)
    for h in range(N_MLA_HEADS):
        o = h * QK_SLAB
        q_ref[:, o:o + LANES] = (qf[:, o:o + LANES] * scale).astype(BF16)
        q_ref[:, o + LANES:o + QK_SLAB] = (qf[:, o + LANES:o + QK_SLAB] * rope_mul).astype(BF16)
        k_ref[:, o:o + LANES] = kn[:, h * LANES:(h + 1) * LANES].astype(BF16)
        k_ref[:, o + LANES:o + QK_SLAB] = k_rope


def _in_proj(x2, pos2, attn_norm, w_in, b_gates, q_norm, w_q_up, kv_norm, w_kv_up):
    n, d = x2.shape
    tm = ROW_TILE
    half = QK_ROPE_DIM // 2
    swap = jnp.concatenate([jnp.arange(half, QK_ROPE_DIM), jnp.arange(0, half)])
    w_kr = w_in[:, OFF_KR:OFF_KR + QK_ROPE_DIM]
    w_krs = w_kr[:, swap]
    wm = w_in[:, :OFF_G].astype(BF16)
    w_g = jnp.pad(w_in[:, OFF_G:OFF_CQ], ((0, 0), (0, LANES - N_GATES)))
    wr = jnp.concatenate([w_in[:, OFF_CQ:OFF_KR], w_kr, w_kr, w_krs, w_krs, w_g], axis=1).astype(BF16)
    bgt = b_gates.reshape(N_GATES, 1)
    wq4 = w_q_up.reshape(Q_LORA, N_MLA_HEADS, QK_NOPE_DIM + QK_ROPE_DIM)
    wq_pe = wq4[:, :, QK_NOPE_DIM:]
    wq = jnp.concatenate([wq4, wq_pe[:, :, swap]], axis=2).reshape(Q_LORA, N_MLA_HEADS * QK_SLAB).astype(BF16)
    wkv4 = w_kv_up.reshape(KV_LORA, N_MLA_HEADS, QK_NOPE_DIM + V_HEAD_DIM)
    wk = wkv4[:, :, :QK_NOPE_DIM].reshape(KV_LORA, N_MLA_HEADS * QK_NOPE_DIM).astype(BF16)
    wv = wkv4[:, :, QK_NOPE_DIM:].reshape(KV_LORA, D_MLA).astype(BF16)
    freqs = ROPE_THETA ** (-jnp.arange(0, QK_ROPE_DIM, 2, dtype=F32) / QK_ROPE_DIM)
    freq_l = jnp.tile(freqs, LANES // half).reshape(1, LANES)
    sgn_l = jnp.tile(jnp.concatenate([-jnp.ones((half,), F32), jnp.ones((half,), F32)]),
                     LANES // QK_ROPE_DIM).reshape(1, LANES)
    full = lambda arr: pl.BlockSpec(arr.shape, lambda i: (0,) * arr.ndim)
    rows = lambda w: pl.BlockSpec((tm, w), lambda i: (i, 0))
    consts = [attn_norm.reshape(1, d), wm, wr, bgt, q_norm.reshape(1, Q_LORA), wq,
              kv_norm.reshape(1, KV_LORA), wk, wv, freq_l, sgn_l]
    return pl.pallas_call(
        _in_proj_kernel,
        grid=(n // tm,),
        in_specs=[rows(d), rows(1)] + [full(c) for c in consts],
        out_specs=[rows(OFF_G), pl.BlockSpec((N_GATES, tm), lambda i: (0, i)),
                   rows(N_MLA_HEADS * QK_SLAB), rows(N_MLA_HEADS * QK_SLAB), rows(D_MLA)],
        out_shape=[jax.ShapeDtypeStruct((n, OFF_G), F32), jax.ShapeDtypeStruct((N_GATES, n), F32),
                   jax.ShapeDtypeStruct((n, N_MLA_HEADS * QK_SLAB), BF16),
                   jax.ShapeDtypeStruct((n, N_MLA_HEADS * QK_SLAB), BF16),
                   jax.ShapeDtypeStruct((n, D_MLA), BF16)],
        compiler_params=_params("parallel"),
        name="in_proj",
    )(x2, pos2, *consts)


def _mlstm_kernel(q_ref, k_ref, v_ref, o_ref, g_ref, cwq_ref, cwk_ref, cbq_ref, cbk_ref, nrm_ref,
                  y_ref,
                  qc_ref, kc_ref, va_ref, cst_ref, b_ref, e_ref, r_ref, ew_ref, mw_ref, bt_ref, mp_ref):
    L = MLSTM_CHUNK
    dh = MLSTM_HEAD_DIM
    nc = q_ref.shape[1] // L
    s_len = q_ref.shape[1]
    h = pl.program_id(1)

    for d in range(2):
        ig = g_ref[2 * d * N_MLSTM_HEADS + h, 0]
        fg = g_ref[(2 * d + 1) * N_MLSTM_HEADS + h, 0]
        b = _cumsum_lanes(_log_sigmoid(fg), reverse=(d == 1))
        btot = b[:, L - 1:L] if d == 0 else b[:, 0:1]
        r = ig - b
        w = btot + r
        mw = jnp.max(w, axis=-1, keepdims=True)
        b_ref[d] = b
        e_ref[d] = b + _cummax_lanes(r, reverse=(d == 1))
        r_ref[d] = r
        ew_ref[d] = jnp.exp(w - mw)
        mw_ref[d] = jnp.broadcast_to(mw, (nc, L))
        bt_ref[d] = jnp.broadcast_to(btot, (nc, L))

    row = lax.broadcasted_iota(I32, (L, dh), 0)

    def conv_silu(ref, cw_ref, cb_ref, c):
        start = pl.multiple_of(c * L, L)
        x = ref[0, pl.ds(start, L), :]
        prev_row = jnp.where(c > 0, ref[0, pl.ds(jnp.maximum(start - 1, 0), 1), :], 0.0)
        next_row = jnp.where(c < nc - 1, ref[0, pl.ds(jnp.minimum(start + L, s_len - 1), 1), :], 0.0)
        x_prev = jnp.where(row == 0, prev_row, pltpu.roll(x, 1, 0))
        x_next = jnp.where(row == L - 1, next_row, pltpu.roll(x, L - 1, 0))
        y = cw_ref[0:1, :] * x_prev + cw_ref[1:2, :] * x + cw_ref[2:3, :] * x_next + cb_ref[...]
        return y * jax.nn.sigmoid(y)

    ones_blk = jnp.ones((L, dh), BF16)

    def pass1(c, carry):
        start = pl.multiple_of(c * L, L)
        qc_ref[pl.ds(start, L), :] = conv_silu(q_ref, cwq_ref, cbq_ref, c).astype(BF16)
        kk = conv_silu(k_ref, cwk_ref, cbk_ref, c) * (dh ** -0.5)
        kc_ref[pl.ds(start, L), :] = kk.astype(BF16)
        va = jnp.concatenate([v_ref[0, pl.ds(start, L), :].astype(BF16), ones_blk], axis=1)
        va_ref[pl.ds(start, L), :] = va
        kt = kk.T
        for d in range(2):
            kw_t = (kt * ew_ref[d, pl.ds(c, 1), :]).astype(BF16)
            cst_ref[d, c] = _dot(kw_t, va)
        return carry

    lax.fori_loop(0, nc, pass1, 0, unroll=CHUNK_UNROLL)

    def scan(i, carry):
        out = []
        for d in range(2):
            st, m = carry[d]
            c = i if d == 0 else nc - 1 - i
            mw = mw_ref[d, pl.ds(c, 1), :]
            bt = bt_ref[d, pl.ds(c, 1), :]
            m_new = jnp.maximum(bt + m, mw)
            a = jnp.exp(bt + m - m_new)[:, 0:1]
            cc = jnp.exp(mw - m_new)[:, 0:1]
            loc = cst_ref[d, c]
            cst_ref[d, c] = st
            mp_ref[d, pl.ds(c, 1), :] = m
            out.append((a * st + cc * loc, m_new))
        return tuple(out)

    init = (jnp.zeros((dh, 2 * dh), F32), jnp.zeros((1, L), F32))
    lax.fori_loop(0, nc, scan, (init, init))

    ti = lax.broadcasted_iota(I32, (L, L), 0)
    si = lax.broadcasted_iota(I32, (L, L), 1)
    masks = (si <= ti, si >= ti)

    def pass3(c, carry):
        start = pl.multiple_of(c * L, L)
        q = qc_ref[pl.ds(start, L), :]
        k = kc_ref[pl.ds(start, L), :]
        va = va_ref[pl.ds(start, L), :]
        qk = _dot_nt(q, k)
        hsum = jnp.zeros((L, dh), F32)
        for d in range(2):
            bmat = jnp.broadcast_to(b_ref[d, pl.ds(c, 1), :], (L, L)).T
            emat = jnp.broadcast_to(e_ref[d, pl.ds(c, 1), :], (L, L)).T
            dmat = jnp.where(masks[d], bmat + r_ref[d, pl.ds(c, 1), :], -jnp.inf)
            inter = bmat + mp_ref[d, pl.ds(c, 1), :]
            m_t = jnp.maximum(inter, emat)
            sc = qk * jnp.exp(dmat - m_t)
            a = jnp.exp(inter - m_t)
            intra = _dot(sc.astype(BF16), va)
            cross = _dot(q, cst_ref[d, c].astype(BF16))
            num = intra[:, :dh] + a * cross[:, :dh]
            den = intra[:, dh:] + a * cross[:, dh:]
            hsum = hsum + num / jnp.maximum(jnp.abs(den), jnp.exp(-m_t))
        hn = _rms(hsum, nrm_ref[...])
        y_ref[0, pl.ds(start, L), :] = hn * jax.nn.sigmoid(o_ref[0, pl.ds(start, L), :])
        return carry

    lax.fori_loop(0, nc, pass3, 0, unroll=CHUNK_UNROLL)


def _mlstm(um3, gt4, conv_w, conv_b, mlstm_norm):
    bsz, s, _ = um3.shape
    H, dh, L = N_MLSTM_HEADS, MLSTM_HEAD_DIM, MLSTM_CHUNK
    nc = s // L
    col = lambda off: pl.BlockSpec((1, s, dh), lambda b, h: (b, 0, off + h))
    vec = lambda rows, off: pl.BlockSpec((rows, dh), lambda b, h: (0, off + h))
    cb = conv_b.reshape(1, 2 * D_MLSTM)
    return pl.pallas_call(
        _mlstm_kernel,
        grid=(bsz, H),
        in_specs=[col(0), col(H), col(2 * H), col(3 * H),
                  pl.BlockSpec((N_GATES, 1, nc, L), lambda b, h: (0, b, 0, 0)),
                  vec(3, 0), vec(3, H), vec(1, 0), vec(1, H), vec(1, 0)],
        out_specs=pl.BlockSpec((1, s, dh), lambda b, h: (b, 0, h)),
        out_shape=jax.ShapeDtypeStruct((bsz, s, D_MLSTM), F32),
        scratch_shapes=[pltpu.VMEM((s, dh), BF16), pltpu.VMEM((s, dh), BF16), pltpu.VMEM((s, 2 * dh), BF16),
                        pltpu.VMEM((2, nc, dh, 2 * dh), F32)]
                       + [pltpu.VMEM((2, nc, L), F32) for _ in range(7)],
        compiler_params=_params("parallel", "parallel"),
        name="mlstm",
    )(um3, um3, um3, um3, gt4, conv_w, conv_w, cb, cb, mlstm_norm.reshape(1, D_MLSTM))


def _attn_kernel(q_ref, k_ref, v_ref, o_ref):
    tq = q_ref.shape[1]
    q = q_ref[0]
    m = jnp.full((tq, 1), -jnp.inf, F32)
    l = jnp.zeros((tq, 1), F32)
    acc = jnp.zeros((tq, V_HEAD_DIM), F32)
    chunk = min(KV_CHUNK, k_ref.shape[1])
    for c in range(k_ref.shape[1] // chunk):
        keys = slice(c * chunk, (c + 1) * chunk)
        s = _dot_nt(q, k_ref[0, keys, :])
        m_new = jnp.maximum(m, jnp.max(s, axis=-1, keepdims=True))
        alpha = jnp.exp2(m - m_new)
        p = jnp.exp2(s - m_new)
        l = alpha * l + jnp.sum(p, axis=-1, keepdims=True)
        acc = alpha * acc + _dot(p.astype(BF16), v_ref[0, keys, :])
        m = m_new
    o_ref[0] = acc / l


def _attention(q3, k3, v3):
    bsz, s, _ = q3.shape
    tq = min(Q_TILE, s)
    return pl.pallas_call(
        _attn_kernel,
        grid=(bsz, N_MLA_HEADS, s // tq),
        in_specs=[pl.BlockSpec((1, tq, QK_SLAB), lambda b, h, i: (b, i, h)),
                  pl.BlockSpec((1, s, QK_SLAB), lambda b, h, i: (b, 0, h)),
                  pl.BlockSpec((1, s, V_HEAD_DIM), lambda b, h, i: (b, 0, h))],
        out_specs=pl.BlockSpec((1, tq, V_HEAD_DIM), lambda b, h, i: (b, i, h)),
        out_shape=jax.ShapeDtypeStruct((bsz, s, D_MLA), F32),
        compiler_params=_params("parallel", "parallel", "parallel"),
        name="attention",
    )(q3, k3, v3)


def _out_route_kernel(ym_ref, ya_ref, x_ref, mn_ref, wom_ref, woa_ref, fn_ref, wr_ref, br_ref,
                      h_ref, xn_ref, gate_ref, slot_ref, slot_t_ref, seg_ref, size_ref,
                      carry_ref):
    i = pl.program_id(0)
    tm = x_ref.shape[0]

    @pl.when(i == 0)
    def _():
        carry_ref[...] = jnp.zeros_like(carry_ref)

    ya = _rms(ya_ref[...], mn_ref[...])
    h1 = x_ref[...] + _dot(ym_ref[...].astype(BF16), wom_ref[...]) + _dot(ya.astype(BF16), woa_ref[...])
    h_ref[...] = h1
    xn = _rms(h1, fn_ref[...])
    xn_hi = xn.astype(BF16)
    xn_ref[...] = xn_hi
    xn_lo = (xn - xn_hi.astype(F32)).astype(BF16)
    both = _dot_nt(wr_ref[...], xn_hi)
    logits = (both[:N_EXPERTS] + both[N_EXPERTS:] + _dot_nt(wr_ref[:N_EXPERTS, :], xn_lo)) + br_ref[...]
    erow = lax.broadcasted_iota(I32, logits.shape, 0)
    work = logits
    vals, hots = [], []
    for k in range(TOP_K):
        mx = jnp.max(work, axis=0, keepdims=True)
        idx = jnp.min(jnp.where(work == mx, erow, N_EXPERTS), axis=0, keepdims=True)
        hot = erow == idx
        work = jnp.where(hot, -jnp.inf, work)
        vals.append(mx)
        hots.append(hot)
    exps = [jnp.exp(v - vals[0]) for v in vals]
    tot = exps[0] + exps[1] + exps[2] + exps[3]
    multi = (hots[0] | hots[1] | hots[2] | hots[3]).astype(BF16)
    ti = lax.broadcasted_iota(I32, (tm, tm), 0)
    tj = lax.broadcasted_iota(I32, (tm, tm), 1)
    local_rank = _dot(multi, (ti < tj).astype(BF16))
    multi_rows = jnp.concatenate([multi, jnp.zeros((LANES - N_EXPERTS, tm), BF16)], axis=0)
    count = _dot_nt(jnp.ones((SUBLANES, tm), BF16), multi_rows)[0:1]
    padded = jnp.ceil(count * (1.0 / SEG_ALIGN)) * SEG_ALIGN
    filled = _cumsum_lanes(padded, reverse=False)
    local_start = filled - padded
    start_col = jnp.broadcast_to(local_start, (SUBLANES, LANES)).T[:N_EXPERTS, 0:1]
    slot_all = local_rank + start_col
    krow = lax.broadcasted_iota(I32, (SUBLANES, tm), 0)
    packed = jnp.zeros((SUBLANES, tm), F32)
    for k in range(TOP_K):
        sk = jnp.sum(jnp.where(hots[k], slot_all, 0.0), axis=0, keepdims=True)
        packed = jnp.where(krow == k, exps[k] / tot, jnp.where(krow == TOP_K + k, sk, packed))
    slot_t_ref[0] = jnp.where(krow < TOP_K, pltpu.roll(packed, TOP_K, 0), 0.0).astype(I32)
    cols = packed.T
    gate_ref[...] = cols[:, :TOP_K]
    slot_ref[...] = cols[:, TOP_K:2 * TOP_K].astype(I32)
    srow = lax.broadcasted_iota(I32, (SUBLANES, LANES), 0)
    seg = jnp.where(srow == 0, padded, jnp.where(srow == 1, local_start, jnp.where(srow == 2, carry_ref[...],
                                                                                  filled[:, LANES - 1:LANES])))
    seg_ref[0] = seg.astype(I32)
    carry_ref[...] += padded
    size_ref[...] = carry_ref[...].astype(I32)


def _out_route(ym2, ya2, x2, mla_norm, w_out, ffn_norm, w_router, b_router):
    n, d = x2.shape
    tm = ROW_TILE
    nt = n // tm
    wom = w_out[:D_MLSTM].astype(BF16)
    woa = w_out[D_MLSTM:].astype(BF16)
    wr_hi = w_router.T.astype(BF16)
    wr_lo = (w_router.T - wr_hi.astype(F32)).astype(BF16)
    wr = jnp.concatenate([wr_hi, wr_lo], axis=0)
    br = b_router.reshape(N_EXPERTS, 1)
    full = lambda arr: pl.BlockSpec(arr.shape, lambda i: (0,) * arr.ndim)
    rows = lambda w: pl.BlockSpec((tm, w), lambda i: (i, 0))
    consts = [mla_norm.reshape(1, D_MLA), wom, woa, ffn_norm.reshape(1, d), wr, br]
    return pl.pallas_call(
        _out_route_kernel,
        grid=(nt,),
        in_specs=[rows(D_MLSTM), rows(D_MLA), rows(d)] + [full(c) for c in consts],
        out_specs=[rows(d), rows(d), rows(TOP_K), rows(TOP_K),
                   pl.BlockSpec((1, SUBLANES, tm), lambda i: (i, 0, 0)),
                   pl.BlockSpec((1, SUBLANES, LANES), lambda i: (i, 0, 0)),
                   pl.BlockSpec((1, LANES), lambda i: (0, 0))],
        out_shape=[jax.ShapeDtypeStruct((n, d), F32), jax.ShapeDtypeStruct((n, d), BF16),
                   jax.ShapeDtypeStruct((n, TOP_K), F32), jax.ShapeDtypeStruct((n, TOP_K), I32),
                   jax.ShapeDtypeStruct((nt, SUBLANES, tm), I32), jax.ShapeDtypeStruct((nt, SUBLANES, LANES), I32),
                   jax.ShapeDtypeStruct((1, LANES), I32)],
        scratch_shapes=[pltpu.VMEM((1, LANES), F32)],
        compiler_params=_params("arbitrary"),
        name="out_route",
    )(ym2, ya2, x2, *consts)


def _segment_copies(seg_ref, starts_ref, local_ref, global_ref, sem, to_global, wait):
    def copy(src, dst, rows):
        loc = local_ref.at[pl.ds(pl.multiple_of(src, SEG_ALIGN), rows)]
        glo = global_ref.at[pl.ds(pl.multiple_of(dst, SEG_ALIGN), rows)]
        return pltpu.make_async_copy(loc, glo, sem) if to_global else pltpu.make_async_copy(glo, loc, sem)

    if wait:
        copy(0, 0, pl.multiple_of(seg_ref[0, 3, 0], SEG_ALIGN)).wait()
        return

    def per_expert(e, carry):
        size = seg_ref[0, 0, e]
        src = seg_ref[0, 1, e]
        dst = starts_ref[e] + seg_ref[0, 2, e]
        off = 0
        rows = ROW_TILE
        while rows >= SEG_ALIGN:
            @pl.when((size & rows) != 0)
            def _(off=off, rows=rows):
                copy(src + off, dst + off, rows).start()

            off = off + (size & rows)
            rows //= 2
        return carry

    lax.fori_loop(0, N_EXPERTS, per_expert, 0)


def _dispatch_kernel(starts_ref, seg_ref, segp_ref, slot_t_ref, xn_ref, xs_ref, sort_ref, zero_ref, sem, zsem):
    i = pl.program_id(0)
    tm = xn_ref.shape[0]
    tmx = zero_ref.shape[0]
    cap = sort_ref.shape[1]

    @pl.when(i == 0)
    def _():
        zero_ref[...] = jnp.zeros_like(zero_ref)
        n_tail = (xs_ref.shape[0] - starts_ref[N_EXPERTS]) // tmx

        def clear_tile(row, wait):
            cp = pltpu.make_async_copy(zero_ref, xs_ref.at[pl.ds(pl.multiple_of(row, tmx), tmx)], zsem)
            if wait:
                cp.wait()
            else:
                cp.start()

        for wait in (False, True):
            def clear_group(e, carry, wait=wait):
                hi = starts_ref[e + 1]

                @pl.when(hi > starts_ref[e])
                def _():
                    clear_tile(hi - tmx, wait)

                return carry

            def clear_tail(t, carry, wait=wait):
                clear_tile(starts_ref[N_EXPERTS] + t * tmx, wait)
                return carry

            lax.fori_loop(0, N_EXPERTS, clear_group, 0)
            lax.fori_loop(0, n_tail, clear_tail, 0)

    slot = i % 2
    for r0 in range(0, cap, SORT_CHUNK):
        pos = lax.broadcasted_iota(I32, (SORT_CHUNK, tm), 0) + r0
        hit = pos == slot_t_ref[0, 0:1, :]
        for k in range(1, TOP_K):
            hit = hit | (pos == slot_t_ref[0, k:k + 1, :])
        sort_ref[slot, r0:r0 + SORT_CHUNK, :] = _pack_halves(_dot(hit.astype(BF16), xn_ref[...]))
    _segment_copies(seg_ref, starts_ref, sort_ref.at[slot], xs_ref, sem.at[slot], True, False)

    @pl.when(i >= 1)
    def _():
        _segment_copies(segp_ref, starts_ref, sort_ref.at[1 - slot], xs_ref, sem.at[1 - slot], True, True)

    @pl.when(i == pl.num_programs(0) - 1)
    def _():
        _segment_copies(seg_ref, starts_ref, sort_ref.at[slot], xs_ref, sem.at[slot], True, True)


def _dispatch(starts, seg, slot_t, xn, n_rows):
    n, d = xn.shape
    tm = ROW_TILE
    nt = n // tm
    smem = lambda f: pl.BlockSpec((1, SUBLANES, LANES), f, memory_space=pltpu.SMEM)
    any_spec = pl.BlockSpec(memory_space=pl.ANY)
    return pl.pallas_call(
        _dispatch_kernel,
        grid_spec=pltpu.PrefetchScalarGridSpec(
            num_scalar_prefetch=1,
            grid=(nt,),
            in_specs=[smem(lambda i, *_: (i, 0, 0)), smem(lambda i, *_: (jnp.maximum(i - 1, 0), 0, 0)),
                      pl.BlockSpec((1, SUBLANES, tm), lambda i, *_: (i, 0, 0)),
                      pl.BlockSpec((tm, d), lambda i, *_: (i, 0))],
            out_specs=any_spec,
            scratch_shapes=[pltpu.VMEM((2, SORT_ROWS, d // 2), U32), pltpu.VMEM((EXPERT_TILE, d // 2), U32),
                            pltpu.SemaphoreType.DMA((2,)), pltpu.SemaphoreType.DMA],
        ),
        out_shape=jax.ShapeDtypeStruct((n_rows, d // 2), U32),
        compiler_params=_params("arbitrary"),
        name="dispatch",
    )(starts, seg, seg, slot_t, xn)


def _experts_kernel(te_ref, tb_ref, tv_ref, xs_ref, wgu_ref, bgu_ref, wd_ref, bd_ref, out_ref,
                    wgu_bf, wd_bf):
    i = pl.program_id(0)
    de = wd_ref.shape[1]
    prev = te_ref[jnp.maximum(i - 1, 0)]

    @pl.when(jnp.logical_or(i == 0, te_ref[i] != prev))
    def _():
        wgu_bf[...] = wgu_ref[0].astype(BF16)
        wd_bf[...] = wd_ref[0].astype(BF16)

    @pl.when(tv_ref[i] == 1)
    def _():
        x = _unpack_halves(xs_ref[...])
        ch = 512
        acc = jnp.zeros((xs_ref.shape[0], bd_ref.shape[2]), F32)
        for j in range(de // ch):
            g = _dot(x, wgu_bf[:, j * ch:(j + 1) * ch]) + bgu_ref[0, :, j * ch:(j + 1) * ch]
            u = _dot(x, wgu_bf[:, de + j * ch:de + (j + 1) * ch]) + bgu_ref[0, :, de + j * ch:de + (j + 1) * ch]
            g = jnp.minimum(g, SWIGLU_LIMIT)
            u = jnp.clip(u, -SWIGLU_LIMIT, SWIGLU_LIMIT)
            hm = (u + 1.0) * (g * jax.nn.sigmoid(g * SWIGLU_ALPHA))
            acc = acc + _dot(hm.astype(BF16), wd_bf[j * ch:(j + 1) * ch, :])
        out_ref[...] = _pack_halves((acc + bd_ref[0]).astype(BF16).astype(F32))

    @pl.when(tv_ref[i] == 0)
    def _():
        out_ref[...] = jnp.zeros_like(out_ref)


def _experts(tile_e, tile_b, tile_v, xs, w_gate_up, b_gate_up, w_down, b_down):
    n_rows, dw = xs.shape
    tmx = EXPERT_TILE
    ne, d, de2 = w_gate_up.shape
    de = de2 // 2
    return pl.pallas_call(
        _experts_kernel,
        grid_spec=pltpu.PrefetchScalarGridSpec(
            num_scalar_prefetch=3,
            grid=(n_rows // tmx,),
            in_specs=[pl.BlockSpec((tmx, dw), lambda i, te, tb, tv: (tb[i], 0)),
                      pl.BlockSpec((1, d, de2), lambda i, te, tb, tv: (te[i], 0, 0)),
                      pl.BlockSpec((1, 1, de2), lambda i, te, tb, tv: (te[i], 0, 0)),
                      pl.BlockSpec((1, de, d), lambda i, te, tb, tv: (te[i], 0, 0)),
                      pl.BlockSpec((1, 1, d), lambda i, te, tb, tv: (te[i], 0, 0))],
            out_specs=pl.BlockSpec((tmx, dw), lambda i, te, tb, tv: (i, 0)),
            scratch_shapes=[pltpu.VMEM((d, de2), BF16), pltpu.VMEM((de, d), BF16)],
        ),
        out_shape=jax.ShapeDtypeStruct((n_rows, dw), U32),
        compiler_params=_params("arbitrary"),
        name="experts",
    )(tile_e, tile_b, tile_v, xs, w_gate_up, b_gate_up.reshape(ne, 1, de2), w_down, b_down.reshape(ne, 1, d))


def _combine_kernel(starts_ref, seg_ref, segn_ref, h_ref, gate_ref, slot_ref, p_ref, ys_ref, pn_ref, wg_ref,
                    wp_ref, fn_ref, out_ref, ybuf, sem, *, final):
    i = pl.program_id(0)
    nt = pl.num_programs(0)
    tm = h_ref.shape[0]
    cap = ybuf.shape[1]

    @pl.when(i == 0)
    def _():
        ybuf[...] = jnp.zeros_like(ybuf)
        _segment_copies(seg_ref, starts_ref, ybuf.at[0], ys_ref, sem.at[0], False, False)

    slot = i % 2

    @pl.when(i + 1 < nt)
    def _():
        _segment_copies(segn_ref, starts_ref, ybuf.at[1 - slot], ys_ref, sem.at[1 - slot], False, False)

    _segment_copies(seg_ref, starts_ref, ybuf.at[slot], ys_ref, sem.at[slot], False, True)

    gate = gate_ref[...]
    h2 = h_ref[...]
    for c0 in range(0, cap, SORT_CHUNK):
        pos = lax.broadcasted_iota(I32, (tm, SORT_CHUNK), 1) + c0
        weights = jnp.zeros((tm, SORT_CHUNK), F32)
        for k in range(TOP_K):
            weights = jnp.where(pos == slot_ref[:, k:k + 1], gate[:, k:k + 1], weights)
        h2 = h2 + _dot(weights.astype(BF16), _unpack_halves(ybuf[slot, c0:c0 + SORT_CHUNK, :]))
    hn = _rms(h2, pn_ref[...]).astype(BF16)
    sg = jax.nn.sigmoid(_dot(hn, wg_ref[...]))
    h3 = h2 + sg * _dot(p_ref[...].astype(BF16), wp_ref[...])
    out_ref[...] = _rms(h3, fn_ref[...]) if final else h3


def _combine(starts, seg, h1, gate, slot, p2, ys, ple_norm, w_ple_gate, w_ple_proj, final_norm, final):
    n, d = h1.shape
    tm = ROW_TILE
    nt = n // tm
    wg = w_ple_gate.astype(BF16)
    wp = w_ple_proj.astype(BF16)
    smem = lambda f: pl.BlockSpec((1, SUBLANES, LANES), f, memory_space=pltpu.SMEM)
    full = lambda arr: pl.BlockSpec(arr.shape, lambda i, *_: (0,) * arr.ndim)
    rows = lambda w: pl.BlockSpec((tm, w), lambda i, *_: (i, 0))
    consts = [ple_norm.reshape(1, d), wg, wp, final_norm.reshape(1, d)]
    return pl.pallas_call(
        functools.partial(_combine_kernel, final=final),
        grid_spec=pltpu.PrefetchScalarGridSpec(
            num_scalar_prefetch=1,
            grid=(nt,),
            in_specs=[smem(lambda i, *_: (i, 0, 0)), smem(lambda i, *_: (jnp.minimum(i + 1, nt - 1), 0, 0)),
                      rows(d), rows(TOP_K), rows(TOP_K), rows(p2.shape[1]), pl.BlockSpec(memory_space=pl.ANY)]
                     + [full(c) for c in consts],
            out_specs=rows(d),
            scratch_shapes=[pltpu.VMEM((2, SORT_ROWS, d // 2), U32), pltpu.SemaphoreType.DMA((2,))],
        ),
        out_shape=jax.ShapeDtypeStruct((n, d), F32),
        compiler_params=_params("arbitrary"),
        name="combine",
    )(starts, seg, seg, h1, gate, slot, p2, ys, *consts)


def _route_tables(sizes, n_tiles):
    tmx = EXPERT_TILE
    tile_end = jnp.cumsum((sizes + tmx - 1) // tmx)
    starts = jnp.concatenate([jnp.zeros((1,), I32), tile_end * tmx]).astype(I32)
    n_valid = tile_end[-1]
    t = jnp.arange(n_tiles, dtype=I32)
    tb = jnp.minimum(t, n_valid - 1).astype(I32)
    te = jnp.sum(tile_end[None, :] <= tb[:, None], axis=1).astype(I32)
    tv = (t < n_valid).astype(I32)
    return starts, te, tb, tv


def kernel(x, p, positions, attn_norm, w_in, b_gates, conv_w, conv_b, mlstm_norm, q_norm, w_q_up, kv_norm, w_kv_up, mla_norm, w_out, ffn_norm, w_router, b_router, w_gate_up, b_gate_up, w_down, b_down, ple_norm, w_ple_gate, w_ple_proj, final_norm):
    bsz, s, d = x.shape
    n = bsz * s
    depth = p.shape[0]
    nc = s // MLSTM_CHUNK
    max_rows = n * TOP_K + (n // ROW_TILE) * N_EXPERTS * (SEG_ALIGN - 1) + N_EXPERTS * (EXPERT_TILE - 1)
    n_tiles = max_rows // EXPERT_TILE
    pos2 = positions.reshape(n, 1)
    h = x.reshape(n, d)
    for i in range(depth):
        um, gt, q, k, v = _in_proj(h, pos2, attn_norm[i], w_in[i], b_gates[i], q_norm[i], w_q_up[i],
                                   kv_norm[i], w_kv_up[i])
        ym = _mlstm(um.reshape(bsz, s, -1), gt.reshape(N_GATES, bsz, nc, MLSTM_CHUNK), conv_w[i], conv_b[i],
                    mlstm_norm[i])
        ya = _attention(q.reshape(bsz, s, -1), k.reshape(bsz, s, -1), v.reshape(bsz, s, -1))
        h1, xn, gate, slot, slot_t, seg, sizes = _out_route(ym.reshape(n, -1), ya.reshape(n, -1), h, mla_norm[i],
                                                            w_out[i], ffn_norm[i], w_router[i], b_router[i])
        starts, te, tb, tv = _route_tables(sizes[0, :N_EXPERTS], n_tiles)
        xs = _dispatch(starts, seg, slot_t, xn, n_tiles * EXPERT_TILE)
        ys = _experts(te, tb, tv, xs, w_gate_up[i], b_gate_up[i], w_down[i], b_down[i])
        h = _combine(starts, seg, h1, gate, slot, p[i].reshape(n, -1), ys, ple_norm[i], w_ple_gate[i],
                     w_ple_proj[i], final_norm, final=(i == depth - 1))
    return h.reshape(bsz, s, d)
```

```python
import functools

import jax
import jax.numpy as jnp
from jax import lax
from jax.experimental import pallas as pl
from jax.experimental.pallas import tpu as pltpu

F32 = jnp.float32
BF16 = jnp.bfloat16
I32 = jnp.int32
U32 = jnp.uint32

N_MLSTM_HEADS = 4
MLSTM_HEAD_DIM = 128
D_MLSTM = N_MLSTM_HEADS * MLSTM_HEAD_DIM
MLSTM_CHUNK = 128
N_MLA_HEADS = 4
QK_NOPE_DIM = 128
QK_ROPE_DIM = 64
V_HEAD_DIM = 128
D_MLA = N_MLA_HEADS * V_HEAD_DIM
Q_LORA = 256
KV_LORA = 128
ROPE_THETA = 10000.0
N_EXPERTS = 32
TOP_K = 4
SWIGLU_LIMIT = 7.0
SWIGLU_ALPHA = 1.702
EPS = 1e-6
N_GATES = 4 * N_MLSTM_HEADS
OFF_G = 4 * D_MLSTM
OFF_CQ = OFF_G + N_GATES
OFF_CKV = OFF_CQ + Q_LORA
OFF_KR = OFF_CKV + KV_LORA

LANES = 128
SUBLANES = 8
QK_SLAB = 2 * LANES
VMEM_LIMIT_BYTES = 56 * 1024 * 1024
LOG2_E = 1.4426950408889634

ROW_TILE = 512
Q_TILE = 2048
KV_CHUNK = 1024
EXPERT_TILE = 512
CHUNK_UNROLL = 8
SEG_ALIGN = SUBLANES
SORT_CHUNK = 256
SORT_ROWS = ROW_TILE * TOP_K + N_EXPERTS * SEG_ALIGN


def _dot(a, b):
    return jnp.dot(a, b, preferred_element_type=F32)


def _dot_nt(a, b):
    return lax.dot_general(a, b, (((1,), (1,)), ((), ())), preferred_element_type=F32)


def _rms(x, g):
    return x * lax.rsqrt(jnp.mean(x * x, axis=-1, keepdims=True) + EPS) * g


def _log_sigmoid(x):
    return jnp.minimum(x, 0.0) - jnp.log(1.0 + jnp.exp(-jnp.abs(x)))


def _scan_lanes(x, reverse, op, identity):
    n = x.shape[-1]
    lane = lax.broadcasted_iota(I32, x.shape, x.ndim - 1)
    sh = 1
    while sh < n:
        if reverse:
            x = op(x, jnp.where(lane < n - sh, pltpu.roll(x, n - sh, x.ndim - 1), identity))
        else:
            x = op(x, jnp.where(lane >= sh, pltpu.roll(x, sh, x.ndim - 1), identity))
        sh *= 2
    return x


def _cumsum_lanes(x, reverse):
    return _scan_lanes(x, reverse, jnp.add, 0.0)


def _cummax_lanes(x, reverse):
    return _scan_lanes(x, reverse, jnp.maximum, -jnp.inf)


def _pack_halves(x):
    w = x.shape[1] // 2
    lo = lax.shift_right_logical(lax.bitcast_convert_type(x[:, :w], U32), jnp.uint32(16))
    hi = lax.bitcast_convert_type(x[:, w:], U32)
    return lo | hi


def _unpack_halves(words):
    lo = lax.bitcast_convert_type(lax.shift_left(words, jnp.uint32(16)), F32)
    hi = lax.bitcast_convert_type(words & jnp.uint32(0xFFFF0000), F32)
    return jnp.concatenate([lo, hi], axis=1).astype(BF16)


def _params(*sem):
    return pltpu.CompilerParams(dimension_semantics=sem, vmem_limit_bytes=VMEM_LIMIT_BYTES)


def _in_proj_kernel(x_ref, pos_ref, an_ref, wm_ref, wr_ref, bgt_ref, qn_ref, wq_ref, kvn_ref,
                    wk_ref, wv_ref, freq_ref,
                    um_ref, gt_ref, q_ref, k_ref, v_ref):
    a = _rms(x_ref[...], an_ref[...]).astype(BF16)
    um_ref[...] = _dot(a, wm_ref[...])
    rest = _dot(a, wr_ref[...])
    cq = rest[:, :Q_LORA]
    ckv = rest[:, Q_LORA:Q_LORA + KV_LORA]
    kr2 = rest[:, Q_LORA + KV_LORA:Q_LORA + KV_LORA + LANES]
    krs2 = rest[:, Q_LORA + KV_LORA + LANES:Q_LORA + KV_LORA + 2 * LANES]
    gt_ref[...] = rest[:, Q_LORA + KV_LORA + 2 * LANES:].T[:N_GATES, :] + bgt_ref[...]
    ang = freq_ref[...] * pos_ref[0].astype(F32)
    cos_t = jnp.cos(ang)
    sin_t = jnp.sin(ang)
    cos_a = jnp.concatenate([cos_t] * (LANES // cos_t.shape[0]), axis=0).T
    sin_a = jnp.concatenate([-sin_t, sin_t] * (LANES // (2 * sin_t.shape[0])), axis=0).T
    scale = (QK_NOPE_DIM + QK_ROPE_DIM) ** -0.5 * LOG2_E
    lane = lax.broadcasted_iota(I32, cos_a.shape, 1)
    rope_mul = jnp.where(lane < QK_ROPE_DIM, cos_a, sin_a) * scale
    qf = _dot(_rms(cq, qn_ref[...]).astype(BF16), wq_ref[...])
    ckvn = _rms(ckv, kvn_ref[...]).astype(BF16)
    kn = _dot(ckvn, wk_ref[...])
    v_ref[...] = _dot(ckvn, wv_ref[...]).astype(BF16)
    k_rope = (kr2 * cos_a + krs2 * sin_a).astype(BF16)
    for h in range(N_MLA_HEADS):
        o = h * QK_SLAB
        q_ref[:, o:o + LANES] = (qf[:, o:o + LANES] * scale).astype(BF16)
        q_ref[:, o + LANES:o + QK_SLAB] = (qf[:, o + LANES:o + QK_SLAB] * rope_mul).astype(BF16)
        k_ref[:, o:o + LANES] = kn[:, h * LANES:(h + 1) * LANES].astype(BF16)
        k_ref[:, o + LANES:o + QK_SLAB] = k_rope


def _in_proj(x2, pos2, attn_norm, w_in, b_gates, q_norm, w_q_up, kv_norm, w_kv_up):
    n, d = x2.shape
    tm = ROW_TILE
    half = QK_ROPE_DIM // 2
    swap = jnp.concatenate([jnp.arange(half, QK_ROPE_DIM), jnp.arange(0, half)])
    w_kr = w_in[:, OFF_KR:OFF_KR + QK_ROPE_DIM]
    w_krs = w_kr[:, swap]
    wm = w_in[:, :OFF_G].astype(BF16)
    w_g = jnp.pad(w_in[:, OFF_G:OFF_CQ], ((0, 0), (0, LANES - N_GATES)))
    wr = jnp.concatenate([w_in[:, OFF_CQ:OFF_KR], w_kr, w_kr, w_krs, w_krs, w_g], axis=1).astype(BF16)
    bgt = b_gates.reshape(N_GATES, 1)
    wq4 = w_q_up.reshape(Q_LORA, N_MLA_HEADS, QK_NOPE_DIM + QK_ROPE_DIM)
    wq_pe = wq4[:, :, QK_NOPE_DIM:]
    wq = jnp.concatenate([wq4, wq_pe[:, :, swap]], axis=2).reshape(Q_LORA, N_MLA_HEADS * QK_SLAB).astype(BF16)
    wkv4 = w_kv_up.reshape(KV_LORA, N_MLA_HEADS, QK_NOPE_DIM + V_HEAD_DIM)
    wk = wkv4[:, :, :QK_NOPE_DIM].reshape(KV_LORA, N_MLA_HEADS * QK_NOPE_DIM).astype(BF16)
    wv = wkv4[:, :, QK_NOPE_DIM:].reshape(KV_LORA, D_MLA).astype(BF16)
    freqs = ROPE_THETA ** (-jnp.arange(0, QK_ROPE_DIM, 2, dtype=F32) / QK_ROPE_DIM)
    freq_c = freqs.reshape(half, 1)
    full = lambda arr: pl.BlockSpec(arr.shape, lambda i: (0,) * arr.ndim)
    rows = lambda w: pl.BlockSpec((tm, w), lambda i: (i, 0))
    consts = [attn_norm.reshape(1, d), wm, wr, bgt, q_norm.reshape(1, Q_LORA), wq,
              kv_norm.reshape(1, KV_LORA), wk, wv, freq_c]
    return pl.pallas_call(
        _in_proj_kernel,
        grid=(n // tm,),
        in_specs=[rows(d), pl.BlockSpec((1, 1, tm), lambda i: (i, 0, 0))] + [full(c) for c in consts],
        out_specs=[rows(OFF_G), pl.BlockSpec((N_GATES, tm), lambda i: (0, i)),
                   rows(N_MLA_HEADS * QK_SLAB), rows(N_MLA_HEADS * QK_SLAB), rows(D_MLA)],
        out_shape=[jax.ShapeDtypeStruct((n, OFF_G), F32), jax.ShapeDtypeStruct((N_GATES, n), F32),
                   jax.ShapeDtypeStruct((n, N_MLA_HEADS * QK_SLAB), BF16),
                   jax.ShapeDtypeStruct((n, N_MLA_HEADS * QK_SLAB), BF16),
                   jax.ShapeDtypeStruct((n, D_MLA), BF16)],
        compiler_params=_params("parallel"),
        name="in_proj",
    )(x2, pos2.reshape(n // tm, 1, tm), *consts)


def _mlstm_kernel(q_ref, k_ref, v_ref, o_ref, g_ref, cwq_ref, cwk_ref, cbq_ref, cbk_ref, nrm_ref,
                  y_ref,
                  qc_ref, kc_ref, va_ref, cst_ref, ent_ref, b_ref, e_ref, r_ref, ew_ref, mw_ref, bt_ref, mp_ref):
    L = MLSTM_CHUNK
    dh = MLSTM_HEAD_DIM
    nc = q_ref.shape[1] // L
    s_len = q_ref.shape[1]
    h = pl.program_id(1)

    for d in range(2):
        ig = g_ref[2 * d * N_MLSTM_HEADS + h, 0]
        fg = g_ref[(2 * d + 1) * N_MLSTM_HEADS + h, 0]
        b = _cumsum_lanes(_log_sigmoid(fg), reverse=(d == 1))
        btot = b[:, L - 1:L] if d == 0 else b[:, 0:1]
        r = ig - b
        w = btot + r
        mw = jnp.max(w, axis=-1, keepdims=True)
        b_ref[d] = b
        e_ref[d] = b + _cummax_lanes(r, reverse=(d == 1))
        r_ref[d] = r
        ew_ref[d] = jnp.exp(w - mw)
        mw_ref[d] = jnp.broadcast_to(mw, (nc, L))
        bt_ref[d] = jnp.broadcast_to(btot, (nc, L))

    row = lax.broadcasted_iota(I32, (L, dh), 0)

    def conv_silu(ref, cw_ref, cb_ref, c):
        start = pl.multiple_of(c * L, L)
        x = ref[0, pl.ds(start, L), :]
        prev_row = jnp.where(c > 0, ref[0, pl.ds(jnp.maximum(start - 1, 0), 1), :], 0.0)
        next_row = jnp.where(c < nc - 1, ref[0, pl.ds(jnp.minimum(start + L, s_len - 1), 1), :], 0.0)
        x_prev = jnp.where(row == 0, prev_row, pltpu.roll(x, 1, 0))
        x_next = jnp.where(row == L - 1, next_row, pltpu.roll(x, L - 1, 0))
        y = cw_ref[0:1, :] * x_prev + cw_ref[1:2, :] * x + cw_ref[2:3, :] * x_next + cb_ref[...]
        return y * jax.nn.sigmoid(y)

    ones_blk = jnp.ones((L, dh), BF16)

    def pass1(c, carry):
        start = pl.multiple_of(c * L, L)
        qc_ref[pl.ds(start, L), :] = conv_silu(q_ref, cwq_ref, cbq_ref, c).astype(BF16)
        kk = conv_silu(k_ref, cwk_ref, cbk_ref, c) * (dh ** -0.5)
        kc_ref[pl.ds(start, L), :] = kk.astype(BF16)
        va = jnp.concatenate([v_ref[0, pl.ds(start, L), :].astype(BF16), ones_blk], axis=1)
        va_ref[pl.ds(start, L), :] = va
        kt = kk.T
        for d in range(2):
            kw_t = (kt * ew_ref[d, pl.ds(c, 1), :]).astype(BF16)
            cst_ref[d, c] = _dot(kw_t, va)
        return carry

    lax.fori_loop(0, nc, pass1, 0, unroll=CHUNK_UNROLL)

    ent_ref[:, 0] = jnp.zeros((2, dh, 2 * dh), F32)

    def scan(i, carry):
        out = []
        for d in range(2):
            m = carry[d]
            c = i if d == 0 else nc - 1 - i
            mw = mw_ref[d, pl.ds(c, 1), :]
            bt = bt_ref[d, pl.ds(c, 1), :]
            m_new = jnp.maximum(bt + m, mw)
            a = jnp.exp(bt + m - m_new)[:, 0:1]
            cc = jnp.exp(mw - m_new)[:, 0:1]
            ent_ref[d, i + 1] = a * ent_ref[d, i] + cc * cst_ref[d, c]
            mp_ref[d, pl.ds(c, 1), :] = m
            out.append(m_new)
        return tuple(out)

    lax.fori_loop(0, nc, scan, (jnp.zeros((1, L), F32), jnp.zeros((1, L), F32)))

    ti = lax.broadcasted_iota(I32, (L, L), 0)
    si = lax.broadcasted_iota(I32, (L, L), 1)
    masks = (si <= ti, si >= ti)

    def pass3(c, carry):
        start = pl.multiple_of(c * L, L)
        q = qc_ref[pl.ds(start, L), :]
        k = kc_ref[pl.ds(start, L), :]
        va = va_ref[pl.ds(start, L), :]
        qk = _dot_nt(q, k)
        hsum = jnp.zeros((L, dh), F32)
        for d in range(2):
            bmat = jnp.broadcast_to(b_ref[d, pl.ds(c, 1), :], (L, L)).T
            emat = jnp.broadcast_to(e_ref[d, pl.ds(c, 1), :], (L, L)).T
            dmat = jnp.where(masks[d], bmat + r_ref[d, pl.ds(c, 1), :], -jnp.inf)
            inter = bmat + mp_ref[d, pl.ds(c, 1), :]
            m_t = jnp.maximum(inter, emat)
            sc = qk * jnp.exp(dmat - m_t)
            a = jnp.exp(inter - m_t)
            intra = _dot(sc.astype(BF16), va)
            cross = _dot(q, ent_ref[d, c if d == 0 else nc - 1 - c].astype(BF16))
            num = intra[:, :dh] + a * cross[:, :dh]
            den = intra[:, dh:] + a * cross[:, dh:]
            hsum = hsum + num / jnp.maximum(jnp.abs(den), jnp.exp(-m_t))
        hn = _rms(hsum, nrm_ref[...])
        y_ref[0, pl.ds(start, L), :] = hn * jax.nn.sigmoid(o_ref[0, pl.ds(start, L), :])
        return carry

    lax.fori_loop(0, nc, pass3, 0, unroll=CHUNK_UNROLL)


def _mlstm(um3, gt4, conv_w, conv_b, mlstm_norm):
    bsz, s, _ = um3.shape
    H, dh, L = N_MLSTM_HEADS, MLSTM_HEAD_DIM, MLSTM_CHUNK
    nc = s // L
    col = lambda off: pl.BlockSpec((1, s, dh), lambda b, h: (b, 0, off + h))
    vec = lambda rows, off: pl.BlockSpec((rows, dh), lambda b, h: (0, off + h))
    cb = conv_b.reshape(1, 2 * D_MLSTM)
    return pl.pallas_call(
        _mlstm_kernel,
        grid=(bsz, H),
        in_specs=[col(0), col(H), col(2 * H), col(3 * H),
                  pl.BlockSpec((N_GATES, 1, nc, L), lambda b, h: (0, b, 0, 0)),
                  vec(3, 0), vec(3, H), vec(1, 0), vec(1, H), vec(1, 0)],
        out_specs=pl.BlockSpec((1, s, dh), lambda b, h: (b, 0, h)),
        out_shape=jax.ShapeDtypeStruct((bsz, s, D_MLSTM), F32),
        scratch_shapes=[pltpu.VMEM((s, dh), BF16), pltpu.VMEM((s, dh), BF16), pltpu.VMEM((s, 2 * dh), BF16),
                        pltpu.VMEM((2, nc, dh, 2 * dh), F32), pltpu.VMEM((2, nc + 1, dh, 2 * dh), F32)]
                       + [pltpu.VMEM((2, nc, L), F32) for _ in range(7)],
        compiler_params=_params("parallel", "parallel"),
        name="mlstm",
    )(um3, um3, um3, um3, gt4, conv_w, conv_w, cb, cb, mlstm_norm.reshape(1, D_MLSTM))


def _attn_kernel(q_ref, k_ref, v_ref, o_ref):
    tq = q_ref.shape[1]
    q = q_ref[0]
    m = jnp.full((tq, 1), -jnp.inf, F32)
    l = jnp.zeros((tq, 1), F32)
    acc = jnp.zeros((tq, V_HEAD_DIM), F32)
    chunk = min(KV_CHUNK, k_ref.shape[1])
    for c in range(k_ref.shape[1] // chunk):
        keys = slice(c * chunk, (c + 1) * chunk)
        s = _dot_nt(q, k_ref[0, keys, :])
        m_new = jnp.maximum(m, jnp.max(s, axis=-1, keepdims=True))
        alpha = jnp.exp2(m - m_new)
        p = jnp.exp2(s - m_new)
        l = alpha * l + jnp.sum(p, axis=-1, keepdims=True)
        acc = alpha * acc + _dot(p.astype(BF16), v_ref[0, keys, :])
        m = m_new
    o_ref[0] = acc / l


def _attention(q3, k3, v3):
    bsz, s, _ = q3.shape
    tq = min(Q_TILE, s)
    return pl.pallas_call(
        _attn_kernel,
        grid=(bsz, N_MLA_HEADS, s // tq),
        in_specs=[pl.BlockSpec((1, tq, QK_SLAB), lambda b, h, i: (b, i, h)),
                  pl.BlockSpec((1, s, QK_SLAB), lambda b, h, i: (b, 0, h)),
                  pl.BlockSpec((1, s, V_HEAD_DIM), lambda b, h, i: (b, 0, h))],
        out_specs=pl.BlockSpec((1, tq, V_HEAD_DIM), lambda b, h, i: (b, i, h)),
        out_shape=jax.ShapeDtypeStruct((bsz, s, D_MLA), F32),
        compiler_params=_params("parallel", "parallel", "parallel"),
        name="attention",
    )(q3, k3, v3)


def _out_route_kernel(ym_ref, ya_ref, x_ref, mn_ref, wom_ref, woa_ref, fn_ref, wr_ref, br_ref,
                      h_ref, xn_ref, gate_ref, slot_ref, slot_t_ref, seg_ref, size_ref,
                      carry_ref):
    i = pl.program_id(0)
    tm = x_ref.shape[0]

    @pl.when(i == 0)
    def _():
        carry_ref[...] = jnp.zeros_like(carry_ref)

    ya = _rms(ya_ref[...], mn_ref[...])
    h1 = x_ref[...] + _dot(ym_ref[...].astype(BF16), wom_ref[...]) + _dot(ya.astype(BF16), woa_ref[...])
    h_ref[...] = h1
    xn = _rms(h1, fn_ref[...])
    xn_hi = xn.astype(BF16)
    xn_ref[...] = xn_hi
    xn_lo = (xn - xn_hi.astype(F32)).astype(BF16)
    both = _dot_nt(wr_ref[...], xn_hi)
    logits = (both[:N_EXPERTS] + both[N_EXPERTS:] + _dot_nt(wr_ref[:N_EXPERTS, :], xn_lo)) + br_ref[...]
    erow = lax.broadcasted_iota(I32, logits.shape, 0)
    work = logits
    vals, hots = [], []
    for k in range(TOP_K):
        mx = jnp.max(work, axis=0, keepdims=True)
        idx = jnp.min(jnp.where(work == mx, erow, N_EXPERTS), axis=0, keepdims=True)
        hot = erow == idx
        work = jnp.where(hot, -jnp.inf, work)
        vals.append(mx)
        hots.append(hot)
    exps = [jnp.exp(v - vals[0]) for v in vals]
    tot = exps[0] + exps[1] + exps[2] + exps[3]
    multi = (hots[0] | hots[1] | hots[2] | hots[3]).astype(BF16)
    ti = lax.broadcasted_iota(I32, (tm, tm), 0)
    tj = lax.broadcasted_iota(I32, (tm, tm), 1)
    local_rank = _dot(multi, (ti < tj).astype(BF16))
    multi_rows = jnp.concatenate([multi, jnp.zeros((LANES - N_EXPERTS, tm), BF16)], axis=0)
    count = _dot_nt(jnp.ones((SUBLANES, tm), BF16), multi_rows)[0:1]
    padded = jnp.ceil(count * (1.0 / SEG_ALIGN)) * SEG_ALIGN
    filled = _cumsum_lanes(padded, reverse=False)
    local_start = filled - padded
    start_col = jnp.broadcast_to(local_start, (SUBLANES, LANES)).T[:N_EXPERTS, 0:1]
    slot_all = local_rank + start_col
    krow = lax.broadcasted_iota(I32, (SUBLANES, tm), 0)
    packed = jnp.zeros((SUBLANES, tm), F32)
    for k in range(TOP_K):
        sk = jnp.sum(jnp.where(hots[k], slot_all, 0.0), axis=0, keepdims=True)
        packed = jnp.where(krow == k, exps[k] / tot, jnp.where(krow == TOP_K + k, sk, packed))
    slot_t_ref[0] = jnp.where(krow < TOP_K, pltpu.roll(packed, TOP_K, 0), 0.0).astype(I32)
    cols = packed.T
    gate_ref[...] = cols[:, :TOP_K]
    slot_ref[...] = cols[:, TOP_K:2 * TOP_K].astype(I32)
    srow = lax.broadcasted_iota(I32, (SUBLANES, LANES), 0)
    seg = jnp.where(srow == 0, padded, jnp.where(srow == 1, local_start, jnp.where(srow == 2, carry_ref[...],
                                                                                  filled[:, LANES - 1:LANES])))
    seg_ref[0] = seg.astype(I32)
    carry_ref[...] += padded
    size_ref[...] = carry_ref[...].astype(I32)


def _out_route(ym2, ya2, x2, mla_norm, w_out, ffn_norm, w_router, b_router):
    n, d = x2.shape
    tm = ROW_TILE
    nt = n // tm
    wom = w_out[:D_MLSTM].astype(BF16)
    woa = w_out[D_MLSTM:].astype(BF16)
    wr_hi = w_router.T.astype(BF16)
    wr_lo = (w_router.T - wr_hi.astype(F32)).astype(BF16)
    wr = jnp.concatenate([wr_hi, wr_lo], axis=0)
    br = b_router.reshape(N_EXPERTS, 1)
    full = lambda arr: pl.BlockSpec(arr.shape, lambda i: (0,) * arr.ndim)
    rows = lambda w: pl.BlockSpec((tm, w), lambda i: (i, 0))
    consts = [mla_norm.reshape(1, D_MLA), wom, woa, ffn_norm.reshape(1, d), wr, br]
    return pl.pallas_call(
        _out_route_kernel,
        grid=(nt,),
        in_specs=[rows(D_MLSTM), rows(D_MLA), rows(d)] + [full(c) for c in consts],
        out_specs=[rows(d), rows(d), rows(TOP_K), rows(TOP_K),
                   pl.BlockSpec((1, SUBLANES, tm), lambda i: (i, 0, 0)),
                   pl.BlockSpec((1, SUBLANES, LANES), lambda i: (i, 0, 0)),
                   pl.BlockSpec((1, LANES), lambda i: (0, 0))],
        out_shape=[jax.ShapeDtypeStruct((n, d), F32), jax.ShapeDtypeStruct((n, d), BF16),
                   jax.ShapeDtypeStruct((n, TOP_K), F32), jax.ShapeDtypeStruct((n, TOP_K), I32),
                   jax.ShapeDtypeStruct((nt, SUBLANES, tm), I32), jax.ShapeDtypeStruct((nt, SUBLANES, LANES), I32),
                   jax.ShapeDtypeStruct((1, LANES), I32)],
        scratch_shapes=[pltpu.VMEM((1, LANES), F32)],
        compiler_params=_params("arbitrary"),
        name="out_route",
    )(ym2, ya2, x2, *consts)


def _segment_copies(seg_ref, starts_ref, local_ref, global_ref, sem, to_global, wait):
    def copy(src, dst, rows):
        loc = local_ref.at[pl.ds(pl.multiple_of(src, SEG_ALIGN), rows)]
        glo = global_ref.at[pl.ds(pl.multiple_of(dst, SEG_ALIGN), rows)]
        return pltpu.make_async_copy(loc, glo, sem) if to_global else pltpu.make_async_copy(glo, loc, sem)

    if wait:
        copy(0, 0, pl.multiple_of(seg_ref[0, 3, 0], SEG_ALIGN)).wait()
        return

    def per_expert(e, carry):
        size = seg_ref[0, 0, e]
        src = seg_ref[0, 1, e]
        dst = starts_ref[e] + seg_ref[0, 2, e]
        off = 0
        rows = ROW_TILE
        while rows >= SEG_ALIGN:
            @pl.when((size & rows) != 0)
            def _(off=off, rows=rows):
                copy(src + off, dst + off, rows).start()

            off = off + (size & rows)
            rows //= 2
        return carry

    lax.fori_loop(0, N_EXPERTS, per_expert, 0)


def _dispatch_kernel(starts_ref, seg_ref, segp_ref, slot_t_ref, xn_ref, xs_ref, sort_ref, zero_ref, sem, zsem):
    i = pl.program_id(0)
    tm = xn_ref.shape[0]
    tmx = zero_ref.shape[0]
    cap = sort_ref.shape[1]

    @pl.when(i == 0)
    def _():
        zero_ref[...] = jnp.zeros_like(zero_ref)
        n_tail = (xs_ref.shape[0] - starts_ref[N_EXPERTS]) // tmx

        def clear_tile(row, wait):
            cp = pltpu.make_async_copy(zero_ref, xs_ref.at[pl.ds(pl.multiple_of(row, tmx), tmx)], zsem)
            if wait:
                cp.wait()
            else:
                cp.start()

        for wait in (False, True):
            def clear_group(e, carry, wait=wait):
                hi = starts_ref[e + 1]

                @pl.when(hi > starts_ref[e])
                def _():
                    clear_tile(hi - tmx, wait)

                return carry

            def clear_tail(t, carry, wait=wait):
                clear_tile(starts_ref[N_EXPERTS] + t * tmx, wait)
                return carry

            lax.fori_loop(0, N_EXPERTS, clear_group, 0)
            lax.fori_loop(0, n_tail, clear_tail, 0)

    slot = i % 2
    for r0 in range(0, cap, SORT_CHUNK):
        pos = lax.broadcasted_iota(I32, (SORT_CHUNK, tm), 0) + r0
        hit = pos == slot_t_ref[0, 0:1, :]
        for k in range(1, TOP_K):
            hit = hit | (pos == slot_t_ref[0, k:k + 1, :])
        sort_ref[slot, r0:r0 + SORT_CHUNK, :] = _pack_halves(_dot(hit.astype(BF16), xn_ref[...]))
    _segment_copies(seg_ref, starts_ref, sort_ref.at[slot], xs_ref, sem.at[slot], True, False)

    @pl.when(i >= 1)
    def _():
        _segment_copies(segp_ref, starts_ref, sort_ref.at[1 - slot], xs_ref, sem.at[1 - slot], True, True)

    @pl.when(i == pl.num_programs(0) - 1)
    def _():
        _segment_copies(seg_ref, starts_ref, sort_ref.at[slot], xs_ref, sem.at[slot], True, True)


def _dispatch(starts, seg, slot_t, xn, n_rows):
    n, d = xn.shape
    tm = ROW_TILE
    nt = n // tm
    smem = lambda f: pl.BlockSpec((1, SUBLANES, LANES), f, memory_space=pltpu.SMEM)
    any_spec = pl.BlockSpec(memory_space=pl.ANY)
    return pl.pallas_call(
        _dispatch_kernel,
        grid_spec=pltpu.PrefetchScalarGridSpec(
            num_scalar_prefetch=1,
            grid=(nt,),
            in_specs=[smem(lambda i, *_: (i, 0, 0)), smem(lambda i, *_: (jnp.maximum(i - 1, 0), 0, 0)),
                      pl.BlockSpec((1, SUBLANES, tm), lambda i, *_: (i, 0, 0)),
                      pl.BlockSpec((tm, d), lambda i, *_: (i, 0))],
            out_specs=any_spec,
            scratch_shapes=[pltpu.VMEM((2, SORT_ROWS, d // 2), U32), pltpu.VMEM((EXPERT_TILE, d // 2), U32),
                            pltpu.SemaphoreType.DMA((2,)), pltpu.SemaphoreType.DMA],
        ),
        out_shape=jax.ShapeDtypeStruct((n_rows, d // 2), U32),
        compiler_params=_params("arbitrary"),
        name="dispatch",
    )(starts, seg, seg, slot_t, xn)


def _experts_kernel(te_ref, tb_ref, tv_ref, xs_ref, wgu_ref, bgu_ref, wd_ref, bd_ref, out_ref,
                    wgu_bf, wd_bf):
    i = pl.program_id(0)
    de = wd_ref.shape[1]
    prev = te_ref[jnp.maximum(i - 1, 0)]

    @pl.when(jnp.logical_or(i == 0, te_ref[i] != prev))
    def _():
        wgu_bf[...] = wgu_ref[0].astype(BF16)
        wd_bf[...] = wd_ref[0].astype(BF16)

    @pl.when(tv_ref[i] == 1)
    def _():
        x = _unpack_halves(xs_ref[...])
        ch = 512
        acc = jnp.zeros((xs_ref.shape[0], bd_ref.shape[2]), F32)
        for j in range(de // ch):
            g = _dot(x, wgu_bf[:, j * ch:(j + 1) * ch]) + bgu_ref[0, :, j * ch:(j + 1) * ch]
            u = _dot(x, wgu_bf[:, de + j * ch:de + (j + 1) * ch]) + bgu_ref[0, :, de + j * ch:de + (j + 1) * ch]
            g = jnp.minimum(g, SWIGLU_LIMIT)
            u = jnp.clip(u, -SWIGLU_LIMIT, SWIGLU_LIMIT)
            hm = (u + 1.0) * (g * jax.nn.sigmoid(g * SWIGLU_ALPHA))
            acc = acc + _dot(hm.astype(BF16), wd_bf[j * ch:(j + 1) * ch, :])
        out_ref[...] = _pack_halves((acc + bd_ref[0]).astype(BF16).astype(F32))

    @pl.when(tv_ref[i] == 0)
    def _():
        out_ref[...] = jnp.zeros_like(out_ref)


def _experts(tile_e, tile_b, tile_v, xs, w_gate_up, b_gate_up, w_down, b_down):
    n_rows, dw = xs.shape
    tmx = EXPERT_TILE
    ne, d, de2 = w_gate_up.shape
    de = de2 // 2
    return pl.pallas_call(
        _experts_kernel,
        grid_spec=pltpu.PrefetchScalarGridSpec(
            num_scalar_prefetch=3,
            grid=(n_rows // tmx,),
            in_specs=[pl.BlockSpec((tmx, dw), lambda i, te, tb, tv: (tb[i], 0)),
                      pl.BlockSpec((1, d, de2), lambda i, te, tb, tv: (te[i], 0, 0)),
                      pl.BlockSpec((1, 1, de2), lambda i, te, tb, tv: (te[i], 0, 0)),
                      pl.BlockSpec((1, de, d), lambda i, te, tb, tv: (te[i], 0, 0)),
                      pl.BlockSpec((1, 1, d), lambda i, te, tb, tv: (te[i], 0, 0))],
            out_specs=pl.BlockSpec((tmx, dw), lambda i, te, tb, tv: (i, 0)),
            scratch_shapes=[pltpu.VMEM((d, de2), BF16), pltpu.VMEM((de, d), BF16)],
        ),
        out_shape=jax.ShapeDtypeStruct((n_rows, dw), U32),
        compiler_params=_params("arbitrary"),
        name="experts",
    )(tile_e, tile_b, tile_v, xs, w_gate_up, b_gate_up.reshape(ne, 1, de2), w_down, b_down.reshape(ne, 1, d))


def _combine_kernel(starts_ref, seg_ref, segn_ref, h_ref, gate_ref, slot_ref, p_ref, ys_ref, pn_ref, wg_ref,
                    wp_ref, fn_ref, out_ref, ybuf, sem, *, final):
    i = pl.program_id(0)
    nt = pl.num_programs(0)
    tm = h_ref.shape[0]
    cap = ybuf.shape[1]

    @pl.when(i == 0)
    def _():
        ybuf[...] = jnp.zeros_like(ybuf)
        _segment_copies(seg_ref, starts_ref, ybuf.at[0], ys_ref, sem.at[0], False, False)

    slot = i % 2

    @pl.when(i + 1 < nt)
    def _():
        _segment_copies(segn_ref, starts_ref, ybuf.at[1 - slot], ys_ref, sem.at[1 - slot], False, False)

    _segment_copies(seg_ref, starts_ref, ybuf.at[slot], ys_ref, sem.at[slot], False, True)

    gate = gate_ref[...]
    h2 = h_ref[...]
    for c0 in range(0, cap, SORT_CHUNK):
        pos = lax.broadcasted_iota(I32, (tm, SORT_CHUNK), 1) + c0
        weights = jnp.zeros((tm, SORT_CHUNK), F32)
        for k in range(TOP_K):
            weights = jnp.where(pos == slot_ref[:, k:k + 1], gate[:, k:k + 1], weights)
        h2 = h2 + _dot(weights.astype(BF16), _unpack_halves(ybuf[slot, c0:c0 + SORT_CHUNK, :]))
    hn = _rms(h2, pn_ref[...]).astype(BF16)
    sg = jax.nn.sigmoid(_dot(hn, wg_ref[...]))
    h3 = h2 + sg * _dot(p_ref[...].astype(BF16), wp_ref[...])
    out_ref[...] = _rms(h3, fn_ref[...]) if final else h3


def _combine(starts, seg, h1, gate, slot, p2, ys, ple_norm, w_ple_gate, w_ple_proj, final_norm, final):
    n, d = h1.shape
    tm = ROW_TILE
    nt = n // tm
    wg = w_ple_gate.astype(BF16)
    wp = w_ple_proj.astype(BF16)
    smem = lambda f: pl.BlockSpec((1, SUBLANES, LANES), f, memory_space=pltpu.SMEM)
    full = lambda arr: pl.BlockSpec(arr.shape, lambda i, *_: (0,) * arr.ndim)
    rows = lambda w: pl.BlockSpec((tm, w), lambda i, *_: (i, 0))
    consts = [ple_norm.reshape(1, d), wg, wp, final_norm.reshape(1, d)]
    return pl.pallas_call(
        functools.partial(_combine_kernel, final=final),
        grid_spec=pltpu.PrefetchScalarGridSpec(
            num_scalar_prefetch=1,
            grid=(nt,),
            in_specs=[smem(lambda i, *_: (i, 0, 0)), smem(lambda i, *_: (jnp.minimum(i + 1, nt - 1), 0, 0)),
                      rows(d), rows(TOP_K), rows(TOP_K), rows(p2.shape[1]), pl.BlockSpec(memory_space=pl.ANY)]
                     + [full(c) for c in consts],
            out_specs=rows(d),
            scratch_shapes=[pltpu.VMEM((2, SORT_ROWS, d // 2), U32), pltpu.SemaphoreType.DMA((2,))],
        ),
        out_shape=jax.ShapeDtypeStruct((n, d), F32),
        compiler_params=_params("arbitrary"),
        name="combine",
    )(starts, seg, seg, h1, gate, slot, p2, ys, *consts)


def _route_tables(sizes, n_tiles):
    tmx = EXPERT_TILE
    tile_end = jnp.cumsum((sizes + tmx - 1) // tmx)
    starts = jnp.concatenate([jnp.zeros((1,), I32), tile_end * tmx]).astype(I32)
    n_valid = tile_end[-1]
    t = jnp.arange(n_tiles, dtype=I32)
    tb = jnp.minimum(t, n_valid - 1).astype(I32)
    te = jnp.sum(tile_end[None, :] <= tb[:, None], axis=1).astype(I32)
    tv = (t < n_valid).astype(I32)
    return starts, te, tb, tv


def kernel(x, p, positions, attn_norm, w_in, b_gates, conv_w, conv_b, mlstm_norm, q_norm, w_q_up, kv_norm, w_kv_up, mla_norm, w_out, ffn_norm, w_router, b_router, w_gate_up, b_gate_up, w_down, b_down, ple_norm, w_ple_gate, w_ple_proj, final_norm):
    bsz, s, d = x.shape
    n = bsz * s
    depth = p.shape[0]
    nc = s // MLSTM_CHUNK
    max_rows = n * TOP_K + (n // ROW_TILE) * N_EXPERTS * (SEG_ALIGN - 1) + N_EXPERTS * (EXPERT_TILE - 1)
    n_tiles = max_rows // EXPERT_TILE
    pos2 = positions.reshape(n, 1)
    h = x.reshape(n, d)
    for i in range(depth):
        um, gt, q, k, v = _in_proj(h, pos2, attn_norm[i], w_in[i], b_gates[i], q_norm[i], w_q_up[i],
                                   kv_norm[i], w_kv_up[i])
        ym = _mlstm(um.reshape(bsz, s, -1), gt.reshape(N_GATES, bsz, nc, MLSTM_CHUNK), conv_w[i], conv_b[i],
                    mlstm_norm[i])
        ya = _attention(q.reshape(bsz, s, -1), k.reshape(bsz, s, -1), v.reshape(bsz, s, -1))
        h1, xn, gate, slot, slot_t, seg, sizes = _out_route(ym.reshape(n, -1), ya.reshape(n, -1), h, mla_norm[i],
                                                            w_out[i], ffn_norm[i], w_router[i], b_router[i])
        starts, te, tb, tv = _route_tables(sizes[0, :N_EXPERTS], n_tiles)
        xs = _dispatch(starts, seg, slot_t, xn, n_tiles * EXPERT_TILE)
        ys = _experts(te, tb, tv, xs, w_gate_up[i], b_gate_up[i], w_down[i], b_down[i])
        h = _combine(starts, seg, h1, gate, slot, p[i].reshape(n, -1), ys, ple_norm[i], w_ple_gate[i],
                     w_ple_proj[i], final_norm, final=(i == depth - 1))
    return h.reshape(bsz, s, d)
```

```python
import functools

import jax
import jax.numpy as jnp
from jax import lax
from jax.experimental import pallas as pl
from jax.experimental.pallas import tpu as pltpu

F32 = jnp.float32
BF16 = jnp.bfloat16
I32 = jnp.int32
U32 = jnp.uint32

N_MLSTM_HEADS = 4
MLSTM_HEAD_DIM = 128
D_MLSTM = N_MLSTM_HEADS * MLSTM_HEAD_DIM
MLSTM_CHUNK = 128
N_MLA_HEADS = 4
QK_NOPE_DIM = 128
QK_ROPE_DIM = 64
V_HEAD_DIM = 128
D_MLA = N_MLA_HEADS * V_HEAD_DIM
Q_LORA = 256
KV_LORA = 128
ROPE_THETA = 10000.0
N_EXPERTS = 32
TOP_K = 4
SWIGLU_LIMIT = 7.0
SWIGLU_ALPHA = 1.702
EPS = 1e-6
N_GATES = 4 * N_MLSTM_HEADS
OFF_G = 4 * D_MLSTM
OFF_CQ = OFF_G + N_GATES
OFF_CKV = OFF_CQ + Q_LORA
OFF_KR = OFF_CKV + KV_LORA

LANES = 128
SUBLANES = 8
QK_SLAB = 2 * LANES
VMEM_LIMIT_BYTES = 56 * 1024 * 1024
LOG2_E = 1.4426950408889634

ROW_TILE = 512
Q_TILE = 2048
KV_CHUNK = 1024
EXPERT_TILE = 512
FF_CHUNK = 512
CHUNK_UNROLL = 8
SEG_ALIGN = SUBLANES
SORT_CHUNK = 256
SORT_ROWS = ROW_TILE * TOP_K + N_EXPERTS * SEG_ALIGN


def _dot(a, b):
    return jnp.dot(a, b, preferred_element_type=F32)


def _dot_nt(a, b):
    return lax.dot_general(a, b, (((1,), (1,)), ((), ())), preferred_element_type=F32)


def _rms(x, g):
    return x * lax.rsqrt(jnp.mean(x * x, axis=-1, keepdims=True) + EPS) * g


def _log_sigmoid(x):
    return jnp.minimum(x, 0.0) - jnp.log(1.0 + jnp.exp(-jnp.abs(x)))


def _scan_lanes(x, reverse, op, identity):
    n = x.shape[-1]
    lane = lax.broadcasted_iota(I32, x.shape, x.ndim - 1)
    sh = 1
    while sh < n:
        if reverse:
            x = op(x, jnp.where(lane < n - sh, pltpu.roll(x, n - sh, x.ndim - 1), identity))
        else:
            x = op(x, jnp.where(lane >= sh, pltpu.roll(x, sh, x.ndim - 1), identity))
        sh *= 2
    return x


def _cumsum_lanes(x, reverse):
    return _scan_lanes(x, reverse, jnp.add, 0.0)


def _cummax_lanes(x, reverse):
    return _scan_lanes(x, reverse, jnp.maximum, -jnp.inf)


def _pack_halves(x):
    w = x.shape[1] // 2
    lo = lax.shift_right_logical(lax.bitcast_convert_type(x[:, :w], U32), jnp.uint32(16))
    hi = lax.bitcast_convert_type(x[:, w:], U32)
    return lo | hi


def _unpack_halves(words):
    lo = lax.bitcast_convert_type(lax.shift_left(words, jnp.uint32(16)), F32)
    hi = lax.bitcast_convert_type(words & jnp.uint32(0xFFFF0000), F32)
    return jnp.concatenate([lo, hi], axis=1).astype(BF16)


def _params(*sem):
    return pltpu.CompilerParams(dimension_semantics=sem, vmem_limit_bytes=VMEM_LIMIT_BYTES)


def _in_proj_kernel(x_ref, pos_ref, an_ref, wm_ref, wr_ref, bgt_ref, qn_ref, wq_ref, kvn_ref,
                    wk_ref, wv_ref, freq_ref,
                    um_ref, gt_ref, q_ref, k_ref, v_ref):
    a = _rms(x_ref[...], an_ref[...]).astype(BF16)
    um_ref[...] = _dot(a, wm_ref[...])
    rest = _dot(a, wr_ref[...])
    cq = rest[:, :Q_LORA]
    ckv = rest[:, Q_LORA:Q_LORA + KV_LORA]
    kr2 = rest[:, Q_LORA + KV_LORA:Q_LORA + KV_LORA + LANES]
    krs2 = rest[:, Q_LORA + KV_LORA + LANES:Q_LORA + KV_LORA + 2 * LANES]
    gt_ref[...] = rest[:, Q_LORA + KV_LORA + 2 * LANES:].T[:N_GATES, :] + bgt_ref[...]
    ang = freq_ref[...] * pos_ref[0].astype(F32)
    cos_t = jnp.cos(ang)
    sin_t = jnp.sin(ang)
    cos_a = jnp.concatenate([cos_t] * (LANES // cos_t.shape[0]), axis=0).T
    sin_a = jnp.concatenate([-sin_t, sin_t] * (LANES // (2 * sin_t.shape[0])), axis=0).T
    scale = (QK_NOPE_DIM + QK_ROPE_DIM) ** -0.5 * LOG2_E
    lane = lax.broadcasted_iota(I32, cos_a.shape, 1)
    rope_mul = jnp.where(lane < QK_ROPE_DIM, cos_a, sin_a) * scale
    qf = _dot(_rms(cq, qn_ref[...]).astype(BF16), wq_ref[...])
    ckvn = _rms(ckv, kvn_ref[...]).astype(BF16)
    kn = _dot(ckvn, wk_ref[...])
    v_ref[...] = _dot(ckvn, wv_ref[...]).astype(BF16)
    k_rope = (kr2 * cos_a + krs2 * sin_a).astype(BF16)
    for h in range(N_MLA_HEADS):
        o = h * QK_SLAB
        q_ref[:, o:o + LANES] = (qf[:, o:o + LANES] * scale).astype(BF16)
        q_ref[:, o + LANES:o + QK_SLAB] = (qf[:, o + LANES:o + QK_SLAB] * rope_mul).astype(BF16)
        k_ref[:, o:o + LANES] = kn[:, h * LANES:(h + 1) * LANES].astype(BF16)
        k_ref[:, o + LANES:o + QK_SLAB] = k_rope


def _in_proj(x2, pos2, attn_norm, w_in, b_gates, q_norm, w_q_up, kv_norm, w_kv_up):
    n, d = x2.shape
    tm = ROW_TILE
    half = QK_ROPE_DIM // 2
    swap = jnp.concatenate([jnp.arange(half, QK_ROPE_DIM), jnp.arange(0, half)])
    w_kr = w_in[:, OFF_KR:OFF_KR + QK_ROPE_DIM]
    w_krs = w_kr[:, swap]
    wm = w_in[:, :OFF_G].astype(BF16)
    w_g = jnp.pad(w_in[:, OFF_G:OFF_CQ], ((0, 0), (0, LANES - N_GATES)))
    wr = jnp.concatenate([w_in[:, OFF_CQ:OFF_KR], w_kr, w_kr, w_krs, w_krs, w_g], axis=1).astype(BF16)
    bgt = b_gates.reshape(N_GATES, 1)
    wq4 = w_q_up.reshape(Q_LORA, N_MLA_HEADS, QK_NOPE_DIM + QK_ROPE_DIM)
    wq_pe = wq4[:, :, QK_NOPE_DIM:]
    wq = jnp.concatenate([wq4, wq_pe[:, :, swap]], axis=2).reshape(Q_LORA, N_MLA_HEADS * QK_SLAB).astype(BF16)
    wkv4 = w_kv_up.reshape(KV_LORA, N_MLA_HEADS, QK_NOPE_DIM + V_HEAD_DIM)
    wk = wkv4[:, :, :QK_NOPE_DIM].reshape(KV_LORA, N_MLA_HEADS * QK_NOPE_DIM).astype(BF16)
    wv = wkv4[:, :, QK_NOPE_DIM:].reshape(KV_LORA, D_MLA).astype(BF16)
    freqs = ROPE_THETA ** (-jnp.arange(0, QK_ROPE_DIM, 2, dtype=F32) / QK_ROPE_DIM)
    freq_c = freqs.reshape(half, 1)
    full = lambda arr: pl.BlockSpec(arr.shape, lambda i: (0,) * arr.ndim)
    rows = lambda w: pl.BlockSpec((tm, w), lambda i: (i, 0))
    consts = [attn_norm.reshape(1, d), wm, wr, bgt, q_norm.reshape(1, Q_LORA), wq,
              kv_norm.reshape(1, KV_LORA), wk, wv, freq_c]
    return pl.pallas_call(
        _in_proj_kernel,
        grid=(n // tm,),
        in_specs=[rows(d), pl.BlockSpec((1, 1, tm), lambda i: (i, 0, 0))] + [full(c) for c in consts],
        out_specs=[rows(OFF_G), pl.BlockSpec((N_GATES, tm), lambda i: (0, i)),
                   rows(N_MLA_HEADS * QK_SLAB), rows(N_MLA_HEADS * QK_SLAB), rows(D_MLA)],
        out_shape=[jax.ShapeDtypeStruct((n, OFF_G), F32), jax.ShapeDtypeStruct((N_GATES, n), F32),
                   jax.ShapeDtypeStruct((n, N_MLA_HEADS * QK_SLAB), BF16),
                   jax.ShapeDtypeStruct((n, N_MLA_HEADS * QK_SLAB), BF16),
                   jax.ShapeDtypeStruct((n, D_MLA), BF16)],
        compiler_params=_params("parallel"),
        name="in_proj",
    )(x2, pos2.reshape(n // tm, 1, tm), *consts)


def _mlstm_kernel(q_ref, k_ref, v_ref, o_ref, g_ref, cwq_ref, cwk_ref, cbq_ref, cbk_ref, nrm_ref,
                  y_ref,
                  qc_ref, kc_ref, va_ref, cst_ref, ent_ref, b_ref, e_ref, r_ref, ew_ref, mw_ref, bt_ref, mp_ref):
    L = MLSTM_CHUNK
    dh = MLSTM_HEAD_DIM
    nc = q_ref.shape[1] // L
    s_len = q_ref.shape[1]
    h = pl.program_id(1)

    for d in range(2):
        ig = g_ref[2 * d * N_MLSTM_HEADS + h, 0]
        fg = g_ref[(2 * d + 1) * N_MLSTM_HEADS + h, 0]
        b = _cumsum_lanes(_log_sigmoid(fg), reverse=(d == 1))
        btot = b[:, L - 1:L] if d == 0 else b[:, 0:1]
        r = ig - b
        w = btot + r
        mw = jnp.max(w, axis=-1, keepdims=True)
        b_ref[d] = b
        e_ref[d] = b + _cummax_lanes(r, reverse=(d == 1))
        r_ref[d] = r
        ew_ref[d] = jnp.exp(w - mw)
        mw_ref[d] = jnp.broadcast_to(mw, (nc, L))
        bt_ref[d] = jnp.broadcast_to(btot, (nc, L))

    row = lax.broadcasted_iota(I32, (L, dh), 0)

    def conv_silu(ref, cw_ref, cb_ref, c):
        start = pl.multiple_of(c * L, L)
        x = ref[0, pl.ds(start, L), :]
        prev_row = jnp.where(c > 0, ref[0, pl.ds(jnp.maximum(start - 1, 0), 1), :], 0.0)
        next_row = jnp.where(c < nc - 1, ref[0, pl.ds(jnp.minimum(start + L, s_len - 1), 1), :], 0.0)
        x_prev = jnp.where(row == 0, prev_row, pltpu.roll(x, 1, 0))
        x_next = jnp.where(row == L - 1, next_row, pltpu.roll(x, L - 1, 0))
        y = cw_ref[0:1, :] * x_prev + cw_ref[1:2, :] * x + cw_ref[2:3, :] * x_next + cb_ref[...]
        return y * jax.nn.sigmoid(y)

    ones_blk = jnp.ones((L, dh), BF16)

    def pass1(c, carry):
        start = pl.multiple_of(c * L, L)
        qc_ref[pl.ds(start, L), :] = conv_silu(q_ref, cwq_ref, cbq_ref, c).astype(BF16)
        kk = conv_silu(k_ref, cwk_ref, cbk_ref, c) * (dh ** -0.5)
        kc_ref[pl.ds(start, L), :] = kk.astype(BF16)
        va = jnp.concatenate([v_ref[0, pl.ds(start, L), :].astype(BF16), ones_blk], axis=1)
        va_ref[pl.ds(start, L), :] = va
        kt = kk.T
        for d in range(2):
            kw_t = (kt * ew_ref[d, pl.ds(c, 1), :]).astype(BF16)
            cst_ref[d, c] = _dot(kw_t, va)
        return carry

    lax.fori_loop(0, nc, pass1, 0, unroll=CHUNK_UNROLL)

    ent_ref[:, 0] = jnp.zeros((2, dh, 2 * dh), F32)

    def scan(i, carry):
        out = []
        for d in range(2):
            m = carry[d]
            c = i if d == 0 else nc - 1 - i
            mw = mw_ref[d, pl.ds(c, 1), :]
            bt = bt_ref[d, pl.ds(c, 1), :]
            m_new = jnp.maximum(bt + m, mw)
            a = jnp.exp(bt + m - m_new)[:, 0:1]
            cc = jnp.exp(mw - m_new)[:, 0:1]
            ent_ref[d, i + 1] = a * ent_ref[d, i] + cc * cst_ref[d, c]
            mp_ref[d, pl.ds(c, 1), :] = m
            out.append(m_new)
        return tuple(out)

    lax.fori_loop(0, nc, scan, (jnp.zeros((1, L), F32), jnp.zeros((1, L), F32)))

    ti = lax.broadcasted_iota(I32, (L, L), 0)
    si = lax.broadcasted_iota(I32, (L, L), 1)
    masks = (si <= ti, si >= ti)

    def pass3(c, carry):
        start = pl.multiple_of(c * L, L)
        q = qc_ref[pl.ds(start, L), :]
        k = kc_ref[pl.ds(start, L), :]
        va = va_ref[pl.ds(start, L), :]
        qk = _dot_nt(q, k)
        hsum = jnp.zeros((L, dh), F32)
        for d in range(2):
            bmat = jnp.broadcast_to(b_ref[d, pl.ds(c, 1), :], (L, L)).T
            emat = jnp.broadcast_to(e_ref[d, pl.ds(c, 1), :], (L, L)).T
            dmat = jnp.where(masks[d], bmat + r_ref[d, pl.ds(c, 1), :], -jnp.inf)
            inter = bmat + mp_ref[d, pl.ds(c, 1), :]
            m_t = jnp.maximum(inter, emat)
            sc = qk * jnp.exp(dmat - m_t)
            a = jnp.exp(inter - m_t)
            intra = _dot(sc.astype(BF16), va)
            cross = _dot(q, ent_ref[d, c if d == 0 else nc - 1 - c].astype(BF16))
            num = intra[:, :dh] + a * cross[:, :dh]
            den = intra[:, dh:] + a * cross[:, dh:]
            hsum = hsum + num / jnp.maximum(jnp.abs(den), jnp.exp(-m_t))
        hn = _rms(hsum, nrm_ref[...])
        y_ref[0, pl.ds(start, L), :] = hn * jax.nn.sigmoid(o_ref[0, pl.ds(start, L), :])
        return carry

    lax.fori_loop(0, nc, pass3, 0, unroll=CHUNK_UNROLL)


def _mlstm(um3, gt4, conv_w, conv_b, mlstm_norm):
    bsz, s, _ = um3.shape
    H, dh, L = N_MLSTM_HEADS, MLSTM_HEAD_DIM, MLSTM_CHUNK
    nc = s // L
    col = lambda off: pl.BlockSpec((1, s, dh), lambda b, h: (b, 0, off + h))
    vec = lambda rows, off: pl.BlockSpec((rows, dh), lambda b, h: (0, off + h))
    cb = conv_b.reshape(1, 2 * D_MLSTM)
    return pl.pallas_call(
        _mlstm_kernel,
        grid=(bsz, H),
        in_specs=[col(0), col(H), col(2 * H), col(3 * H),
                  pl.BlockSpec((N_GATES, 1, nc, L), lambda b, h: (0, b, 0, 0)),
                  vec(3, 0), vec(3, H), vec(1, 0), vec(1, H), vec(1, 0)],
        out_specs=pl.BlockSpec((1, s, dh), lambda b, h: (b, 0, h)),
        out_shape=jax.ShapeDtypeStruct((bsz, s, D_MLSTM), F32),
        scratch_shapes=[pltpu.VMEM((s, dh), BF16), pltpu.VMEM((s, dh), BF16), pltpu.VMEM((s, 2 * dh), BF16),
                        pltpu.VMEM((2, nc, dh, 2 * dh), F32), pltpu.VMEM((2, nc + 1, dh, 2 * dh), F32)]
                       + [pltpu.VMEM((2, nc, L), F32) for _ in range(7)],
        compiler_params=_params("parallel", "parallel"),
        name="mlstm",
    )(um3, um3, um3, um3, gt4, conv_w, conv_w, cb, cb, mlstm_norm.reshape(1, D_MLSTM))


def _attn_kernel(q_ref, k_ref, v_ref, o_ref):
    tq = q_ref.shape[1]
    q = q_ref[0]
    m = jnp.full((tq, 1), -jnp.inf, F32)
    l = jnp.zeros((tq, 1), F32)
    acc = jnp.zeros((tq, V_HEAD_DIM), F32)
    chunk = min(KV_CHUNK, k_ref.shape[1])
    for c in range(k_ref.shape[1] // chunk):
        keys = slice(c * chunk, (c + 1) * chunk)
        s = _dot_nt(q, k_ref[0, keys, :])
        m_new = jnp.maximum(m, jnp.max(s, axis=-1, keepdims=True))
        alpha = jnp.exp2(m - m_new)
        p = jnp.exp2(s - m_new)
        l = alpha * l + jnp.sum(p, axis=-1, keepdims=True)
        acc = alpha * acc + _dot(p.astype(BF16), v_ref[0, keys, :])
        m = m_new
    o_ref[0] = acc / l


def _attention(q3, k3, v3):
    bsz, s, _ = q3.shape
    tq = min(Q_TILE, s)
    return pl.pallas_call(
        _attn_kernel,
        grid=(bsz, N_MLA_HEADS, s // tq),
        in_specs=[pl.BlockSpec((1, tq, QK_SLAB), lambda b, h, i: (b, i, h)),
                  pl.BlockSpec((1, s, QK_SLAB), lambda b, h, i: (b, 0, h)),
                  pl.BlockSpec((1, s, V_HEAD_DIM), lambda b, h, i: (b, 0, h))],
        out_specs=pl.BlockSpec((1, tq, V_HEAD_DIM), lambda b, h, i: (b, i, h)),
        out_shape=jax.ShapeDtypeStruct((bsz, s, D_MLA), F32),
        compiler_params=_params("parallel", "parallel", "parallel"),
        name="attention",
    )(q3, k3, v3)


def _out_route_kernel(ym_ref, ya_ref, x_ref, mn_ref, wom_ref, woa_ref, fn_ref, wr_ref, br_ref,
                      h_ref, xn_ref, gate_ref, slot_ref, slot_t_ref, seg_ref, size_ref,
                      carry_ref):
    i = pl.program_id(0)
    tm = x_ref.shape[0]

    @pl.when(i == 0)
    def _():
        carry_ref[...] = jnp.zeros_like(carry_ref)

    ya = _rms(ya_ref[...], mn_ref[...])
    h1 = x_ref[...] + _dot(ym_ref[...].astype(BF16), wom_ref[...]) + _dot(ya.astype(BF16), woa_ref[...])
    h_ref[...] = h1
    xn = _rms(h1, fn_ref[...])
    xn_hi = xn.astype(BF16)
    xn_ref[...] = xn_hi
    xn_lo = (xn - xn_hi.astype(F32)).astype(BF16)
    both = _dot_nt(wr_ref[...], xn_hi)
    logits = (both[:N_EXPERTS] + both[N_EXPERTS:] + _dot_nt(wr_ref[:N_EXPERTS, :], xn_lo)) + br_ref[...]
    erow = lax.broadcasted_iota(I32, logits.shape, 0)
    work = logits
    vals, hots = [], []
    for k in range(TOP_K):
        mx = jnp.max(work, axis=0, keepdims=True)
        idx = jnp.min(jnp.where(work == mx, erow, N_EXPERTS), axis=0, keepdims=True)
        hot = erow == idx
        work = jnp.where(hot, -jnp.inf, work)
        vals.append(mx)
        hots.append(hot)
    exps = [jnp.exp(v - vals[0]) for v in vals]
    tot = exps[0] + exps[1] + exps[2] + exps[3]
    multi = (hots[0] | hots[1] | hots[2] | hots[3]).astype(BF16)
    ti = lax.broadcasted_iota(I32, (tm, tm), 0)
    tj = lax.broadcasted_iota(I32, (tm, tm), 1)
    local_rank = _dot(multi, (ti < tj).astype(BF16))
    multi_rows = jnp.concatenate([multi, jnp.zeros((LANES - N_EXPERTS, tm), BF16)], axis=0)
    count = _dot_nt(jnp.ones((SUBLANES, tm), BF16), multi_rows)[0:1]
    padded = jnp.ceil(count * (1.0 / SEG_ALIGN)) * SEG_ALIGN
    filled = _cumsum_lanes(padded, reverse=False)
    local_start = filled - padded
    start_col = jnp.broadcast_to(local_start, (SUBLANES, LANES)).T[:N_EXPERTS, 0:1]
    slot_all = local_rank + start_col
    krow = lax.broadcasted_iota(I32, (SUBLANES, tm), 0)
    packed = jnp.zeros((SUBLANES, tm), F32)
    for k in range(TOP_K):
        sk = jnp.sum(jnp.where(hots[k], slot_all, 0.0), axis=0, keepdims=True)
        packed = jnp.where(krow == k, exps[k] / tot, jnp.where(krow == TOP_K + k, sk, packed))
    slot_t_ref[0] = jnp.where(krow < TOP_K, pltpu.roll(packed, TOP_K, 0), 0.0).astype(I32)
    cols = packed.T
    gate_ref[...] = cols[:, :TOP_K]
    slot_ref[...] = cols[:, TOP_K:2 * TOP_K].astype(I32)
    srow = lax.broadcasted_iota(I32, (SUBLANES, LANES), 0)
    seg = jnp.where(srow == 0, padded, jnp.where(srow == 1, local_start, jnp.where(srow == 2, carry_ref[...],
                                                                                  filled[:, LANES - 1:LANES])))
    seg_ref[0] = seg.astype(I32)
    carry_ref[...] += padded
    size_ref[...] = carry_ref[...].astype(I32)


def _out_route(ym2, ya2, x2, mla_norm, w_out, ffn_norm, w_router, b_router):
    n, d = x2.shape
    tm = ROW_TILE
    nt = n // tm
    wom = w_out[:D_MLSTM].astype(BF16)
    woa = w_out[D_MLSTM:].astype(BF16)
    wr_hi = w_router.T.astype(BF16)
    wr_lo = (w_router.T - wr_hi.astype(F32)).astype(BF16)
    wr = jnp.concatenate([wr_hi, wr_lo], axis=0)
    br = b_router.reshape(N_EXPERTS, 1)
    full = lambda arr: pl.BlockSpec(arr.shape, lambda i: (0,) * arr.ndim)
    rows = lambda w: pl.BlockSpec((tm, w), lambda i: (i, 0))
    consts = [mla_norm.reshape(1, D_MLA), wom, woa, ffn_norm.reshape(1, d), wr, br]
    return pl.pallas_call(
        _out_route_kernel,
        grid=(nt,),
        in_specs=[rows(D_MLSTM), rows(D_MLA), rows(d)] + [full(c) for c in consts],
        out_specs=[rows(d), rows(d), rows(TOP_K), rows(TOP_K),
                   pl.BlockSpec((1, SUBLANES, tm), lambda i: (i, 0, 0)),
                   pl.BlockSpec((1, SUBLANES, LANES), lambda i: (i, 0, 0)),
                   pl.BlockSpec((1, LANES), lambda i: (0, 0))],
        out_shape=[jax.ShapeDtypeStruct((n, d), F32), jax.ShapeDtypeStruct((n, d), BF16),
                   jax.ShapeDtypeStruct((n, TOP_K), F32), jax.ShapeDtypeStruct((n, TOP_K), I32),
                   jax.ShapeDtypeStruct((nt, SUBLANES, tm), I32), jax.ShapeDtypeStruct((nt, SUBLANES, LANES), I32),
                   jax.ShapeDtypeStruct((1, LANES), I32)],
        scratch_shapes=[pltpu.VMEM((1, LANES), F32)],
        compiler_params=_params("arbitrary"),
        name="out_route",
    )(ym2, ya2, x2, *consts)


def _segment_copies(seg_ref, starts_ref, local_ref, global_ref, sem, to_global, wait):
    def copy(src, dst, rows):
        loc = local_ref.at[pl.ds(pl.multiple_of(src, SEG_ALIGN), rows)]
        glo = global_ref.at[pl.ds(pl.multiple_of(dst, SEG_ALIGN), rows)]
        return pltpu.make_async_copy(loc, glo, sem) if to_global else pltpu.make_async_copy(glo, loc, sem)

    if wait:
        copy(0, 0, pl.multiple_of(seg_ref[0, 3, 0], SEG_ALIGN)).wait()
        return

    def per_expert(e, carry):
        size = seg_ref[0, 0, e]
        src = seg_ref[0, 1, e]
        dst = starts_ref[e] + seg_ref[0, 2, e]
        off = 0
        rows = ROW_TILE
        while rows >= SEG_ALIGN:
            @pl.when((size & rows) != 0)
            def _(off=off, rows=rows):
                copy(src + off, dst + off, rows).start()

            off = off + (size & rows)
            rows //= 2
        return carry

    lax.fori_loop(0, N_EXPERTS, per_expert, 0)


def _dispatch_kernel(starts_ref, seg_ref, segp_ref, slot_t_ref, xn_ref, xs_ref, sort_ref, zero_ref, sem, zsem):
    i = pl.program_id(0)
    tm = xn_ref.shape[0]
    tmx = zero_ref.shape[0]
    cap = sort_ref.shape[1]

    @pl.when(i == 0)
    def _():
        zero_ref[...] = jnp.zeros_like(zero_ref)
        n_tail = (xs_ref.shape[0] - starts_ref[N_EXPERTS]) // tmx

        def clear_tile(row, wait):
            cp = pltpu.make_async_copy(zero_ref, xs_ref.at[pl.ds(pl.multiple_of(row, tmx), tmx)], zsem)
            if wait:
                cp.wait()
            else:
                cp.start()

        for wait in (False, True):
            def clear_group(e, carry, wait=wait):
                hi = starts_ref[e + 1]

                @pl.when(hi > starts_ref[e])
                def _():
                    clear_tile(hi - tmx, wait)

                return carry

            def clear_tail(t, carry, wait=wait):
                clear_tile(starts_ref[N_EXPERTS] + t * tmx, wait)
                return carry

            lax.fori_loop(0, N_EXPERTS, clear_group, 0)
            lax.fori_loop(0, n_tail, clear_tail, 0)

    slot = i % 2
    for r0 in range(0, cap, SORT_CHUNK):
        pos = lax.broadcasted_iota(I32, (SORT_CHUNK, tm), 0) + r0
        hit = pos == slot_t_ref[0, 0:1, :]
        for k in range(1, TOP_K):
            hit = hit | (pos == slot_t_ref[0, k:k + 1, :])
        sort_ref[slot, r0:r0 + SORT_CHUNK, :] = _pack_halves(_dot(hit.astype(BF16), xn_ref[...]))
    _segment_copies(seg_ref, starts_ref, sort_ref.at[slot], xs_ref, sem.at[slot], True, False)

    @pl.when(i >= 1)
    def _():
        _segment_copies(segp_ref, starts_ref, sort_ref.at[1 - slot], xs_ref, sem.at[1 - slot], True, True)

    @pl.when(i == pl.num_programs(0) - 1)
    def _():
        _segment_copies(seg_ref, starts_ref, sort_ref.at[slot], xs_ref, sem.at[slot], True, True)


def _dispatch(starts, seg, slot_t, xn, n_rows):
    n, d = xn.shape
    tm = ROW_TILE
    nt = n // tm
    smem = lambda f: pl.BlockSpec((1, SUBLANES, LANES), f, memory_space=pltpu.SMEM)
    any_spec = pl.BlockSpec(memory_space=pl.ANY)
    return pl.pallas_call(
        _dispatch_kernel,
        grid_spec=pltpu.PrefetchScalarGridSpec(
            num_scalar_prefetch=1,
            grid=(nt,),
            in_specs=[smem(lambda i, *_: (i, 0, 0)), smem(lambda i, *_: (jnp.maximum(i - 1, 0), 0, 0)),
                      pl.BlockSpec((1, SUBLANES, tm), lambda i, *_: (i, 0, 0)),
                      pl.BlockSpec((tm, d), lambda i, *_: (i, 0))],
            out_specs=any_spec,
            scratch_shapes=[pltpu.VMEM((2, SORT_ROWS, d // 2), U32), pltpu.VMEM((EXPERT_TILE, d // 2), U32),
                            pltpu.SemaphoreType.DMA((2,)), pltpu.SemaphoreType.DMA],
        ),
        out_shape=jax.ShapeDtypeStruct((n_rows, d // 2), U32),
        compiler_params=_params("arbitrary"),
        name="dispatch",
    )(starts, seg, seg, slot_t, xn)


def _experts_kernel(starts_ref, xs_ref, wgu_ref, bgu_ref, wd_ref, bd_ref, ys_ref,
                    wgu_bf, wd_bf, xbuf, obuf, in_sem, out_sem):
    e = pl.program_id(0)
    tmx = xbuf.shape[1]
    de = wd_ref.shape[1]
    first = starts_ref[e]
    n_tiles = (starts_ref[e + 1] - first) // tmx

    wgu_bf[...] = wgu_ref[0].astype(BF16)
    wd_bf[...] = wd_ref[0].astype(BF16)

    def rows(t):
        return pl.ds(pl.multiple_of(first + t * tmx, tmx), tmx)

    def fetch(t, slot):
        return pltpu.make_async_copy(xs_ref.at[rows(t)], xbuf.at[slot], in_sem.at[slot])

    def store(t, slot):
        return pltpu.make_async_copy(obuf.at[slot], ys_ref.at[rows(t)], out_sem.at[slot])

    @pl.when(n_tiles > 0)
    def _():
        fetch(0, 0).start()

    def tile(t, carry):
        slot = t % 2
        fetch(t, slot).wait()

        @pl.when(t + 1 < n_tiles)
        def _():
            fetch(t + 1, 1 - slot).start()

        @pl.when(t >= 2)
        def _():
            store(t - 2, slot).wait()

        x = _unpack_halves(xbuf[slot])
        acc = jnp.zeros((tmx, bd_ref.shape[2]), F32)
        for j in range(de // FF_CHUNK):
            cols = slice(j * FF_CHUNK, (j + 1) * FF_CHUNK)
            up_cols = slice(de + j * FF_CHUNK, de + (j + 1) * FF_CHUNK)
            g = _dot(x, wgu_bf[:, cols]) + bgu_ref[0, :, cols]
            u = _dot(x, wgu_bf[:, up_cols]) + bgu_ref[0, :, up_cols]
            g = jnp.minimum(g, SWIGLU_LIMIT)
            u = jnp.clip(u, -SWIGLU_LIMIT, SWIGLU_LIMIT)
            hm = (u + 1.0) * (g * jax.nn.sigmoid(g * SWIGLU_ALPHA))
            acc = acc + _dot(hm.astype(BF16), wd_bf[cols, :])
        obuf[slot] = _pack_halves((acc + bd_ref[0]).astype(BF16).astype(F32))
        store(t, slot).start()
        return carry

    lax.fori_loop(0, n_tiles, tile, 0)

    @pl.when(n_tiles >= 2)
    def _():
        store(n_tiles - 2, n_tiles % 2).wait()

    @pl.when(n_tiles >= 1)
    def _():
        store(n_tiles - 1, (n_tiles - 1) % 2).wait()

    @pl.when(e == pl.num_programs(0) - 1)
    def _():
        last = starts_ref[e + 1]
        n_tail = (ys_ref.shape[0] - last) // tmx
        obuf[0] = jnp.zeros(obuf.shape[1:], obuf.dtype)

        def tail(wait):
            def body(t, carry):
                cp = pltpu.make_async_copy(obuf.at[0], ys_ref.at[pl.ds(pl.multiple_of(last + t * tmx, tmx), tmx)],
                                           out_sem.at[0])
                if wait:
                    cp.wait()
                else:
                    cp.start()
                return carry
            return body

        lax.fori_loop(0, n_tail, tail(False), 0)
        lax.fori_loop(0, n_tail, tail(True), 0)


def _experts(starts, xs, w_gate_up, b_gate_up, w_down, b_down):
    n_rows, dw = xs.shape
    tmx = EXPERT_TILE
    ne, d, de2 = w_gate_up.shape
    de = de2 // 2
    any_spec = pl.BlockSpec(memory_space=pl.ANY)
    return pl.pallas_call(
        _experts_kernel,
        grid_spec=pltpu.PrefetchScalarGridSpec(
            num_scalar_prefetch=1,
            grid=(ne,),
            in_specs=[any_spec,
                      pl.BlockSpec((1, d, de2), lambda e, *_: (e, 0, 0)),
                      pl.BlockSpec((1, 1, de2), lambda e, *_: (e, 0, 0)),
                      pl.BlockSpec((1, de, d), lambda e, *_: (e, 0, 0)),
                      pl.BlockSpec((1, 1, d), lambda e, *_: (e, 0, 0))],
            out_specs=any_spec,
            scratch_shapes=[pltpu.VMEM((d, de2), BF16), pltpu.VMEM((de, d), BF16),
                            pltpu.VMEM((2, tmx, dw), U32), pltpu.VMEM((2, tmx, dw), U32),
                            pltpu.SemaphoreType.DMA((2,)), pltpu.SemaphoreType.DMA((2,))],
        ),
        out_shape=jax.ShapeDtypeStruct((n_rows, dw), U32),
        compiler_params=_params("arbitrary"),
        name="experts",
    )(starts, xs, w_gate_up, b_gate_up.reshape(ne, 1, de2), w_down, b_down.reshape(ne, 1, d))


def _combine_kernel(starts_ref, seg_ref, segn_ref, h_ref, gate_ref, slot_ref, p_ref, ys_ref, pn_ref, wg_ref,
                    wp_ref, fn_ref, out_ref, ybuf, sem, *, final):
    i = pl.program_id(0)
    nt = pl.num_programs(0)
    tm = h_ref.shape[0]
    cap = ybuf.shape[1]

    @pl.when(i == 0)
    def _():
        ybuf[...] = jnp.zeros_like(ybuf)
        _segment_copies(seg_ref, starts_ref, ybuf.at[0], ys_ref, sem.at[0], False, False)

    slot = i % 2

    @pl.when(i + 1 < nt)
    def _():
        _segment_copies(segn_ref, starts_ref, ybuf.at[1 - slot], ys_ref, sem.at[1 - slot], False, False)

    _segment_copies(seg_ref, starts_ref, ybuf.at[slot], ys_ref, sem.at[slot], False, True)

    gate = gate_ref[...]
    h2 = h_ref[...]
    for c0 in range(0, cap, SORT_CHUNK):
        pos = lax.broadcasted_iota(I32, (tm, SORT_CHUNK), 1) + c0
        weights = jnp.zeros((tm, SORT_CHUNK), F32)
        for k in range(TOP_K):
            weights = jnp.where(pos == slot_ref[:, k:k + 1], gate[:, k:k + 1], weights)
        h2 = h2 + _dot(weights.astype(BF16), _unpack_halves(ybuf[slot, c0:c0 + SORT_CHUNK, :]))
    hn = _rms(h2, pn_ref[...]).astype(BF16)
    sg = jax.nn.sigmoid(_dot(hn, wg_ref[...]))
    h3 = h2 + sg * _dot(p_ref[...].astype(BF16), wp_ref[...])
    out_ref[...] = _rms(h3, fn_ref[...]) if final else h3


def _combine(starts, seg, h1, gate, slot, p2, ys, ple_norm, w_ple_gate, w_ple_proj, final_norm, final):
    n, d = h1.shape
    tm = ROW_TILE
    nt = n // tm
    wg = w_ple_gate.astype(BF16)
    wp = w_ple_proj.astype(BF16)
    smem = lambda f: pl.BlockSpec((1, SUBLANES, LANES), f, memory_space=pltpu.SMEM)
    full = lambda arr: pl.BlockSpec(arr.shape, lambda i, *_: (0,) * arr.ndim)
    rows = lambda w: pl.BlockSpec((tm, w), lambda i, *_: (i, 0))
    consts = [ple_norm.reshape(1, d), wg, wp, final_norm.reshape(1, d)]
    return pl.pallas_call(
        functools.partial(_combine_kernel, final=final),
        grid_spec=pltpu.PrefetchScalarGridSpec(
            num_scalar_prefetch=1,
            grid=(nt,),
            in_specs=[smem(lambda i, *_: (i, 0, 0)), smem(lambda i, *_: (jnp.minimum(i + 1, nt - 1), 0, 0)),
                      rows(d), rows(TOP_K), rows(TOP_K), rows(p2.shape[1]), pl.BlockSpec(memory_space=pl.ANY)]
                     + [full(c) for c in consts],
            out_specs=rows(d),
            scratch_shapes=[pltpu.VMEM((2, SORT_ROWS, d // 2), U32), pltpu.SemaphoreType.DMA((2,))],
        ),
        out_shape=jax.ShapeDtypeStruct((n, d), F32),
        compiler_params=_params("arbitrary"),
        name="combine",
    )(starts, seg, seg, h1, gate, slot, p2, ys, *consts)


def _group_starts(sizes):
    tmx = EXPERT_TILE
    tile_end = jnp.cumsum((sizes + tmx - 1) // tmx)
    return jnp.concatenate([jnp.zeros((1,), I32), tile_end * tmx]).astype(I32)


def kernel(x, p, positions, attn_norm, w_in, b_gates, conv_w, conv_b, mlstm_norm, q_norm, w_q_up, kv_norm, w_kv_up, mla_norm, w_out, ffn_norm, w_router, b_router, w_gate_up, b_gate_up, w_down, b_down, ple_norm, w_ple_gate, w_ple_proj, final_norm):
    bsz, s, d = x.shape
    n = bsz * s
    depth = p.shape[0]
    nc = s // MLSTM_CHUNK
    max_rows = n * TOP_K + (n // ROW_TILE) * N_EXPERTS * (SEG_ALIGN - 1) + N_EXPERTS * (EXPERT_TILE - 1)
    n_tiles = max_rows // EXPERT_TILE
    pos2 = positions.reshape(n, 1)
    h = x.reshape(n, d)
    for i in range(depth):
        um, gt, q, k, v = _in_proj(h, pos2, attn_norm[i], w_in[i], b_gates[i], q_norm[i], w_q_up[i],
                                   kv_norm[i], w_kv_up[i])
        ym = _mlstm(um.reshape(bsz, s, -1), gt.reshape(N_GATES, bsz, nc, MLSTM_CHUNK), conv_w[i], conv_b[i],
                    mlstm_norm[i])
        ya = _attention(q.reshape(bsz, s, -1), k.reshape(bsz, s, -1), v.reshape(bsz, s, -1))
        h1, xn, gate, slot, slot_t, seg, sizes = _out_route(ym.reshape(n, -1), ya.reshape(n, -1), h, mla_norm[i],
                                                            w_out[i], ffn_norm[i], w_router[i], b_router[i])
        starts = _group_starts(sizes[0, :N_EXPERTS])
        xs = _dispatch(starts, seg, slot_t, xn, n_tiles * EXPERT_TILE)
        ys = _experts(starts, xs, w_gate_up[i], b_gate_up[i], w_down[i], b_down[i])
        h = _combine(starts, seg, h1, gate, slot, p[i].reshape(n, -1), ys, ple_norm[i], w_ple_gate[i],
                     w_ple_proj[i], final_norm, final=(i == depth - 1))
    return h.reshape(bsz, s, d)
```

```python
import functools

import jax
import jax.numpy as jnp
from jax import lax
from jax.experimental import pallas as pl
from jax.experimental.pallas import tpu as pltpu

F32 = jnp.float32
BF16 = jnp.bfloat16
I32 = jnp.int32
U32 = jnp.uint32

N_MLSTM_HEADS = 4
MLSTM_HEAD_DIM = 128
D_MLSTM = N_MLSTM_HEADS * MLSTM_HEAD_DIM
MLSTM_CHUNK = 128
N_MLA_HEADS = 4
QK_NOPE_DIM = 128
QK_ROPE_DIM = 64
V_HEAD_DIM = 128
D_MLA = N_MLA_HEADS * V_HEAD_DIM
Q_LORA = 256
KV_LORA = 128
ROPE_THETA = 10000.0
N_EXPERTS = 32
TOP_K = 4
SWIGLU_LIMIT = 7.0
SWIGLU_ALPHA = 1.702
EPS = 1e-6
N_GATES = 4 * N_MLSTM_HEADS
OFF_G = 4 * D_MLSTM
OFF_CQ = OFF_G + N_GATES
OFF_CKV = OFF_CQ + Q_LORA
OFF_KR = OFF_CKV + KV_LORA

LANES = 128
SUBLANES = 8
QK_SLAB = 2 * LANES
VMEM_LIMIT_BYTES = 56 * 1024 * 1024
LOG2_E = 1.4426950408889634

ROW_TILE = 512
Q_TILE = 2048
KV_CHUNK = 1024
EXPERT_TILE = 512
FF_CHUNK = 512
CHUNK_UNROLL = 16
SEG_ALIGN = SUBLANES
SORT_CHUNK = 256
SORT_ROWS = ROW_TILE * TOP_K + N_EXPERTS * SEG_ALIGN


def _dot(a, b):
    return jnp.dot(a, b, preferred_element_type=F32)


def _dot_nt(a, b):
    return lax.dot_general(a, b, (((1,), (1,)), ((), ())), preferred_element_type=F32)


def _rms(x, g):
    return x * lax.rsqrt(jnp.mean(x * x, axis=-1, keepdims=True) + EPS) * g


def _log_sigmoid(x):
    return jnp.minimum(x, 0.0) - jnp.log(1.0 + jnp.exp(-jnp.abs(x)))


def _scan_lanes(x, reverse, op, identity):
    n = x.shape[-1]
    lane = lax.broadcasted_iota(I32, x.shape, x.ndim - 1)
    sh = 1
    while sh < n:
        if reverse:
            x = op(x, jnp.where(lane < n - sh, pltpu.roll(x, n - sh, x.ndim - 1), identity))
        else:
            x = op(x, jnp.where(lane >= sh, pltpu.roll(x, sh, x.ndim - 1), identity))
        sh *= 2
    return x


def _cumsum_lanes(x, reverse):
    return _scan_lanes(x, reverse, jnp.add, 0.0)


def _cummax_lanes(x, reverse):
    return _scan_lanes(x, reverse, jnp.maximum, -jnp.inf)


def _pack_halves(x):
    w = x.shape[1] // 2
    lo = lax.shift_right_logical(lax.bitcast_convert_type(x[:, :w], U32), jnp.uint32(16))
    hi = lax.bitcast_convert_type(x[:, w:], U32)
    return lo | hi


def _unpack_halves(words):
    lo = lax.bitcast_convert_type(lax.shift_left(words, jnp.uint32(16)), F32)
    hi = lax.bitcast_convert_type(words & jnp.uint32(0xFFFF0000), F32)
    return jnp.concatenate([lo, hi], axis=1).astype(BF16)


def _params(*sem):
    return pltpu.CompilerParams(dimension_semantics=sem, vmem_limit_bytes=VMEM_LIMIT_BYTES)


def _in_proj_kernel(x_ref, pos_ref, an_ref, wm_ref, wr_ref, bgt_ref, qn_ref, wq_ref, kvn_ref,
                    wk_ref, wv_ref, freq_ref,
                    um_ref, gt_ref, q_ref, k_ref, v_ref):
    a = _rms(x_ref[...], an_ref[...]).astype(BF16)
    um_ref[...] = _dot(a, wm_ref[...])
    rest = _dot(a, wr_ref[...])
    cq = rest[:, :Q_LORA]
    ckv = rest[:, Q_LORA:Q_LORA + KV_LORA]
    kr2 = rest[:, Q_LORA + KV_LORA:Q_LORA + KV_LORA + LANES]
    krs2 = rest[:, Q_LORA + KV_LORA + LANES:Q_LORA + KV_LORA + 2 * LANES]
    gt_ref[...] = rest[:, Q_LORA + KV_LORA + 2 * LANES:].T[:N_GATES, :] + bgt_ref[...]
    ang = freq_ref[...] * pos_ref[0].astype(F32)
    cos_t = jnp.cos(ang)
    sin_t = jnp.sin(ang)
    cos_a = jnp.concatenate([cos_t] * (LANES // cos_t.shape[0]), axis=0).T
    sin_a = jnp.concatenate([-sin_t, sin_t] * (LANES // (2 * sin_t.shape[0])), axis=0).T
    scale = (QK_NOPE_DIM + QK_ROPE_DIM) ** -0.5 * LOG2_E
    lane = lax.broadcasted_iota(I32, cos_a.shape, 1)
    rope_mul = jnp.where(lane < QK_ROPE_DIM, cos_a, sin_a) * scale
    qf = _dot(_rms(cq, qn_ref[...]).astype(BF16), wq_ref[...])
    ckvn = _rms(ckv, kvn_ref[...]).astype(BF16)
    kn = _dot(ckvn, wk_ref[...])
    v_ref[...] = _dot(ckvn, wv_ref[...]).astype(BF16)
    k_rope = (kr2 * cos_a + krs2 * sin_a).astype(BF16)
    for h in range(N_MLA_HEADS):
        o = h * QK_SLAB
        q_ref[:, o:o + LANES] = (qf[:, o:o + LANES] * scale).astype(BF16)
        q_ref[:, o + LANES:o + QK_SLAB] = (qf[:, o + LANES:o + QK_SLAB] * rope_mul).astype(BF16)
        k_ref[:, o:o + LANES] = kn[:, h * LANES:(h + 1) * LANES].astype(BF16)
        k_ref[:, o + LANES:o + QK_SLAB] = k_rope


def _in_proj(x2, pos2, attn_norm, w_in, b_gates, q_norm, w_q_up, kv_norm, w_kv_up):
    n, d = x2.shape
    tm = ROW_TILE
    half = QK_ROPE_DIM // 2
    swap = jnp.concatenate([jnp.arange(half, QK_ROPE_DIM), jnp.arange(0, half)])
    w_kr = w_in[:, OFF_KR:OFF_KR + QK_ROPE_DIM]
    w_krs = w_kr[:, swap]
    wm = w_in[:, :OFF_G].astype(BF16)
    w_g = jnp.pad(w_in[:, OFF_G:OFF_CQ], ((0, 0), (0, LANES - N_GATES)))
    wr = jnp.concatenate([w_in[:, OFF_CQ:OFF_KR], w_kr, w_kr, w_krs, w_krs, w_g], axis=1).astype(BF16)
    bgt = b_gates.reshape(N_GATES, 1)
    wq4 = w_q_up.reshape(Q_LORA, N_MLA_HEADS, QK_NOPE_DIM + QK_ROPE_DIM)
    wq_pe = wq4[:, :, QK_NOPE_DIM:]
    wq = jnp.concatenate([wq4, wq_pe[:, :, swap]], axis=2).reshape(Q_LORA, N_MLA_HEADS * QK_SLAB).astype(BF16)
    wkv4 = w_kv_up.reshape(KV_LORA, N_MLA_HEADS, QK_NOPE_DIM + V_HEAD_DIM)
    wk = wkv4[:, :, :QK_NOPE_DIM].reshape(KV_LORA, N_MLA_HEADS * QK_NOPE_DIM).astype(BF16)
    wv = wkv4[:, :, QK_NOPE_DIM:].reshape(KV_LORA, D_MLA).astype(BF16)
    freqs = ROPE_THETA ** (-jnp.arange(0, QK_ROPE_DIM, 2, dtype=F32) / QK_ROPE_DIM)
    freq_c = freqs.reshape(half, 1)
    full = lambda arr: pl.BlockSpec(arr.shape, lambda i: (0,) * arr.ndim)
    rows = lambda w: pl.BlockSpec((tm, w), lambda i: (i, 0))
    consts = [attn_norm.reshape(1, d), wm, wr, bgt, q_norm.reshape(1, Q_LORA), wq,
              kv_norm.reshape(1, KV_LORA), wk, wv, freq_c]
    return pl.pallas_call(
        _in_proj_kernel,
        grid=(n // tm,),
        in_specs=[rows(d), pl.BlockSpec((1, 1, tm), lambda i: (i, 0, 0))] + [full(c) for c in consts],
        out_specs=[rows(OFF_G), pl.BlockSpec((N_GATES, tm), lambda i: (0, i)),
                   rows(N_MLA_HEADS * QK_SLAB), rows(N_MLA_HEADS * QK_SLAB), rows(D_MLA)],
        out_shape=[jax.ShapeDtypeStruct((n, OFF_G), F32), jax.ShapeDtypeStruct((N_GATES, n), F32),
                   jax.ShapeDtypeStruct((n, N_MLA_HEADS * QK_SLAB), BF16),
                   jax.ShapeDtypeStruct((n, N_MLA_HEADS * QK_SLAB), BF16),
                   jax.ShapeDtypeStruct((n, D_MLA), BF16)],
        compiler_params=_params("parallel"),
        name="in_proj",
    )(x2, pos2.reshape(n // tm, 1, tm), *consts)


def _mlstm_kernel(q_ref, k_ref, v_ref, o_ref, g_ref, cwq_ref, cwk_ref, cbq_ref, cbk_ref, nrm_ref,
                  y_ref,
                  qc_ref, kc_ref, va_ref, cst_ref, ent_ref, b_ref, e_ref, r_ref, ew_ref, mw_ref, bt_ref, mp_ref):
    L = MLSTM_CHUNK
    dh = MLSTM_HEAD_DIM
    nc = q_ref.shape[1] // L
    s_len = q_ref.shape[1]
    h = pl.program_id(1)

    for d in range(2):
        ig = g_ref[2 * d * N_MLSTM_HEADS + h, 0]
        fg = g_ref[(2 * d + 1) * N_MLSTM_HEADS + h, 0]
        b = _cumsum_lanes(_log_sigmoid(fg), reverse=(d == 1))
        btot = b[:, L - 1:L] if d == 0 else b[:, 0:1]
        r = ig - b
        w = btot + r
        mw = jnp.max(w, axis=-1, keepdims=True)
        b_ref[d] = b
        e_ref[d] = b + _cummax_lanes(r, reverse=(d == 1))
        r_ref[d] = r
        ew_ref[d] = jnp.exp(w - mw)
        mw_ref[d] = jnp.broadcast_to(mw, (nc, L))
        bt_ref[d] = jnp.broadcast_to(btot, (nc, L))

    row = lax.broadcasted_iota(I32, (L, dh), 0)

    def conv_silu(ref, cw_ref, cb_ref, c):
        start = pl.multiple_of(c * L, L)
        x = ref[0, pl.ds(start, L), :]
        prev_row = jnp.where(c > 0, ref[0, pl.ds(jnp.maximum(start - 1, 0), 1), :], 0.0)
        next_row = jnp.where(c < nc - 1, ref[0, pl.ds(jnp.minimum(start + L, s_len - 1), 1), :], 0.0)
        x_prev = jnp.where(row == 0, prev_row, pltpu.roll(x, 1, 0))
        x_next = jnp.where(row == L - 1, next_row, pltpu.roll(x, L - 1, 0))
        y = cw_ref[0:1, :] * x_prev + cw_ref[1:2, :] * x + cw_ref[2:3, :] * x_next + cb_ref[...]
        return y * jax.nn.sigmoid(y)

    ones_blk = jnp.ones((L, dh), BF16)

    def pass1(c, carry):
        start = pl.multiple_of(c * L, L)
        qc_ref[pl.ds(start, L), :] = conv_silu(q_ref, cwq_ref, cbq_ref, c).astype(BF16)
        kk = conv_silu(k_ref, cwk_ref, cbk_ref, c) * (dh ** -0.5)
        kc_ref[pl.ds(start, L), :] = kk.astype(BF16)
        va = jnp.concatenate([v_ref[0, pl.ds(start, L), :].astype(BF16), ones_blk], axis=1)
        va_ref[pl.ds(start, L), :] = va
        kt = kk.T
        for d in range(2):
            kw_t = (kt * ew_ref[d, pl.ds(c, 1), :]).astype(BF16)
            cst_ref[d, c] = _dot(kw_t, va)
        return carry

    lax.fori_loop(0, nc, pass1, 0, unroll=CHUNK_UNROLL)

    ent_ref[:, 0] = jnp.zeros((2, dh, 2 * dh), F32)

    def scan(i, carry):
        out = []
        for d in range(2):
            m = carry[d]
            c = i if d == 0 else nc - 1 - i
            mw = mw_ref[d, pl.ds(c, 1), :]
            bt = bt_ref[d, pl.ds(c, 1), :]
            m_new = jnp.maximum(bt + m, mw)
            a = jnp.exp(bt + m - m_new)[:, 0:1]
            cc = jnp.exp(mw - m_new)[:, 0:1]
            ent_ref[d, i + 1] = a * ent_ref[d, i] + cc * cst_ref[d, c]
            mp_ref[d, pl.ds(c, 1), :] = m
            out.append(m_new)
        return tuple(out)

    lax.fori_loop(0, nc, scan, (jnp.zeros((1, L), F32), jnp.zeros((1, L), F32)))

    ti = lax.broadcasted_iota(I32, (L, L), 0)
    si = lax.broadcasted_iota(I32, (L, L), 1)
    masks = (si <= ti, si >= ti)

    def pass3(c, carry):
        start = pl.multiple_of(c * L, L)
        q = qc_ref[pl.ds(start, L), :]
        k = kc_ref[pl.ds(start, L), :]
        va = va_ref[pl.ds(start, L), :]
        qk = _dot_nt(q, k)
        hsum = jnp.zeros((L, dh), F32)
        for d in range(2):
            bmat = jnp.broadcast_to(b_ref[d, pl.ds(c, 1), :], (L, L)).T
            emat = jnp.broadcast_to(e_ref[d, pl.ds(c, 1), :], (L, L)).T
            dmat = jnp.where(masks[d], bmat + r_ref[d, pl.ds(c, 1), :], -jnp.inf)
            inter = bmat + mp_ref[d, pl.ds(c, 1), :]
            m_t = jnp.maximum(inter, emat)
            sc = qk * jnp.exp(dmat - m_t)
            a = jnp.exp(inter - m_t)
            intra = _dot(sc.astype(BF16), va)
            cross = _dot(q, ent_ref[d, c if d == 0 else nc - 1 - c].astype(BF16))
            num = intra[:, :dh] + a * cross[:, :dh]
            den = intra[:, dh:] + a * cross[:, dh:]
            hsum = hsum + num / jnp.maximum(jnp.abs(den), jnp.exp(-m_t))
        hn = _rms(hsum, nrm_ref[...])
        y_ref[0, pl.ds(start, L), :] = hn * jax.nn.sigmoid(o_ref[0, pl.ds(start, L), :])
        return carry

    lax.fori_loop(0, nc, pass3, 0, unroll=CHUNK_UNROLL)


def _mlstm(um3, gt4, conv_w, conv_b, mlstm_norm):
    bsz, s, _ = um3.shape
    H, dh, L = N_MLSTM_HEADS, MLSTM_HEAD_DIM, MLSTM_CHUNK
    nc = s // L
    col = lambda off: pl.BlockSpec((1, s, dh), lambda b, h: (b, 0, off + h))
    vec = lambda rows, off: pl.BlockSpec((rows, dh), lambda b, h: (0, off + h))
    cb = conv_b.reshape(1, 2 * D_MLSTM)
    return pl.pallas_call(
        _mlstm_kernel,
        grid=(bsz, H),
        in_specs=[col(0), col(H), col(2 * H), col(3 * H),
                  pl.BlockSpec((N_GATES, 1, nc, L), lambda b, h: (0, b, 0, 0)),
                  vec(3, 0), vec(3, H), vec(1, 0), vec(1, H), vec(1, 0)],
        out_specs=pl.BlockSpec((1, s, dh), lambda b, h: (b, 0, h)),
        out_shape=jax.ShapeDtypeStruct((bsz, s, D_MLSTM), F32),
        scratch_shapes=[pltpu.VMEM((s, dh), BF16), pltpu.VMEM((s, dh), BF16), pltpu.VMEM((s, 2 * dh), BF16),
                        pltpu.VMEM((2, nc, dh, 2 * dh), F32), pltpu.VMEM((2, nc + 1, dh, 2 * dh), F32)]
                       + [pltpu.VMEM((2, nc, L), F32) for _ in range(7)],
        compiler_params=_params("parallel", "parallel"),
        name="mlstm",
    )(um3, um3, um3, um3, gt4, conv_w, conv_w, cb, cb, mlstm_norm.reshape(1, D_MLSTM))


def _attn_kernel(q_ref, k_ref, v_ref, o_ref):
    tq = q_ref.shape[1]
    q = q_ref[0]
    m = jnp.full((tq, 1), -jnp.inf, F32)
    l = jnp.zeros((tq, 1), F32)
    acc = jnp.zeros((tq, V_HEAD_DIM), F32)
    chunk = min(KV_CHUNK, k_ref.shape[1])
    for c in range(k_ref.shape[1] // chunk):
        keys = slice(c * chunk, (c + 1) * chunk)
        s = _dot_nt(q, k_ref[0, keys, :])
        m_new = jnp.maximum(m, jnp.max(s, axis=-1, keepdims=True))
        alpha = jnp.exp2(m - m_new)
        p = jnp.exp2(s - m_new)
        l = alpha * l + jnp.sum(p, axis=-1, keepdims=True)
        acc = alpha * acc + _dot(p.astype(BF16), v_ref[0, keys, :])
        m = m_new
    o_ref[0] = acc / l


def _attention(q3, k3, v3):
    bsz, s, _ = q3.shape
    tq = min(Q_TILE, s)
    return pl.pallas_call(
        _attn_kernel,
        grid=(bsz, N_MLA_HEADS, s // tq),
        in_specs=[pl.BlockSpec((1, tq, QK_SLAB), lambda b, h, i: (b, i, h)),
                  pl.BlockSpec((1, s, QK_SLAB), lambda b, h, i: (b, 0, h)),
                  pl.BlockSpec((1, s, V_HEAD_DIM), lambda b, h, i: (b, 0, h))],
        out_specs=pl.BlockSpec((1, tq, V_HEAD_DIM), lambda b, h, i: (b, i, h)),
        out_shape=jax.ShapeDtypeStruct((bsz, s, D_MLA), F32),
        compiler_params=_params("parallel", "parallel", "parallel"),
        name="attention",
    )(q3, k3, v3)


def _out_route_kernel(ym_ref, ya_ref, x_ref, mn_ref, wom_ref, woa_ref, fn_ref, wr_ref, br_ref,
                      h_ref, xn_ref, gate_ref, slot_ref, slot_t_ref, seg_ref, size_ref,
                      carry_ref):
    i = pl.program_id(0)
    tm = x_ref.shape[0]

    @pl.when(i == 0)
    def _():
        carry_ref[...] = jnp.zeros_like(carry_ref)

    ya = _rms(ya_ref[...], mn_ref[...])
    h1 = x_ref[...] + _dot(ym_ref[...].astype(BF16), wom_ref[...]) + _dot(ya.astype(BF16), woa_ref[...])
    h_ref[...] = h1
    xn = _rms(h1, fn_ref[...])
    xn_hi = xn.astype(BF16)
    xn_ref[...] = xn_hi
    xn_lo = (xn - xn_hi.astype(F32)).astype(BF16)
    both = _dot_nt(wr_ref[...], xn_hi)
    logits = (both[:N_EXPERTS] + both[N_EXPERTS:] + _dot_nt(wr_ref[:N_EXPERTS, :], xn_lo)) + br_ref[...]
    erow = lax.broadcasted_iota(I32, logits.shape, 0)
    work = logits
    vals, hots = [], []
    for k in range(TOP_K):
        mx = jnp.max(work, axis=0, keepdims=True)
        idx = jnp.min(jnp.where(work == mx, erow, N_EXPERTS), axis=0, keepdims=True)
        hot = erow == idx
        work = jnp.where(hot, -jnp.inf, work)
        vals.append(mx)
        hots.append(hot)
    exps = [jnp.exp(v - vals[0]) for v in vals]
    tot = exps[0] + exps[1] + exps[2] + exps[3]
    multi = (hots[0] | hots[1] | hots[2] | hots[3]).astype(BF16)
    ti = lax.broadcasted_iota(I32, (tm, tm), 0)
    tj = lax.broadcasted_iota(I32, (tm, tm), 1)
    local_rank = _dot(multi, (ti < tj).astype(BF16))
    multi_rows = jnp.concatenate([multi, jnp.zeros((LANES - N_EXPERTS, tm), BF16)], axis=0)
    count = _dot_nt(jnp.ones((SUBLANES, tm), BF16), multi_rows)[0:1]
    padded = jnp.ceil(count * (1.0 / SEG_ALIGN)) * SEG_ALIGN
    filled = _cumsum_lanes(padded, reverse=False)
    local_start = filled - padded
    start_col = jnp.broadcast_to(local_start, (SUBLANES, LANES)).T[:N_EXPERTS, 0:1]
    slot_all = local_rank + start_col
    krow = lax.broadcasted_iota(I32, (SUBLANES, tm), 0)
    packed = jnp.zeros((SUBLANES, tm), F32)
    for k in range(TOP_K):
        sk = jnp.sum(jnp.where(hots[k], slot_all, 0.0), axis=0, keepdims=True)
        packed = jnp.where(krow == k, exps[k] / tot, jnp.where(krow == TOP_K + k, sk, packed))
    slot_t_ref[0] = jnp.where(krow < TOP_K, pltpu.roll(packed, TOP_K, 0), 0.0).astype(I32)
    cols = packed.T
    gate_ref[...] = cols[:, :TOP_K]
    slot_ref[...] = cols[:, TOP_K:2 * TOP_K].astype(I32)
    srow = lax.broadcasted_iota(I32, (SUBLANES, LANES), 0)
    seg = jnp.where(srow == 0, padded, jnp.where(srow == 1, local_start, jnp.where(srow == 2, carry_ref[...],
                                                                                  filled[:, LANES - 1:LANES])))
    seg_ref[0] = seg.astype(I32)
    carry_ref[...] += padded
    size_ref[...] = carry_ref[...].astype(I32)


def _out_route(ym2, ya2, x2, mla_norm, w_out, ffn_norm, w_router, b_router):
    n, d = x2.shape
    tm = ROW_TILE
    nt = n // tm
    wom = w_out[:D_MLSTM].astype(BF16)
    woa = w_out[D_MLSTM:].astype(BF16)
    wr_hi = w_router.T.astype(BF16)
    wr_lo = (w_router.T - wr_hi.astype(F32)).astype(BF16)
    wr = jnp.concatenate([wr_hi, wr_lo], axis=0)
    br = b_router.reshape(N_EXPERTS, 1)
    full = lambda arr: pl.BlockSpec(arr.shape, lambda i: (0,) * arr.ndim)
    rows = lambda w: pl.BlockSpec((tm, w), lambda i: (i, 0))
    consts = [mla_norm.reshape(1, D_MLA), wom, woa, ffn_norm.reshape(1, d), wr, br]
    return pl.pallas_call(
        _out_route_kernel,
        grid=(nt,),
        in_specs=[rows(D_MLSTM), rows(D_MLA), rows(d)] + [full(c) for c in consts],
        out_specs=[rows(d), rows(d), rows(TOP_K), rows(TOP_K),
                   pl.BlockSpec((1, SUBLANES, tm), lambda i: (i, 0, 0)),
                   pl.BlockSpec((1, SUBLANES, LANES), lambda i: (i, 0, 0)),
                   pl.BlockSpec((1, LANES), lambda i: (0, 0))],
        out_shape=[jax.ShapeDtypeStruct((n, d), F32), jax.ShapeDtypeStruct((n, d), BF16),
                   jax.ShapeDtypeStruct((n, TOP_K), F32), jax.ShapeDtypeStruct((n, TOP_K), I32),
                   jax.ShapeDtypeStruct((nt, SUBLANES, tm), I32), jax.ShapeDtypeStruct((nt, SUBLANES, LANES), I32),
                   jax.ShapeDtypeStruct((1, LANES), I32)],
        scratch_shapes=[pltpu.VMEM((1, LANES), F32)],
        compiler_params=_params("arbitrary"),
        name="out_route",
    )(ym2, ya2, x2, *consts)


def _segment_copies(seg_ref, starts_ref, local_ref, global_ref, sem, to_global, wait):
    def copy(src, dst, rows):
        loc = local_ref.at[pl.ds(pl.multiple_of(src, SEG_ALIGN), rows)]
        glo = global_ref.at[pl.ds(pl.multiple_of(dst, SEG_ALIGN), rows)]
        return pltpu.make_async_copy(loc, glo, sem) if to_global else pltpu.make_async_copy(glo, loc, sem)

    if wait:
        copy(0, 0, pl.multiple_of(seg_ref[0, 3, 0], SEG_ALIGN)).wait()
        return

    def per_expert(e, carry):
        size = seg_ref[0, 0, e]
        src = seg_ref[0, 1, e]
        dst = starts_ref[e] + seg_ref[0, 2, e]
        off = 0
        rows = ROW_TILE
        while rows >= SEG_ALIGN:
            @pl.when((size & rows) != 0)
            def _(off=off, rows=rows):
                copy(src + off, dst + off, rows).start()

            off = off + (size & rows)
            rows //= 2
        return carry

    lax.fori_loop(0, N_EXPERTS, per_expert, 0)


def _dispatch_kernel(starts_ref, seg_ref, segp_ref, slot_t_ref, xn_ref, xs_ref, sort_ref, zero_ref, sem, zsem):
    i = pl.program_id(0)
    tm = xn_ref.shape[0]
    tmx = zero_ref.shape[0]
    cap = sort_ref.shape[1]

    @pl.when(i == 0)
    def _():
        zero_ref[...] = jnp.zeros_like(zero_ref)
        n_tail = (xs_ref.shape[0] - starts_ref[N_EXPERTS]) // tmx

        def clear_tile(row, wait):
            cp = pltpu.make_async_copy(zero_ref, xs_ref.at[pl.ds(pl.multiple_of(row, tmx), tmx)], zsem)
            if wait:
                cp.wait()
            else:
                cp.start()

        for wait in (False, True):
            def clear_group(e, carry, wait=wait):
                hi = starts_ref[e + 1]

                @pl.when(hi > starts_ref[e])
                def _():
                    clear_tile(hi - tmx, wait)

                return carry

            def clear_tail(t, carry, wait=wait):
                clear_tile(starts_ref[N_EXPERTS] + t * tmx, wait)
                return carry

            lax.fori_loop(0, N_EXPERTS, clear_group, 0)
            lax.fori_loop(0, n_tail, clear_tail, 0)

    slot = i % 2
    for r0 in range(0, cap, SORT_CHUNK):
        pos = lax.broadcasted_iota(I32, (SORT_CHUNK, tm), 0) + r0
        hit = pos == slot_t_ref[0, 0:1, :]
        for k in range(1, TOP_K):
            hit = hit | (pos == slot_t_ref[0, k:k + 1, :])
        sort_ref[slot, r0:r0 + SORT_CHUNK, :] = _pack_halves(_dot(hit.astype(BF16), xn_ref[...]))
    _segment_copies(seg_ref, starts_ref, sort_ref.at[slot], xs_ref, sem.at[slot], True, False)

    @pl.when(i >= 1)
    def _():
        _segment_copies(segp_ref, starts_ref, sort_ref.at[1 - slot], xs_ref, sem.at[1 - slot], True, True)

    @pl.when(i == pl.num_programs(0) - 1)
    def _():
        _segment_copies(seg_ref, starts_ref, sort_ref.at[slot], xs_ref, sem.at[slot], True, True)


def _dispatch(starts, seg, slot_t, xn, n_rows):
    n, d = xn.shape
    tm = ROW_TILE
    nt = n // tm
    smem = lambda f: pl.BlockSpec((1, SUBLANES, LANES), f, memory_space=pltpu.SMEM)
    any_spec = pl.BlockSpec(memory_space=pl.ANY)
    return pl.pallas_call(
        _dispatch_kernel,
        grid_spec=pltpu.PrefetchScalarGridSpec(
            num_scalar_prefetch=1,
            grid=(nt,),
            in_specs=[smem(lambda i, *_: (i, 0, 0)), smem(lambda i, *_: (jnp.maximum(i - 1, 0), 0, 0)),
                      pl.BlockSpec((1, SUBLANES, tm), lambda i, *_: (i, 0, 0)),
                      pl.BlockSpec((tm, d), lambda i, *_: (i, 0))],
            out_specs=any_spec,
            scratch_shapes=[pltpu.VMEM((2, SORT_ROWS, d // 2), U32), pltpu.VMEM((EXPERT_TILE, d // 2), U32),
                            pltpu.SemaphoreType.DMA((2,)), pltpu.SemaphoreType.DMA],
        ),
        out_shape=jax.ShapeDtypeStruct((n_rows, d // 2), U32),
        compiler_params=_params("arbitrary"),
        name="dispatch",
    )(starts, seg, seg, slot_t, xn)


def _experts_kernel(te_ref, tb_ref, tv_ref, xs_ref, wgu_ref, bgu_ref, wd_ref, bd_ref, out_ref,
                    wgu_bf, wd_bf):
    i = pl.program_id(0)
    de = wd_ref.shape[1]
    prev = te_ref[jnp.maximum(i - 1, 0)]

    @pl.when(jnp.logical_or(i == 0, te_ref[i] != prev))
    def _():
        wgu_bf[...] = wgu_ref[0].astype(BF16)
        wd_bf[...] = wd_ref[0].astype(BF16)

    @pl.when(tv_ref[i] == 1)
    def _():
        x = _unpack_halves(xs_ref[...])
        acc = jnp.zeros((xs_ref.shape[0], bd_ref.shape[2]), F32)
        for j in range(de // FF_CHUNK):
            cols = slice(j * FF_CHUNK, (j + 1) * FF_CHUNK)
            up_cols = slice(de + j * FF_CHUNK, de + (j + 1) * FF_CHUNK)
            g = _dot(x, wgu_bf[:, cols]) + bgu_ref[0, :, cols]
            u = _dot(x, wgu_bf[:, up_cols]) + bgu_ref[0, :, up_cols]
            g = jnp.minimum(g, SWIGLU_LIMIT)
            u = jnp.clip(u, -SWIGLU_LIMIT, SWIGLU_LIMIT)
            hm = (u + 1.0) * (g * jax.nn.sigmoid(g * SWIGLU_ALPHA))
            acc = acc + _dot(hm.astype(BF16), wd_bf[cols, :])
        out_ref[...] = _pack_halves((acc + bd_ref[0]).astype(BF16).astype(F32))

    @pl.when(tv_ref[i] == 0)
    def _():
        out_ref[...] = jnp.zeros_like(out_ref)


def _experts(tile_e, tile_b, tile_v, xs, w_gate_up, b_gate_up, w_down, b_down):
    n_rows, dw = xs.shape
    tmx = EXPERT_TILE
    ne, d, de2 = w_gate_up.shape
    de = de2 // 2
    return pl.pallas_call(
        _experts_kernel,
        grid_spec=pltpu.PrefetchScalarGridSpec(
            num_scalar_prefetch=3,
            grid=(n_rows // tmx,),
            in_specs=[pl.BlockSpec((tmx, dw), lambda i, te, tb, tv: (tb[i], 0)),
                      pl.BlockSpec((1, d, de2), lambda i, te, tb, tv: (te[i], 0, 0)),
                      pl.BlockSpec((1, 1, de2), lambda i, te, tb, tv: (te[i], 0, 0)),
                      pl.BlockSpec((1, de, d), lambda i, te, tb, tv: (te[i], 0, 0)),
                      pl.BlockSpec((1, 1, d), lambda i, te, tb, tv: (te[i], 0, 0))],
            out_specs=pl.BlockSpec((tmx, dw), lambda i, te, tb, tv: (i, 0)),
            scratch_shapes=[pltpu.VMEM((d, de2), BF16), pltpu.VMEM((de, d), BF16)],
        ),
        out_shape=jax.ShapeDtypeStruct((n_rows, dw), U32),
        compiler_params=_params("arbitrary"),
        name="experts",
    )(tile_e, tile_b, tile_v, xs, w_gate_up, b_gate_up.reshape(ne, 1, de2), w_down, b_down.reshape(ne, 1, d))


def _combine_kernel(starts_ref, seg_ref, segn_ref, h_ref, gate_ref, slot_ref, p_ref, ys_ref, pn_ref, wg_ref,
                    wp_ref, fn_ref, out_ref, ybuf, sem, *, final):
    i = pl.program_id(0)
    nt = pl.num_programs(0)
    tm = h_ref.shape[0]
    cap = ybuf.shape[1]

    @pl.when(i == 0)
    def _():
        ybuf[...] = jnp.zeros_like(ybuf)
        _segment_copies(seg_ref, starts_ref, ybuf.at[0], ys_ref, sem.at[0], False, False)

    slot = i % 2

    @pl.when(i + 1 < nt)
    def _():
        _segment_copies(segn_ref, starts_ref, ybuf.at[1 - slot], ys_ref, sem.at[1 - slot], False, False)

    _segment_copies(seg_ref, starts_ref, ybuf.at[slot], ys_ref, sem.at[slot], False, True)

    gate = gate_ref[...]
    h2 = h_ref[...]
    for c0 in range(0, cap, SORT_CHUNK):
        pos = lax.broadcasted_iota(I32, (tm, SORT_CHUNK), 1) + c0
        weights = jnp.zeros((tm, SORT_CHUNK), F32)
        for k in range(TOP_K):
            weights = jnp.where(pos == slot_ref[:, k:k + 1], gate[:, k:k + 1], weights)
        h2 = h2 + _dot(weights.astype(BF16), _unpack_halves(ybuf[slot, c0:c0 + SORT_CHUNK, :]))
    hn = _rms(h2, pn_ref[...]).astype(BF16)
    sg = jax.nn.sigmoid(_dot(hn, wg_ref[...]))
    h3 = h2 + sg * _dot(p_ref[...].astype(BF16), wp_ref[...])
    out_ref[...] = _rms(h3, fn_ref[...]) if final else h3


def _combine(starts, seg, h1, gate, slot, p2, ys, ple_norm, w_ple_gate, w_ple_proj, final_norm, final):
    n, d = h1.shape
    tm = ROW_TILE
    nt = n // tm
    wg = w_ple_gate.astype(BF16)
    wp = w_ple_proj.astype(BF16)
    smem = lambda f: pl.BlockSpec((1, SUBLANES, LANES), f, memory_space=pltpu.SMEM)
    full = lambda arr: pl.BlockSpec(arr.shape, lambda i, *_: (0,) * arr.ndim)
    rows = lambda w: pl.BlockSpec((tm, w), lambda i, *_: (i, 0))
    consts = [ple_norm.reshape(1, d), wg, wp, final_norm.reshape(1, d)]
    return pl.pallas_call(
        functools.partial(_combine_kernel, final=final),
        grid_spec=pltpu.PrefetchScalarGridSpec(
            num_scalar_prefetch=1,
            grid=(nt,),
            in_specs=[smem(lambda i, *_: (i, 0, 0)), smem(lambda i, *_: (jnp.minimum(i + 1, nt - 1), 0, 0)),
                      rows(d), rows(TOP_K), rows(TOP_K), rows(p2.shape[1]), pl.BlockSpec(memory_space=pl.ANY)]
                     + [full(c) for c in consts],
            out_specs=rows(d),
            scratch_shapes=[pltpu.VMEM((2, SORT_ROWS, d // 2), U32), pltpu.SemaphoreType.DMA((2,))],
        ),
        out_shape=jax.ShapeDtypeStruct((n, d), F32),
        compiler_params=_params("arbitrary"),
        name="combine",
    )(starts, seg, seg, h1, gate, slot, p2, ys, *consts)


def _route_tables(sizes, n_tiles):
    tmx = EXPERT_TILE
    tile_end = jnp.cumsum((sizes + tmx - 1) // tmx)
    starts = jnp.concatenate([jnp.zeros((1,), I32), tile_end * tmx]).astype(I32)
    n_valid = tile_end[-1]
    t = jnp.arange(n_tiles, dtype=I32)
    tb = jnp.minimum(t, n_valid - 1).astype(I32)
    te = jnp.sum(tile_end[None, :] <= tb[:, None], axis=1).astype(I32)
    tv = (t < n_valid).astype(I32)
    return starts, te, tb, tv


def kernel(x, p, positions, attn_norm, w_in, b_gates, conv_w, conv_b, mlstm_norm, q_norm, w_q_up, kv_norm, w_kv_up, mla_norm, w_out, ffn_norm, w_router, b_router, w_gate_up, b_gate_up, w_down, b_down, ple_norm, w_ple_gate, w_ple_proj, final_norm):
    bsz, s, d = x.shape
    n = bsz * s
    depth = p.shape[0]
    nc = s // MLSTM_CHUNK
    max_rows = n * TOP_K + (n // ROW_TILE) * N_EXPERTS * (SEG_ALIGN - 1) + N_EXPERTS * (EXPERT_TILE - 1)
    n_tiles = max_rows // EXPERT_TILE
    pos2 = positions.reshape(n, 1)
    h = x.reshape(n, d)
    for i in range(depth):
        um, gt, q, k, v = _in_proj(h, pos2, attn_norm[i], w_in[i], b_gates[i], q_norm[i], w_q_up[i],
                                   kv_norm[i], w_kv_up[i])
        ym = _mlstm(um.reshape(bsz, s, -1), gt.reshape(N_GATES, bsz, nc, MLSTM_CHUNK), conv_w[i], conv_b[i],
                    mlstm_norm[i])
        ya = _attention(q.reshape(bsz, s, -1), k.reshape(bsz, s, -1), v.reshape(bsz, s, -1))
        h1, xn, gate, slot, slot_t, seg, sizes = _out_route(ym.reshape(n, -1), ya.reshape(n, -1), h, mla_norm[i],
                                                            w_out[i], ffn_norm[i], w_router[i], b_router[i])
        starts, te, tb, tv = _route_tables(sizes[0, :N_EXPERTS], n_tiles)
        xs = _dispatch(starts, seg, slot_t, xn, n_tiles * EXPERT_TILE)
        ys = _experts(te, tb, tv, xs, w_gate_up[i], b_gate_up[i], w_down[i], b_down[i])
        h = _combine(starts, seg, h1, gate, slot, p[i].reshape(n, -1), ys, ple_norm[i], w_ple_gate[i],
                     w_ple_proj[i], final_norm, final=(i == depth - 1))
    return h.reshape(bsz, s, d)
```

```python
import functools

import jax
import jax.numpy as jnp
from jax import lax
from jax.experimental import pallas as pl
from jax.experimental.pallas import tpu as pltpu

F32 = jnp.float32
BF16 = jnp.bfloat16
I32 = jnp.int32
U32 = jnp.uint32

N_MLSTM_HEADS = 4
MLSTM_HEAD_DIM = 128
D_MLSTM = N_MLSTM_HEADS * MLSTM_HEAD_DIM
MLSTM_CHUNK = 128
N_MLA_HEADS = 4
QK_NOPE_DIM = 128
QK_ROPE_DIM = 64
V_HEAD_DIM = 128
D_MLA = N_MLA_HEADS * V_HEAD_DIM
Q_LORA = 256
KV_LORA = 128
ROPE_THETA = 10000.0
N_EXPERTS = 32
TOP_K = 4
SWIGLU_LIMIT = 7.0
SWIGLU_ALPHA = 1.702
EPS = 1e-6
N_GATES = 4 * N_MLSTM_HEADS
OFF_G = 4 * D_MLSTM
OFF_CQ = OFF_G + N_GATES
OFF_CKV = OFF_CQ + Q_LORA
OFF_KR = OFF_CKV + KV_LORA

LANES = 128
SUBLANES = 8
QK_SLAB = 2 * LANES
VMEM_LIMIT_BYTES = 56 * 1024 * 1024
LOG2_E = 1.4426950408889634

ROW_TILE = 512
Q_TILE = 2048
KV_CHUNK = 1024
EXPERT_TILE = 512
FF_CHUNK = 512
CHUNK_UNROLL = 16
SEG_ALIGN = SUBLANES
ROUTE_TILES_PER_STEP = 2
SORT_CHUNK = 256
SORT_ROWS = ROW_TILE * TOP_K + N_EXPERTS * SEG_ALIGN


def _dot(a, b):
    return jnp.dot(a, b, preferred_element_type=F32)


def _dot_nt(a, b):
    return lax.dot_general(a, b, (((1,), (1,)), ((), ())), preferred_element_type=F32)


def _rms(x, g):
    return x * lax.rsqrt(jnp.mean(x * x, axis=-1, keepdims=True) + EPS) * g


def _log_sigmoid(x):
    return jnp.minimum(x, 0.0) - jnp.log(1.0 + jnp.exp(-jnp.abs(x)))


def _scan_lanes(x, reverse, op, identity):
    n = x.shape[-1]
    lane = lax.broadcasted_iota(I32, x.shape, x.ndim - 1)
    sh = 1
    while sh < n:
        if reverse:
            x = op(x, jnp.where(lane < n - sh, pltpu.roll(x, n - sh, x.ndim - 1), identity))
        else:
            x = op(x, jnp.where(lane >= sh, pltpu.roll(x, sh, x.ndim - 1), identity))
        sh *= 2
    return x


def _cumsum_lanes(x, reverse):
    return _scan_lanes(x, reverse, jnp.add, 0.0)


def _cummax_lanes(x, reverse):
    return _scan_lanes(x, reverse, jnp.maximum, -jnp.inf)


def _pack_halves(x):
    w = x.shape[1] // 2
    lo = lax.shift_right_logical(lax.bitcast_convert_type(x[:, :w], U32), jnp.uint32(16))
    hi = lax.bitcast_convert_type(x[:, w:], U32)
    return lo | hi


def _unpack_halves(words):
    lo = lax.bitcast_convert_type(lax.shift_left(words, jnp.uint32(16)), F32)
    hi = lax.bitcast_convert_type(words & jnp.uint32(0xFFFF0000), F32)
    return jnp.concatenate([lo, hi], axis=1).astype(BF16)


def _params(*sem):
    return pltpu.CompilerParams(dimension_semantics=sem, vmem_limit_bytes=VMEM_LIMIT_BYTES)


def _in_proj_kernel(x_ref, pos_ref, an_ref, wm_ref, wr_ref, bgt_ref, qn_ref, wq_ref, kvn_ref,
                    wk_ref, wv_ref, freq_ref,
                    um_ref, gt_ref, q_ref, k_ref, v_ref):
    a = _rms(x_ref[...], an_ref[...]).astype(BF16)
    um_ref[...] = _dot(a, wm_ref[...])
    rest = _dot(a, wr_ref[...])
    cq = rest[:, :Q_LORA]
    ckv = rest[:, Q_LORA:Q_LORA + KV_LORA]
    kr2 = rest[:, Q_LORA + KV_LORA:Q_LORA + KV_LORA + LANES]
    krs2 = rest[:, Q_LORA + KV_LORA + LANES:Q_LORA + KV_LORA + 2 * LANES]
    gt_ref[...] = rest[:, Q_LORA + KV_LORA + 2 * LANES:].T[:N_GATES, :] + bgt_ref[...]
    ang = freq_ref[...] * pos_ref[0].astype(F32)
    cos_t = jnp.cos(ang)
    sin_t = jnp.sin(ang)
    cos_a = jnp.concatenate([cos_t] * (LANES // cos_t.shape[0]), axis=0).T
    sin_a = jnp.concatenate([-sin_t, sin_t] * (LANES // (2 * sin_t.shape[0])), axis=0).T
    scale = (QK_NOPE_DIM + QK_ROPE_DIM) ** -0.5 * LOG2_E
    lane = lax.broadcasted_iota(I32, cos_a.shape, 1)
    rope_mul = jnp.where(lane < QK_ROPE_DIM, cos_a, sin_a) * scale
    qf = _dot(_rms(cq, qn_ref[...]).astype(BF16), wq_ref[...])
    ckvn = _rms(ckv, kvn_ref[...]).astype(BF16)
    kn = _dot(ckvn, wk_ref[...])
    v_ref[...] = _dot(ckvn, wv_ref[...]).astype(BF16)
    k_rope = (kr2 * cos_a + krs2 * sin_a).astype(BF16)
    for h in range(N_MLA_HEADS):
        o = h * QK_SLAB
        q_ref[:, o:o + LANES] = (qf[:, o:o + LANES] * scale).astype(BF16)
        q_ref[:, o + LANES:o + QK_SLAB] = (qf[:, o + LANES:o + QK_SLAB] * rope_mul).astype(BF16)
        k_ref[:, o:o + LANES] = kn[:, h * LANES:(h + 1) * LANES].astype(BF16)
        k_ref[:, o + LANES:o + QK_SLAB] = k_rope


def _in_proj(x2, pos2, attn_norm, w_in, b_gates, q_norm, w_q_up, kv_norm, w_kv_up):
    n, d = x2.shape
    tm = ROW_TILE
    half = QK_ROPE_DIM // 2
    swap = jnp.concatenate([jnp.arange(half, QK_ROPE_DIM), jnp.arange(0, half)])
    w_kr = w_in[:, OFF_KR:OFF_KR + QK_ROPE_DIM]
    w_krs = w_kr[:, swap]
    wm = w_in[:, :OFF_G].astype(BF16)
    w_g = jnp.pad(w_in[:, OFF_G:OFF_CQ], ((0, 0), (0, LANES - N_GATES)))
    wr = jnp.concatenate([w_in[:, OFF_CQ:OFF_KR], w_kr, w_kr, w_krs, w_krs, w_g], axis=1).astype(BF16)
    bgt = b_gates.reshape(N_GATES, 1)
    wq4 = w_q_up.reshape(Q_LORA, N_MLA_HEADS, QK_NOPE_DIM + QK_ROPE_DIM)
    wq_pe = wq4[:, :, QK_NOPE_DIM:]
    wq = jnp.concatenate([wq4, wq_pe[:, :, swap]], axis=2).reshape(Q_LORA, N_MLA_HEADS * QK_SLAB).astype(BF16)
    wkv4 = w_kv_up.reshape(KV_LORA, N_MLA_HEADS, QK_NOPE_DIM + V_HEAD_DIM)
    wk = wkv4[:, :, :QK_NOPE_DIM].reshape(KV_LORA, N_MLA_HEADS * QK_NOPE_DIM).astype(BF16)
    wv = wkv4[:, :, QK_NOPE_DIM:].reshape(KV_LORA, D_MLA).astype(BF16)
    freqs = ROPE_THETA ** (-jnp.arange(0, QK_ROPE_DIM, 2, dtype=F32) / QK_ROPE_DIM)
    freq_c = freqs.reshape(half, 1)
    full = lambda arr: pl.BlockSpec(arr.shape, lambda i: (0,) * arr.ndim)
    rows = lambda w: pl.BlockSpec((tm, w), lambda i: (i, 0))
    consts = [attn_norm.reshape(1, d), wm, wr, bgt, q_norm.reshape(1, Q_LORA), wq,
              kv_norm.reshape(1, KV_LORA), wk, wv, freq_c]
    return pl.pallas_call(
        _in_proj_kernel,
        grid=(n // tm,),
        in_specs=[rows(d), pl.BlockSpec((1, 1, tm), lambda i: (i, 0, 0))] + [full(c) for c in consts],
        out_specs=[rows(OFF_G), pl.BlockSpec((N_GATES, tm), lambda i: (0, i)),
                   rows(N_MLA_HEADS * QK_SLAB), rows(N_MLA_HEADS * QK_SLAB), rows(D_MLA)],
        out_shape=[jax.ShapeDtypeStruct((n, OFF_G), F32), jax.ShapeDtypeStruct((N_GATES, n), F32),
                   jax.ShapeDtypeStruct((n, N_MLA_HEADS * QK_SLAB), BF16),
                   jax.ShapeDtypeStruct((n, N_MLA_HEADS * QK_SLAB), BF16),
                   jax.ShapeDtypeStruct((n, D_MLA), BF16)],
        compiler_params=_params("parallel"),
        name="in_proj",
    )(x2, pos2.reshape(n // tm, 1, tm), *consts)


def _mlstm_kernel(q_ref, k_ref, v_ref, o_ref, g_ref, cwq_ref, cwk_ref, cbq_ref, cbk_ref, nrm_ref,
                  y_ref,
                  qc_ref, kc_ref, va_ref, cst_ref, ent_ref, b_ref, e_ref, r_ref, ew_ref, mw_ref, bt_ref, mp_ref):
    L = MLSTM_CHUNK
    dh = MLSTM_HEAD_DIM
    nc = q_ref.shape[1] // L
    s_len = q_ref.shape[1]
    h = pl.program_id(1)

    for d in range(2):
        ig = g_ref[2 * d * N_MLSTM_HEADS + h, 0]
        fg = g_ref[(2 * d + 1) * N_MLSTM_HEADS + h, 0]
        b = _cumsum_lanes(_log_sigmoid(fg), reverse=(d == 1))
        btot = b[:, L - 1:L] if d == 0 else b[:, 0:1]
        r = ig - b
        w = btot + r
        mw = jnp.max(w, axis=-1, keepdims=True)
        b_ref[d] = b
        e_ref[d] = b + _cummax_lanes(r, reverse=(d == 1))
        r_ref[d] = r
        ew_ref[d] = jnp.exp(w - mw)
        mw_ref[d] = jnp.broadcast_to(mw, (nc, L))
        bt_ref[d] = jnp.broadcast_to(btot, (nc, L))

    row = lax.broadcasted_iota(I32, (L, dh), 0)

    def conv_silu(ref, cw_ref, cb_ref, c):
        start = pl.multiple_of(c * L, L)
        x = ref[0, pl.ds(start, L), :]
        prev_row = jnp.where(c > 0, ref[0, pl.ds(jnp.maximum(start - 1, 0), 1), :], 0.0)
        next_row = jnp.where(c < nc - 1, ref[0, pl.ds(jnp.minimum(start + L, s_len - 1), 1), :], 0.0)
        x_prev = jnp.where(row == 0, prev_row, pltpu.roll(x, 1, 0))
        x_next = jnp.where(row == L - 1, next_row, pltpu.roll(x, L - 1, 0))
        y = cw_ref[0:1, :] * x_prev + cw_ref[1:2, :] * x + cw_ref[2:3, :] * x_next + cb_ref[...]
        return y * jax.nn.sigmoid(y)

    ones_blk = jnp.ones((L, dh), BF16)

    def pass1(c, carry):
        start = pl.multiple_of(c * L, L)
        qc_ref[pl.ds(start, L), :] = conv_silu(q_ref, cwq_ref, cbq_ref, c).astype(BF16)
        kk = conv_silu(k_ref, cwk_ref, cbk_ref, c) * (dh ** -0.5)
        kc_ref[pl.ds(start, L), :] = kk.astype(BF16)
        va = jnp.concatenate([v_ref[0, pl.ds(start, L), :].astype(BF16), ones_blk], axis=1)
        va_ref[pl.ds(start, L), :] = va
        kt = kk.T
        for d in range(2):
            kw_t = (kt * ew_ref[d, pl.ds(c, 1), :]).astype(BF16)
            cst_ref[d, c] = _dot(kw_t, va)
        return carry

    lax.fori_loop(0, nc, pass1, 0, unroll=CHUNK_UNROLL)

    ent_ref[:, 0] = jnp.zeros((2, dh, 2 * dh), F32)

    def scan(i, carry):
        out = []
        for d in range(2):
            m = carry[d]
            c = i if d == 0 else nc - 1 - i
            mw = mw_ref[d, pl.ds(c, 1), :]
            bt = bt_ref[d, pl.ds(c, 1), :]
            m_new = jnp.maximum(bt + m, mw)
            a = jnp.exp(bt + m - m_new)[:, 0:1]
            cc = jnp.exp(mw - m_new)[:, 0:1]
            ent_ref[d, i + 1] = a * ent_ref[d, i] + cc * cst_ref[d, c]
            mp_ref[d, pl.ds(c, 1), :] = m
            out.append(m_new)
        return tuple(out)

    lax.fori_loop(0, nc, scan, (jnp.zeros((1, L), F32), jnp.zeros((1, L), F32)))

    ti = lax.broadcasted_iota(I32, (L, L), 0)
    si = lax.broadcasted_iota(I32, (L, L), 1)
    masks = (si <= ti, si >= ti)

    def pass3(c, carry):
        start = pl.multiple_of(c * L, L)
        q = qc_ref[pl.ds(start, L), :]
        k = kc_ref[pl.ds(start, L), :]
        va = va_ref[pl.ds(start, L), :]
        qk = _dot_nt(q, k)
        hsum = jnp.zeros((L, dh), F32)
        for d in range(2):
            bmat = jnp.broadcast_to(b_ref[d, pl.ds(c, 1), :], (L, L)).T
            emat = jnp.broadcast_to(e_ref[d, pl.ds(c, 1), :], (L, L)).T
            dmat = jnp.where(masks[d], bmat + r_ref[d, pl.ds(c, 1), :], -jnp.inf)
            inter = bmat + mp_ref[d, pl.ds(c, 1), :]
            m_t = jnp.maximum(inter, emat)
            sc = qk * jnp.exp(dmat - m_t)
            a = jnp.exp(inter - m_t)
            intra = _dot(sc.astype(BF16), va)
            cross = _dot(q, ent_ref[d, c if d == 0 else nc - 1 - c].astype(BF16))
            num = intra[:, :dh] + a * cross[:, :dh]
            den = intra[:, dh:] + a * cross[:, dh:]
            hsum = hsum + num / jnp.maximum(jnp.abs(den), jnp.exp(-m_t))
        hn = _rms(hsum, nrm_ref[...])
        y_ref[0, pl.ds(start, L), :] = hn * jax.nn.sigmoid(o_ref[0, pl.ds(start, L), :])
        return carry

    lax.fori_loop(0, nc, pass3, 0, unroll=CHUNK_UNROLL)


def _mlstm(um3, gt4, conv_w, conv_b, mlstm_norm):
    bsz, s, _ = um3.shape
    H, dh, L = N_MLSTM_HEADS, MLSTM_HEAD_DIM, MLSTM_CHUNK
    nc = s // L
    col = lambda off: pl.BlockSpec((1, s, dh), lambda b, h: (b, 0, off + h))
    vec = lambda rows, off: pl.BlockSpec((rows, dh), lambda b, h: (0, off + h))
    cb = conv_b.reshape(1, 2 * D_MLSTM)
    return pl.pallas_call(
        _mlstm_kernel,
        grid=(bsz, H),
        in_specs=[col(0), col(H), col(2 * H), col(3 * H),
                  pl.BlockSpec((N_GATES, 1, nc, L), lambda b, h: (0, b, 0, 0)),
                  vec(3, 0), vec(3, H), vec(1, 0), vec(1, H), vec(1, 0)],
        out_specs=pl.BlockSpec((1, s, dh), lambda b, h: (b, 0, h)),
        out_shape=jax.ShapeDtypeStruct((bsz, s, D_MLSTM), F32),
        scratch_shapes=[pltpu.VMEM((s, dh), BF16), pltpu.VMEM((s, dh), BF16), pltpu.VMEM((s, 2 * dh), BF16),
                        pltpu.VMEM((2, nc, dh, 2 * dh), F32), pltpu.VMEM((2, nc + 1, dh, 2 * dh), F32)]
                       + [pltpu.VMEM((2, nc, L), F32) for _ in range(7)],
        compiler_params=_params("parallel", "parallel"),
        name="mlstm",
    )(um3, um3, um3, um3, gt4, conv_w, conv_w, cb, cb, mlstm_norm.reshape(1, D_MLSTM))


def _attn_kernel(q_ref, k_ref, v_ref, o_ref):
    tq = q_ref.shape[1]
    q = q_ref[0]
    m = jnp.full((tq, 1), -jnp.inf, F32)
    l = jnp.zeros((tq, 1), F32)
    acc = jnp.zeros((tq, V_HEAD_DIM), F32)
    chunk = min(KV_CHUNK, k_ref.shape[1])
    for c in range(k_ref.shape[1] // chunk):
        keys = slice(c * chunk, (c + 1) * chunk)
        s = _dot_nt(q, k_ref[0, keys, :])
        m_new = jnp.maximum(m, jnp.max(s, axis=-1, keepdims=True))
        alpha = jnp.exp2(m - m_new)
        p = jnp.exp2(s - m_new)
        l = alpha * l + jnp.sum(p, axis=-1, keepdims=True)
        acc = alpha * acc + _dot(p.astype(BF16), v_ref[0, keys, :])
        m = m_new
    o_ref[0] = acc / l


def _attention(q3, k3, v3):
    bsz, s, _ = q3.shape
    tq = min(Q_TILE, s)
    return pl.pallas_call(
        _attn_kernel,
        grid=(bsz, N_MLA_HEADS, s // tq),
        in_specs=[pl.BlockSpec((1, tq, QK_SLAB), lambda b, h, i: (b, i, h)),
                  pl.BlockSpec((1, s, QK_SLAB), lambda b, h, i: (b, 0, h)),
                  pl.BlockSpec((1, s, V_HEAD_DIM), lambda b, h, i: (b, 0, h))],
        out_specs=pl.BlockSpec((1, tq, V_HEAD_DIM), lambda b, h, i: (b, i, h)),
        out_shape=jax.ShapeDtypeStruct((bsz, s, D_MLA), F32),
        compiler_params=_params("parallel", "parallel", "parallel"),
        name="attention",
    )(q3, k3, v3)


def _route_tile(logits, carry):
    tm = logits.shape[1]
    erow = lax.broadcasted_iota(I32, logits.shape, 0)
    work = logits
    vals, hots = [], []
    for k in range(TOP_K):
        mx = jnp.max(work, axis=0, keepdims=True)
        idx = jnp.min(jnp.where(work == mx, erow, N_EXPERTS), axis=0, keepdims=True)
        hot = erow == idx
        work = jnp.where(hot, -jnp.inf, work)
        vals.append(mx)
        hots.append(hot)
    exps = [jnp.exp(v - vals[0]) for v in vals]
    tot = exps[0] + exps[1] + exps[2] + exps[3]
    multi = (hots[0] | hots[1] | hots[2] | hots[3]).astype(BF16)
    ti = lax.broadcasted_iota(I32, (tm, tm), 0)
    tj = lax.broadcasted_iota(I32, (tm, tm), 1)
    local_rank = _dot(multi, (ti < tj).astype(BF16))
    multi_rows = jnp.concatenate([multi, jnp.zeros((LANES - N_EXPERTS, tm), BF16)], axis=0)
    count = _dot_nt(jnp.ones((SUBLANES, tm), BF16), multi_rows)[0:1]
    padded = jnp.ceil(count * (1.0 / SEG_ALIGN)) * SEG_ALIGN
    filled = _cumsum_lanes(padded, reverse=False)
    local_start = filled - padded
    start_col = jnp.broadcast_to(local_start, (SUBLANES, LANES)).T[:N_EXPERTS, 0:1]
    slot_all = local_rank + start_col
    krow = lax.broadcasted_iota(I32, (SUBLANES, tm), 0)
    packed = jnp.zeros((SUBLANES, tm), F32)
    for k in range(TOP_K):
        sk = jnp.sum(jnp.where(hots[k], slot_all, 0.0), axis=0, keepdims=True)
        packed = jnp.where(krow == k, exps[k] / tot, jnp.where(krow == TOP_K + k, sk, packed))
    srow = lax.broadcasted_iota(I32, (SUBLANES, LANES), 0)
    seg = jnp.where(srow == 0, padded, jnp.where(srow == 1, local_start, jnp.where(srow == 2, carry,
                                                                                  filled[:, LANES - 1:LANES])))
    return packed, seg, padded


def _out_route_kernel(ym_ref, ya_ref, x_ref, mn_ref, wom_ref, woa_ref, fn_ref, wr_ref, br_ref,
                      h_ref, xn_ref, gate_ref, slot_ref, slot_t_ref, seg_ref, size_ref,
                      carry_ref):
    i = pl.program_id(0)
    tm = ROW_TILE

    @pl.when(i == 0)
    def _():
        carry_ref[...] = jnp.zeros_like(carry_ref)

    ya = _rms(ya_ref[...], mn_ref[...])
    h1 = x_ref[...] + _dot(ym_ref[...].astype(BF16), wom_ref[...]) + _dot(ya.astype(BF16), woa_ref[...])
    h_ref[...] = h1
    xn = _rms(h1, fn_ref[...])
    xn_hi = xn.astype(BF16)
    xn_ref[...] = xn_hi
    xn_lo = (xn - xn_hi.astype(F32)).astype(BF16)
    both = _dot_nt(wr_ref[...], xn_hi)
    logits = (both[:N_EXPERTS] + both[N_EXPERTS:] + _dot_nt(wr_ref[:N_EXPERTS, :], xn_lo)) + br_ref[...]
    carry = carry_ref[...]
    for s in range(x_ref.shape[0] // tm):
        packed, seg, padded = _route_tile(logits[:, s * tm:(s + 1) * tm], carry)
        krow = lax.broadcasted_iota(I32, packed.shape, 0)
        slot_t_ref[s] = jnp.where(krow < TOP_K, pltpu.roll(packed, TOP_K, 0), 0.0).astype(I32)
        cols = packed.T
        gate_ref[s * tm:(s + 1) * tm, :] = cols[:, :TOP_K]
        slot_ref[s * tm:(s + 1) * tm, :] = cols[:, TOP_K:2 * TOP_K].astype(I32)
        seg_ref[s] = seg.astype(I32)
        carry = carry + padded
    carry_ref[...] = carry
    size_ref[...] = carry.astype(I32)


def _out_route(ym2, ya2, x2, mla_norm, w_out, ffn_norm, w_router, b_router):
    n, d = x2.shape
    sub = min(ROUTE_TILES_PER_STEP, n // ROW_TILE)
    tm = ROW_TILE * sub
    nt = n // ROW_TILE
    wom = w_out[:D_MLSTM].astype(BF16)
    woa = w_out[D_MLSTM:].astype(BF16)
    wr_hi = w_router.T.astype(BF16)
    wr_lo = (w_router.T - wr_hi.astype(F32)).astype(BF16)
    wr = jnp.concatenate([wr_hi, wr_lo], axis=0)
    br = b_router.reshape(N_EXPERTS, 1)
    full = lambda arr: pl.BlockSpec(arr.shape, lambda i: (0,) * arr.ndim)
    rows = lambda w: pl.BlockSpec((tm, w), lambda i: (i, 0))
    consts = [mla_norm.reshape(1, D_MLA), wom, woa, ffn_norm.reshape(1, d), wr, br]
    return pl.pallas_call(
        _out_route_kernel,
        grid=(nt // sub,),
        in_specs=[rows(D_MLSTM), rows(D_MLA), rows(d)] + [full(c) for c in consts],
        out_specs=[rows(d), rows(d), rows(TOP_K), rows(TOP_K),
                   pl.BlockSpec((sub, SUBLANES, ROW_TILE), lambda i: (i, 0, 0)),
                   pl.BlockSpec((sub, SUBLANES, LANES), lambda i: (i, 0, 0)),
                   pl.BlockSpec((1, LANES), lambda i: (0, 0))],
        out_shape=[jax.ShapeDtypeStruct((n, d), F32), jax.ShapeDtypeStruct((n, d), BF16),
                   jax.ShapeDtypeStruct((n, TOP_K), F32), jax.ShapeDtypeStruct((n, TOP_K), I32),
                   jax.ShapeDtypeStruct((nt, SUBLANES, ROW_TILE), I32), jax.ShapeDtypeStruct((nt, SUBLANES, LANES), I32),
                   jax.ShapeDtypeStruct((1, LANES), I32)],
        scratch_shapes=[pltpu.VMEM((1, LANES), F32)],
        compiler_params=_params("arbitrary"),
        name="out_route",
    )(ym2, ya2, x2, *consts)


def _segment_copies(seg_ref, starts_ref, local_ref, global_ref, sem, to_global, wait):
    def copy(src, dst, rows):
        loc = local_ref.at[pl.ds(pl.multiple_of(src, SEG_ALIGN), rows)]
        glo = global_ref.at[pl.ds(pl.multiple_of(dst, SEG_ALIGN), rows)]
        return pltpu.make_async_copy(loc, glo, sem) if to_global else pltpu.make_async_copy(glo, loc, sem)

    if wait:
        copy(0, 0, pl.multiple_of(seg_ref[0, 3, 0], SEG_ALIGN)).wait()
        return

    def per_expert(e, carry):
        size = seg_ref[0, 0, e]
        src = seg_ref[0, 1, e]
        dst = starts_ref[e] + seg_ref[0, 2, e]
        off = 0
        rows = ROW_TILE
        while rows >= SEG_ALIGN:
            @pl.when((size & rows) != 0)
            def _(off=off, rows=rows):
                copy(src + off, dst + off, rows).start()

            off = off + (size & rows)
            rows //= 2
        return carry

    lax.fori_loop(0, N_EXPERTS, per_expert, 0)


def _dispatch_kernel(starts_ref, seg_ref, segp_ref, slot_t_ref, xn_ref, xs_ref, sort_ref, zero_ref, sem, zsem):
    i = pl.program_id(0)
    tm = xn_ref.shape[0]
    tmx = zero_ref.shape[0]
    cap = sort_ref.shape[1]

    @pl.when(i == 0)
    def _():
        zero_ref[...] = jnp.zeros_like(zero_ref)
        n_tail = (xs_ref.shape[0] - starts_ref[N_EXPERTS]) // tmx

        def clear_tile(row, wait):
            cp = pltpu.make_async_copy(zero_ref, xs_ref.at[pl.ds(pl.multiple_of(row, tmx), tmx)], zsem)
            if wait:
                cp.wait()
            else:
                cp.start()

        for wait in (False, True):
            def clear_group(e, carry, wait=wait):
                hi = starts_ref[e + 1]

                @pl.when(hi > starts_ref[e])
                def _():
                    clear_tile(hi - tmx, wait)

                return carry

            def clear_tail(t, carry, wait=wait):
                clear_tile(starts_ref[N_EXPERTS] + t * tmx, wait)
                return carry

            lax.fori_loop(0, N_EXPERTS, clear_group, 0)
            lax.fori_loop(0, n_tail, clear_tail, 0)

    slot = i % 2
    for r0 in range(0, cap, SORT_CHUNK):
        pos = lax.broadcasted_iota(I32, (SORT_CHUNK, tm), 0) + r0
        hit = pos == slot_t_ref[0, 0:1, :]
        for k in range(1, TOP_K):
            hit = hit | (pos == slot_t_ref[0, k:k + 1, :])
        sort_ref[slot, r0:r0 + SORT_CHUNK, :] = _pack_halves(_dot(hit.astype(BF16), xn_ref[...]))
    _segment_copies(seg_ref, starts_ref, sort_ref.at[slot], xs_ref, sem.at[slot], True, False)

    @pl.when(i >= 1)
    def _():
        _segment_copies(segp_ref, starts_ref, sort_ref.at[1 - slot], xs_ref, sem.at[1 - slot], True, True)

    @pl.when(i == pl.num_programs(0) - 1)
    def _():
        _segment_copies(seg_ref, starts_ref, sort_ref.at[slot], xs_ref, sem.at[slot], True, True)


def _dispatch(starts, seg, slot_t, xn, n_rows):
    n, d = xn.shape
    tm = ROW_TILE
    nt = n // tm
    smem = lambda f: pl.BlockSpec((1, SUBLANES, LANES), f, memory_space=pltpu.SMEM)
    any_spec = pl.BlockSpec(memory_space=pl.ANY)
    return pl.pallas_call(
        _dispatch_kernel,
        grid_spec=pltpu.PrefetchScalarGridSpec(
            num_scalar_prefetch=1,
            grid=(nt,),
            in_specs=[smem(lambda i, *_: (i, 0, 0)), smem(lambda i, *_: (jnp.maximum(i - 1, 0), 0, 0)),
                      pl.BlockSpec((1, SUBLANES, tm), lambda i, *_: (i, 0, 0)),
                      pl.BlockSpec((tm, d), lambda i, *_: (i, 0))],
            out_specs=any_spec,
            scratch_shapes=[pltpu.VMEM((2, SORT_ROWS, d // 2), U32), pltpu.VMEM((EXPERT_TILE, d // 2), U32),
                            pltpu.SemaphoreType.DMA((2,)), pltpu.SemaphoreType.DMA],
        ),
        out_shape=jax.ShapeDtypeStruct((n_rows, d // 2), U32),
        compiler_params=_params("arbitrary"),
        name="dispatch",
    )(starts, seg, seg, slot_t, xn)


def _experts_kernel(te_ref, tb_ref, tv_ref, xs_ref, wgu_ref, bgu_ref, wd_ref, bd_ref, out_ref,
                    wgu_bf, wd_bf):
    i = pl.program_id(0)
    de = wd_ref.shape[1]
    prev = te_ref[jnp.maximum(i - 1, 0)]

    @pl.when(jnp.logical_or(i == 0, te_ref[i] != prev))
    def _():
        wgu_bf[...] = wgu_ref[0].astype(BF16)
        wd_bf[...] = wd_ref[0].astype(BF16)

    @pl.when(tv_ref[i] == 1)
    def _():
        x = _unpack_halves(xs_ref[...])
        acc = jnp.zeros((xs_ref.shape[0], bd_ref.shape[2]), F32)
        for j in range(de // FF_CHUNK):
            cols = slice(j * FF_CHUNK, (j + 1) * FF_CHUNK)
            up_cols = slice(de + j * FF_CHUNK, de + (j + 1) * FF_CHUNK)
            g = _dot(x, wgu_bf[:, cols]) + bgu_ref[0, :, cols]
            u = _dot(x, wgu_bf[:, up_cols]) + bgu_ref[0, :, up_cols]
            g = jnp.minimum(g, SWIGLU_LIMIT)
            u = jnp.clip(u, -SWIGLU_LIMIT, SWIGLU_LIMIT)
            hm = (u + 1.0) * (g * jax.nn.sigmoid(g * SWIGLU_ALPHA))
            acc = acc + _dot(hm.astype(BF16), wd_bf[cols, :])
        out_ref[...] = _pack_halves((acc + bd_ref[0]).astype(BF16).astype(F32))

    @pl.when(tv_ref[i] == 0)
    def _():
        out_ref[...] = jnp.zeros_like(out_ref)


def _experts(tile_e, tile_b, tile_v, xs, w_gate_up, b_gate_up, w_down, b_down):
    n_rows, dw = xs.shape
    tmx = EXPERT_TILE
    ne, d, de2 = w_gate_up.shape
    de = de2 // 2
    return pl.pallas_call(
        _experts_kernel,
        grid_spec=pltpu.PrefetchScalarGridSpec(
            num_scalar_prefetch=3,
            grid=(n_rows // tmx,),
            in_specs=[pl.BlockSpec((tmx, dw), lambda i, te, tb, tv: (tb[i], 0)),
                      pl.BlockSpec((1, d, de2), lambda i, te, tb, tv: (te[i], 0, 0)),
                      pl.BlockSpec((1, 1, de2), lambda i, te, tb, tv: (te[i], 0, 0)),
                      pl.BlockSpec((1, de, d), lambda i, te, tb, tv: (te[i], 0, 0)),
                      pl.BlockSpec((1, 1, d), lambda i, te, tb, tv: (te[i], 0, 0))],
            out_specs=pl.BlockSpec((tmx, dw), lambda i, te, tb, tv: (i, 0)),
            scratch_shapes=[pltpu.VMEM((d, de2), BF16), pltpu.VMEM((de, d), BF16)],
        ),
        out_shape=jax.ShapeDtypeStruct((n_rows, dw), U32),
        compiler_params=_params("arbitrary"),
        name="experts",
    )(tile_e, tile_b, tile_v, xs, w_gate_up, b_gate_up.reshape(ne, 1, de2), w_down, b_down.reshape(ne, 1, d))


def _combine_kernel(starts_ref, seg_ref, segn_ref, h_ref, gate_ref, slot_ref, p_ref, ys_ref, pn_ref, wg_ref,
                    wp_ref, fn_ref, out_ref, ybuf, sem, *, final):
    i = pl.program_id(0)
    nt = pl.num_programs(0)
    tm = h_ref.shape[0]
    cap = ybuf.shape[1]

    @pl.when(i == 0)
    def _():
        ybuf[...] = jnp.zeros_like(ybuf)
        _segment_copies(seg_ref, starts_ref, ybuf.at[0], ys_ref, sem.at[0], False, False)

    slot = i % 2

    @pl.when(i + 1 < nt)
    def _():
        _segment_copies(segn_ref, starts_ref, ybuf.at[1 - slot], ys_ref, sem.at[1 - slot], False, False)

    _segment_copies(seg_ref, starts_ref, ybuf.at[slot], ys_ref, sem.at[slot], False, True)

    gate = gate_ref[...]
    h2 = h_ref[...]
    for c0 in range(0, cap, SORT_CHUNK):
        pos = lax.broadcasted_iota(I32, (tm, SORT_CHUNK), 1) + c0
        weights = jnp.zeros((tm, SORT_CHUNK), F32)
        for k in range(TOP_K):
            weights = jnp.where(pos == slot_ref[:, k:k + 1], gate[:, k:k + 1], weights)
        h2 = h2 + _dot(weights.astype(BF16), _unpack_halves(ybuf[slot, c0:c0 + SORT_CHUNK, :]))
    hn = _rms(h2, pn_ref[...]).astype(BF16)
    sg = jax.nn.sigmoid(_dot(hn, wg_ref[...]))
    h3 = h2 + sg * _dot(p_ref[...].astype(BF16), wp_ref[...])
    out_ref[...] = _rms(h3, fn_ref[...]) if final else h3


def _combine(starts, seg, h1, gate, slot, p2, ys, ple_norm, w_ple_gate, w_ple_proj, final_norm, final):
    n, d = h1.shape
    tm = ROW_TILE
    nt = n // tm
    wg = w_ple_gate.astype(BF16)
    wp = w_ple_proj.astype(BF16)
    smem = lambda f: pl.BlockSpec((1, SUBLANES, LANES), f, memory_space=pltpu.SMEM)
    full = lambda arr: pl.BlockSpec(arr.shape, lambda i, *_: (0,) * arr.ndim)
    rows = lambda w: pl.BlockSpec((tm, w), lambda i, *_: (i, 0))
    consts = [ple_norm.reshape(1, d), wg, wp, final_norm.reshape(1, d)]
    return pl.pallas_call(
        functools.partial(_combine_kernel, final=final),
        grid_spec=pltpu.PrefetchScalarGridSpec(
            num_scalar_prefetch=1,
            grid=(nt,),
            in_specs=[smem(lambda i, *_: (i, 0, 0)), smem(lambda i, *_: (jnp.minimum(i + 1, nt - 1), 0, 0)),
                      rows(d), rows(TOP_K), rows(TOP_K), rows(p2.shape[1]), pl.BlockSpec(memory_space=pl.ANY)]
                     + [full(c) for c in consts],
            out_specs=rows(d),
            scratch_shapes=[pltpu.VMEM((2, SORT_ROWS, d // 2), U32), pltpu.SemaphoreType.DMA((2,))],
        ),
        out_shape=jax.ShapeDtypeStruct((n, d), F32),
        compiler_params=_params("arbitrary"),
        name="combine",
    )(starts, seg, seg, h1, gate, slot, p2, ys, *consts)


def _route_tables(sizes, n_tiles):
    tmx = EXPERT_TILE
    tile_end = jnp.cumsum((sizes + tmx - 1) // tmx)
    starts = jnp.concatenate([jnp.zeros((1,), I32), tile_end * tmx]).astype(I32)
    n_valid = tile_end[-1]
    t = jnp.arange(n_tiles, dtype=I32)
    tb = jnp.minimum(t, n_valid - 1).astype(I32)
    te = jnp.sum(tile_end[None, :] <= tb[:, None], axis=1).astype(I32)
    tv = (t < n_valid).astype(I32)
    return starts, te, tb, tv


def kernel(x, p, positions, attn_norm, w_in, b_gates, conv_w, conv_b, mlstm_norm, q_norm, w_q_up, kv_norm, w_kv_up, mla_norm, w_out, ffn_norm, w_router, b_router, w_gate_up, b_gate_up, w_down, b_down, ple_norm, w_ple_gate, w_ple_proj, final_norm):
    bsz, s, d = x.shape
    n = bsz * s
    depth = p.shape[0]
    nc = s // MLSTM_CHUNK
    max_rows = n * TOP_K + (n // ROW_TILE) * N_EXPERTS * (SEG_ALIGN - 1) + N_EXPERTS * (EXPERT_TILE - 1)
    n_tiles = max_rows // EXPERT_TILE
    pos2 = positions.reshape(n, 1)
    h = x.reshape(n, d)
    for i in range(depth):
        um, gt, q, k, v = _in_proj(h, pos2, attn_norm[i], w_in[i], b_gates[i], q_norm[i], w_q_up[i],
                                   kv_norm[i], w_kv_up[i])
        ym = _mlstm(um.reshape(bsz, s, -1), gt.reshape(N_GATES, bsz, nc, MLSTM_CHUNK), conv_w[i], conv_b[i],
                    mlstm_norm[i])
        ya = _attention(q.reshape(bsz, s, -1), k.reshape(bsz, s, -1), v.reshape(bsz, s, -1))
        h1, xn, gate, slot, slot_t, seg, sizes = _out_route(ym.reshape(n, -1), ya.reshape(n, -1), h, mla_norm[i],
                                                            w_out[i], ffn_norm[i], w_router[i], b_router[i])
        starts, te, tb, tv = _route_tables(sizes[0, :N_EXPERTS], n_tiles)
        xs = _dispatch(starts, seg, slot_t, xn, n_tiles * EXPERT_TILE)
        ys = _experts(te, tb, tv, xs, w_gate_up[i], b_gate_up[i], w_down[i], b_down[i])
        h = _combine(starts, seg, h1, gate, slot, p[i].reshape(n, -1), ys, ple_norm[i], w_ple_gate[i],
                     w_ple_proj[i], final_norm, final=(i == depth - 1))
    return h.reshape(bsz, s, d)
```

```python
import functools

import jax
import jax.numpy as jnp
from jax import lax
from jax.experimental import pallas as pl
from jax.experimental.pallas import tpu as pltpu

F32 = jnp.float32
BF16 = jnp.bfloat16
I32 = jnp.int32
U32 = jnp.uint32
I16 = jnp.int16

N_MLSTM_HEADS = 4
MLSTM_HEAD_DIM = 128
D_MLSTM = N_MLSTM_HEADS * MLSTM_HEAD_DIM
MLSTM_CHUNK = 128
N_MLA_HEADS = 4
QK_NOPE_DIM = 128
QK_ROPE_DIM = 64
V_HEAD_DIM = 128
D_MLA = N_MLA_HEADS * V_HEAD_DIM
Q_LORA = 256
KV_LORA = 128
ROPE_THETA = 10000.0
N_EXPERTS = 32
TOP_K = 4
SWIGLU_LIMIT = 7.0
SWIGLU_ALPHA = 1.702
EPS = 1e-6
N_GATES = 4 * N_MLSTM_HEADS
OFF_G = 4 * D_MLSTM
OFF_CQ = OFF_G + N_GATES
OFF_CKV = OFF_CQ + Q_LORA
OFF_KR = OFF_CKV + KV_LORA

LANES = 128
SUBLANES = 8
QK_SLAB = 2 * LANES
VMEM_LIMIT_BYTES = 56 * 1024 * 1024
LOG2_E = 1.4426950408889634

ROW_TILE = 512
Q_TILE = 2048
KV_CHUNK = 1024
EXPERT_TILE = 512
FF_CHUNK = 512
CHUNK_UNROLL = 16
SEG_ALIGN = SUBLANES
ROUTE_TILES_PER_STEP = 2
SORT_CHUNK = 256
SORT_ROWS = ROW_TILE * TOP_K + N_EXPERTS * SEG_ALIGN


def _dot(a, b):
    return jnp.dot(a, b, preferred_element_type=F32)


def _dot_nt(a, b):
    return lax.dot_general(a, b, (((1,), (1,)), ((), ())), preferred_element_type=F32)


def _rms(x, g):
    return x * lax.rsqrt(jnp.mean(x * x, axis=-1, keepdims=True) + EPS) * g


def _log_sigmoid(x):
    return jnp.minimum(x, 0.0) - jnp.log(1.0 + jnp.exp(-jnp.abs(x)))


def _scan_lanes(x, reverse, op, identity):
    n = x.shape[-1]
    lane = lax.broadcasted_iota(I32, x.shape, x.ndim - 1)
    sh = 1
    while sh < n:
        if reverse:
            x = op(x, jnp.where(lane < n - sh, pltpu.roll(x, n - sh, x.ndim - 1), identity))
        else:
            x = op(x, jnp.where(lane >= sh, pltpu.roll(x, sh, x.ndim - 1), identity))
        sh *= 2
    return x


def _cumsum_lanes(x, reverse):
    return _scan_lanes(x, reverse, jnp.add, 0.0)


def _cummax_lanes(x, reverse):
    return _scan_lanes(x, reverse, jnp.maximum, -jnp.inf)


def _pack_halves(x):
    w = x.shape[1] // 2
    lo = lax.shift_right_logical(lax.bitcast_convert_type(x[:, :w], U32), jnp.uint32(16))
    hi = lax.bitcast_convert_type(x[:, w:], U32)
    return lo | hi


def _unpack_halves(words):
    lo = lax.bitcast_convert_type(lax.shift_left(words, jnp.uint32(16)), F32)
    hi = lax.bitcast_convert_type(words & jnp.uint32(0xFFFF0000), F32)
    return jnp.concatenate([lo, hi], axis=1).astype(BF16)


def _params(*sem):
    return pltpu.CompilerParams(dimension_semantics=sem, vmem_limit_bytes=VMEM_LIMIT_BYTES)


def _in_proj_kernel(x_ref, pos_ref, an_ref, wm_ref, wr_ref, bgt_ref, qn_ref, wq_ref, kvn_ref,
                    wk_ref, wv_ref, freq_ref,
                    um_ref, gt_ref, q_ref, k_ref, v_ref):
    a = _rms(x_ref[...], an_ref[...]).astype(BF16)
    um_ref[...] = _dot(a, wm_ref[...])
    rest = _dot(a, wr_ref[...])
    cq = rest[:, :Q_LORA]
    ckv = rest[:, Q_LORA:Q_LORA + KV_LORA]
    kr2 = rest[:, Q_LORA + KV_LORA:Q_LORA + KV_LORA + LANES]
    krs2 = rest[:, Q_LORA + KV_LORA + LANES:Q_LORA + KV_LORA + 2 * LANES]
    gt_ref[...] = rest[:, Q_LORA + KV_LORA + 2 * LANES:].T[:N_GATES, :] + bgt_ref[...]
    ang = freq_ref[...] * pos_ref[0].astype(F32)
    cos_t = jnp.cos(ang)
    sin_t = jnp.sin(ang)
    cos_a = jnp.concatenate([cos_t] * (LANES // cos_t.shape[0]), axis=0).T
    sin_a = jnp.concatenate([-sin_t, sin_t] * (LANES // (2 * sin_t.shape[0])), axis=0).T
    scale = (QK_NOPE_DIM + QK_ROPE_DIM) ** -0.5 * LOG2_E
    lane = lax.broadcasted_iota(I32, cos_a.shape, 1)
    rope_mul = jnp.where(lane < QK_ROPE_DIM, cos_a, sin_a) * scale
    qf = _dot(_rms(cq, qn_ref[...]).astype(BF16), wq_ref[...])
    ckvn = _rms(ckv, kvn_ref[...]).astype(BF16)
    kn = _dot(ckvn, wk_ref[...])
    v_ref[...] = _dot(ckvn, wv_ref[...]).astype(BF16)
    k_rope = (kr2 * cos_a + krs2 * sin_a).astype(BF16)
    for h in range(N_MLA_HEADS):
        o = h * QK_SLAB
        q_ref[:, o:o + LANES] = (qf[:, o:o + LANES] * scale).astype(BF16)
        q_ref[:, o + LANES:o + QK_SLAB] = (qf[:, o + LANES:o + QK_SLAB] * rope_mul).astype(BF16)
        k_ref[:, o:o + LANES] = kn[:, h * LANES:(h + 1) * LANES].astype(BF16)
        k_ref[:, o + LANES:o + QK_SLAB] = k_rope


def _in_proj(x2, pos2, attn_norm, w_in, b_gates, q_norm, w_q_up, kv_norm, w_kv_up):
    n, d = x2.shape
    tm = ROW_TILE
    half = QK_ROPE_DIM // 2
    swap = jnp.concatenate([jnp.arange(half, QK_ROPE_DIM), jnp.arange(0, half)])
    w_kr = w_in[:, OFF_KR:OFF_KR + QK_ROPE_DIM]
    w_krs = w_kr[:, swap]
    wm = w_in[:, :OFF_G].astype(BF16)
    w_g = jnp.pad(w_in[:, OFF_G:OFF_CQ], ((0, 0), (0, LANES - N_GATES)))
    wr = jnp.concatenate([w_in[:, OFF_CQ:OFF_KR], w_kr, w_kr, w_krs, w_krs, w_g], axis=1).astype(BF16)
    bgt = b_gates.reshape(N_GATES, 1)
    wq4 = w_q_up.reshape(Q_LORA, N_MLA_HEADS, QK_NOPE_DIM + QK_ROPE_DIM)
    wq_pe = wq4[:, :, QK_NOPE_DIM:]
    wq = jnp.concatenate([wq4, wq_pe[:, :, swap]], axis=2).reshape(Q_LORA, N_MLA_HEADS * QK_SLAB).astype(BF16)
    wkv4 = w_kv_up.reshape(KV_LORA, N_MLA_HEADS, QK_NOPE_DIM + V_HEAD_DIM)
    wk = wkv4[:, :, :QK_NOPE_DIM].reshape(KV_LORA, N_MLA_HEADS * QK_NOPE_DIM).astype(BF16)
    wv = wkv4[:, :, QK_NOPE_DIM:].reshape(KV_LORA, D_MLA).astype(BF16)
    freqs = ROPE_THETA ** (-jnp.arange(0, QK_ROPE_DIM, 2, dtype=F32) / QK_ROPE_DIM)
    freq_c = freqs.reshape(half, 1)
    full = lambda arr: pl.BlockSpec(arr.shape, lambda i: (0,) * arr.ndim)
    rows = lambda w: pl.BlockSpec((tm, w), lambda i: (i, 0))
    consts = [attn_norm.reshape(1, d), wm, wr, bgt, q_norm.reshape(1, Q_LORA), wq,
              kv_norm.reshape(1, KV_LORA), wk, wv, freq_c]
    return pl.pallas_call(
        _in_proj_kernel,
        grid=(n // tm,),
        in_specs=[rows(d), pl.BlockSpec((1, 1, tm), lambda i: (i, 0, 0))] + [full(c) for c in consts],
        out_specs=[rows(OFF_G), pl.BlockSpec((N_GATES, tm), lambda i: (0, i)),
                   rows(N_MLA_HEADS * QK_SLAB), rows(N_MLA_HEADS * QK_SLAB), rows(D_MLA)],
        out_shape=[jax.ShapeDtypeStruct((n, OFF_G), F32), jax.ShapeDtypeStruct((N_GATES, n), F32),
                   jax.ShapeDtypeStruct((n, N_MLA_HEADS * QK_SLAB), BF16),
                   jax.ShapeDtypeStruct((n, N_MLA_HEADS * QK_SLAB), BF16),
                   jax.ShapeDtypeStruct((n, D_MLA), BF16)],
        compiler_params=_params("parallel"),
        name="in_proj",
    )(x2, pos2.reshape(n // tm, 1, tm), *consts)


def _mlstm_kernel(q_ref, k_ref, v_ref, o_ref, g_ref, cwq_ref, cwk_ref, cbq_ref, cbk_ref, nrm_ref,
                  y_ref,
                  qc_ref, kc_ref, va_ref, cst_ref, ent_ref, b_ref, e_ref, r_ref, ew_ref, mw_ref, bt_ref, mp_ref):
    L = MLSTM_CHUNK
    dh = MLSTM_HEAD_DIM
    nc = q_ref.shape[1] // L
    s_len = q_ref.shape[1]
    h = pl.program_id(1)

    for d in range(2):
        ig = g_ref[2 * d * N_MLSTM_HEADS + h, 0]
        fg = g_ref[(2 * d + 1) * N_MLSTM_HEADS + h, 0]
        b = _cumsum_lanes(_log_sigmoid(fg), reverse=(d == 1))
        btot = b[:, L - 1:L] if d == 0 else b[:, 0:1]
        r = ig - b
        w = btot + r
        mw = jnp.max(w, axis=-1, keepdims=True)
        b_ref[d] = b
        e_ref[d] = b + _cummax_lanes(r, reverse=(d == 1))
        r_ref[d] = r
        ew_ref[d] = jnp.exp(w - mw)
        mw_ref[d] = jnp.broadcast_to(mw, (nc, L))
        bt_ref[d] = jnp.broadcast_to(btot, (nc, L))

    row = lax.broadcasted_iota(I32, (L, dh), 0)

    def conv_silu(ref, cw_ref, cb_ref, c):
        start = pl.multiple_of(c * L, L)
        x = ref[0, pl.ds(start, L), :]
        prev_row = jnp.where(c > 0, ref[0, pl.ds(jnp.maximum(start - 1, 0), 1), :], 0.0)
        next_row = jnp.where(c < nc - 1, ref[0, pl.ds(jnp.minimum(start + L, s_len - 1), 1), :], 0.0)
        x_prev = jnp.where(row == 0, prev_row, pltpu.roll(x, 1, 0))
        x_next = jnp.where(row == L - 1, next_row, pltpu.roll(x, L - 1, 0))
        y = cw_ref[0:1, :] * x_prev + cw_ref[1:2, :] * x + cw_ref[2:3, :] * x_next + cb_ref[...]
        return y * jax.nn.sigmoid(y)

    ones_blk = jnp.ones((L, dh), BF16)

    def pass1(c, carry):
        start = pl.multiple_of(c * L, L)
        qc_ref[pl.ds(start, L), :] = conv_silu(q_ref, cwq_ref, cbq_ref, c).astype(BF16)
        kk = conv_silu(k_ref, cwk_ref, cbk_ref, c) * (dh ** -0.5)
        kc_ref[pl.ds(start, L), :] = kk.astype(BF16)
        va = jnp.concatenate([v_ref[0, pl.ds(start, L), :].astype(BF16), ones_blk], axis=1)
        va_ref[pl.ds(start, L), :] = va
        kt = kk.T
        for d in range(2):
            kw_t = (kt * ew_ref[d, pl.ds(c, 1), :]).astype(BF16)
            cst_ref[d, c] = _dot(kw_t, va)
        return carry

    lax.fori_loop(0, nc, pass1, 0, unroll=CHUNK_UNROLL)

    ent_ref[:, 0] = jnp.zeros((2, dh, 2 * dh), F32)

    def scan(i, carry):
        out = []
        for d in range(2):
            m = carry[d]
            c = i if d == 0 else nc - 1 - i
            mw = mw_ref[d, pl.ds(c, 1), :]
            bt = bt_ref[d, pl.ds(c, 1), :]
            m_new = jnp.maximum(bt + m, mw)
            a = jnp.exp(bt + m - m_new)[:, 0:1]
            cc = jnp.exp(mw - m_new)[:, 0:1]
            ent_ref[d, i + 1] = a * ent_ref[d, i] + cc * cst_ref[d, c]
            mp_ref[d, pl.ds(c, 1), :] = m
            out.append(m_new)
        return tuple(out)

    lax.fori_loop(0, nc, scan, (jnp.zeros((1, L), F32), jnp.zeros((1, L), F32)))

    ti = lax.broadcasted_iota(I32, (L, L), 0)
    si = lax.broadcasted_iota(I32, (L, L), 1)
    masks = (si <= ti, si >= ti)

    def pass3(c, carry):
        start = pl.multiple_of(c * L, L)
        q = qc_ref[pl.ds(start, L), :]
        k = kc_ref[pl.ds(start, L), :]
        va = va_ref[pl.ds(start, L), :]
        qk = _dot_nt(q, k)
        hsum = jnp.zeros((L, dh), F32)
        for d in range(2):
            bmat = jnp.broadcast_to(b_ref[d, pl.ds(c, 1), :], (L, L)).T
            emat = jnp.broadcast_to(e_ref[d, pl.ds(c, 1), :], (L, L)).T
            dmat = jnp.where(masks[d], bmat + r_ref[d, pl.ds(c, 1), :], -jnp.inf)
            inter = bmat + mp_ref[d, pl.ds(c, 1), :]
            m_t = jnp.maximum(inter, emat)
            sc = qk * jnp.exp(dmat - m_t)
            a = jnp.exp(inter - m_t)
            intra = _dot(sc.astype(BF16), va)
            cross = _dot(q, ent_ref[d, c if d == 0 else nc - 1 - c].astype(BF16))
            num = intra[:, :dh] + a * cross[:, :dh]
            den = intra[:, dh:] + a * cross[:, dh:]
            hsum = hsum + num / jnp.maximum(jnp.abs(den), jnp.exp(-m_t))
        hn = _rms(hsum, nrm_ref[...])
        y_ref[0, pl.ds(start, L), :] = hn * jax.nn.sigmoid(o_ref[0, pl.ds(start, L), :])
        return carry

    lax.fori_loop(0, nc, pass3, 0, unroll=CHUNK_UNROLL)


def _mlstm(um3, gt4, conv_w, conv_b, mlstm_norm):
    bsz, s, _ = um3.shape
    H, dh, L = N_MLSTM_HEADS, MLSTM_HEAD_DIM, MLSTM_CHUNK
    nc = s // L
    col = lambda off: pl.BlockSpec((1, s, dh), lambda b, h: (b, 0, off + h))
    vec = lambda rows, off: pl.BlockSpec((rows, dh), lambda b, h: (0, off + h))
    cb = conv_b.reshape(1, 2 * D_MLSTM)
    return pl.pallas_call(
        _mlstm_kernel,
        grid=(bsz, H),
        in_specs=[col(0), col(H), col(2 * H), col(3 * H),
                  pl.BlockSpec((N_GATES, 1, nc, L), lambda b, h: (0, b, 0, 0)),
                  vec(3, 0), vec(3, H), vec(1, 0), vec(1, H), vec(1, 0)],
        out_specs=pl.BlockSpec((1, s, dh), lambda b, h: (b, 0, h)),
        out_shape=jax.ShapeDtypeStruct((bsz, s, D_MLSTM), F32),
        scratch_shapes=[pltpu.VMEM((s, dh), BF16), pltpu.VMEM((s, dh), BF16), pltpu.VMEM((s, 2 * dh), BF16),
                        pltpu.VMEM((2, nc, dh, 2 * dh), F32), pltpu.VMEM((2, nc + 1, dh, 2 * dh), F32)]
                       + [pltpu.VMEM((2, nc, L), F32) for _ in range(7)],
        compiler_params=_params("parallel", "parallel"),
        name="mlstm",
    )(um3, um3, um3, um3, gt4, conv_w, conv_w, cb, cb, mlstm_norm.reshape(1, D_MLSTM))


def _attn_kernel(q_ref, k_ref, v_ref, o_ref):
    tq = q_ref.shape[1]
    q = q_ref[0]
    m = jnp.full((tq, 1), -jnp.inf, F32)
    l = jnp.zeros((tq, 1), F32)
    acc = jnp.zeros((tq, V_HEAD_DIM), F32)
    chunk = min(KV_CHUNK, k_ref.shape[1])
    for c in range(k_ref.shape[1] // chunk):
        keys = slice(c * chunk, (c + 1) * chunk)
        s = _dot_nt(q, k_ref[0, keys, :])
        m_new = jnp.maximum(m, jnp.max(s, axis=-1, keepdims=True))
        alpha = jnp.exp2(m - m_new)
        p = jnp.exp2(s - m_new)
        l = alpha * l + jnp.sum(p, axis=-1, keepdims=True)
        acc = alpha * acc + _dot(p.astype(BF16), v_ref[0, keys, :])
        m = m_new
    o_ref[0] = acc / l


def _attention(q3, k3, v3):
    bsz, s, _ = q3.shape
    tq = min(Q_TILE, s)
    return pl.pallas_call(
        _attn_kernel,
        grid=(bsz, N_MLA_HEADS, s // tq),
        in_specs=[pl.BlockSpec((1, tq, QK_SLAB), lambda b, h, i: (b, i, h)),
                  pl.BlockSpec((1, s, QK_SLAB), lambda b, h, i: (b, 0, h)),
                  pl.BlockSpec((1, s, V_HEAD_DIM), lambda b, h, i: (b, 0, h))],
        out_specs=pl.BlockSpec((1, tq, V_HEAD_DIM), lambda b, h, i: (b, i, h)),
        out_shape=jax.ShapeDtypeStruct((bsz, s, D_MLA), F32),
        compiler_params=_params("parallel", "parallel", "parallel"),
        name="attention",
    )(q3, k3, v3)


def _route_tile(logits, carry):
    tm = logits.shape[1]
    erow = lax.broadcasted_iota(I32, logits.shape, 0)
    work = logits
    vals, hots = [], []
    for k in range(TOP_K):
        mx = jnp.max(work, axis=0, keepdims=True)
        idx = jnp.min(jnp.where(work == mx, erow, N_EXPERTS), axis=0, keepdims=True)
        hot = erow == idx
        work = jnp.where(hot, -jnp.inf, work)
        vals.append(mx)
        hots.append(hot)
    exps = [jnp.exp(v - vals[0]) for v in vals]
    tot = exps[0] + exps[1] + exps[2] + exps[3]
    multi = (hots[0] | hots[1] | hots[2] | hots[3]).astype(BF16)
    ti = lax.broadcasted_iota(I32, (tm, tm), 0)
    tj = lax.broadcasted_iota(I32, (tm, tm), 1)
    local_rank = _dot(multi, (ti < tj).astype(BF16))
    multi_rows = jnp.concatenate([multi, jnp.zeros((LANES - N_EXPERTS, tm), BF16)], axis=0)
    count = _dot_nt(jnp.ones((SUBLANES, tm), BF16), multi_rows)[0:1]
    padded = jnp.ceil(count * (1.0 / SEG_ALIGN)) * SEG_ALIGN
    filled = _cumsum_lanes(padded, reverse=False)
    local_start = filled - padded
    start_col = jnp.broadcast_to(local_start, (SUBLANES, LANES)).T[:N_EXPERTS, 0:1]
    slot_all = local_rank + start_col
    krow = lax.broadcasted_iota(I32, (SUBLANES, tm), 0)
    packed = jnp.zeros((SUBLANES, tm), F32)
    for k in range(TOP_K):
        sk = jnp.sum(jnp.where(hots[k], slot_all, 0.0), axis=0, keepdims=True)
        packed = jnp.where(krow == k, exps[k] / tot, jnp.where(krow == TOP_K + k, sk, packed))
    srow = lax.broadcasted_iota(I32, (SUBLANES, LANES), 0)
    seg = jnp.where(srow == 0, padded, jnp.where(srow == 1, local_start, jnp.where(srow == 2, carry,
                                                                                  filled[:, LANES - 1:LANES])))
    return packed, seg, padded


def _out_route_kernel(ym_ref, ya_ref, x_ref, mn_ref, wom_ref, woa_ref, fn_ref, wr_ref, br_ref,
                      h_ref, xn_ref, gate_ref, slot_ref, slot_t_ref, seg_ref, size_ref,
                      carry_ref):
    i = pl.program_id(0)
    tm = ROW_TILE

    @pl.when(i == 0)
    def _():
        carry_ref[...] = jnp.zeros_like(carry_ref)

    ya = _rms(ya_ref[...], mn_ref[...])
    h1 = x_ref[...] + _dot(ym_ref[...].astype(BF16), wom_ref[...]) + _dot(ya.astype(BF16), woa_ref[...])
    h_ref[...] = h1
    xn = _rms(h1, fn_ref[...])
    xn_hi = xn.astype(BF16)
    xn_ref[...] = xn_hi
    xn_lo = (xn - xn_hi.astype(F32)).astype(BF16)
    both = _dot_nt(wr_ref[...], xn_hi)
    logits = (both[:N_EXPERTS] + both[N_EXPERTS:] + _dot_nt(wr_ref[:N_EXPERTS, :], xn_lo)) + br_ref[...]
    carry = carry_ref[...]
    for s in range(x_ref.shape[0] // tm):
        packed, seg, padded = _route_tile(logits[:, s * tm:(s + 1) * tm], carry)
        krow = lax.broadcasted_iota(I32, packed.shape, 0)
        slot_t_ref[s] = jnp.where(krow < TOP_K, pltpu.roll(packed, TOP_K, 0), 0.0).astype(I32)
        cols = packed.T
        gate_ref[s * tm:(s + 1) * tm, :] = cols[:, :TOP_K]
        slot_ref[s * tm:(s + 1) * tm, :] = cols[:, TOP_K:2 * TOP_K].astype(I32)
        seg_ref[s] = seg.astype(I32)
        carry = carry + padded
    carry_ref[...] = carry
    size_ref[...] = carry.astype(I32)


def _out_route(ym2, ya2, x2, mla_norm, w_out, ffn_norm, w_router, b_router):
    n, d = x2.shape
    sub = min(ROUTE_TILES_PER_STEP, n // ROW_TILE)
    tm = ROW_TILE * sub
    nt = n // ROW_TILE
    wom = w_out[:D_MLSTM].astype(BF16)
    woa = w_out[D_MLSTM:].astype(BF16)
    wr_hi = w_router.T.astype(BF16)
    wr_lo = (w_router.T - wr_hi.astype(F32)).astype(BF16)
    wr = jnp.concatenate([wr_hi, wr_lo], axis=0)
    br = b_router.reshape(N_EXPERTS, 1)
    full = lambda arr: pl.BlockSpec(arr.shape, lambda i: (0,) * arr.ndim)
    rows = lambda w: pl.BlockSpec((tm, w), lambda i: (i, 0))
    consts = [mla_norm.reshape(1, D_MLA), wom, woa, ffn_norm.reshape(1, d), wr, br]
    return pl.pallas_call(
        _out_route_kernel,
        grid=(nt // sub,),
        in_specs=[rows(D_MLSTM), rows(D_MLA), rows(d)] + [full(c) for c in consts],
        out_specs=[rows(d), rows(d), rows(TOP_K), rows(TOP_K),
                   pl.BlockSpec((sub, SUBLANES, ROW_TILE), lambda i: (i, 0, 0)),
                   pl.BlockSpec((sub, SUBLANES, LANES), lambda i: (i, 0, 0)),
                   pl.BlockSpec((1, LANES), lambda i: (0, 0))],
        out_shape=[jax.ShapeDtypeStruct((n, d), F32), jax.ShapeDtypeStruct((n, d), BF16),
                   jax.ShapeDtypeStruct((n, TOP_K), F32), jax.ShapeDtypeStruct((n, TOP_K), I32),
                   jax.ShapeDtypeStruct((nt, SUBLANES, ROW_TILE), I32), jax.ShapeDtypeStruct((nt, SUBLANES, LANES), I32),
                   jax.ShapeDtypeStruct((1, LANES), I32)],
        scratch_shapes=[pltpu.VMEM((1, LANES), F32)],
        compiler_params=_params("arbitrary"),
        name="out_route",
    )(ym2, ya2, x2, *consts)


def _segment_copies(seg_ref, starts_ref, local_ref, global_ref, sem, to_global, wait):
    def copy(src, dst, rows):
        loc = local_ref.at[pl.ds(pl.multiple_of(src, SEG_ALIGN), rows)]
        glo = global_ref.at[pl.ds(pl.multiple_of(dst, SEG_ALIGN), rows)]
        return pltpu.make_async_copy(loc, glo, sem) if to_global else pltpu.make_async_copy(glo, loc, sem)

    if wait:
        copy(0, 0, pl.multiple_of(seg_ref[0, 3, 0], SEG_ALIGN)).wait()
        return

    def per_expert(e, carry):
        size = seg_ref[0, 0, e]
        src = seg_ref[0, 1, e]
        dst = starts_ref[e] + seg_ref[0, 2, e]
        off = 0
        rows = ROW_TILE
        while rows >= SEG_ALIGN:
            @pl.when((size & rows) != 0)
            def _(off=off, rows=rows):
                copy(src + off, dst + off, rows).start()

            off = off + (size & rows)
            rows //= 2
        return carry

    lax.fori_loop(0, N_EXPERTS, per_expert, 0)


def _dispatch_kernel(starts_ref, seg_ref, segp_ref, slot_t_ref, xn_ref, xs_ref, sort_ref, zero_ref, sem, zsem):
    i = pl.program_id(0)
    tm = xn_ref.shape[0]
    tmx = zero_ref.shape[0]
    cap = sort_ref.shape[1]

    @pl.when(i == 0)
    def _():
        zero_ref[...] = jnp.zeros_like(zero_ref)
        n_tail = (xs_ref.shape[0] - starts_ref[N_EXPERTS]) // tmx

        def clear_tile(row, wait):
            cp = pltpu.make_async_copy(zero_ref, xs_ref.at[pl.ds(pl.multiple_of(row, tmx), tmx)], zsem)
            if wait:
                cp.wait()
            else:
                cp.start()

        for wait in (False, True):
            def clear_group(e, carry, wait=wait):
                hi = starts_ref[e + 1]

                @pl.when(hi > starts_ref[e])
                def _():
                    clear_tile(hi - tmx, wait)

                return carry

            def clear_tail(t, carry, wait=wait):
                clear_tile(starts_ref[N_EXPERTS] + t * tmx, wait)
                return carry

            lax.fori_loop(0, N_EXPERTS, clear_group, 0)
            lax.fori_loop(0, n_tail, clear_tail, 0)

    slot = i % 2
    slots16 = slot_t_ref[0].astype(I16)
    for r0 in range(0, cap, SORT_CHUNK):
        pos = (lax.broadcasted_iota(I32, (SORT_CHUNK, tm), 0) + r0).astype(I16)
        hit = pos == slots16[0:1, :]
        for k in range(1, TOP_K):
            hit = hit | (pos == slots16[k:k + 1, :])
        sort_ref[slot, r0:r0 + SORT_CHUNK, :] = _pack_halves(_dot(hit.astype(BF16), xn_ref[...]))
    _segment_copies(seg_ref, starts_ref, sort_ref.at[slot], xs_ref, sem.at[slot], True, False)

    @pl.when(i >= 1)
    def _():
        _segment_copies(segp_ref, starts_ref, sort_ref.at[1 - slot], xs_ref, sem.at[1 - slot], True, True)

    @pl.when(i == pl.num_programs(0) - 1)
    def _():
        _segment_copies(seg_ref, starts_ref, sort_ref.at[slot], xs_ref, sem.at[slot], True, True)


def _dispatch(starts, seg, slot_t, xn, n_rows):
    n, d = xn.shape
    tm = ROW_TILE
    nt = n // tm
    smem = lambda f: pl.BlockSpec((1, SUBLANES, LANES), f, memory_space=pltpu.SMEM)
    any_spec = pl.BlockSpec(memory_space=pl.ANY)
    return pl.pallas_call(
        _dispatch_kernel,
        grid_spec=pltpu.PrefetchScalarGridSpec(
            num_scalar_prefetch=1,
            grid=(nt,),
            in_specs=[smem(lambda i, *_: (i, 0, 0)), smem(lambda i, *_: (jnp.maximum(i - 1, 0), 0, 0)),
                      pl.BlockSpec((1, SUBLANES, tm), lambda i, *_: (i, 0, 0)),
                      pl.BlockSpec((tm, d), lambda i, *_: (i, 0))],
            out_specs=any_spec,
            scratch_shapes=[pltpu.VMEM((2, SORT_ROWS, d // 2), U32), pltpu.VMEM((EXPERT_TILE, d // 2), U32),
                            pltpu.SemaphoreType.DMA((2,)), pltpu.SemaphoreType.DMA],
        ),
        out_shape=jax.ShapeDtypeStruct((n_rows, d // 2), U32),
        compiler_params=_params("arbitrary"),
        name="dispatch",
    )(starts, seg, seg, slot_t, xn)


def _experts_kernel(te_ref, tb_ref, tv_ref, xs_ref, wgu_ref, bgu_ref, wd_ref, bd_ref, out_ref,
                    wgu_bf, wd_bf):
    i = pl.program_id(0)
    de = wd_ref.shape[1]
    prev = te_ref[jnp.maximum(i - 1, 0)]

    @pl.when(jnp.logical_or(i == 0, te_ref[i] != prev))
    def _():
        wgu_bf[...] = wgu_ref[0].astype(BF16)
        wd_bf[...] = wd_ref[0].astype(BF16)

    @pl.when(tv_ref[i] == 1)
    def _():
        x = _unpack_halves(xs_ref[...])
        acc = jnp.zeros((xs_ref.shape[0], bd_ref.shape[2]), F32)
        for j in range(de // FF_CHUNK):
            cols = slice(j * FF_CHUNK, (j + 1) * FF_CHUNK)
            up_cols = slice(de + j * FF_CHUNK, de + (j + 1) * FF_CHUNK)
            g = _dot(x, wgu_bf[:, cols]) + bgu_ref[0, :, cols]
            u = _dot(x, wgu_bf[:, up_cols]) + bgu_ref[0, :, up_cols]
            g = jnp.minimum(g, SWIGLU_LIMIT)
            u = jnp.clip(u, -SWIGLU_LIMIT, SWIGLU_LIMIT)
            hm = (u + 1.0) * (g * jax.nn.sigmoid(g * SWIGLU_ALPHA))
            acc = acc + _dot(hm.astype(BF16), wd_bf[cols, :])
        out_ref[...] = _pack_halves((acc + bd_ref[0]).astype(BF16).astype(F32))

    @pl.when(tv_ref[i] == 0)
    def _():
        out_ref[...] = jnp.zeros_like(out_ref)


def _experts(tile_e, tile_b, tile_v, xs, w_gate_up, b_gate_up, w_down, b_down):
    n_rows, dw = xs.shape
    tmx = EXPERT_TILE
    ne, d, de2 = w_gate_up.shape
    de = de2 // 2
    return pl.pallas_call(
        _experts_kernel,
        grid_spec=pltpu.PrefetchScalarGridSpec(
            num_scalar_prefetch=3,
            grid=(n_rows // tmx,),
            in_specs=[pl.BlockSpec((tmx, dw), lambda i, te, tb, tv: (tb[i], 0)),
                      pl.BlockSpec((1, d, de2), lambda i, te, tb, tv: (te[i], 0, 0)),
                      pl.BlockSpec((1, 1, de2), lambda i, te, tb, tv: (te[i], 0, 0)),
                      pl.BlockSpec((1, de, d), lambda i, te, tb, tv: (te[i], 0, 0)),
                      pl.BlockSpec((1, 1, d), lambda i, te, tb, tv: (te[i], 0, 0))],
            out_specs=pl.BlockSpec((tmx, dw), lambda i, te, tb, tv: (i, 0)),
            scratch_shapes=[pltpu.VMEM((d, de2), BF16), pltpu.VMEM((de, d), BF16)],
        ),
        out_shape=jax.ShapeDtypeStruct((n_rows, dw), U32),
        compiler_params=_params("arbitrary"),
        name="experts",
    )(tile_e, tile_b, tile_v, xs, w_gate_up, b_gate_up.reshape(ne, 1, de2), w_down, b_down.reshape(ne, 1, d))


def _combine_kernel(starts_ref, seg_ref, segn_ref, h_ref, gate_ref, slot_ref, p_ref, ys_ref, pn_ref, wg_ref,
                    wp_ref, fn_ref, out_ref, ybuf, sem, *, final):
    i = pl.program_id(0)
    nt = pl.num_programs(0)
    tm = h_ref.shape[0]
    cap = ybuf.shape[1]

    @pl.when(i == 0)
    def _():
        ybuf[...] = jnp.zeros_like(ybuf)
        _segment_copies(seg_ref, starts_ref, ybuf.at[0], ys_ref, sem.at[0], False, False)

    slot = i % 2

    @pl.when(i + 1 < nt)
    def _():
        _segment_copies(segn_ref, starts_ref, ybuf.at[1 - slot], ys_ref, sem.at[1 - slot], False, False)

    _segment_copies(seg_ref, starts_ref, ybuf.at[slot], ys_ref, sem.at[slot], False, True)

    gate = gate_ref[...].astype(BF16)
    slot16 = slot_ref[...].astype(I16)
    h2 = h_ref[...]
    for c0 in range(0, cap, SORT_CHUNK):
        pos = (lax.broadcasted_iota(I32, (tm, SORT_CHUNK), 1) + c0).astype(I16)
        weights = jnp.zeros((tm, SORT_CHUNK), BF16)
        for k in range(TOP_K):
            weights = jnp.where(pos == slot16[:, k:k + 1], gate[:, k:k + 1], weights)
        h2 = h2 + _dot(weights, _unpack_halves(ybuf[slot, c0:c0 + SORT_CHUNK, :]))
    hn = _rms(h2, pn_ref[...]).astype(BF16)
    sg = jax.nn.sigmoid(_dot(hn, wg_ref[...]))
    h3 = h2 + sg * _dot(p_ref[...].astype(BF16), wp_ref[...])
    out_ref[...] = _rms(h3, fn_ref[...]) if final else h3


def _combine(starts, seg, h1, gate, slot, p2, ys, ple_norm, w_ple_gate, w_ple_proj, final_norm, final):
    n, d = h1.shape
    tm = ROW_TILE
    nt = n // tm
    wg = w_ple_gate.astype(BF16)
    wp = w_ple_proj.astype(BF16)
    smem = lambda f: pl.BlockSpec((1, SUBLANES, LANES), f, memory_space=pltpu.SMEM)
    full = lambda arr: pl.BlockSpec(arr.shape, lambda i, *_: (0,) * arr.ndim)
    rows = lambda w: pl.BlockSpec((tm, w), lambda i, *_: (i, 0))
    consts = [ple_norm.reshape(1, d), wg, wp, final_norm.reshape(1, d)]
    return pl.pallas_call(
        functools.partial(_combine_kernel, final=final),
        grid_spec=pltpu.PrefetchScalarGridSpec(
            num_scalar_prefetch=1,
            grid=(nt,),
            in_specs=[smem(lambda i, *_: (i, 0, 0)), smem(lambda i, *_: (jnp.minimum(i + 1, nt - 1), 0, 0)),
                      rows(d), rows(TOP_K), rows(TOP_K), rows(p2.shape[1]), pl.BlockSpec(memory_space=pl.ANY)]
                     + [full(c) for c in consts],
            out_specs=rows(d),
            scratch_shapes=[pltpu.VMEM((2, SORT_ROWS, d // 2), U32), pltpu.SemaphoreType.DMA((2,))],
        ),
        out_shape=jax.ShapeDtypeStruct((n, d), F32),
        compiler_params=_params("arbitrary"),
        name="combine",
    )(starts, seg, seg, h1, gate, slot, p2, ys, *consts)


def _route_tables(sizes, n_tiles):
    tmx = EXPERT_TILE
    tile_end = jnp.cumsum((sizes + tmx - 1) // tmx)
    starts = jnp.concatenate([jnp.zeros((1,), I32), tile_end * tmx]).astype(I32)
    n_valid = tile_end[-1]
    t = jnp.arange(n_tiles, dtype=I32)
    tb = jnp.minimum(t, n_valid - 1).astype(I32)
    te = jnp.sum(tile_end[None, :] <= tb[:, None], axis=1).astype(I32)
    tv = (t < n_valid).astype(I32)
    return starts, te, tb, tv


def kernel(x, p, positions, attn_norm, w_in, b_gates, conv_w, conv_b, mlstm_norm, q_norm, w_q_up, kv_norm, w_kv_up, mla_norm, w_out, ffn_norm, w_router, b_router, w_gate_up, b_gate_up, w_down, b_down, ple_norm, w_ple_gate, w_ple_proj, final_norm):
    bsz, s, d = x.shape
    n = bsz * s
    depth = p.shape[0]
    nc = s // MLSTM_CHUNK
    max_rows = n * TOP_K + (n // ROW_TILE) * N_EXPERTS * (SEG_ALIGN - 1) + N_EXPERTS * (EXPERT_TILE - 1)
    n_tiles = max_rows // EXPERT_TILE
    pos2 = positions.reshape(n, 1)
    h = x.reshape(n, d)
    for i in range(depth):
        um, gt, q, k, v = _in_proj(h, pos2, attn_norm[i], w_in[i], b_gates[i], q_norm[i], w_q_up[i],
                                   kv_norm[i], w_kv_up[i])
        ym = _mlstm(um.reshape(bsz, s, -1), gt.reshape(N_GATES, bsz, nc, MLSTM_CHUNK), conv_w[i], conv_b[i],
                    mlstm_norm[i])
        ya = _attention(q.reshape(bsz, s, -1), k.reshape(bsz, s, -1), v.reshape(bsz, s, -1))
        h1, xn, gate, slot, slot_t, seg, sizes = _out_route(ym.reshape(n, -1), ya.reshape(n, -1), h, mla_norm[i],
                                                            w_out[i], ffn_norm[i], w_router[i], b_router[i])
        starts, te, tb, tv = _route_tables(sizes[0, :N_EXPERTS], n_tiles)
        xs = _dispatch(starts, seg, slot_t, xn, n_tiles * EXPERT_TILE)
        ys = _experts(te, tb, tv, xs, w_gate_up[i], b_gate_up[i], w_down[i], b_down[i])
        h = _combine(starts, seg, h1, gate, slot, p[i].reshape(n, -1), ys, ple_norm[i], w_ple_gate[i],
                     w_ple_proj[i], final_norm, final=(i == depth - 1))
    return h.reshape(bsz, s, d)
```

```python
import functools

import jax
import jax.numpy as jnp
from jax import lax
from jax.experimental import pallas as pl
from jax.experimental.pallas import tpu as pltpu

F32 = jnp.float32
BF16 = jnp.bfloat16
I32 = jnp.int32
U32 = jnp.uint32
I16 = jnp.int16

N_MLSTM_HEADS = 4
MLSTM_HEAD_DIM = 128
D_MLSTM = N_MLSTM_HEADS * MLSTM_HEAD_DIM
MLSTM_CHUNK = 128
N_MLA_HEADS = 4
QK_NOPE_DIM = 128
QK_ROPE_DIM = 64
V_HEAD_DIM = 128
D_MLA = N_MLA_HEADS * V_HEAD_DIM
Q_LORA = 256
KV_LORA = 128
ROPE_THETA = 10000.0
N_EXPERTS = 32
TOP_K = 4
SWIGLU_LIMIT = 7.0
SWIGLU_ALPHA = 1.702
EPS = 1e-6
N_GATES = 4 * N_MLSTM_HEADS
OFF_G = 4 * D_MLSTM
OFF_CQ = OFF_G + N_GATES
OFF_CKV = OFF_CQ + Q_LORA
OFF_KR = OFF_CKV + KV_LORA

LANES = 128
SUBLANES = 8
QK_SLAB = 2 * LANES
VMEM_LIMIT_BYTES = 56 * 1024 * 1024
LOG2_E = 1.4426950408889634

ROW_TILE = 512
Q_TILE = 2048
KV_CHUNK = 1024
EXPERT_TILE = 768
FF_CHUNK = 512
CHUNK_UNROLL = 16
SEG_ALIGN = SUBLANES
ROUTE_TILES_PER_STEP = 2
SORT_CHUNK = 256
SORT_ROWS = ROW_TILE * TOP_K + N_EXPERTS * SEG_ALIGN


def _dot(a, b):
    return jnp.dot(a, b, preferred_element_type=F32)


def _dot_nt(a, b):
    return lax.dot_general(a, b, (((1,), (1,)), ((), ())), preferred_element_type=F32)


def _rms(x, g):
    return x * lax.rsqrt(jnp.mean(x * x, axis=-1, keepdims=True) + EPS) * g


def _log_sigmoid(x):
    return jnp.minimum(x, 0.0) - jnp.log(1.0 + jnp.exp(-jnp.abs(x)))


def _scan_lanes(x, reverse, op, identity):
    n = x.shape[-1]
    lane = lax.broadcasted_iota(I32, x.shape, x.ndim - 1)
    sh = 1
    while sh < n:
        if reverse:
            x = op(x, jnp.where(lane < n - sh, pltpu.roll(x, n - sh, x.ndim - 1), identity))
        else:
            x = op(x, jnp.where(lane >= sh, pltpu.roll(x, sh, x.ndim - 1), identity))
        sh *= 2
    return x


def _cumsum_lanes(x, reverse):
    return _scan_lanes(x, reverse, jnp.add, 0.0)


def _cummax_lanes(x, reverse):
    return _scan_lanes(x, reverse, jnp.maximum, -jnp.inf)


def _pack_halves(x):
    w = x.shape[1] // 2
    lo = lax.shift_right_logical(lax.bitcast_convert_type(x[:, :w], U32), jnp.uint32(16))
    hi = lax.bitcast_convert_type(x[:, w:], U32)
    return lo | hi


def _unpack_halves(words):
    lo = lax.bitcast_convert_type(lax.shift_left(words, jnp.uint32(16)), F32)
    hi = lax.bitcast_convert_type(words & jnp.uint32(0xFFFF0000), F32)
    return jnp.concatenate([lo, hi], axis=1).astype(BF16)


def _params(*sem):
    return pltpu.CompilerParams(dimension_semantics=sem, vmem_limit_bytes=VMEM_LIMIT_BYTES)


def _in_proj_kernel(x_ref, pos_ref, an_ref, wm_ref, wr_ref, bgt_ref, qn_ref, wq_ref, kvn_ref,
                    wk_ref, wv_ref, freq_ref,
                    um_ref, gt_ref, q_ref, k_ref, v_ref):
    a = _rms(x_ref[...], an_ref[...]).astype(BF16)
    um_ref[...] = _dot(a, wm_ref[...])
    rest = _dot(a, wr_ref[...])
    cq = rest[:, :Q_LORA]
    ckv = rest[:, Q_LORA:Q_LORA + KV_LORA]
    kr2 = rest[:, Q_LORA + KV_LORA:Q_LORA + KV_LORA + LANES]
    krs2 = rest[:, Q_LORA + KV_LORA + LANES:Q_LORA + KV_LORA + 2 * LANES]
    gt_ref[...] = rest[:, Q_LORA + KV_LORA + 2 * LANES:].T[:N_GATES, :] + bgt_ref[...]
    ang = freq_ref[...] * pos_ref[0].astype(F32)
    cos_t = jnp.cos(ang)
    sin_t = jnp.sin(ang)
    cos_a = jnp.concatenate([cos_t] * (LANES // cos_t.shape[0]), axis=0).T
    sin_a = jnp.concatenate([-sin_t, sin_t] * (LANES // (2 * sin_t.shape[0])), axis=0).T
    scale = (QK_NOPE_DIM + QK_ROPE_DIM) ** -0.5 * LOG2_E
    lane = lax.broadcasted_iota(I32, cos_a.shape, 1)
    rope_mul = jnp.where(lane < QK_ROPE_DIM, cos_a, sin_a) * scale
    qf = _dot(_rms(cq, qn_ref[...]).astype(BF16), wq_ref[...])
    ckvn = _rms(ckv, kvn_ref[...]).astype(BF16)
    kn = _dot(ckvn, wk_ref[...])
    v_ref[...] = _dot(ckvn, wv_ref[...]).astype(BF16)
    k_rope = (kr2 * cos_a + krs2 * sin_a).astype(BF16)
    for h in range(N_MLA_HEADS):
        o = h * QK_SLAB
        q_ref[:, o:o + LANES] = (qf[:, o:o + LANES] * scale).astype(BF16)
        q_ref[:, o + LANES:o + QK_SLAB] = (qf[:, o + LANES:o + QK_SLAB] * rope_mul).astype(BF16)
        k_ref[:, o:o + LANES] = kn[:, h * LANES:(h + 1) * LANES].astype(BF16)
        k_ref[:, o + LANES:o + QK_SLAB] = k_rope


def _in_proj(x2, pos2, attn_norm, w_in, b_gates, q_norm, w_q_up, kv_norm, w_kv_up):
    n, d = x2.shape
    tm = ROW_TILE
    half = QK_ROPE_DIM // 2
    swap = jnp.concatenate([jnp.arange(half, QK_ROPE_DIM), jnp.arange(0, half)])
    w_kr = w_in[:, OFF_KR:OFF_KR + QK_ROPE_DIM]
    w_krs = w_kr[:, swap]
    wm = w_in[:, :OFF_G].astype(BF16)
    w_g = jnp.pad(w_in[:, OFF_G:OFF_CQ], ((0, 0), (0, LANES - N_GATES)))
    wr = jnp.concatenate([w_in[:, OFF_CQ:OFF_KR], w_kr, w_kr, w_krs, w_krs, w_g], axis=1).astype(BF16)
    bgt = b_gates.reshape(N_GATES, 1)
    wq4 = w_q_up.reshape(Q_LORA, N_MLA_HEADS, QK_NOPE_DIM + QK_ROPE_DIM)
    wq_pe = wq4[:, :, QK_NOPE_DIM:]
    wq = jnp.concatenate([wq4, wq_pe[:, :, swap]], axis=2).reshape(Q_LORA, N_MLA_HEADS * QK_SLAB).astype(BF16)
    wkv4 = w_kv_up.reshape(KV_LORA, N_MLA_HEADS, QK_NOPE_DIM + V_HEAD_DIM)
    wk = wkv4[:, :, :QK_NOPE_DIM].reshape(KV_LORA, N_MLA_HEADS * QK_NOPE_DIM).astype(BF16)
    wv = wkv4[:, :, QK_NOPE_DIM:].reshape(KV_LORA, D_MLA).astype(BF16)
    freqs = ROPE_THETA ** (-jnp.arange(0, QK_ROPE_DIM, 2, dtype=F32) / QK_ROPE_DIM)
    freq_c = freqs.reshape(half, 1)
    full = lambda arr: pl.BlockSpec(arr.shape, lambda i: (0,) * arr.ndim)
    rows = lambda w: pl.BlockSpec((tm, w), lambda i: (i, 0))
    consts = [attn_norm.reshape(1, d), wm, wr, bgt, q_norm.reshape(1, Q_LORA), wq,
              kv_norm.reshape(1, KV_LORA), wk, wv, freq_c]
    return pl.pallas_call(
        _in_proj_kernel,
        grid=(n // tm,),
        in_specs=[rows(d), pl.BlockSpec((1, 1, tm), lambda i: (i, 0, 0))] + [full(c) for c in consts],
        out_specs=[rows(OFF_G), pl.BlockSpec((N_GATES, tm), lambda i: (0, i)),
                   rows(N_MLA_HEADS * QK_SLAB), rows(N_MLA_HEADS * QK_SLAB), rows(D_MLA)],
        out_shape=[jax.ShapeDtypeStruct((n, OFF_G), F32), jax.ShapeDtypeStruct((N_GATES, n), F32),
                   jax.ShapeDtypeStruct((n, N_MLA_HEADS * QK_SLAB), BF16),
                   jax.ShapeDtypeStruct((n, N_MLA_HEADS * QK_SLAB), BF16),
                   jax.ShapeDtypeStruct((n, D_MLA), BF16)],
        compiler_params=_params("parallel"),
        name="in_proj",
    )(x2, pos2.reshape(n // tm, 1, tm), *consts)


def _mlstm_kernel(q_ref, k_ref, v_ref, o_ref, g_ref, cwq_ref, cwk_ref, cbq_ref, cbk_ref, nrm_ref,
                  y_ref,
                  qc_ref, kc_ref, va_ref, cst_ref, ent_ref, b_ref, e_ref, r_ref, ew_ref, mw_ref, bt_ref, mp_ref):
    L = MLSTM_CHUNK
    dh = MLSTM_HEAD_DIM
    nc = q_ref.shape[1] // L
    s_len = q_ref.shape[1]
    h = pl.program_id(1)

    for d in range(2):
        ig = g_ref[2 * d * N_MLSTM_HEADS + h, 0]
        fg = g_ref[(2 * d + 1) * N_MLSTM_HEADS + h, 0]
        b = _cumsum_lanes(_log_sigmoid(fg), reverse=(d == 1))
        btot = b[:, L - 1:L] if d == 0 else b[:, 0:1]
        r = ig - b
        w = btot + r
        mw = jnp.max(w, axis=-1, keepdims=True)
        b_ref[d] = b
        e_ref[d] = b + _cummax_lanes(r, reverse=(d == 1))
        r_ref[d] = r
        ew_ref[d] = jnp.exp(w - mw)
        mw_ref[d] = jnp.broadcast_to(mw, (nc, L))
        bt_ref[d] = jnp.broadcast_to(btot, (nc, L))

    row = lax.broadcasted_iota(I32, (L, dh), 0)

    def conv_silu(ref, cw_ref, cb_ref, c):
        start = pl.multiple_of(c * L, L)
        x = ref[0, pl.ds(start, L), :]
        prev_row = jnp.where(c > 0, ref[0, pl.ds(jnp.maximum(start - 1, 0), 1), :], 0.0)
        next_row = jnp.where(c < nc - 1, ref[0, pl.ds(jnp.minimum(start + L, s_len - 1), 1), :], 0.0)
        x_prev = jnp.where(row == 0, prev_row, pltpu.roll(x, 1, 0))
        x_next = jnp.where(row == L - 1, next_row, pltpu.roll(x, L - 1, 0))
        y = cw_ref[0:1, :] * x_prev + cw_ref[1:2, :] * x + cw_ref[2:3, :] * x_next + cb_ref[...]
        return y * jax.nn.sigmoid(y)

    ones_blk = jnp.ones((L, dh), BF16)

    def pass1(c, carry):
        start = pl.multiple_of(c * L, L)
        qc_ref[pl.ds(start, L), :] = conv_silu(q_ref, cwq_ref, cbq_ref, c).astype(BF16)
        kk = conv_silu(k_ref, cwk_ref, cbk_ref, c) * (dh ** -0.5)
        kc_ref[pl.ds(start, L), :] = kk.astype(BF16)
        va = jnp.concatenate([v_ref[0, pl.ds(start, L), :].astype(BF16), ones_blk], axis=1)
        va_ref[pl.ds(start, L), :] = va
        kt = kk.T
        for d in range(2):
            kw_t = (kt * ew_ref[d, pl.ds(c, 1), :]).astype(BF16)
            cst_ref[d, c] = _dot(kw_t, va)
        return carry

    lax.fori_loop(0, nc, pass1, 0, unroll=CHUNK_UNROLL)

    ent_ref[:, 0] = jnp.zeros((2, dh, 2 * dh), F32)

    def scan(i, carry):
        out = []
        for d in range(2):
            m = carry[d]
            c = i if d == 0 else nc - 1 - i
            mw = mw_ref[d, pl.ds(c, 1), :]
            bt = bt_ref[d, pl.ds(c, 1), :]
            m_new = jnp.maximum(bt + m, mw)
            a = jnp.exp(bt + m - m_new)[:, 0:1]
            cc = jnp.exp(mw - m_new)[:, 0:1]
            ent_ref[d, i + 1] = a * ent_ref[d, i] + cc * cst_ref[d, c]
            mp_ref[d, pl.ds(c, 1), :] = m
            out.append(m_new)
        return tuple(out)

    lax.fori_loop(0, nc, scan, (jnp.zeros((1, L), F32), jnp.zeros((1, L), F32)))

    ti = lax.broadcasted_iota(I32, (L, L), 0)
    si = lax.broadcasted_iota(I32, (L, L), 1)
    masks = (si <= ti, si >= ti)

    def pass3(c, carry):
        start = pl.multiple_of(c * L, L)
        q = qc_ref[pl.ds(start, L), :]
        k = kc_ref[pl.ds(start, L), :]
        va = va_ref[pl.ds(start, L), :]
        qk = _dot_nt(q, k)
        hsum = jnp.zeros((L, dh), F32)
        for d in range(2):
            bmat = jnp.broadcast_to(b_ref[d, pl.ds(c, 1), :], (L, L)).T
            emat = jnp.broadcast_to(e_ref[d, pl.ds(c, 1), :], (L, L)).T
            dmat = jnp.where(masks[d], bmat + r_ref[d, pl.ds(c, 1), :], -jnp.inf)
            inter = bmat + mp_ref[d, pl.ds(c, 1), :]
            m_t = jnp.maximum(inter, emat)
            sc = qk * jnp.exp(dmat - m_t)
            a = jnp.exp(inter - m_t)
            intra = _dot(sc.astype(BF16), va)
            cross = _dot(q, ent_ref[d, c if d == 0 else nc - 1 - c].astype(BF16))
            num = intra[:, :dh] + a * cross[:, :dh]
            den = intra[:, dh:] + a * cross[:, dh:]
            hsum = hsum + num / jnp.maximum(jnp.abs(den), jnp.exp(-m_t))
        hn = _rms(hsum, nrm_ref[...])
        y_ref[0, pl.ds(start, L), :] = hn * jax.nn.sigmoid(o_ref[0, pl.ds(start, L), :])
        return carry

    lax.fori_loop(0, nc, pass3, 0, unroll=CHUNK_UNROLL)


def _mlstm(um3, gt4, conv_w, conv_b, mlstm_norm):
    bsz, s, _ = um3.shape
    H, dh, L = N_MLSTM_HEADS, MLSTM_HEAD_DIM, MLSTM_CHUNK
    nc = s // L
    col = lambda off: pl.BlockSpec((1, s, dh), lambda b, h: (b, 0, off + h))
    vec = lambda rows, off: pl.BlockSpec((rows, dh), lambda b, h: (0, off + h))
    cb = conv_b.reshape(1, 2 * D_MLSTM)
    return pl.pallas_call(
        _mlstm_kernel,
        grid=(bsz, H),
        in_specs=[col(0), col(H), col(2 * H), col(3 * H),
                  pl.BlockSpec((N_GATES, 1, nc, L), lambda b, h: (0, b, 0, 0)),
                  vec(3, 0), vec(3, H), vec(1, 0), vec(1, H), vec(1, 0)],
        out_specs=pl.BlockSpec((1, s, dh), lambda b, h: (b, 0, h)),
        out_shape=jax.ShapeDtypeStruct((bsz, s, D_MLSTM), F32),
        scratch_shapes=[pltpu.VMEM((s, dh), BF16), pltpu.VMEM((s, dh), BF16), pltpu.VMEM((s, 2 * dh), BF16),
                        pltpu.VMEM((2, nc, dh, 2 * dh), F32), pltpu.VMEM((2, nc + 1, dh, 2 * dh), F32)]
                       + [pltpu.VMEM((2, nc, L), F32) for _ in range(7)],
        compiler_params=_params("parallel", "parallel"),
        name="mlstm",
    )(um3, um3, um3, um3, gt4, conv_w, conv_w, cb, cb, mlstm_norm.reshape(1, D_MLSTM))


def _attn_kernel(q_ref, k_ref, v_ref, o_ref):
    tq = q_ref.shape[1]
    q = q_ref[0]
    m = jnp.full((tq, 1), -jnp.inf, F32)
    l = jnp.zeros((tq, 1), F32)
    acc = jnp.zeros((tq, V_HEAD_DIM), F32)
    chunk = min(KV_CHUNK, k_ref.shape[1])
    for c in range(k_ref.shape[1] // chunk):
        keys = slice(c * chunk, (c + 1) * chunk)
        s = _dot_nt(q, k_ref[0, keys, :])
        m_new = jnp.maximum(m, jnp.max(s, axis=-1, keepdims=True))
        alpha = jnp.exp2(m - m_new)
        p = jnp.exp2(s - m_new)
        l = alpha * l + jnp.sum(p, axis=-1, keepdims=True)
        acc = alpha * acc + _dot(p.astype(BF16), v_ref[0, keys, :])
        m = m_new
    o_ref[0] = acc / l


def _attention(q3, k3, v3):
    bsz, s, _ = q3.shape
    tq = min(Q_TILE, s)
    return pl.pallas_call(
        _attn_kernel,
        grid=(bsz, N_MLA_HEADS, s // tq),
        in_specs=[pl.BlockSpec((1, tq, QK_SLAB), lambda b, h, i: (b, i, h)),
                  pl.BlockSpec((1, s, QK_SLAB), lambda b, h, i: (b, 0, h)),
                  pl.BlockSpec((1, s, V_HEAD_DIM), lambda b, h, i: (b, 0, h))],
        out_specs=pl.BlockSpec((1, tq, V_HEAD_DIM), lambda b, h, i: (b, i, h)),
        out_shape=jax.ShapeDtypeStruct((bsz, s, D_MLA), F32),
        compiler_params=_params("parallel", "parallel", "parallel"),
        name="attention",
    )(q3, k3, v3)


def _route_tile(logits, carry):
    tm = logits.shape[1]
    erow = lax.broadcasted_iota(I32, logits.shape, 0)
    work = logits
    vals, hots = [], []
    for k in range(TOP_K):
        mx = jnp.max(work, axis=0, keepdims=True)
        idx = jnp.min(jnp.where(work == mx, erow, N_EXPERTS), axis=0, keepdims=True)
        hot = erow == idx
        work = jnp.where(hot, -jnp.inf, work)
        vals.append(mx)
        hots.append(hot)
    exps = [jnp.exp(v - vals[0]) for v in vals]
    tot = exps[0] + exps[1] + exps[2] + exps[3]
    multi = (hots[0] | hots[1] | hots[2] | hots[3]).astype(BF16)
    ti = lax.broadcasted_iota(I32, (tm, tm), 0)
    tj = lax.broadcasted_iota(I32, (tm, tm), 1)
    local_rank = _dot(multi, (ti < tj).astype(BF16))
    multi_rows = jnp.concatenate([multi, jnp.zeros((LANES - N_EXPERTS, tm), BF16)], axis=0)
    count = _dot_nt(jnp.ones((SUBLANES, tm), BF16), multi_rows)[0:1]
    padded = jnp.ceil(count * (1.0 / SEG_ALIGN)) * SEG_ALIGN
    filled = _cumsum_lanes(padded, reverse=False)
    local_start = filled - padded
    start_col = jnp.broadcast_to(local_start, (SUBLANES, LANES)).T[:N_EXPERTS, 0:1]
    slot_all = local_rank + start_col
    krow = lax.broadcasted_iota(I32, (SUBLANES, tm), 0)
    packed = jnp.zeros((SUBLANES, tm), F32)
    for k in range(TOP_K):
        sk = jnp.sum(jnp.where(hots[k], slot_all, 0.0), axis=0, keepdims=True)
        packed = jnp.where(krow == k, exps[k] / tot, jnp.where(krow == TOP_K + k, sk, packed))
    srow = lax.broadcasted_iota(I32, (SUBLANES, LANES), 0)
    seg = jnp.where(srow == 0, padded, jnp.where(srow == 1, local_start, jnp.where(srow == 2, carry,
                                                                                  filled[:, LANES - 1:LANES])))
    return packed, seg, padded


def _out_route_kernel(ym_ref, ya_ref, x_ref, mn_ref, wom_ref, woa_ref, fn_ref, wr_ref, br_ref,
                      h_ref, xn_ref, gate_ref, slot_ref, slot_t_ref, seg_ref, size_ref,
                      carry_ref):
    i = pl.program_id(0)
    tm = ROW_TILE

    @pl.when(i == 0)
    def _():
        carry_ref[...] = jnp.zeros_like(carry_ref)

    ya = _rms(ya_ref[...], mn_ref[...])
    h1 = x_ref[...] + _dot(ym_ref[...].astype(BF16), wom_ref[...]) + _dot(ya.astype(BF16), woa_ref[...])
    h_ref[...] = h1
    xn = _rms(h1, fn_ref[...])
    xn_hi = xn.astype(BF16)
    xn_ref[...] = xn_hi
    xn_lo = (xn - xn_hi.astype(F32)).astype(BF16)
    both = _dot_nt(wr_ref[...], xn_hi)
    logits = (both[:N_EXPERTS] + both[N_EXPERTS:] + _dot_nt(wr_ref[:N_EXPERTS, :], xn_lo)) + br_ref[...]
    carry = carry_ref[...]
    for s in range(x_ref.shape[0] // tm):
        packed, seg, padded = _route_tile(logits[:, s * tm:(s + 1) * tm], carry)
        krow = lax.broadcasted_iota(I32, packed.shape, 0)
        slot_t_ref[s] = jnp.where(krow < TOP_K, pltpu.roll(packed, TOP_K, 0), 0.0).astype(I32)
        cols = packed.T
        gate_ref[s * tm:(s + 1) * tm, :] = cols[:, :TOP_K]
        slot_ref[s * tm:(s + 1) * tm, :] = cols[:, TOP_K:2 * TOP_K].astype(I32)
        seg_ref[s] = seg.astype(I32)
        carry = carry + padded
    carry_ref[...] = carry
    size_ref[...] = carry.astype(I32)


def _out_route(ym2, ya2, x2, mla_norm, w_out, ffn_norm, w_router, b_router):
    n, d = x2.shape
    sub = min(ROUTE_TILES_PER_STEP, n // ROW_TILE)
    tm = ROW_TILE * sub
    nt = n // ROW_TILE
    wom = w_out[:D_MLSTM].astype(BF16)
    woa = w_out[D_MLSTM:].astype(BF16)
    wr_hi = w_router.T.astype(BF16)
    wr_lo = (w_router.T - wr_hi.astype(F32)).astype(BF16)
    wr = jnp.concatenate([wr_hi, wr_lo], axis=0)
    br = b_router.reshape(N_EXPERTS, 1)
    full = lambda arr: pl.BlockSpec(arr.shape, lambda i: (0,) * arr.ndim)
    rows = lambda w: pl.BlockSpec((tm, w), lambda i: (i, 0))
    consts = [mla_norm.reshape(1, D_MLA), wom, woa, ffn_norm.reshape(1, d), wr, br]
    return pl.pallas_call(
        _out_route_kernel,
        grid=(nt // sub,),
        in_specs=[rows(D_MLSTM), rows(D_MLA), rows(d)] + [full(c) for c in consts],
        out_specs=[rows(d), rows(d), rows(TOP_K), rows(TOP_K),
                   pl.BlockSpec((sub, SUBLANES, ROW_TILE), lambda i: (i, 0, 0)),
                   pl.BlockSpec((sub, SUBLANES, LANES), lambda i: (i, 0, 0)),
                   pl.BlockSpec((1, LANES), lambda i: (0, 0))],
        out_shape=[jax.ShapeDtypeStruct((n, d), F32), jax.ShapeDtypeStruct((n, d), BF16),
                   jax.ShapeDtypeStruct((n, TOP_K), F32), jax.ShapeDtypeStruct((n, TOP_K), I32),
                   jax.ShapeDtypeStruct((nt, SUBLANES, ROW_TILE), I32), jax.ShapeDtypeStruct((nt, SUBLANES, LANES), I32),
                   jax.ShapeDtypeStruct((1, LANES), I32)],
        scratch_shapes=[pltpu.VMEM((1, LANES), F32)],
        compiler_params=_params("arbitrary"),
        name="out_route",
    )(ym2, ya2, x2, *consts)


def _segment_copies(seg_ref, starts_ref, local_ref, global_ref, sem, to_global, wait):
    def copy(src, dst, rows):
        loc = local_ref.at[pl.ds(pl.multiple_of(src, SEG_ALIGN), rows)]
        glo = global_ref.at[pl.ds(pl.multiple_of(dst, SEG_ALIGN), rows)]
        return pltpu.make_async_copy(loc, glo, sem) if to_global else pltpu.make_async_copy(glo, loc, sem)

    if wait:
        copy(0, 0, pl.multiple_of(seg_ref[0, 3, 0], SEG_ALIGN)).wait()
        return

    def per_expert(e, carry):
        size = seg_ref[0, 0, e]
        src = seg_ref[0, 1, e]
        dst = starts_ref[e] + seg_ref[0, 2, e]
        off = 0
        rows = ROW_TILE
        while rows >= SEG_ALIGN:
            @pl.when((size & rows) != 0)
            def _(off=off, rows=rows):
                copy(src + off, dst + off, rows).start()

            off = off + (size & rows)
            rows //= 2
        return carry

    lax.fori_loop(0, N_EXPERTS, per_expert, 0)


def _dispatch_kernel(starts_ref, seg_ref, segp_ref, slot_t_ref, xn_ref, xs_ref, sort_ref, zero_ref, sem, zsem):
    i = pl.program_id(0)
    tm = xn_ref.shape[0]
    tmx = zero_ref.shape[0]
    cap = sort_ref.shape[1]

    @pl.when(i == 0)
    def _():
        zero_ref[...] = jnp.zeros_like(zero_ref)
        n_tail = (xs_ref.shape[0] - starts_ref[N_EXPERTS]) // tmx

        def clear_tile(row, wait):
            cp = pltpu.make_async_copy(zero_ref, xs_ref.at[pl.ds(pl.multiple_of(row, tmx), tmx)], zsem)
            if wait:
                cp.wait()
            else:
                cp.start()

        for wait in (False, True):
            def clear_group(e, carry, wait=wait):
                hi = starts_ref[e + 1]

                @pl.when(hi > starts_ref[e])
                def _():
                    clear_tile(hi - tmx, wait)

                return carry

            def clear_tail(t, carry, wait=wait):
                clear_tile(starts_ref[N_EXPERTS] + t * tmx, wait)
                return carry

            lax.fori_loop(0, N_EXPERTS, clear_group, 0)
            lax.fori_loop(0, n_tail, clear_tail, 0)

    slot = i % 2
    slots16 = slot_t_ref[0].astype(I16)
    for r0 in range(0, cap, SORT_CHUNK):
        pos = (lax.broadcasted_iota(I32, (SORT_CHUNK, tm), 0) + r0).astype(I16)
        hit = pos == slots16[0:1, :]
        for k in range(1, TOP_K):
            hit = hit | (pos == slots16[k:k + 1, :])
        sort_ref[slot, r0:r0 + SORT_CHUNK, :] = _pack_halves(_dot(hit.astype(BF16), xn_ref[...]))
    _segment_copies(seg_ref, starts_ref, sort_ref.at[slot], xs_ref, sem.at[slot], True, False)

    @pl.when(i >= 1)
    def _():
        _segment_copies(segp_ref, starts_ref, sort_ref.at[1 - slot], xs_ref, sem.at[1 - slot], True, True)

    @pl.when(i == pl.num_programs(0) - 1)
    def _():
        _segment_copies(seg_ref, starts_ref, sort_ref.at[slot], xs_ref, sem.at[slot], True, True)


def _dispatch(starts, seg, slot_t, xn, n_rows):
    n, d = xn.shape
    tm = ROW_TILE
    nt = n // tm
    smem = lambda f: pl.BlockSpec((1, SUBLANES, LANES), f, memory_space=pltpu.SMEM)
    any_spec = pl.BlockSpec(memory_space=pl.ANY)
    return pl.pallas_call(
        _dispatch_kernel,
        grid_spec=pltpu.PrefetchScalarGridSpec(
            num_scalar_prefetch=1,
            grid=(nt,),
            in_specs=[smem(lambda i, *_: (i, 0, 0)), smem(lambda i, *_: (jnp.maximum(i - 1, 0), 0, 0)),
                      pl.BlockSpec((1, SUBLANES, tm), lambda i, *_: (i, 0, 0)),
                      pl.BlockSpec((tm, d), lambda i, *_: (i, 0))],
            out_specs=any_spec,
            scratch_shapes=[pltpu.VMEM((2, SORT_ROWS, d // 2), U32), pltpu.VMEM((EXPERT_TILE, d // 2), U32),
                            pltpu.SemaphoreType.DMA((2,)), pltpu.SemaphoreType.DMA],
        ),
        out_shape=jax.ShapeDtypeStruct((n_rows, d // 2), U32),
        compiler_params=_params("arbitrary"),
        name="dispatch",
    )(starts, seg, seg, slot_t, xn)


def _experts_kernel(te_ref, tb_ref, tv_ref, xs_ref, wgu_ref, bgu_ref, wd_ref, bd_ref, out_ref,
                    wgu_bf, wd_bf):
    i = pl.program_id(0)
    de = wd_ref.shape[1]
    prev = te_ref[jnp.maximum(i - 1, 0)]

    @pl.when(jnp.logical_or(i == 0, te_ref[i] != prev))
    def _():
        wgu_bf[...] = wgu_ref[0].astype(BF16)
        wd_bf[...] = wd_ref[0].astype(BF16)

    @pl.when(tv_ref[i] == 1)
    def _():
        x = _unpack_halves(xs_ref[...])
        acc = jnp.zeros((xs_ref.shape[0], bd_ref.shape[2]), F32)
        for j in range(de // FF_CHUNK):
            cols = slice(j * FF_CHUNK, (j + 1) * FF_CHUNK)
            up_cols = slice(de + j * FF_CHUNK, de + (j + 1) * FF_CHUNK)
            g = _dot(x, wgu_bf[:, cols]) + bgu_ref[0, :, cols]
            u = _dot(x, wgu_bf[:, up_cols]) + bgu_ref[0, :, up_cols]
            g = jnp.minimum(g, SWIGLU_LIMIT)
            u = jnp.clip(u, -SWIGLU_LIMIT, SWIGLU_LIMIT)
            hm = (u + 1.0) * (g * jax.nn.sigmoid(g * SWIGLU_ALPHA))
            acc = acc + _dot(hm.astype(BF16), wd_bf[cols, :])
        out_ref[...] = _pack_halves((acc + bd_ref[0]).astype(BF16).astype(F32))

    @pl.when(tv_ref[i] == 0)
    def _():
        out_ref[...] = jnp.zeros_like(out_ref)


def _experts(tile_e, tile_b, tile_v, xs, w_gate_up, b_gate_up, w_down, b_down):
    n_rows, dw = xs.shape
    tmx = EXPERT_TILE
    ne, d, de2 = w_gate_up.shape
    de = de2 // 2
    return pl.pallas_call(
        _experts_kernel,
        grid_spec=pltpu.PrefetchScalarGridSpec(
            num_scalar_prefetch=3,
            grid=(n_rows // tmx,),
            in_specs=[pl.BlockSpec((tmx, dw), lambda i, te, tb, tv: (tb[i], 0)),
                      pl.BlockSpec((1, d, de2), lambda i, te, tb, tv: (te[i], 0, 0)),
                      pl.BlockSpec((1, 1, de2), lambda i, te, tb, tv: (te[i], 0, 0)),
                      pl.BlockSpec((1, de, d), lambda i, te, tb, tv: (te[i], 0, 0)),
                      pl.BlockSpec((1, 1, d), lambda i, te, tb, tv: (te[i], 0, 0))],
            out_specs=pl.BlockSpec((tmx, dw), lambda i, te, tb, tv: (i, 0)),
            scratch_shapes=[pltpu.VMEM((d, de2), BF16), pltpu.VMEM((de, d), BF16)],
        ),
        out_shape=jax.ShapeDtypeStruct((n_rows, dw), U32),
        compiler_params=_params("arbitrary"),
        name="experts",
    )(tile_e, tile_b, tile_v, xs, w_gate_up, b_gate_up.reshape(ne, 1, de2), w_down, b_down.reshape(ne, 1, d))


def _combine_kernel(starts_ref, seg_ref, segn_ref, h_ref, gate_ref, slot_ref, p_ref, ys_ref, pn_ref, wg_ref,
                    wp_ref, fn_ref, out_ref, ybuf, sem, *, final):
    i = pl.program_id(0)
    nt = pl.num_programs(0)
    tm = h_ref.shape[0]
    cap = ybuf.shape[1]

    @pl.when(i == 0)
    def _():
        ybuf[...] = jnp.zeros_like(ybuf)
        _segment_copies(seg_ref, starts_ref, ybuf.at[0], ys_ref, sem.at[0], False, False)

    slot = i % 2

    @pl.when(i + 1 < nt)
    def _():
        _segment_copies(segn_ref, starts_ref, ybuf.at[1 - slot], ys_ref, sem.at[1 - slot], False, False)

    _segment_copies(seg_ref, starts_ref, ybuf.at[slot], ys_ref, sem.at[slot], False, True)

    gate = gate_ref[...].astype(BF16)
    slot16 = slot_ref[...].astype(I16)
    h2 = h_ref[...]
    for c0 in range(0, cap, SORT_CHUNK):
        pos = (lax.broadcasted_iota(I32, (tm, SORT_CHUNK), 1) + c0).astype(I16)
        weights = jnp.zeros((tm, SORT_CHUNK), BF16)
        for k in range(TOP_K):
            weights = jnp.where(pos == slot16[:, k:k + 1], gate[:, k:k + 1], weights)
        h2 = h2 + _dot(weights, _unpack_halves(ybuf[slot, c0:c0 + SORT_CHUNK, :]))
    hn = _rms(h2, pn_ref[...]).astype(BF16)
    sg = jax.nn.sigmoid(_dot(hn, wg_ref[...]))
    h3 = h2 + sg * _dot(p_ref[...].astype(BF16), wp_ref[...])
    out_ref[...] = _rms(h3, fn_ref[...]) if final else h3


def _combine(starts, seg, h1, gate, slot, p2, ys, ple_norm, w_ple_gate, w_ple_proj, final_norm, final):
    n, d = h1.shape
    tm = ROW_TILE
    nt = n // tm
    wg = w_ple_gate.astype(BF16)
    wp = w_ple_proj.astype(BF16)
    smem = lambda f: pl.BlockSpec((1, SUBLANES, LANES), f, memory_space=pltpu.SMEM)
    full = lambda arr: pl.BlockSpec(arr.shape, lambda i, *_: (0,) * arr.ndim)
    rows = lambda w: pl.BlockSpec((tm, w), lambda i, *_: (i, 0))
    consts = [ple_norm.reshape(1, d), wg, wp, final_norm.reshape(1, d)]
    return pl.pallas_call(
        functools.partial(_combine_kernel, final=final),
        grid_spec=pltpu.PrefetchScalarGridSpec(
            num_scalar_prefetch=1,
            grid=(nt,),
            in_specs=[smem(lambda i, *_: (i, 0, 0)), smem(lambda i, *_: (jnp.minimum(i + 1, nt - 1), 0, 0)),
                      rows(d), rows(TOP_K), rows(TOP_K), rows(p2.shape[1]), pl.BlockSpec(memory_space=pl.ANY)]
                     + [full(c) for c in consts],
            out_specs=rows(d),
            scratch_shapes=[pltpu.VMEM((2, SORT_ROWS, d // 2), U32), pltpu.SemaphoreType.DMA((2,))],
        ),
        out_shape=jax.ShapeDtypeStruct((n, d), F32),
        compiler_params=_params("arbitrary"),
        name="combine",
    )(starts, seg, seg, h1, gate, slot, p2, ys, *consts)


def _route_tables(sizes, n_tiles):
    tmx = EXPERT_TILE
    tile_end = jnp.cumsum((sizes + tmx - 1) // tmx)
    starts = jnp.concatenate([jnp.zeros((1,), I32), tile_end * tmx]).astype(I32)
    n_valid = tile_end[-1]
    t = jnp.arange(n_tiles, dtype=I32)
    tb = jnp.minimum(t, n_valid - 1).astype(I32)
    te = jnp.sum(tile_end[None, :] <= tb[:, None], axis=1).astype(I32)
    tv = (t < n_valid).astype(I32)
    return starts, te, tb, tv


def kernel(x, p, positions, attn_norm, w_in, b_gates, conv_w, conv_b, mlstm_norm, q_norm, w_q_up, kv_norm, w_kv_up, mla_norm, w_out, ffn_norm, w_router, b_router, w_gate_up, b_gate_up, w_down, b_down, ple_norm, w_ple_gate, w_ple_proj, final_norm):
    bsz, s, d = x.shape
    n = bsz * s
    depth = p.shape[0]
    nc = s // MLSTM_CHUNK
    max_rows = n * TOP_K + (n // ROW_TILE) * N_EXPERTS * (SEG_ALIGN - 1) + N_EXPERTS * (EXPERT_TILE - 1)
    n_tiles = max_rows // EXPERT_TILE
    pos2 = positions.reshape(n, 1)
    h = x.reshape(n, d)
    for i in range(depth):
        um, gt, q, k, v = _in_proj(h, pos2, attn_norm[i], w_in[i], b_gates[i], q_norm[i], w_q_up[i],
                                   kv_norm[i], w_kv_up[i])
        ym = _mlstm(um.reshape(bsz, s, -1), gt.reshape(N_GATES, bsz, nc, MLSTM_CHUNK), conv_w[i], conv_b[i],
                    mlstm_norm[i])
        ya = _attention(q.reshape(bsz, s, -1), k.reshape(bsz, s, -1), v.reshape(bsz, s, -1))
        h1, xn, gate, slot, slot_t, seg, sizes = _out_route(ym.reshape(n, -1), ya.reshape(n, -1), h, mla_norm[i],
                                                            w_out[i], ffn_norm[i], w_router[i], b_router[i])
        starts, te, tb, tv = _route_tables(sizes[0, :N_EXPERTS], n_tiles)
        xs = _dispatch(starts, seg, slot_t, xn, n_tiles * EXPERT_TILE)
        ys = _experts(te, tb, tv, xs, w_gate_up[i], b_gate_up[i], w_down[i], b_down[i])
        h = _combine(starts, seg, h1, gate, slot, p[i].reshape(n, -1), ys, ple_norm[i], w_ple_gate[i],
                     w_ple_proj[i], final_norm, final=(i == depth - 1))
    return h.reshape(bsz, s, d)
```

```python
import functools

import jax
import jax.numpy as jnp
from jax import lax
from jax.experimental import pallas as pl
from jax.experimental.pallas import tpu as pltpu

F32 = jnp.float32
BF16 = jnp.bfloat16
I32 = jnp.int32
U32 = jnp.uint32
I16 = jnp.int16

N_MLSTM_HEADS = 4
MLSTM_HEAD_DIM = 128
D_MLSTM = N_MLSTM_HEADS * MLSTM_HEAD_DIM
MLSTM_CHUNK = 128
N_MLA_HEADS = 4
QK_NOPE_DIM = 128
QK_ROPE_DIM = 64
V_HEAD_DIM = 128
D_MLA = N_MLA_HEADS * V_HEAD_DIM
Q_LORA = 256
KV_LORA = 128
ROPE_THETA = 10000.0
N_EXPERTS = 32
TOP_K = 4
SWIGLU_LIMIT = 7.0
SWIGLU_ALPHA = 1.702
EPS = 1e-6
N_GATES = 4 * N_MLSTM_HEADS
OFF_G = 4 * D_MLSTM
OFF_CQ = OFF_G + N_GATES
OFF_CKV = OFF_CQ + Q_LORA
OFF_KR = OFF_CKV + KV_LORA

LANES = 128
SUBLANES = 8
QK_SLAB = 2 * LANES
VMEM_LIMIT_BYTES = 56 * 1024 * 1024
LOG2_E = 1.4426950408889634

IN_TILE = 1024
ROW_TILE = 512
Q_TILE = 2048
KV_CHUNK = 1024
EXPERT_TILE = 1024
FF_CHUNK = 512
CHUNK_UNROLL = 16
SEG_ALIGN = SUBLANES
ROUTE_TILES_PER_STEP = 2
SORT_CHUNK = 256
SORT_ROWS = ROW_TILE * TOP_K + N_EXPERTS * SEG_ALIGN


def _dot(a, b):
    return jnp.dot(a, b, preferred_element_type=F32)


def _dot_nt(a, b):
    return lax.dot_general(a, b, (((1,), (1,)), ((), ())), preferred_element_type=F32)


def _rms(x, g):
    return x * lax.rsqrt(jnp.mean(x * x, axis=-1, keepdims=True) + EPS) * g


def _log_sigmoid(x):
    return jnp.minimum(x, 0.0) - jnp.log(1.0 + jnp.exp(-jnp.abs(x)))


def _scan_lanes(x, reverse, op, identity):
    n = x.shape[-1]
    lane = lax.broadcasted_iota(I32, x.shape, x.ndim - 1)
    sh = 1
    while sh < n:
        if reverse:
            x = op(x, jnp.where(lane < n - sh, pltpu.roll(x, n - sh, x.ndim - 1), identity))
        else:
            x = op(x, jnp.where(lane >= sh, pltpu.roll(x, sh, x.ndim - 1), identity))
        sh *= 2
    return x


def _cumsum_lanes(x, reverse):
    return _scan_lanes(x, reverse, jnp.add, 0.0)


def _cummax_lanes(x, reverse):
    return _scan_lanes(x, reverse, jnp.maximum, -jnp.inf)


def _pack_halves(x):
    w = x.shape[1] // 2
    lo = lax.shift_right_logical(lax.bitcast_convert_type(x[:, :w], U32), jnp.uint32(16))
    hi = lax.bitcast_convert_type(x[:, w:], U32)
    return lo | hi


def _unpack_halves(words):
    lo = lax.bitcast_convert_type(lax.shift_left(words, jnp.uint32(16)), F32)
    hi = lax.bitcast_convert_type(words & jnp.uint32(0xFFFF0000), F32)
    return jnp.concatenate([lo, hi], axis=1).astype(BF16)


def _params(*sem):
    return pltpu.CompilerParams(dimension_semantics=sem, vmem_limit_bytes=VMEM_LIMIT_BYTES)


def _in_proj_kernel(x_ref, pos_ref, an_ref, wm_ref, wr_ref, bgt_ref, qn_ref, wq_ref, kvn_ref,
                    wk_ref, wv_ref, freq_ref,
                    um_ref, gt_ref, q_ref, k_ref, v_ref):
    a = _rms(x_ref[...], an_ref[...]).astype(BF16)
    um_ref[...] = _dot(a, wm_ref[...])
    rest = _dot(a, wr_ref[...])
    cq = rest[:, :Q_LORA]
    ckv = rest[:, Q_LORA:Q_LORA + KV_LORA]
    kr2 = rest[:, Q_LORA + KV_LORA:Q_LORA + KV_LORA + LANES]
    krs2 = rest[:, Q_LORA + KV_LORA + LANES:Q_LORA + KV_LORA + 2 * LANES]
    gt_ref[...] = rest[:, Q_LORA + KV_LORA + 2 * LANES:].T[:N_GATES, :] + bgt_ref[...]
    ang = freq_ref[...] * pos_ref[0].astype(F32)
    cos_t = jnp.cos(ang)
    sin_t = jnp.sin(ang)
    cos_a = jnp.concatenate([cos_t] * (LANES // cos_t.shape[0]), axis=0).T
    sin_a = jnp.concatenate([-sin_t, sin_t] * (LANES // (2 * sin_t.shape[0])), axis=0).T
    scale = (QK_NOPE_DIM + QK_ROPE_DIM) ** -0.5 * LOG2_E
    lane = lax.broadcasted_iota(I32, cos_a.shape, 1)
    rope_mul = jnp.where(lane < QK_ROPE_DIM, cos_a, sin_a) * scale
    qf = _dot(_rms(cq, qn_ref[...]).astype(BF16), wq_ref[...])
    ckvn = _rms(ckv, kvn_ref[...]).astype(BF16)
    kn = _dot(ckvn, wk_ref[...])
    v_ref[...] = _dot(ckvn, wv_ref[...]).astype(BF16)
    k_rope = (kr2 * cos_a + krs2 * sin_a).astype(BF16)
    for h in range(N_MLA_HEADS):
        o = h * QK_SLAB
        q_ref[:, o:o + LANES] = (qf[:, o:o + LANES] * scale).astype(BF16)
        q_ref[:, o + LANES:o + QK_SLAB] = (qf[:, o + LANES:o + QK_SLAB] * rope_mul).astype(BF16)
        k_ref[:, o:o + LANES] = kn[:, h * LANES:(h + 1) * LANES].astype(BF16)
        k_ref[:, o + LANES:o + QK_SLAB] = k_rope


def _in_proj(x2, pos2, attn_norm, w_in, b_gates, q_norm, w_q_up, kv_norm, w_kv_up):
    n, d = x2.shape
    tm = min(IN_TILE, n)
    half = QK_ROPE_DIM // 2
    swap = jnp.concatenate([jnp.arange(half, QK_ROPE_DIM), jnp.arange(0, half)])
    w_kr = w_in[:, OFF_KR:OFF_KR + QK_ROPE_DIM]
    w_krs = w_kr[:, swap]
    wm = w_in[:, :OFF_G].astype(BF16)
    w_g = jnp.pad(w_in[:, OFF_G:OFF_CQ], ((0, 0), (0, LANES - N_GATES)))
    wr = jnp.concatenate([w_in[:, OFF_CQ:OFF_KR], w_kr, w_kr, w_krs, w_krs, w_g], axis=1).astype(BF16)
    bgt = b_gates.reshape(N_GATES, 1)
    wq4 = w_q_up.reshape(Q_LORA, N_MLA_HEADS, QK_NOPE_DIM + QK_ROPE_DIM)
    wq_pe = wq4[:, :, QK_NOPE_DIM:]
    wq = jnp.concatenate([wq4, wq_pe[:, :, swap]], axis=2).reshape(Q_LORA, N_MLA_HEADS * QK_SLAB).astype(BF16)
    wkv4 = w_kv_up.reshape(KV_LORA, N_MLA_HEADS, QK_NOPE_DIM + V_HEAD_DIM)
    wk = wkv4[:, :, :QK_NOPE_DIM].reshape(KV_LORA, N_MLA_HEADS * QK_NOPE_DIM).astype(BF16)
    wv = wkv4[:, :, QK_NOPE_DIM:].reshape(KV_LORA, D_MLA).astype(BF16)
    freqs = ROPE_THETA ** (-jnp.arange(0, QK_ROPE_DIM, 2, dtype=F32) / QK_ROPE_DIM)
    freq_c = freqs.reshape(half, 1)
    full = lambda arr: pl.BlockSpec(arr.shape, lambda i: (0,) * arr.ndim)
    rows = lambda w: pl.BlockSpec((tm, w), lambda i: (i, 0))
    consts = [attn_norm.reshape(1, d), wm, wr, bgt, q_norm.reshape(1, Q_LORA), wq,
              kv_norm.reshape(1, KV_LORA), wk, wv, freq_c]
    return pl.pallas_call(
        _in_proj_kernel,
        grid=(n // tm,),
        in_specs=[rows(d), pl.BlockSpec((1, 1, tm), lambda i: (i, 0, 0))] + [full(c) for c in consts],
        out_specs=[rows(OFF_G), pl.BlockSpec((N_GATES, tm), lambda i: (0, i)),
                   rows(N_MLA_HEADS * QK_SLAB), rows(N_MLA_HEADS * QK_SLAB), rows(D_MLA)],
        out_shape=[jax.ShapeDtypeStruct((n, OFF_G), F32), jax.ShapeDtypeStruct((N_GATES, n), F32),
                   jax.ShapeDtypeStruct((n, N_MLA_HEADS * QK_SLAB), BF16),
                   jax.ShapeDtypeStruct((n, N_MLA_HEADS * QK_SLAB), BF16),
                   jax.ShapeDtypeStruct((n, D_MLA), BF16)],
        compiler_params=_params("parallel"),
        name="in_proj",
    )(x2, pos2.reshape(n // tm, 1, tm), *consts)


def _mlstm_kernel(q_ref, k_ref, v_ref, o_ref, g_ref, cwq_ref, cwk_ref, cbq_ref, cbk_ref, nrm_ref,
                  y_ref,
                  qc_ref, kc_ref, va_ref, cst_ref, ent_ref, b_ref, e_ref, r_ref, ew_ref, mw_ref, bt_ref, mp_ref):
    L = MLSTM_CHUNK
    dh = MLSTM_HEAD_DIM
    nc = q_ref.shape[1] // L
    s_len = q_ref.shape[1]
    h = pl.program_id(1)

    for d in range(2):
        ig = g_ref[2 * d * N_MLSTM_HEADS + h, 0]
        fg = g_ref[(2 * d + 1) * N_MLSTM_HEADS + h, 0]
        b = _cumsum_lanes(_log_sigmoid(fg), reverse=(d == 1))
        btot = b[:, L - 1:L] if d == 0 else b[:, 0:1]
        r = ig - b
        w = btot + r
        mw = jnp.max(w, axis=-1, keepdims=True)
        b_ref[d] = b
        e_ref[d] = b + _cummax_lanes(r, reverse=(d == 1))
        r_ref[d] = r
        ew_ref[d] = jnp.exp(w - mw)
        mw_ref[d] = jnp.broadcast_to(mw, (nc, L))
        bt_ref[d] = jnp.broadcast_to(btot, (nc, L))

    row = lax.broadcasted_iota(I32, (L, dh), 0)

    def conv_silu(ref, cw_ref, cb_ref, c):
        start = pl.multiple_of(c * L, L)
        x = ref[0, pl.ds(start, L), :]
        prev_row = jnp.where(c > 0, ref[0, pl.ds(jnp.maximum(start - 1, 0), 1), :], 0.0)
        next_row = jnp.where(c < nc - 1, ref[0, pl.ds(jnp.minimum(start + L, s_len - 1), 1), :], 0.0)
        x_prev = jnp.where(row == 0, prev_row, pltpu.roll(x, 1, 0))
        x_next = jnp.where(row == L - 1, next_row, pltpu.roll(x, L - 1, 0))
        y = cw_ref[0:1, :] * x_prev + cw_ref[1:2, :] * x + cw_ref[2:3, :] * x_next + cb_ref[...]
        return y * jax.nn.sigmoid(y)

    ones_blk = jnp.ones((L, dh), BF16)

    def pass1(c, carry):
        start = pl.multiple_of(c * L, L)
        qc_ref[pl.ds(start, L), :] = conv_silu(q_ref, cwq_ref, cbq_ref, c).astype(BF16)
        kk = conv_silu(k_ref, cwk_ref, cbk_ref, c) * (dh ** -0.5)
        kc_ref[pl.ds(start, L), :] = kk.astype(BF16)
        va = jnp.concatenate([v_ref[0, pl.ds(start, L), :].astype(BF16), ones_blk], axis=1)
        va_ref[pl.ds(start, L), :] = va
        kt = kk.T
        for d in range(2):
            kw_t = (kt * ew_ref[d, pl.ds(c, 1), :]).astype(BF16)
            cst_ref[d, c] = _dot(kw_t, va)
        return carry

    lax.fori_loop(0, nc, pass1, 0, unroll=CHUNK_UNROLL)

    ent_ref[:, 0] = jnp.zeros((2, dh, 2 * dh), F32)

    def scan(i, carry):
        out = []
        for d in range(2):
            m = carry[d]
            c = i if d == 0 else nc - 1 - i
            mw = mw_ref[d, pl.ds(c, 1), :]
            bt = bt_ref[d, pl.ds(c, 1), :]
            m_new = jnp.maximum(bt + m, mw)
            a = jnp.exp(bt + m - m_new)[:, 0:1]
            cc = jnp.exp(mw - m_new)[:, 0:1]
            ent_ref[d, i + 1] = a * ent_ref[d, i] + cc * cst_ref[d, c]
            mp_ref[d, pl.ds(c, 1), :] = m
            out.append(m_new)
        return tuple(out)

    lax.fori_loop(0, nc, scan, (jnp.zeros((1, L), F32), jnp.zeros((1, L), F32)))

    ti = lax.broadcasted_iota(I32, (L, L), 0)
    si = lax.broadcasted_iota(I32, (L, L), 1)
    masks = (si <= ti, si >= ti)

    def pass3(c, carry):
        start = pl.multiple_of(c * L, L)
        q = qc_ref[pl.ds(start, L), :]
        k = kc_ref[pl.ds(start, L), :]
        va = va_ref[pl.ds(start, L), :]
        qk = _dot_nt(q, k)
        hsum = jnp.zeros((L, dh), F32)
        for d in range(2):
            bmat = jnp.broadcast_to(b_ref[d, pl.ds(c, 1), :], (L, L)).T
            emat = jnp.broadcast_to(e_ref[d, pl.ds(c, 1), :], (L, L)).T
            dmat = jnp.where(masks[d], bmat + r_ref[d, pl.ds(c, 1), :], -jnp.inf)
            inter = bmat + mp_ref[d, pl.ds(c, 1), :]
            m_t = jnp.maximum(inter, emat)
            sc = qk * jnp.exp(dmat - m_t)
            a = jnp.exp(inter - m_t)
            intra = _dot(sc.astype(BF16), va)
            cross = _dot(q, ent_ref[d, c if d == 0 else nc - 1 - c].astype(BF16))
            num = intra[:, :dh] + a * cross[:, :dh]
            den = intra[:, dh:] + a * cross[:, dh:]
            hsum = hsum + num / jnp.maximum(jnp.abs(den), jnp.exp(-m_t))
        hn = _rms(hsum, nrm_ref[...])
        y_ref[0, pl.ds(start, L), :] = hn * jax.nn.sigmoid(o_ref[0, pl.ds(start, L), :])
        return carry

    lax.fori_loop(0, nc, pass3, 0, unroll=CHUNK_UNROLL)


def _mlstm(um3, gt4, conv_w, conv_b, mlstm_norm):
    bsz, s, _ = um3.shape
    H, dh, L = N_MLSTM_HEADS, MLSTM_HEAD_DIM, MLSTM_CHUNK
    nc = s // L
    col = lambda off: pl.BlockSpec((1, s, dh), lambda b, h: (b, 0, off + h))
    vec = lambda rows, off: pl.BlockSpec((rows, dh), lambda b, h: (0, off + h))
    cb = conv_b.reshape(1, 2 * D_MLSTM)
    return pl.pallas_call(
        _mlstm_kernel,
        grid=(bsz, H),
        in_specs=[col(0), col(H), col(2 * H), col(3 * H),
                  pl.BlockSpec((N_GATES, 1, nc, L), lambda b, h: (0, b, 0, 0)),
                  vec(3, 0), vec(3, H), vec(1, 0), vec(1, H), vec(1, 0)],
        out_specs=pl.BlockSpec((1, s, dh), lambda b, h: (b, 0, h)),
        out_shape=jax.ShapeDtypeStruct((bsz, s, D_MLSTM), F32),
        scratch_shapes=[pltpu.VMEM((s, dh), BF16), pltpu.VMEM((s, dh), BF16), pltpu.VMEM((s, 2 * dh), BF16),
                        pltpu.VMEM((2, nc, dh, 2 * dh), F32), pltpu.VMEM((2, nc + 1, dh, 2 * dh), F32)]
                       + [pltpu.VMEM((2, nc, L), F32) for _ in range(7)],
        compiler_params=_params("parallel", "parallel"),
        name="mlstm",
    )(um3, um3, um3, um3, gt4, conv_w, conv_w, cb, cb, mlstm_norm.reshape(1, D_MLSTM))


def _attn_kernel(q_ref, k_ref, v_ref, o_ref):
    tq = q_ref.shape[1]
    q = q_ref[0]
    m = jnp.full((tq, 1), -jnp.inf, F32)
    l = jnp.zeros((tq, 1), F32)
    acc = jnp.zeros((tq, V_HEAD_DIM), F32)
    chunk = min(KV_CHUNK, k_ref.shape[1])
    for c in range(k_ref.shape[1] // chunk):
        keys = slice(c * chunk, (c + 1) * chunk)
        s = _dot_nt(q, k_ref[0, keys, :])
        m_new = jnp.maximum(m, jnp.max(s, axis=-1, keepdims=True))
        alpha = jnp.exp2(m - m_new)
        p = jnp.exp2(s - m_new)
        l = alpha * l + jnp.sum(p, axis=-1, keepdims=True)
        acc = alpha * acc + _dot(p.astype(BF16), v_ref[0, keys, :])
        m = m_new
    o_ref[0] = acc / l


def _attention(q3, k3, v3):
    bsz, s, _ = q3.shape
    tq = min(Q_TILE, s)
    return pl.pallas_call(
        _attn_kernel,
        grid=(bsz, N_MLA_HEADS, s // tq),
        in_specs=[pl.BlockSpec((1, tq, QK_SLAB), lambda b, h, i: (b, i, h)),
                  pl.BlockSpec((1, s, QK_SLAB), lambda b, h, i: (b, 0, h)),
                  pl.BlockSpec((1, s, V_HEAD_DIM), lambda b, h, i: (b, 0, h))],
        out_specs=pl.BlockSpec((1, tq, V_HEAD_DIM), lambda b, h, i: (b, i, h)),
        out_shape=jax.ShapeDtypeStruct((bsz, s, D_MLA), F32),
        compiler_params=_params("parallel", "parallel", "parallel"),
        name="attention",
    )(q3, k3, v3)


def _route_tile(logits, carry):
    tm = logits.shape[1]
    erow = lax.broadcasted_iota(I32, logits.shape, 0)
    work = logits
    vals, hots = [], []
    for k in range(TOP_K):
        mx = jnp.max(work, axis=0, keepdims=True)
        idx = jnp.min(jnp.where(work == mx, erow, N_EXPERTS), axis=0, keepdims=True)
        hot = erow == idx
        work = jnp.where(hot, -jnp.inf, work)
        vals.append(mx)
        hots.append(hot)
    exps = [jnp.exp(v - vals[0]) for v in vals]
    tot = exps[0] + exps[1] + exps[2] + exps[3]
    multi = (hots[0] | hots[1] | hots[2] | hots[3]).astype(BF16)
    ti = lax.broadcasted_iota(I32, (tm, tm), 0)
    tj = lax.broadcasted_iota(I32, (tm, tm), 1)
    local_rank = _dot(multi, (ti < tj).astype(BF16))
    multi_rows = jnp.concatenate([multi, jnp.zeros((LANES - N_EXPERTS, tm), BF16)], axis=0)
    count = _dot_nt(jnp.ones((SUBLANES, tm), BF16), multi_rows)[0:1]
    padded = jnp.ceil(count * (1.0 / SEG_ALIGN)) * SEG_ALIGN
    filled = _cumsum_lanes(padded, reverse=False)
    local_start = filled - padded
    start_col = jnp.broadcast_to(local_start, (SUBLANES, LANES)).T[:N_EXPERTS, 0:1]
    slot_all = local_rank + start_col
    krow = lax.broadcasted_iota(I32, (SUBLANES, tm), 0)
    packed = jnp.zeros((SUBLANES, tm), F32)
    for k in range(TOP_K):
        sk = jnp.sum(jnp.where(hots[k], slot_all, 0.0), axis=0, keepdims=True)
        packed = jnp.where(krow == k, exps[k] / tot, jnp.where(krow == TOP_K + k, sk, packed))
    srow = lax.broadcasted_iota(I32, (SUBLANES, LANES), 0)
    seg = jnp.where(srow == 0, padded, jnp.where(srow == 1, local_start, jnp.where(srow == 2, carry,
                                                                                  filled[:, LANES - 1:LANES])))
    return packed, seg, padded


def _out_route_kernel(ym_ref, ya_ref, x_ref, mn_ref, wom_ref, woa_ref, fn_ref, wr_ref, br_ref,
                      h_ref, xn_ref, gate_ref, slot_ref, slot_t_ref, seg_ref, size_ref,
                      carry_ref):
    i = pl.program_id(0)
    tm = ROW_TILE

    @pl.when(i == 0)
    def _():
        carry_ref[...] = jnp.zeros_like(carry_ref)

    ya = _rms(ya_ref[...], mn_ref[...])
    h1 = x_ref[...] + _dot(ym_ref[...].astype(BF16), wom_ref[...]) + _dot(ya.astype(BF16), woa_ref[...])
    h_ref[...] = h1
    xn = _rms(h1, fn_ref[...])
    xn_hi = xn.astype(BF16)
    xn_ref[...] = xn_hi
    xn_lo = (xn - xn_hi.astype(F32)).astype(BF16)
    both = _dot_nt(wr_ref[...], xn_hi)
    logits = (both[:N_EXPERTS] + both[N_EXPERTS:] + _dot_nt(wr_ref[:N_EXPERTS, :], xn_lo)) + br_ref[...]
    carry = carry_ref[...]
    for s in range(x_ref.shape[0] // tm):
        packed, seg, padded = _route_tile(logits[:, s * tm:(s + 1) * tm], carry)
        krow = lax.broadcasted_iota(I32, packed.shape, 0)
        slot_t_ref[s] = jnp.where(krow < TOP_K, pltpu.roll(packed, TOP_K, 0), 0.0).astype(I32)
        cols = packed.T
        gate_ref[s * tm:(s + 1) * tm, :] = cols[:, :TOP_K]
        slot_ref[s * tm:(s + 1) * tm, :] = cols[:, TOP_K:2 * TOP_K].astype(I32)
        seg_ref[s] = seg.astype(I32)
        carry = carry + padded
    carry_ref[...] = carry
    size_ref[...] = carry.astype(I32)


def _out_route(ym2, ya2, x2, mla_norm, w_out, ffn_norm, w_router, b_router):
    n, d = x2.shape
    sub = min(ROUTE_TILES_PER_STEP, n // ROW_TILE)
    tm = ROW_TILE * sub
    nt = n // ROW_TILE
    wom = w_out[:D_MLSTM].astype(BF16)
    woa = w_out[D_MLSTM:].astype(BF16)
    wr_hi = w_router.T.astype(BF16)
    wr_lo = (w_router.T - wr_hi.astype(F32)).astype(BF16)
    wr = jnp.concatenate([wr_hi, wr_lo], axis=0)
    br = b_router.reshape(N_EXPERTS, 1)
    full = lambda arr: pl.BlockSpec(arr.shape, lambda i: (0,) * arr.ndim)
    rows = lambda w: pl.BlockSpec((tm, w), lambda i: (i, 0))
    consts = [mla_norm.reshape(1, D_MLA), wom, woa, ffn_norm.reshape(1, d), wr, br]
    return pl.pallas_call(
        _out_route_kernel,
        grid=(nt // sub,),
        in_specs=[rows(D_MLSTM), rows(D_MLA), rows(d)] + [full(c) for c in consts],
        out_specs=[rows(d), rows(d), rows(TOP_K), rows(TOP_K),
                   pl.BlockSpec((sub, SUBLANES, ROW_TILE), lambda i: (i, 0, 0)),
                   pl.BlockSpec((sub, SUBLANES, LANES), lambda i: (i, 0, 0)),
                   pl.BlockSpec((1, LANES), lambda i: (0, 0))],
        out_shape=[jax.ShapeDtypeStruct((n, d), F32), jax.ShapeDtypeStruct((n, d), BF16),
                   jax.ShapeDtypeStruct((n, TOP_K), F32), jax.ShapeDtypeStruct((n, TOP_K), I32),
                   jax.ShapeDtypeStruct((nt, SUBLANES, ROW_TILE), I32), jax.ShapeDtypeStruct((nt, SUBLANES, LANES), I32),
                   jax.ShapeDtypeStruct((1, LANES), I32)],
        scratch_shapes=[pltpu.VMEM((1, LANES), F32)],
        compiler_params=_params("arbitrary"),
        name="out_route",
    )(ym2, ya2, x2, *consts)


def _segment_copies(seg_ref, starts_ref, local_ref, global_ref, sem, to_global, wait):
    def copy(src, dst, rows):
        loc = local_ref.at[pl.ds(pl.multiple_of(src, SEG_ALIGN), rows)]
        glo = global_ref.at[pl.ds(pl.multiple_of(dst, SEG_ALIGN), rows)]
        return pltpu.make_async_copy(loc, glo, sem) if to_global else pltpu.make_async_copy(glo, loc, sem)

    if wait:
        copy(0, 0, pl.multiple_of(seg_ref[0, 3, 0], SEG_ALIGN)).wait()
        return

    def per_expert(e, carry):
        size = seg_ref[0, 0, e]
        src = seg_ref[0, 1, e]
        dst = starts_ref[e] + seg_ref[0, 2, e]
        off = 0
        rows = ROW_TILE
        while rows >= SEG_ALIGN:
            @pl.when((size & rows) != 0)
            def _(off=off, rows=rows):
                copy(src + off, dst + off, rows).start()

            off = off + (size & rows)
            rows //= 2
        return carry

    lax.fori_loop(0, N_EXPERTS, per_expert, 0)


def _dispatch_kernel(starts_ref, seg_ref, segp_ref, slot_t_ref, xn_ref, xs_ref, sort_ref, zero_ref, sem, zsem):
    i = pl.program_id(0)
    tm = xn_ref.shape[0]
    tmx = zero_ref.shape[0]
    cap = sort_ref.shape[1]

    @pl.when(i == 0)
    def _():
        zero_ref[...] = jnp.zeros_like(zero_ref)
        n_tail = (xs_ref.shape[0] - starts_ref[N_EXPERTS]) // tmx

        def clear_tile(row, wait):
            cp = pltpu.make_async_copy(zero_ref, xs_ref.at[pl.ds(pl.multiple_of(row, tmx), tmx)], zsem)
            if wait:
                cp.wait()
            else:
                cp.start()

        for wait in (False, True):
            def clear_group(e, carry, wait=wait):
                hi = starts_ref[e + 1]

                @pl.when(hi > starts_ref[e])
                def _():
                    clear_tile(hi - tmx, wait)

                return carry

            def clear_tail(t, carry, wait=wait):
                clear_tile(starts_ref[N_EXPERTS] + t * tmx, wait)
                return carry

            lax.fori_loop(0, N_EXPERTS, clear_group, 0)
            lax.fori_loop(0, n_tail, clear_tail, 0)

    slot = i % 2
    slots16 = slot_t_ref[0].astype(I16)
    for r0 in range(0, cap, SORT_CHUNK):
        pos = (lax.broadcasted_iota(I32, (SORT_CHUNK, tm), 0) + r0).astype(I16)
        hit = pos == slots16[0:1, :]
        for k in range(1, TOP_K):
            hit = hit | (pos == slots16[k:k + 1, :])
        sort_ref[slot, r0:r0 + SORT_CHUNK, :] = _pack_halves(_dot(hit.astype(BF16), xn_ref[...]))
    _segment_copies(seg_ref, starts_ref, sort_ref.at[slot], xs_ref, sem.at[slot], True, False)

    @pl.when(i >= 1)
    def _():
        _segment_copies(segp_ref, starts_ref, sort_ref.at[1 - slot], xs_ref, sem.at[1 - slot], True, True)

    @pl.when(i == pl.num_programs(0) - 1)
    def _():
        _segment_copies(seg_ref, starts_ref, sort_ref.at[slot], xs_ref, sem.at[slot], True, True)


def _dispatch(starts, seg, slot_t, xn, n_rows):
    n, d = xn.shape
    tm = ROW_TILE
    nt = n // tm
    smem = lambda f: pl.BlockSpec((1, SUBLANES, LANES), f, memory_space=pltpu.SMEM)
    any_spec = pl.BlockSpec(memory_space=pl.ANY)
    return pl.pallas_call(
        _dispatch_kernel,
        grid_spec=pltpu.PrefetchScalarGridSpec(
            num_scalar_prefetch=1,
            grid=(nt,),
            in_specs=[smem(lambda i, *_: (i, 0, 0)), smem(lambda i, *_: (jnp.maximum(i - 1, 0), 0, 0)),
                      pl.BlockSpec((1, SUBLANES, tm), lambda i, *_: (i, 0, 0)),
                      pl.BlockSpec((tm, d), lambda i, *_: (i, 0))],
            out_specs=any_spec,
            scratch_shapes=[pltpu.VMEM((2, SORT_ROWS, d // 2), U32), pltpu.VMEM((EXPERT_TILE, d // 2), U32),
                            pltpu.SemaphoreType.DMA((2,)), pltpu.SemaphoreType.DMA],
        ),
        out_shape=jax.ShapeDtypeStruct((n_rows, d // 2), U32),
        compiler_params=_params("arbitrary"),
        name="dispatch",
    )(starts, seg, seg, slot_t, xn)


def _experts_kernel(te_ref, tb_ref, tv_ref, xs_ref, wgu_ref, bgu_ref, wd_ref, bd_ref, out_ref,
                    wgu_bf, wd_bf):
    i = pl.program_id(0)
    de = wd_ref.shape[1]
    prev = te_ref[jnp.maximum(i - 1, 0)]

    @pl.when(jnp.logical_or(i == 0, te_ref[i] != prev))
    def _():
        wgu_bf[...] = wgu_ref[0].astype(BF16)
        wd_bf[...] = wd_ref[0].astype(BF16)

    @pl.when(tv_ref[i] == 1)
    def _():
        x = _unpack_halves(xs_ref[...])
        acc = jnp.zeros((xs_ref.shape[0], bd_ref.shape[2]), F32)
        for j in range(de // FF_CHUNK):
            cols = slice(j * FF_CHUNK, (j + 1) * FF_CHUNK)
            up_cols = slice(de + j * FF_CHUNK, de + (j + 1) * FF_CHUNK)
            g = _dot(x, wgu_bf[:, cols]) + bgu_ref[0, :, cols]
            u = _dot(x, wgu_bf[:, up_cols]) + bgu_ref[0, :, up_cols]
            g = jnp.minimum(g, SWIGLU_LIMIT)
            u = jnp.clip(u, -SWIGLU_LIMIT, SWIGLU_LIMIT)
            hm = (u + 1.0) * (g * jax.nn.sigmoid(g * SWIGLU_ALPHA))
            acc = acc + _dot(hm.astype(BF16), wd_bf[cols, :])
        out_ref[...] = _pack_halves((acc + bd_ref[0]).astype(BF16).astype(F32))

    @pl.when(tv_ref[i] == 0)
    def _():
        out_ref[...] = jnp.zeros_like(out_ref)


def _experts(tile_e, tile_b, tile_v, xs, w_gate_up, b_gate_up, w_down, b_down):
    n_rows, dw = xs.shape
    tmx = EXPERT_TILE
    ne, d, de2 = w_gate_up.shape
    de = de2 // 2
    return pl.pallas_call(
        _experts_kernel,
        grid_spec=pltpu.PrefetchScalarGridSpec(
            num_scalar_prefetch=3,
            grid=(n_rows // tmx,),
            in_specs=[pl.BlockSpec((tmx, dw), lambda i, te, tb, tv: (tb[i], 0)),
                      pl.BlockSpec((1, d, de2), lambda i, te, tb, tv: (te[i], 0, 0)),
                      pl.BlockSpec((1, 1, de2), lambda i, te, tb, tv: (te[i], 0, 0)),
                      pl.BlockSpec((1, de, d), lambda i, te, tb, tv: (te[i], 0, 0)),
                      pl.BlockSpec((1, 1, d), lambda i, te, tb, tv: (te[i], 0, 0))],
            out_specs=pl.BlockSpec((tmx, dw), lambda i, te, tb, tv: (i, 0)),
            scratch_shapes=[pltpu.VMEM((d, de2), BF16), pltpu.VMEM((de, d), BF16)],
        ),
        out_shape=jax.ShapeDtypeStruct((n_rows, dw), U32),
        compiler_params=_params("arbitrary"),
        name="experts",
    )(tile_e, tile_b, tile_v, xs, w_gate_up, b_gate_up.reshape(ne, 1, de2), w_down, b_down.reshape(ne, 1, d))


def _combine_kernel(starts_ref, seg_ref, segn_ref, h_ref, gate_ref, slot_ref, p_ref, ys_ref, pn_ref, wg_ref,
                    wp_ref, fn_ref, out_ref, ybuf, sem, *, final):
    i = pl.program_id(0)
    nt = pl.num_programs(0)
    tm = h_ref.shape[0]
    cap = ybuf.shape[1]

    @pl.when(i == 0)
    def _():
        ybuf[...] = jnp.zeros_like(ybuf)
        _segment_copies(seg_ref, starts_ref, ybuf.at[0], ys_ref, sem.at[0], False, False)

    slot = i % 2

    @pl.when(i + 1 < nt)
    def _():
        _segment_copies(segn_ref, starts_ref, ybuf.at[1 - slot], ys_ref, sem.at[1 - slot], False, False)

    _segment_copies(seg_ref, starts_ref, ybuf.at[slot], ys_ref, sem.at[slot], False, True)

    gate = gate_ref[...].astype(BF16)
    slot16 = slot_ref[...].astype(I16)
    h2 = h_ref[...]
    for c0 in range(0, cap, SORT_CHUNK):
        pos = (lax.broadcasted_iota(I32, (tm, SORT_CHUNK), 1) + c0).astype(I16)
        weights = jnp.zeros((tm, SORT_CHUNK), BF16)
        for k in range(TOP_K):
            weights = jnp.where(pos == slot16[:, k:k + 1], gate[:, k:k + 1], weights)
        h2 = h2 + _dot(weights, _unpack_halves(ybuf[slot, c0:c0 + SORT_CHUNK, :]))
    hn = _rms(h2, pn_ref[...]).astype(BF16)
    sg = jax.nn.sigmoid(_dot(hn, wg_ref[...]))
    h3 = h2 + sg * _dot(p_ref[...].astype(BF16), wp_ref[...])
    out_ref[...] = _rms(h3, fn_ref[...]) if final else h3


def _combine(starts, seg, h1, gate, slot, p2, ys, ple_norm, w_ple_gate, w_ple_proj, final_norm, final):
    n, d = h1.shape
    tm = ROW_TILE
    nt = n // tm
    wg = w_ple_gate.astype(BF16)
    wp = w_ple_proj.astype(BF16)
    smem = lambda f: pl.BlockSpec((1, SUBLANES, LANES), f, memory_space=pltpu.SMEM)
    full = lambda arr: pl.BlockSpec(arr.shape, lambda i, *_: (0,) * arr.ndim)
    rows = lambda w: pl.BlockSpec((tm, w), lambda i, *_: (i, 0))
    consts = [ple_norm.reshape(1, d), wg, wp, final_norm.reshape(1, d)]
    return pl.pallas_call(
        functools.partial(_combine_kernel, final=final),
        grid_spec=pltpu.PrefetchScalarGridSpec(
            num_scalar_prefetch=1,
            grid=(nt,),
            in_specs=[smem(lambda i, *_: (i, 0, 0)), smem(lambda i, *_: (jnp.minimum(i + 1, nt - 1), 0, 0)),
                      rows(d), rows(TOP_K), rows(TOP_K), rows(p2.shape[1]), pl.BlockSpec(memory_space=pl.ANY)]
                     + [full(c) for c in consts],
            out_specs=rows(d),
            scratch_shapes=[pltpu.VMEM((2, SORT_ROWS, d // 2), U32), pltpu.SemaphoreType.DMA((2,))],
        ),
        out_shape=jax.ShapeDtypeStruct((n, d), F32),
        compiler_params=_params("arbitrary"),
        name="combine",
    )(starts, seg, seg, h1, gate, slot, p2, ys, *consts)


def _route_tables(sizes, n_tiles):
    tmx = EXPERT_TILE
    tile_end = jnp.cumsum((sizes + tmx - 1) // tmx)
    starts = jnp.concatenate([jnp.zeros((1,), I32), tile_end * tmx]).astype(I32)
    n_valid = tile_end[-1]
    t = jnp.arange(n_tiles, dtype=I32)
    tb = jnp.minimum(t, n_valid - 1).astype(I32)
    te = jnp.sum(tile_end[None, :] <= tb[:, None], axis=1).astype(I32)
    tv = (t < n_valid).astype(I32)
    return starts, te, tb, tv


def kernel(x, p, positions, attn_norm, w_in, b_gates, conv_w, conv_b, mlstm_norm, q_norm, w_q_up, kv_norm, w_kv_up, mla_norm, w_out, ffn_norm, w_router, b_router, w_gate_up, b_gate_up, w_down, b_down, ple_norm, w_ple_gate, w_ple_proj, final_norm):
    bsz, s, d = x.shape
    n = bsz * s
    depth = p.shape[0]
    nc = s // MLSTM_CHUNK
    max_rows = n * TOP_K + (n // ROW_TILE) * N_EXPERTS * (SEG_ALIGN - 1) + N_EXPERTS * (EXPERT_TILE - 1)
    n_tiles = max_rows // EXPERT_TILE
    pos2 = positions.reshape(n, 1)
    h = x.reshape(n, d)
    for i in range(depth):
        um, gt, q, k, v = _in_proj(h, pos2, attn_norm[i], w_in[i], b_gates[i], q_norm[i], w_q_up[i],
                                   kv_norm[i], w_kv_up[i])
        ym = _mlstm(um.reshape(bsz, s, -1), gt.reshape(N_GATES, bsz, nc, MLSTM_CHUNK), conv_w[i], conv_b[i],
                    mlstm_norm[i])
        ya = _attention(q.reshape(bsz, s, -1), k.reshape(bsz, s, -1), v.reshape(bsz, s, -1))
        h1, xn, gate, slot, slot_t, seg, sizes = _out_route(ym.reshape(n, -1), ya.reshape(n, -1), h, mla_norm[i],
                                                            w_out[i], ffn_norm[i], w_router[i], b_router[i])
        starts, te, tb, tv = _route_tables(sizes[0, :N_EXPERTS], n_tiles)
        xs = _dispatch(starts, seg, slot_t, xn, n_tiles * EXPERT_TILE)
        ys = _experts(te, tb, tv, xs, w_gate_up[i], b_gate_up[i], w_down[i], b_down[i])
        h = _combine(starts, seg, h1, gate, slot, p[i].reshape(n, -1), ys, ple_norm[i], w_ple_gate[i],
                     w_ple_proj[i], final_norm, final=(i == depth - 1))
    return h.reshape(bsz, s, d)
```

```python
import functools

import jax
import jax.numpy as jnp
from jax import lax
from jax.experimental import pallas as pl
from jax.experimental.pallas import tpu as pltpu

F32 = jnp.float32
BF16 = jnp.bfloat16
I32 = jnp.int32
U32 = jnp.uint32
I16 = jnp.int16

N_MLSTM_HEADS = 4
MLSTM_HEAD_DIM = 128
D_MLSTM = N_MLSTM_HEADS * MLSTM_HEAD_DIM
MLSTM_CHUNK = 128
N_MLA_HEADS = 4
QK_NOPE_DIM = 128
QK_ROPE_DIM = 64
V_HEAD_DIM = 128
D_MLA = N_MLA_HEADS * V_HEAD_DIM
Q_LORA = 256
KV_LORA = 128
ROPE_THETA = 10000.0
N_EXPERTS = 32
TOP_K = 4
SWIGLU_LIMIT = 7.0
SWIGLU_ALPHA = 1.702
EPS = 1e-6
N_GATES = 4 * N_MLSTM_HEADS
OFF_G = 4 * D_MLSTM
OFF_CQ = OFF_G + N_GATES
OFF_CKV = OFF_CQ + Q_LORA
OFF_KR = OFF_CKV + KV_LORA

LANES = 128
SUBLANES = 8
QK_SLAB = 2 * LANES
VMEM_LIMIT_BYTES = 56 * 1024 * 1024
LOG2_E = 1.4426950408889634

IN_TILE = 1024
ROW_TILE = 512
Q_TILE = 2048
KV_CHUNK = 1024
EXPERT_TILE = 768
FF_CHUNK = 512
CHUNK_UNROLL = 16
SEG_ALIGN = SUBLANES
ROUTE_TILES_PER_STEP = 2
SORT_CHUNK = 256
SORT_ROWS = ROW_TILE * TOP_K + N_EXPERTS * SEG_ALIGN


def _dot(a, b):
    return jnp.dot(a, b, preferred_element_type=F32)


def _dot_nt(a, b):
    return lax.dot_general(a, b, (((1,), (1,)), ((), ())), preferred_element_type=F32)


def _rms(x, g):
    return x * lax.rsqrt(jnp.mean(x * x, axis=-1, keepdims=True) + EPS) * g


def _log_sigmoid(x):
    return jnp.minimum(x, 0.0) - jnp.log(1.0 + jnp.exp(-jnp.abs(x)))


def _scan_lanes(x, reverse, op, identity):
    n = x.shape[-1]
    lane = lax.broadcasted_iota(I32, x.shape, x.ndim - 1)
    sh = 1
    while sh < n:
        if reverse:
            x = op(x, jnp.where(lane < n - sh, pltpu.roll(x, n - sh, x.ndim - 1), identity))
        else:
            x = op(x, jnp.where(lane >= sh, pltpu.roll(x, sh, x.ndim - 1), identity))
        sh *= 2
    return x


def _cumsum_lanes(x, reverse):
    return _scan_lanes(x, reverse, jnp.add, 0.0)


def _cummax_lanes(x, reverse):
    return _scan_lanes(x, reverse, jnp.maximum, -jnp.inf)


def _pack_halves(x):
    w = x.shape[1] // 2
    lo = lax.shift_right_logical(lax.bitcast_convert_type(x[:, :w], U32), jnp.uint32(16))
    hi = lax.bitcast_convert_type(x[:, w:], U32)
    return lo | hi


def _unpack_halves(words):
    lo = lax.bitcast_convert_type(lax.shift_left(words, jnp.uint32(16)), F32)
    hi = lax.bitcast_convert_type(words & jnp.uint32(0xFFFF0000), F32)
    return jnp.concatenate([lo, hi], axis=1).astype(BF16)


def _params(*sem):
    return pltpu.CompilerParams(dimension_semantics=sem, vmem_limit_bytes=VMEM_LIMIT_BYTES)


def _in_proj_kernel(x_ref, pos_ref, an_ref, wm_ref, wr_ref, bgt_ref, qn_ref, wq_ref, kvn_ref,
                    wk_ref, wv_ref, freq_ref,
                    um_ref, gt_ref, q_ref, k_ref, v_ref):
    a = _rms(x_ref[...], an_ref[...]).astype(BF16)
    um_ref[...] = _dot(a, wm_ref[...])
    rest = _dot(a, wr_ref[...])
    cq = rest[:, :Q_LORA]
    ckv = rest[:, Q_LORA:Q_LORA + KV_LORA]
    kr2 = rest[:, Q_LORA + KV_LORA:Q_LORA + KV_LORA + LANES]
    krs2 = rest[:, Q_LORA + KV_LORA + LANES:Q_LORA + KV_LORA + 2 * LANES]
    gt_ref[...] = rest[:, Q_LORA + KV_LORA + 2 * LANES:].T[:N_GATES, :] + bgt_ref[...]
    ang = freq_ref[...] * pos_ref[0].astype(F32)
    cos_t = jnp.cos(ang)
    sin_t = jnp.sin(ang)
    cos_a = jnp.concatenate([cos_t] * (LANES // cos_t.shape[0]), axis=0).T
    sin_a = jnp.concatenate([-sin_t, sin_t] * (LANES // (2 * sin_t.shape[0])), axis=0).T
    scale = (QK_NOPE_DIM + QK_ROPE_DIM) ** -0.5 * LOG2_E
    lane = lax.broadcasted_iota(I32, cos_a.shape, 1)
    rope_mul = jnp.where(lane < QK_ROPE_DIM, cos_a, sin_a) * scale
    qf = _dot(_rms(cq, qn_ref[...]).astype(BF16), wq_ref[...])
    ckvn = _rms(ckv, kvn_ref[...]).astype(BF16)
    kn = _dot(ckvn, wk_ref[...])
    v_ref[...] = _dot(ckvn, wv_ref[...]).astype(BF16)
    k_rope = (kr2 * cos_a + krs2 * sin_a).astype(BF16)
    for h in range(N_MLA_HEADS):
        o = h * QK_SLAB
        q_ref[:, o:o + LANES] = (qf[:, o:o + LANES] * scale).astype(BF16)
        q_ref[:, o + LANES:o + QK_SLAB] = (qf[:, o + LANES:o + QK_SLAB] * rope_mul).astype(BF16)
        k_ref[:, o:o + LANES] = kn[:, h * LANES:(h + 1) * LANES].astype(BF16)
        k_ref[:, o + LANES:o + QK_SLAB] = k_rope


def _in_proj(x2, pos2, attn_norm, w_in, b_gates, q_norm, w_q_up, kv_norm, w_kv_up):
    n, d = x2.shape
    tm = min(IN_TILE, n)
    half = QK_ROPE_DIM // 2
    swap = jnp.concatenate([jnp.arange(half, QK_ROPE_DIM), jnp.arange(0, half)])
    w_kr = w_in[:, OFF_KR:OFF_KR + QK_ROPE_DIM]
    w_krs = w_kr[:, swap]
    wm = w_in[:, :OFF_G].astype(BF16)
    w_g = jnp.pad(w_in[:, OFF_G:OFF_CQ], ((0, 0), (0, LANES - N_GATES)))
    wr = jnp.concatenate([w_in[:, OFF_CQ:OFF_KR], w_kr, w_kr, w_krs, w_krs, w_g], axis=1).astype(BF16)
    bgt = b_gates.reshape(N_GATES, 1)
    wq4 = w_q_up.reshape(Q_LORA, N_MLA_HEADS, QK_NOPE_DIM + QK_ROPE_DIM)
    wq_pe = wq4[:, :, QK_NOPE_DIM:]
    wq = jnp.concatenate([wq4, wq_pe[:, :, swap]], axis=2).reshape(Q_LORA, N_MLA_HEADS * QK_SLAB).astype(BF16)
    wkv4 = w_kv_up.reshape(KV_LORA, N_MLA_HEADS, QK_NOPE_DIM + V_HEAD_DIM)
    wk = wkv4[:, :, :QK_NOPE_DIM].reshape(KV_LORA, N_MLA_HEADS * QK_NOPE_DIM).astype(BF16)
    wv = wkv4[:, :, QK_NOPE_DIM:].reshape(KV_LORA, D_MLA).astype(BF16)
    freqs = ROPE_THETA ** (-jnp.arange(0, QK_ROPE_DIM, 2, dtype=F32) / QK_ROPE_DIM)
    freq_c = freqs.reshape(half, 1)
    full = lambda arr: pl.BlockSpec(arr.shape, lambda i: (0,) * arr.ndim)
    rows = lambda w: pl.BlockSpec((tm, w), lambda i: (i, 0))
    consts = [attn_norm.reshape(1, d), wm, wr, bgt, q_norm.reshape(1, Q_LORA), wq,
              kv_norm.reshape(1, KV_LORA), wk, wv, freq_c]
    return pl.pallas_call(
        _in_proj_kernel,
        grid=(n // tm,),
        in_specs=[rows(d), pl.BlockSpec((1, 1, tm), lambda i: (i, 0, 0))] + [full(c) for c in consts],
        out_specs=[rows(OFF_G), pl.BlockSpec((N_GATES, tm), lambda i: (0, i)),
                   rows(N_MLA_HEADS * QK_SLAB), rows(N_MLA_HEADS * QK_SLAB), rows(D_MLA)],
        out_shape=[jax.ShapeDtypeStruct((n, OFF_G), F32), jax.ShapeDtypeStruct((N_GATES, n), F32),
                   jax.ShapeDtypeStruct((n, N_MLA_HEADS * QK_SLAB), BF16),
                   jax.ShapeDtypeStruct((n, N_MLA_HEADS * QK_SLAB), BF16),
                   jax.ShapeDtypeStruct((n, D_MLA), BF16)],
        compiler_params=_params("parallel"),
        name="in_proj",
    )(x2, pos2.reshape(n // tm, 1, tm), *consts)


def _mlstm_kernel(q_ref, k_ref, v_ref, o_ref, g_ref, cwq_ref, cwk_ref, cbq_ref, cbk_ref, nrm_ref,
                  y_ref,
                  qc_ref, kc_ref, va_ref, cst_ref, ent_ref, b_ref, e_ref, r_ref, ew_ref, mw_ref, bt_ref, mp_ref):
    L = MLSTM_CHUNK
    dh = MLSTM_HEAD_DIM
    nc = q_ref.shape[1] // L
    s_len = q_ref.shape[1]
    h = pl.program_id(1)

    for d in range(2):
        ig = g_ref[2 * d * N_MLSTM_HEADS + h, 0]
        fg = g_ref[(2 * d + 1) * N_MLSTM_HEADS + h, 0]
        b = _cumsum_lanes(_log_sigmoid(fg), reverse=(d == 1))
        btot = b[:, L - 1:L] if d == 0 else b[:, 0:1]
        r = ig - b
        w = btot + r
        mw = jnp.max(w, axis=-1, keepdims=True)
        b_ref[d] = b
        e_ref[d] = b + _cummax_lanes(r, reverse=(d == 1))
        r_ref[d] = r
        ew_ref[d] = jnp.exp(w - mw)
        mw_ref[d] = jnp.broadcast_to(mw, (nc, L))
        bt_ref[d] = jnp.broadcast_to(btot, (nc, L))

    row = lax.broadcasted_iota(I32, (L, dh), 0)

    def conv_silu(ref, cw_ref, cb_ref, c):
        start = pl.multiple_of(c * L, L)
        x = ref[0, pl.ds(start, L), :]
        prev_row = jnp.where(c > 0, ref[0, pl.ds(jnp.maximum(start - 1, 0), 1), :], 0.0)
        next_row = jnp.where(c < nc - 1, ref[0, pl.ds(jnp.minimum(start + L, s_len - 1), 1), :], 0.0)
        x_prev = jnp.where(row == 0, prev_row, pltpu.roll(x, 1, 0))
        x_next = jnp.where(row == L - 1, next_row, pltpu.roll(x, L - 1, 0))
        y = cw_ref[0:1, :] * x_prev + cw_ref[1:2, :] * x + cw_ref[2:3, :] * x_next + cb_ref[...]
        return y * jax.nn.sigmoid(y)

    ones_blk = jnp.ones((L, dh), BF16)

    def pass1(c, carry):
        start = pl.multiple_of(c * L, L)
        qc_ref[pl.ds(start, L), :] = conv_silu(q_ref, cwq_ref, cbq_ref, c).astype(BF16)
        kk = conv_silu(k_ref, cwk_ref, cbk_ref, c) * (dh ** -0.5)
        kc_ref[pl.ds(start, L), :] = kk.astype(BF16)
        va = jnp.concatenate([v_ref[0, pl.ds(start, L), :].astype(BF16), ones_blk], axis=1)
        va_ref[pl.ds(start, L), :] = va
        kt = kk.T
        for d in range(2):
            kw_t = (kt * ew_ref[d, pl.ds(c, 1), :]).astype(BF16)
            cst_ref[d, c] = _dot(kw_t, va)
        return carry

    lax.fori_loop(0, nc, pass1, 0, unroll=CHUNK_UNROLL)

    ent_ref[:, 0] = jnp.zeros((2, dh, 2 * dh), F32)

    def scan(i, carry):
        out = []
        for d in range(2):
            m = carry[d]
            c = i if d == 0 else nc - 1 - i
            mw = mw_ref[d, pl.ds(c, 1), :]
            bt = bt_ref[d, pl.ds(c, 1), :]
            m_new = jnp.maximum(bt + m, mw)
            a = jnp.exp(bt + m - m_new)[:, 0:1]
            cc = jnp.exp(mw - m_new)[:, 0:1]
            ent_ref[d, i + 1] = a * ent_ref[d, i] + cc * cst_ref[d, c]
            mp_ref[d, pl.ds(c, 1), :] = m
            out.append(m_new)
        return tuple(out)

    lax.fori_loop(0, nc, scan, (jnp.zeros((1, L), F32), jnp.zeros((1, L), F32)))

    ti = lax.broadcasted_iota(I32, (L, L), 0)
    si = lax.broadcasted_iota(I32, (L, L), 1)
    masks = (si <= ti, si >= ti)

    def pass3(c, carry):
        start = pl.multiple_of(c * L, L)
        q = qc_ref[pl.ds(start, L), :]
        k = kc_ref[pl.ds(start, L), :]
        va = va_ref[pl.ds(start, L), :]
        qk = _dot_nt(q, k)
        hsum = jnp.zeros((L, dh), F32)
        for d in range(2):
            bmat = jnp.broadcast_to(b_ref[d, pl.ds(c, 1), :], (L, L)).T
            emat = jnp.broadcast_to(e_ref[d, pl.ds(c, 1), :], (L, L)).T
            dmat = jnp.where(masks[d], bmat + r_ref[d, pl.ds(c, 1), :], -jnp.inf)
            inter = bmat + mp_ref[d, pl.ds(c, 1), :]
            m_t = jnp.maximum(inter, emat)
            sc = qk * jnp.exp(dmat - m_t)
            a = jnp.exp(inter - m_t)
            intra = _dot(sc.astype(BF16), va)
            cross = _dot(q, ent_ref[d, c if d == 0 else nc - 1 - c].astype(BF16))
            num = intra[:, :dh] + a * cross[:, :dh]
            den = intra[:, dh:] + a * cross[:, dh:]
            hsum = hsum + num / jnp.maximum(jnp.abs(den), jnp.exp(-m_t))
        hn = _rms(hsum, nrm_ref[...])
        y_ref[0, pl.ds(start, L), :] = hn * jax.nn.sigmoid(o_ref[0, pl.ds(start, L), :])
        return carry

    lax.fori_loop(0, nc, pass3, 0, unroll=CHUNK_UNROLL)


def _mlstm(um3, gt4, conv_w, conv_b, mlstm_norm):
    bsz, s, _ = um3.shape
    H, dh, L = N_MLSTM_HEADS, MLSTM_HEAD_DIM, MLSTM_CHUNK
    nc = s // L
    col = lambda off: pl.BlockSpec((1, s, dh), lambda b, h: (b, 0, off + h))
    vec = lambda rows, off: pl.BlockSpec((rows, dh), lambda b, h: (0, off + h))
    cb = conv_b.reshape(1, 2 * D_MLSTM)
    return pl.pallas_call(
        _mlstm_kernel,
        grid=(bsz, H),
        in_specs=[col(0), col(H), col(2 * H), col(3 * H),
                  pl.BlockSpec((N_GATES, 1, nc, L), lambda b, h: (0, b, 0, 0)),
                  vec(3, 0), vec(3, H), vec(1, 0), vec(1, H), vec(1, 0)],
        out_specs=pl.BlockSpec((1, s, dh), lambda b, h: (b, 0, h)),
        out_shape=jax.ShapeDtypeStruct((bsz, s, D_MLSTM), F32),
        scratch_shapes=[pltpu.VMEM((s, dh), BF16), pltpu.VMEM((s, dh), BF16), pltpu.VMEM((s, 2 * dh), BF16),
                        pltpu.VMEM((2, nc, dh, 2 * dh), F32), pltpu.VMEM((2, nc + 1, dh, 2 * dh), F32)]
                       + [pltpu.VMEM((2, nc, L), F32) for _ in range(7)],
        compiler_params=_params("parallel", "parallel"),
        name="mlstm",
    )(um3, um3, um3, um3, gt4, conv_w, conv_w, cb, cb, mlstm_norm.reshape(1, D_MLSTM))


def _attn_kernel(q_ref, k_ref, v_ref, o_ref):
    tq = q_ref.shape[1]
    q = q_ref[0]
    m = jnp.full((tq, 1), -jnp.inf, F32)
    l = jnp.zeros((tq, 1), F32)
    acc = jnp.zeros((tq, V_HEAD_DIM), F32)
    chunk = min(KV_CHUNK, k_ref.shape[1])
    for c in range(k_ref.shape[1] // chunk):
        keys = slice(c * chunk, (c + 1) * chunk)
        s = _dot_nt(q, k_ref[0, keys, :])
        m_new = jnp.maximum(m, jnp.max(s, axis=-1, keepdims=True))
        alpha = jnp.exp2(m - m_new)
        p = jnp.exp2(s - m_new)
        l = alpha * l + jnp.sum(p, axis=-1, keepdims=True)
        acc = alpha * acc + _dot(p.astype(BF16), v_ref[0, keys, :])
        m = m_new
    o_ref[0] = acc / l


def _attention(q3, k3, v3):
    bsz, s, _ = q3.shape
    tq = min(Q_TILE, s)
    return pl.pallas_call(
        _attn_kernel,
        grid=(bsz, N_MLA_HEADS, s // tq),
        in_specs=[pl.BlockSpec((1, tq, QK_SLAB), lambda b, h, i: (b, i, h)),
                  pl.BlockSpec((1, s, QK_SLAB), lambda b, h, i: (b, 0, h)),
                  pl.BlockSpec((1, s, V_HEAD_DIM), lambda b, h, i: (b, 0, h))],
        out_specs=pl.BlockSpec((1, tq, V_HEAD_DIM), lambda b, h, i: (b, i, h)),
        out_shape=jax.ShapeDtypeStruct((bsz, s, D_MLA), F32),
        compiler_params=_params("parallel", "parallel", "parallel"),
        name="attention",
    )(q3, k3, v3)


def _route_tile(logits, carry):
    tm = logits.shape[1]
    erow = lax.broadcasted_iota(I32, logits.shape, 0)
    work = logits
    vals, hots = [], []
    for k in range(TOP_K):
        mx = jnp.max(work, axis=0, keepdims=True)
        idx = jnp.min(jnp.where(work == mx, erow, N_EXPERTS), axis=0, keepdims=True)
        hot = erow == idx
        work = jnp.where(hot, -jnp.inf, work)
        vals.append(mx)
        hots.append(hot)
    exps = [jnp.exp(v - vals[0]) for v in vals]
    tot = exps[0] + exps[1] + exps[2] + exps[3]
    multi = (hots[0] | hots[1] | hots[2] | hots[3]).astype(BF16)
    ti = lax.broadcasted_iota(I32, (tm, tm), 0)
    tj = lax.broadcasted_iota(I32, (tm, tm), 1)
    local_rank = _dot(multi, (ti < tj).astype(BF16))
    multi_rows = jnp.concatenate([multi, jnp.zeros((LANES - N_EXPERTS, tm), BF16)], axis=0)
    count = _dot_nt(jnp.ones((SUBLANES, tm), BF16), multi_rows)[0:1]
    padded = jnp.ceil(count * (1.0 / SEG_ALIGN)) * SEG_ALIGN
    filled = _cumsum_lanes(padded, reverse=False)
    local_start = filled - padded
    start_col = jnp.broadcast_to(local_start, (SUBLANES, LANES)).T[:N_EXPERTS, 0:1]
    slot_all = local_rank + start_col
    krow = lax.broadcasted_iota(I32, (SUBLANES, tm), 0)
    packed = jnp.zeros((SUBLANES, tm), F32)
    for k in range(TOP_K):
        sk = jnp.sum(jnp.where(hots[k], slot_all, 0.0), axis=0, keepdims=True)
        packed = jnp.where(krow == k, exps[k] / tot, jnp.where(krow == TOP_K + k, sk, packed))
    srow = lax.broadcasted_iota(I32, (SUBLANES, LANES), 0)
    seg = jnp.where(srow == 0, padded, jnp.where(srow == 1, local_start, jnp.where(srow == 2, carry,
                                                                                  filled[:, LANES - 1:LANES])))
    return packed, seg, padded


def _out_route_kernel(ym_ref, ya_ref, x_ref, mn_ref, wom_ref, woa_ref, fn_ref, wr_ref, br_ref,
                      h_ref, xn_ref, gate_ref, slot_ref, slot_t_ref, seg_ref, size_ref,
                      carry_ref):
    i = pl.program_id(0)
    tm = ROW_TILE

    @pl.when(i == 0)
    def _():
        carry_ref[...] = jnp.zeros_like(carry_ref)

    ya = _rms(ya_ref[...], mn_ref[...])
    h1 = x_ref[...] + _dot(ym_ref[...].astype(BF16), wom_ref[...]) + _dot(ya.astype(BF16), woa_ref[...])
    h_ref[...] = h1
    xn = _rms(h1, fn_ref[...])
    xn_hi = xn.astype(BF16)
    xn_ref[...] = xn_hi
    xn_lo = (xn - xn_hi.astype(F32)).astype(BF16)
    both = _dot_nt(wr_ref[...], xn_hi)
    logits = (both[:N_EXPERTS] + both[N_EXPERTS:] + _dot_nt(wr_ref[:N_EXPERTS, :], xn_lo)) + br_ref[...]
    carry = carry_ref[...]
    for s in range(x_ref.shape[0] // tm):
        packed, seg, padded = _route_tile(logits[:, s * tm:(s + 1) * tm], carry)
        krow = lax.broadcasted_iota(I32, packed.shape, 0)
        slot_t_ref[s] = jnp.where(krow < TOP_K, pltpu.roll(packed, TOP_K, 0), 0.0).astype(I32)
        cols = packed.T
        gate_ref[s * tm:(s + 1) * tm, :] = cols[:, :TOP_K]
        slot_ref[s * tm:(s + 1) * tm, :] = cols[:, TOP_K:2 * TOP_K].astype(I32)
        seg_ref[s] = seg.astype(I32)
        carry = carry + padded
    carry_ref[...] = carry
    size_ref[...] = carry.astype(I32)


def _out_route(ym2, ya2, x2, mla_norm, w_out, ffn_norm, w_router, b_router):
    n, d = x2.shape
    sub = min(ROUTE_TILES_PER_STEP, n // ROW_TILE)
    tm = ROW_TILE * sub
    nt = n // ROW_TILE
    wom = w_out[:D_MLSTM].astype(BF16)
    woa = w_out[D_MLSTM:].astype(BF16)
    wr_hi = w_router.T.astype(BF16)
    wr_lo = (w_router.T - wr_hi.astype(F32)).astype(BF16)
    wr = jnp.concatenate([wr_hi, wr_lo], axis=0)
    br = b_router.reshape(N_EXPERTS, 1)
    full = lambda arr: pl.BlockSpec(arr.shape, lambda i: (0,) * arr.ndim)
    rows = lambda w: pl.BlockSpec((tm, w), lambda i: (i, 0))
    consts = [mla_norm.reshape(1, D_MLA), wom, woa, ffn_norm.reshape(1, d), wr, br]
    return pl.pallas_call(
        _out_route_kernel,
        grid=(nt // sub,),
        in_specs=[rows(D_MLSTM), rows(D_MLA), rows(d)] + [full(c) for c in consts],
        out_specs=[rows(d), rows(d), rows(TOP_K), rows(TOP_K),
                   pl.BlockSpec((sub, SUBLANES, ROW_TILE), lambda i: (i, 0, 0)),
                   pl.BlockSpec((sub, SUBLANES, LANES), lambda i: (i, 0, 0)),
                   pl.BlockSpec((1, LANES), lambda i: (0, 0))],
        out_shape=[jax.ShapeDtypeStruct((n, d), F32), jax.ShapeDtypeStruct((n, d), BF16),
                   jax.ShapeDtypeStruct((n, TOP_K), F32), jax.ShapeDtypeStruct((n, TOP_K), I32),
                   jax.ShapeDtypeStruct((nt, SUBLANES, ROW_TILE), I32), jax.ShapeDtypeStruct((nt, SUBLANES, LANES), I32),
                   jax.ShapeDtypeStruct((1, LANES), I32)],
        scratch_shapes=[pltpu.VMEM((1, LANES), F32)],
        compiler_params=_params("arbitrary"),
        name="out_route",
    )(ym2, ya2, x2, *consts)


def _segment_copies(seg_ref, starts_ref, local_ref, global_ref, sem, to_global, wait):
    def copy(src, dst, rows):
        loc = local_ref.at[pl.ds(pl.multiple_of(src, SEG_ALIGN), rows)]
        glo = global_ref.at[pl.ds(pl.multiple_of(dst, SEG_ALIGN), rows)]
        return pltpu.make_async_copy(loc, glo, sem) if to_global else pltpu.make_async_copy(glo, loc, sem)

    if wait:
        copy(0, 0, pl.multiple_of(seg_ref[0, 3, 0], SEG_ALIGN)).wait()
        return

    def per_expert(e, carry):
        size = seg_ref[0, 0, e]
        src = seg_ref[0, 1, e]
        dst = starts_ref[e] + seg_ref[0, 2, e]
        off = 0
        rows = ROW_TILE
        while rows >= SEG_ALIGN:
            @pl.when((size & rows) != 0)
            def _(off=off, rows=rows):
                copy(src + off, dst + off, rows).start()

            off = off + (size & rows)
            rows //= 2
        return carry

    lax.fori_loop(0, N_EXPERTS, per_expert, 0)


def _dispatch_kernel(starts_ref, seg_ref, segp_ref, slot_t_ref, xn_ref, xs_ref, sort_ref, zero_ref, sem, zsem):
    i = pl.program_id(0)
    tm = xn_ref.shape[0]
    tmx = zero_ref.shape[0]
    cap = sort_ref.shape[1]

    @pl.when(i == 0)
    def _():
        zero_ref[...] = jnp.zeros_like(zero_ref)
        n_tail = (xs_ref.shape[0] - starts_ref[N_EXPERTS]) // tmx

        def clear_tile(row, wait):
            cp = pltpu.make_async_copy(zero_ref, xs_ref.at[pl.ds(pl.multiple_of(row, tmx), tmx)], zsem)
            if wait:
                cp.wait()
            else:
                cp.start()

        for wait in (False, True):
            def clear_group(e, carry, wait=wait):
                hi = starts_ref[e + 1]

                @pl.when(hi > starts_ref[e])
                def _():
                    clear_tile(hi - tmx, wait)

                return carry

            def clear_tail(t, carry, wait=wait):
                clear_tile(starts_ref[N_EXPERTS] + t * tmx, wait)
                return carry

            lax.fori_loop(0, N_EXPERTS, clear_group, 0)
            lax.fori_loop(0, n_tail, clear_tail, 0)

    slot = i % 2
    slots16 = slot_t_ref[0].astype(I16)
    for r0 in range(0, cap, SORT_CHUNK):
        pos = (lax.broadcasted_iota(I32, (SORT_CHUNK, tm), 0) + r0).astype(I16)
        hit = pos == slots16[0:1, :]
        for k in range(1, TOP_K):
            hit = hit | (pos == slots16[k:k + 1, :])
        sort_ref[slot, r0:r0 + SORT_CHUNK, :] = _pack_halves(_dot(hit.astype(BF16), xn_ref[...]))
    _segment_copies(seg_ref, starts_ref, sort_ref.at[slot], xs_ref, sem.at[slot], True, False)

    @pl.when(i >= 1)
    def _():
        _segment_copies(segp_ref, starts_ref, sort_ref.at[1 - slot], xs_ref, sem.at[1 - slot], True, True)

    @pl.when(i == pl.num_programs(0) - 1)
    def _():
        _segment_copies(seg_ref, starts_ref, sort_ref.at[slot], xs_ref, sem.at[slot], True, True)


def _dispatch(starts, seg, slot_t, xn, n_rows):
    n, d = xn.shape
    tm = ROW_TILE
    nt = n // tm
    smem = lambda f: pl.BlockSpec((1, SUBLANES, LANES), f, memory_space=pltpu.SMEM)
    any_spec = pl.BlockSpec(memory_space=pl.ANY)
    return pl.pallas_call(
        _dispatch_kernel,
        grid_spec=pltpu.PrefetchScalarGridSpec(
            num_scalar_prefetch=1,
            grid=(nt,),
            in_specs=[smem(lambda i, *_: (i, 0, 0)), smem(lambda i, *_: (jnp.maximum(i - 1, 0), 0, 0)),
                      pl.BlockSpec((1, SUBLANES, tm), lambda i, *_: (i, 0, 0)),
                      pl.BlockSpec((tm, d), lambda i, *_: (i, 0))],
            out_specs=any_spec,
            scratch_shapes=[pltpu.VMEM((2, SORT_ROWS, d // 2), U32), pltpu.VMEM((EXPERT_TILE, d // 2), U32),
                            pltpu.SemaphoreType.DMA((2,)), pltpu.SemaphoreType.DMA],
        ),
        out_shape=jax.ShapeDtypeStruct((n_rows, d // 2), U32),
        compiler_params=_params("arbitrary"),
        name="dispatch",
    )(starts, seg, seg, slot_t, xn)


def _experts_kernel(te_ref, tb_ref, tv_ref, xs_ref, wgu_ref, bgu_ref, wd_ref, bd_ref, out_ref,
                    wgu_bf, wd_bf):
    i = pl.program_id(0)
    de = wd_ref.shape[1]
    prev = te_ref[jnp.maximum(i - 1, 0)]

    @pl.when(jnp.logical_or(i == 0, te_ref[i] != prev))
    def _():
        wgu_bf[...] = wgu_ref[0].astype(BF16)
        wd_bf[...] = wd_ref[0].astype(BF16)

    @pl.when(tv_ref[i] == 1)
    def _():
        x = _unpack_halves(xs_ref[...])
        acc = jnp.zeros((xs_ref.shape[0], bd_ref.shape[2]), F32)
        for j in range(de // FF_CHUNK):
            cols = slice(j * FF_CHUNK, (j + 1) * FF_CHUNK)
            up_cols = slice(de + j * FF_CHUNK, de + (j + 1) * FF_CHUNK)
            g = _dot(x, wgu_bf[:, cols]) + bgu_ref[0, :, cols]
            u = _dot(x, wgu_bf[:, up_cols]) + bgu_ref[0, :, up_cols]
            g = jnp.minimum(g, SWIGLU_LIMIT)
            u = jnp.clip(u, -SWIGLU_LIMIT, SWIGLU_LIMIT)
            hm = (u + 1.0) * (g * jax.nn.sigmoid(g * SWIGLU_ALPHA))
            acc = acc + _dot(hm.astype(BF16), wd_bf[cols, :])
        out_ref[...] = _pack_halves((acc + bd_ref[0]).astype(BF16).astype(F32))

    @pl.when(tv_ref[i] == 0)
    def _():
        out_ref[...] = jnp.zeros_like(out_ref)


def _experts(tile_e, tile_b, tile_v, xs, w_gate_up, b_gate_up, w_down, b_down):
    n_rows, dw = xs.shape
    tmx = EXPERT_TILE
    ne, d, de2 = w_gate_up.shape
    de = de2 // 2
    return pl.pallas_call(
        _experts_kernel,
        grid_spec=pltpu.PrefetchScalarGridSpec(
            num_scalar_prefetch=3,
            grid=(n_rows // tmx,),
            in_specs=[pl.BlockSpec((tmx, dw), lambda i, te, tb, tv: (tb[i], 0)),
                      pl.BlockSpec((1, d, de2), lambda i, te, tb, tv: (te[i], 0, 0)),
                      pl.BlockSpec((1, 1, de2), lambda i, te, tb, tv: (te[i], 0, 0)),
                      pl.BlockSpec((1, de, d), lambda i, te, tb, tv: (te[i], 0, 0)),
                      pl.BlockSpec((1, 1, d), lambda i, te, tb, tv: (te[i], 0, 0))],
            out_specs=pl.BlockSpec((tmx, dw), lambda i, te, tb, tv: (i, 0)),
            scratch_shapes=[pltpu.VMEM((d, de2), BF16), pltpu.VMEM((de, d), BF16)],
        ),
        out_shape=jax.ShapeDtypeStruct((n_rows, dw), U32),
        compiler_params=_params("arbitrary"),
        name="experts",
    )(tile_e, tile_b, tile_v, xs, w_gate_up, b_gate_up.reshape(ne, 1, de2), w_down, b_down.reshape(ne, 1, d))


def _combine_kernel(starts_ref, seg_ref, segn_ref, h_ref, gate_ref, slot_ref, p_ref, ys_ref, pn_ref, wg_ref,
                    wp_ref, fn_ref, out_ref, ybuf, sem, *, final):
    i = pl.program_id(0)
    nt = pl.num_programs(0)
    tm = h_ref.shape[0]
    cap = ybuf.shape[1]

    @pl.when(i == 0)
    def _():
        ybuf[...] = jnp.zeros_like(ybuf)
        _segment_copies(seg_ref, starts_ref, ybuf.at[0], ys_ref, sem.at[0], False, False)

    slot = i % 2

    @pl.when(i + 1 < nt)
    def _():
        _segment_copies(segn_ref, starts_ref, ybuf.at[1 - slot], ys_ref, sem.at[1 - slot], False, False)

    _segment_copies(seg_ref, starts_ref, ybuf.at[slot], ys_ref, sem.at[slot], False, True)

    gate = gate_ref[...].astype(BF16)
    slot16 = slot_ref[...].astype(I16)
    h2 = h_ref[...]
    for c0 in range(0, cap, SORT_CHUNK):
        pos = (lax.broadcasted_iota(I32, (tm, SORT_CHUNK), 1) + c0).astype(I16)
        weights = jnp.zeros((tm, SORT_CHUNK), BF16)
        for k in range(TOP_K):
            weights = jnp.where(pos == slot16[:, k:k + 1], gate[:, k:k + 1], weights)
        h2 = h2 + _dot(weights, _unpack_halves(ybuf[slot, c0:c0 + SORT_CHUNK, :]))
    hn = _rms(h2, pn_ref[...]).astype(BF16)
    sg = jax.nn.sigmoid(_dot(hn, wg_ref[...]))
    h3 = h2 + sg * _dot(p_ref[...].astype(BF16), wp_ref[...])
    out_ref[...] = _rms(h3, fn_ref[...]) if final else h3


def _combine(starts, seg, h1, gate, slot, p2, ys, ple_norm, w_ple_gate, w_ple_proj, final_norm, final):
    n, d = h1.shape
    tm = ROW_TILE
    nt = n // tm
    wg = w_ple_gate.astype(BF16)
    wp = w_ple_proj.astype(BF16)
    smem = lambda f: pl.BlockSpec((1, SUBLANES, LANES), f, memory_space=pltpu.SMEM)
    full = lambda arr: pl.BlockSpec(arr.shape, lambda i, *_: (0,) * arr.ndim)
    rows = lambda w: pl.BlockSpec((tm, w), lambda i, *_: (i, 0))
    consts = [ple_norm.reshape(1, d), wg, wp, final_norm.reshape(1, d)]
    return pl.pallas_call(
        functools.partial(_combine_kernel, final=final),
        grid_spec=pltpu.PrefetchScalarGridSpec(
            num_scalar_prefetch=1,
            grid=(nt,),
            in_specs=[smem(lambda i, *_: (i, 0, 0)), smem(lambda i, *_: (jnp.minimum(i + 1, nt - 1), 0, 0)),
                      rows(d), rows(TOP_K), rows(TOP_K), rows(p2.shape[1]), pl.BlockSpec(memory_space=pl.ANY)]
                     + [full(c) for c in consts],
            out_specs=rows(d),
            scratch_shapes=[pltpu.VMEM((2, SORT_ROWS, d // 2), U32), pltpu.SemaphoreType.DMA((2,))],
        ),
        out_shape=jax.ShapeDtypeStruct((n, d), F32),
        compiler_params=_params("arbitrary"),
        name="combine",
    )(starts, seg, seg, h1, gate, slot, p2, ys, *consts)


def _route_tables(sizes, n_tiles):
    tmx = EXPERT_TILE
    tile_end = jnp.cumsum((sizes + tmx - 1) // tmx)
    starts = jnp.concatenate([jnp.zeros((1,), I32), tile_end * tmx]).astype(I32)
    n_valid = tile_end[-1]
    t = jnp.arange(n_tiles, dtype=I32)
    tb = jnp.minimum(t, n_valid - 1).astype(I32)
    te = jnp.sum(tile_end[None, :] <= tb[:, None], axis=1).astype(I32)
    tv = (t < n_valid).astype(I32)
    return starts, te, tb, tv


def kernel(x, p, positions, attn_norm, w_in, b_gates, conv_w, conv_b, mlstm_norm, q_norm, w_q_up, kv_norm, w_kv_up, mla_norm, w_out, ffn_norm, w_router, b_router, w_gate_up, b_gate_up, w_down, b_down, ple_norm, w_ple_gate, w_ple_proj, final_norm):
    bsz, s, d = x.shape
    n = bsz * s
    depth = p.shape[0]
    nc = s // MLSTM_CHUNK
    max_rows = n * TOP_K + (n // ROW_TILE) * N_EXPERTS * (SEG_ALIGN - 1) + N_EXPERTS * (EXPERT_TILE - 1)
    n_tiles = max_rows // EXPERT_TILE
    pos2 = positions.reshape(n, 1)
    h = x.reshape(n, d)
    for i in range(depth):
        um, gt, q, k, v = _in_proj(h, pos2, attn_norm[i], w_in[i], b_gates[i], q_norm[i], w_q_up[i],
                                   kv_norm[i], w_kv_up[i])
        ym = _mlstm(um.reshape(bsz, s, -1), gt.reshape(N_GATES, bsz, nc, MLSTM_CHUNK), conv_w[i], conv_b[i],
                    mlstm_norm[i])
        ya = _attention(q.reshape(bsz, s, -1), k.reshape(bsz, s, -1), v.reshape(bsz, s, -1))
        h1, xn, gate, slot, slot_t, seg, sizes = _out_route(ym.reshape(n, -1), ya.reshape(n, -1), h, mla_norm[i],
                                                            w_out[i], ffn_norm[i], w_router[i], b_router[i])
        starts, te, tb, tv = _route_tables(sizes[0, :N_EXPERTS], n_tiles)
        xs = _dispatch(starts, seg, slot_t, xn, n_tiles * EXPERT_TILE)
        ys = _experts(te, tb, tv, xs, w_gate_up[i], b_gate_up[i], w_down[i], b_down[i])
        h = _combine(starts, seg, h1, gate, slot, p[i].reshape(n, -1), ys, ple_norm[i], w_ple_gate[i],
                     w_ple_proj[i], final_norm, final=(i == depth - 1))
    return h.reshape(bsz, s, d)
```

```python
import functools

import jax
import jax.numpy as jnp
from jax import lax
from jax.experimental import pallas as pl
from jax.experimental.pallas import tpu as pltpu

F32 = jnp.float32
BF16 = jnp.bfloat16
I32 = jnp.int32
U32 = jnp.uint32
I16 = jnp.int16

N_MLSTM_HEADS = 4
MLSTM_HEAD_DIM = 128
D_MLSTM = N_MLSTM_HEADS * MLSTM_HEAD_DIM
MLSTM_CHUNK = 128
N_MLA_HEADS = 4
QK_NOPE_DIM = 128
QK_ROPE_DIM = 64
V_HEAD_DIM = 128
D_MLA = N_MLA_HEADS * V_HEAD_DIM
Q_LORA = 256
KV_LORA = 128
ROPE_THETA = 10000.0
N_EXPERTS = 32
TOP_K = 4
SWIGLU_LIMIT = 7.0
SWIGLU_ALPHA = 1.702
EPS = 1e-6
N_GATES = 4 * N_MLSTM_HEADS
OFF_G = 4 * D_MLSTM
OFF_CQ = OFF_G + N_GATES
OFF_CKV = OFF_CQ + Q_LORA
OFF_KR = OFF_CKV + KV_LORA

LANES = 128
SUBLANES = 8
QK_SLAB = 2 * LANES
VMEM_LIMIT_BYTES = 56 * 1024 * 1024
LOG2_E = 1.4426950408889634

ROW_TILE = 512
Q_TILE = 2048
KV_CHUNK = 1024
EXPERT_TILE = 768
FF_CHUNK = 512
CHUNK_UNROLL = 16
SEG_ALIGN = SUBLANES
ROUTE_TILES_PER_STEP = 2
SORT_CHUNK = 256
SORT_ROWS = ROW_TILE * TOP_K + N_EXPERTS * SEG_ALIGN


def _dot(a, b):
    return jnp.dot(a, b, preferred_element_type=F32)


def _dot_nt(a, b):
    return lax.dot_general(a, b, (((1,), (1,)), ((), ())), preferred_element_type=F32)


def _rms(x, g):
    return x * lax.rsqrt(jnp.mean(x * x, axis=-1, keepdims=True) + EPS) * g


def _log_sigmoid(x):
    return jnp.minimum(x, 0.0) - jnp.log(1.0 + jnp.exp(-jnp.abs(x)))


def _scan_lanes(x, reverse, op, identity):
    n = x.shape[-1]
    lane = lax.broadcasted_iota(I32, x.shape, x.ndim - 1)
    sh = 1
    while sh < n:
        if reverse:
            x = op(x, jnp.where(lane < n - sh, pltpu.roll(x, n - sh, x.ndim - 1), identity))
        else:
            x = op(x, jnp.where(lane >= sh, pltpu.roll(x, sh, x.ndim - 1), identity))
        sh *= 2
    return x


def _cumsum_lanes(x, reverse):
    return _scan_lanes(x, reverse, jnp.add, 0.0)


def _cummax_lanes(x, reverse):
    return _scan_lanes(x, reverse, jnp.maximum, -jnp.inf)


def _pack_halves(x):
    w = x.shape[1] // 2
    lo = lax.shift_right_logical(lax.bitcast_convert_type(x[:, :w], U32), jnp.uint32(16))
    hi = lax.bitcast_convert_type(x[:, w:], U32)
    return lo | hi


def _unpack_halves(words):
    lo = lax.bitcast_convert_type(lax.shift_left(words, jnp.uint32(16)), F32)
    hi = lax.bitcast_convert_type(words & jnp.uint32(0xFFFF0000), F32)
    return jnp.concatenate([lo, hi], axis=1).astype(BF16)


def _params(*sem):
    return pltpu.CompilerParams(dimension_semantics=sem, vmem_limit_bytes=VMEM_LIMIT_BYTES)


def _in_proj_kernel(x_ref, pos_ref, an_ref, wm_ref, wr_ref, bgt_ref, qn_ref, wq_ref, kvn_ref,
                    wk_ref, wv_ref, freq_ref,
                    um_ref, gt_ref, q_ref, k_ref, v_ref):
    a = _rms(x_ref[...], an_ref[...]).astype(BF16)
    um_ref[...] = _dot(a, wm_ref[...])
    rest = _dot(a, wr_ref[...])
    cq = rest[:, :Q_LORA]
    ckv = rest[:, Q_LORA:Q_LORA + KV_LORA]
    kr2 = rest[:, Q_LORA + KV_LORA:Q_LORA + KV_LORA + LANES]
    krs2 = rest[:, Q_LORA + KV_LORA + LANES:Q_LORA + KV_LORA + 2 * LANES]
    gt_ref[...] = rest[:, Q_LORA + KV_LORA + 2 * LANES:].T[:N_GATES, :] + bgt_ref[...]
    ang = freq_ref[...] * pos_ref[0].astype(F32)
    cos_t = jnp.cos(ang)
    sin_t = jnp.sin(ang)
    cos_a = jnp.concatenate([cos_t] * (LANES // cos_t.shape[0]), axis=0).T
    sin_a = jnp.concatenate([-sin_t, sin_t] * (LANES // (2 * sin_t.shape[0])), axis=0).T
    scale = (QK_NOPE_DIM + QK_ROPE_DIM) ** -0.5 * LOG2_E
    lane = lax.broadcasted_iota(I32, cos_a.shape, 1)
    rope_mul = jnp.where(lane < QK_ROPE_DIM, cos_a, sin_a) * scale
    qf = _dot(_rms(cq, qn_ref[...]).astype(BF16), wq_ref[...])
    ckvn = _rms(ckv, kvn_ref[...]).astype(BF16)
    kn = _dot(ckvn, wk_ref[...])
    v_ref[...] = _dot(ckvn, wv_ref[...]).astype(BF16)
    k_rope = (kr2 * cos_a + krs2 * sin_a).astype(BF16)
    for h in range(N_MLA_HEADS):
        o = h * QK_SLAB
        q_ref[:, o:o + LANES] = (qf[:, o:o + LANES] * scale).astype(BF16)
        q_ref[:, o + LANES:o + QK_SLAB] = (qf[:, o + LANES:o + QK_SLAB] * rope_mul).astype(BF16)
        k_ref[:, o:o + LANES] = kn[:, h * LANES:(h + 1) * LANES].astype(BF16)
        k_ref[:, o + LANES:o + QK_SLAB] = k_rope


def _in_proj(x2, pos2, attn_norm, w_in, b_gates, q_norm, w_q_up, kv_norm, w_kv_up):
    n, d = x2.shape
    tm = ROW_TILE
    half = QK_ROPE_DIM // 2
    swap = jnp.concatenate([jnp.arange(half, QK_ROPE_DIM), jnp.arange(0, half)])
    w_kr = w_in[:, OFF_KR:OFF_KR + QK_ROPE_DIM]
    w_krs = w_kr[:, swap]
    wm = w_in[:, :OFF_G].astype(BF16)
    w_g = jnp.pad(w_in[:, OFF_G:OFF_CQ], ((0, 0), (0, LANES - N_GATES)))
    wr = jnp.concatenate([w_in[:, OFF_CQ:OFF_KR], w_kr, w_kr, w_krs, w_krs, w_g], axis=1).astype(BF16)
    bgt = b_gates.reshape(N_GATES, 1)
    wq4 = w_q_up.reshape(Q_LORA, N_MLA_HEADS, QK_NOPE_DIM + QK_ROPE_DIM)
    wq_pe = wq4[:, :, QK_NOPE_DIM:]
    wq = jnp.concatenate([wq4, wq_pe[:, :, swap]], axis=2).reshape(Q_LORA, N_MLA_HEADS * QK_SLAB).astype(BF16)
    wkv4 = w_kv_up.reshape(KV_LORA, N_MLA_HEADS, QK_NOPE_DIM + V_HEAD_DIM)
    wk = wkv4[:, :, :QK_NOPE_DIM].reshape(KV_LORA, N_MLA_HEADS * QK_NOPE_DIM).astype(BF16)
    wv = wkv4[:, :, QK_NOPE_DIM:].reshape(KV_LORA, D_MLA).astype(BF16)
    freqs = ROPE_THETA ** (-jnp.arange(0, QK_ROPE_DIM, 2, dtype=F32) / QK_ROPE_DIM)
    freq_c = freqs.reshape(half, 1)
    full = lambda arr: pl.BlockSpec(arr.shape, lambda i: (0,) * arr.ndim)
    rows = lambda w: pl.BlockSpec((tm, w), lambda i: (i, 0))
    consts = [attn_norm.reshape(1, d), wm, wr, bgt, q_norm.reshape(1, Q_LORA), wq,
              kv_norm.reshape(1, KV_LORA), wk, wv, freq_c]
    return pl.pallas_call(
        _in_proj_kernel,
        grid=(n // tm,),
        in_specs=[rows(d), pl.BlockSpec((1, 1, tm), lambda i: (i, 0, 0))] + [full(c) for c in consts],
        out_specs=[rows(OFF_G), pl.BlockSpec((N_GATES, tm), lambda i: (0, i)),
                   rows(N_MLA_HEADS * QK_SLAB), rows(N_MLA_HEADS * QK_SLAB), rows(D_MLA)],
        out_shape=[jax.ShapeDtypeStruct((n, OFF_G), F32), jax.ShapeDtypeStruct((N_GATES, n), F32),
                   jax.ShapeDtypeStruct((n, N_MLA_HEADS * QK_SLAB), BF16),
                   jax.ShapeDtypeStruct((n, N_MLA_HEADS * QK_SLAB), BF16),
                   jax.ShapeDtypeStruct((n, D_MLA), BF16)],
        compiler_params=_params("parallel"),
        name="in_proj",
    )(x2, pos2.reshape(n // tm, 1, tm), *consts)


def _mlstm_kernel(q_ref, k_ref, v_ref, o_ref, g_ref, cwq_ref, cwk_ref, cbq_ref, cbk_ref, nrm_ref,
                  y_ref,
                  qc_ref, kc_ref, va_ref, cst_ref, ent_ref, b_ref, e_ref, r_ref, ew_ref, mw_ref, bt_ref, mp_ref):
    L = MLSTM_CHUNK
    dh = MLSTM_HEAD_DIM
    nc = q_ref.shape[1] // L
    s_len = q_ref.shape[1]
    h = pl.program_id(1)

    for d in range(2):
        ig = g_ref[2 * d * N_MLSTM_HEADS + h, 0]
        fg = g_ref[(2 * d + 1) * N_MLSTM_HEADS + h, 0]
        b = _cumsum_lanes(_log_sigmoid(fg), reverse=(d == 1))
        btot = b[:, L - 1:L] if d == 0 else b[:, 0:1]
        r = ig - b
        w = btot + r
        mw = jnp.max(w, axis=-1, keepdims=True)
        b_ref[d] = b
        e_ref[d] = b + _cummax_lanes(r, reverse=(d == 1))
        r_ref[d] = r
        ew_ref[d] = jnp.exp(w - mw)
        mw_ref[d] = jnp.broadcast_to(mw, (nc, L))
        bt_ref[d] = jnp.broadcast_to(btot, (nc, L))

    row = lax.broadcasted_iota(I32, (L, dh), 0)

    def conv_silu(ref, cw_ref, cb_ref, c):
        start = pl.multiple_of(c * L, L)
        x = ref[0, pl.ds(start, L), :]
        prev_row = jnp.where(c > 0, ref[0, pl.ds(jnp.maximum(start - 1, 0), 1), :], 0.0)
        next_row = jnp.where(c < nc - 1, ref[0, pl.ds(jnp.minimum(start + L, s_len - 1), 1), :], 0.0)
        x_prev = jnp.where(row == 0, prev_row, pltpu.roll(x, 1, 0))
        x_next = jnp.where(row == L - 1, next_row, pltpu.roll(x, L - 1, 0))
        y = cw_ref[0:1, :] * x_prev + cw_ref[1:2, :] * x + cw_ref[2:3, :] * x_next + cb_ref[...]
        return y * jax.nn.sigmoid(y)

    ones_blk = jnp.ones((L, dh), BF16)

    def pass1(c, carry):
        start = pl.multiple_of(c * L, L)
        qc_ref[pl.ds(start, L), :] = conv_silu(q_ref, cwq_ref, cbq_ref, c).astype(BF16)
        kk = conv_silu(k_ref, cwk_ref, cbk_ref, c) * (dh ** -0.5)
        kc_ref[pl.ds(start, L), :] = kk.astype(BF16)
        va = jnp.concatenate([v_ref[0, pl.ds(start, L), :].astype(BF16), ones_blk], axis=1)
        va_ref[pl.ds(start, L), :] = va
        kt = kk.T
        for d in range(2):
            kw_t = (kt * ew_ref[d, pl.ds(c, 1), :]).astype(BF16)
            cst_ref[d, c] = _dot(kw_t, va)
        return carry

    lax.fori_loop(0, nc, pass1, 0, unroll=CHUNK_UNROLL)

    ent_ref[:, 0] = jnp.zeros((2, dh, 2 * dh), F32)

    def scan(i, carry):
        out = []
        for d in range(2):
            m = carry[d]
            c = i if d == 0 else nc - 1 - i
            mw = mw_ref[d, pl.ds(c, 1), :]
            bt = bt_ref[d, pl.ds(c, 1), :]
            m_new = jnp.maximum(bt + m, mw)
            a = jnp.exp(bt + m - m_new)[:, 0:1]
            cc = jnp.exp(mw - m_new)[:, 0:1]
            ent_ref[d, i + 1] = a * ent_ref[d, i] + cc * cst_ref[d, c]
            mp_ref[d, pl.ds(c, 1), :] = m
            out.append(m_new)
        return tuple(out)

    lax.fori_loop(0, nc, scan, (jnp.zeros((1, L), F32), jnp.zeros((1, L), F32)))

    ti = lax.broadcasted_iota(I32, (L, L), 0)
    si = lax.broadcasted_iota(I32, (L, L), 1)
    masks = (si <= ti, si >= ti)

    def pass3(c, carry):
        start = pl.multiple_of(c * L, L)
        q = qc_ref[pl.ds(start, L), :]
        k = kc_ref[pl.ds(start, L), :]
        va = va_ref[pl.ds(start, L), :]
        qk = _dot_nt(q, k)
        hsum = jnp.zeros((L, dh), F32)
        for d in range(2):
            bmat = jnp.broadcast_to(b_ref[d, pl.ds(c, 1), :], (L, L)).T
            emat = jnp.broadcast_to(e_ref[d, pl.ds(c, 1), :], (L, L)).T
            dmat = jnp.where(masks[d], bmat + r_ref[d, pl.ds(c, 1), :], -jnp.inf)
            inter = bmat + mp_ref[d, pl.ds(c, 1), :]
            m_t = jnp.maximum(inter, emat)
            sc = qk * jnp.exp(dmat - m_t)
            a = jnp.exp(inter - m_t)
            intra = _dot(sc.astype(BF16), va)
            cross = _dot(q, ent_ref[d, c if d == 0 else nc - 1 - c].astype(BF16))
            num = intra[:, :dh] + a * cross[:, :dh]
            den = intra[:, dh:] + a * cross[:, dh:]
            hsum = hsum + num / jnp.maximum(jnp.abs(den), jnp.exp(-m_t))
        hn = _rms(hsum, nrm_ref[...])
        y_ref[0, pl.ds(start, L), :] = hn * jax.nn.sigmoid(o_ref[0, pl.ds(start, L), :])
        return carry

    lax.fori_loop(0, nc, pass3, 0, unroll=CHUNK_UNROLL)


def _mlstm(um3, gt4, conv_w, conv_b, mlstm_norm):
    bsz, s, _ = um3.shape
    H, dh, L = N_MLSTM_HEADS, MLSTM_HEAD_DIM, MLSTM_CHUNK
    nc = s // L
    col = lambda off: pl.BlockSpec((1, s, dh), lambda b, h: (b, 0, off + h))
    vec = lambda rows, off: pl.BlockSpec((rows, dh), lambda b, h: (0, off + h))
    cb = conv_b.reshape(1, 2 * D_MLSTM)
    return pl.pallas_call(
        _mlstm_kernel,
        grid=(bsz, H),
        in_specs=[col(0), col(H), col(2 * H), col(3 * H),
                  pl.BlockSpec((N_GATES, 1, nc, L), lambda b, h: (0, b, 0, 0)),
                  vec(3, 0), vec(3, H), vec(1, 0), vec(1, H), vec(1, 0)],
        out_specs=pl.BlockSpec((1, s, dh), lambda b, h: (b, 0, h)),
        out_shape=jax.ShapeDtypeStruct((bsz, s, D_MLSTM), F32),
        scratch_shapes=[pltpu.VMEM((s, dh), BF16), pltpu.VMEM((s, dh), BF16), pltpu.VMEM((s, 2 * dh), BF16),
                        pltpu.VMEM((2, nc, dh, 2 * dh), F32), pltpu.VMEM((2, nc + 1, dh, 2 * dh), F32)]
                       + [pltpu.VMEM((2, nc, L), F32) for _ in range(7)],
        compiler_params=_params("parallel", "parallel"),
        name="mlstm",
    )(um3, um3, um3, um3, gt4, conv_w, conv_w, cb, cb, mlstm_norm.reshape(1, D_MLSTM))


def _attn_kernel(q_ref, k_ref, v_ref, o_ref):
    tq = q_ref.shape[1]
    q = q_ref[0]
    m = jnp.full((tq, 1), -jnp.inf, F32)
    l = jnp.zeros((tq, 1), F32)
    acc = jnp.zeros((tq, V_HEAD_DIM), F32)
    chunk = min(KV_CHUNK, k_ref.shape[1])
    for c in range(k_ref.shape[1] // chunk):
        keys = slice(c * chunk, (c + 1) * chunk)
        s = _dot_nt(q, k_ref[0, keys, :])
        m_new = jnp.maximum(m, jnp.max(s, axis=-1, keepdims=True))
        alpha = jnp.exp2(m - m_new)
        p = jnp.exp2(s - m_new)
        l = alpha * l + jnp.sum(p, axis=-1, keepdims=True)
        acc = alpha * acc + _dot(p.astype(BF16), v_ref[0, keys, :])
        m = m_new
    o_ref[0] = acc / l


def _attention(q3, k3, v3):
    bsz, s, _ = q3.shape
    tq = min(Q_TILE, s)
    return pl.pallas_call(
        _attn_kernel,
        grid=(bsz, N_MLA_HEADS, s // tq),
        in_specs=[pl.BlockSpec((1, tq, QK_SLAB), lambda b, h, i: (b, i, h)),
                  pl.BlockSpec((1, s, QK_SLAB), lambda b, h, i: (b, 0, h)),
                  pl.BlockSpec((1, s, V_HEAD_DIM), lambda b, h, i: (b, 0, h))],
        out_specs=pl.BlockSpec((1, tq, V_HEAD_DIM), lambda b, h, i: (b, i, h)),
        out_shape=jax.ShapeDtypeStruct((bsz, s, D_MLA), F32),
        compiler_params=_params("parallel", "parallel", "parallel"),
        name="attention",
    )(q3, k3, v3)


def _route_tile(logits, carry):
    tm = logits.shape[1]
    erow = lax.broadcasted_iota(I32, logits.shape, 0)
    work = logits
    vals, hots = [], []
    for k in range(TOP_K):
        mx = jnp.max(work, axis=0, keepdims=True)
        idx = jnp.min(jnp.where(work == mx, erow, N_EXPERTS), axis=0, keepdims=True)
        hot = erow == idx
        work = jnp.where(hot, -jnp.inf, work)
        vals.append(mx)
        hots.append(hot)
    exps = [jnp.exp(v - vals[0]) for v in vals]
    tot = exps[0] + exps[1] + exps[2] + exps[3]
    multi = (hots[0] | hots[1] | hots[2] | hots[3]).astype(BF16)
    ti = lax.broadcasted_iota(I32, (tm, tm), 0)
    tj = lax.broadcasted_iota(I32, (tm, tm), 1)
    local_rank = _dot(multi, (ti < tj).astype(BF16))
    multi_rows = jnp.concatenate([multi, jnp.zeros((LANES - N_EXPERTS, tm), BF16)], axis=0)
    count = _dot_nt(jnp.ones((SUBLANES, tm), BF16), multi_rows)[0:1]
    padded = jnp.ceil(count * (1.0 / SEG_ALIGN)) * SEG_ALIGN
    filled = _cumsum_lanes(padded, reverse=False)
    local_start = filled - padded
    start_col = jnp.broadcast_to(local_start, (SUBLANES, LANES)).T[:N_EXPERTS, 0:1]
    slot_all = local_rank + start_col
    krow = lax.broadcasted_iota(I32, (SUBLANES, tm), 0)
    packed = jnp.zeros((SUBLANES, tm), F32)
    for k in range(TOP_K):
        sk = jnp.sum(jnp.where(hots[k], slot_all, 0.0), axis=0, keepdims=True)
        packed = jnp.where(krow == k, exps[k] / tot, jnp.where(krow == TOP_K + k, sk, packed))
    srow = lax.broadcasted_iota(I32, (SUBLANES, LANES), 0)
    seg = jnp.where(srow == 0, padded, jnp.where(srow == 1, local_start, jnp.where(srow == 2, carry,
                                                                                  filled[:, LANES - 1:LANES])))
    return packed, seg, padded


def _out_route_kernel(ym_ref, ya_ref, x_ref, mn_ref, wom_ref, woa_ref, fn_ref, wr_ref, br_ref,
                      h_ref, xn_ref, gate_ref, slot_ref, slot_t_ref, seg_ref, size_ref,
                      carry_ref):
    i = pl.program_id(0)
    tm = ROW_TILE

    @pl.when(i == 0)
    def _():
        carry_ref[...] = jnp.zeros_like(carry_ref)

    ya = _rms(ya_ref[...], mn_ref[...])
    h1 = x_ref[...] + _dot(ym_ref[...].astype(BF16), wom_ref[...]) + _dot(ya.astype(BF16), woa_ref[...])
    h_ref[...] = h1
    xn = _rms(h1, fn_ref[...])
    xn_hi = xn.astype(BF16)
    xn_ref[...] = xn_hi
    xn_lo = (xn - xn_hi.astype(F32)).astype(BF16)
    both = _dot_nt(wr_ref[...], xn_hi)
    logits = (both[:N_EXPERTS] + both[N_EXPERTS:] + _dot_nt(wr_ref[:N_EXPERTS, :], xn_lo)) + br_ref[...]
    carry = carry_ref[...]
    for s in range(x_ref.shape[0] // tm):
        packed, seg, padded = _route_tile(logits[:, s * tm:(s + 1) * tm], carry)
        krow = lax.broadcasted_iota(I32, packed.shape, 0)
        slot_t_ref[s] = jnp.where(krow < TOP_K, pltpu.roll(packed, TOP_K, 0), 0.0).astype(I32)
        cols = packed.T
        gate_ref[s * tm:(s + 1) * tm, :] = cols[:, :TOP_K]
        slot_ref[s * tm:(s + 1) * tm, :] = cols[:, TOP_K:2 * TOP_K].astype(I32)
        seg_ref[s] = seg.astype(I32)
        carry = carry + padded
    carry_ref[...] = carry
    size_ref[...] = carry.astype(I32)


def _out_route(ym2, ya2, x2, mla_norm, w_out, ffn_norm, w_router, b_router):
    n, d = x2.shape
    sub = min(ROUTE_TILES_PER_STEP, n // ROW_TILE)
    tm = ROW_TILE * sub
    nt = n // ROW_TILE
    wom = w_out[:D_MLSTM].astype(BF16)
    woa = w_out[D_MLSTM:].astype(BF16)
    wr_hi = w_router.T.astype(BF16)
    wr_lo = (w_router.T - wr_hi.astype(F32)).astype(BF16)
    wr = jnp.concatenate([wr_hi, wr_lo], axis=0)
    br = b_router.reshape(N_EXPERTS, 1)
    full = lambda arr: pl.BlockSpec(arr.shape, lambda i: (0,) * arr.ndim)
    rows = lambda w: pl.BlockSpec((tm, w), lambda i: (i, 0))
    consts = [mla_norm.reshape(1, D_MLA), wom, woa, ffn_norm.reshape(1, d), wr, br]
    return pl.pallas_call(
        _out_route_kernel,
        grid=(nt // sub,),
        in_specs=[rows(D_MLSTM), rows(D_MLA), rows(d)] + [full(c) for c in consts],
        out_specs=[rows(d), rows(d), rows(TOP_K), rows(TOP_K),
                   pl.BlockSpec((sub, SUBLANES, ROW_TILE), lambda i: (i, 0, 0)),
                   pl.BlockSpec((sub, SUBLANES, LANES), lambda i: (i, 0, 0)),
                   pl.BlockSpec((1, LANES), lambda i: (0, 0))],
        out_shape=[jax.ShapeDtypeStruct((n, d), F32), jax.ShapeDtypeStruct((n, d), BF16),
                   jax.ShapeDtypeStruct((n, TOP_K), F32), jax.ShapeDtypeStruct((n, TOP_K), I32),
                   jax.ShapeDtypeStruct((nt, SUBLANES, ROW_TILE), I32), jax.ShapeDtypeStruct((nt, SUBLANES, LANES), I32),
                   jax.ShapeDtypeStruct((1, LANES), I32)],
        scratch_shapes=[pltpu.VMEM((1, LANES), F32)],
        compiler_params=_params("arbitrary"),
        name="out_route",
    )(ym2, ya2, x2, *consts)


def _segment_copies(seg_ref, starts_ref, local_ref, global_ref, sem, to_global, wait):
    def copy(src, dst, rows):
        loc = local_ref.at[pl.ds(pl.multiple_of(src, SEG_ALIGN), rows)]
        glo = global_ref.at[pl.ds(pl.multiple_of(dst, SEG_ALIGN), rows)]
        return pltpu.make_async_copy(loc, glo, sem) if to_global else pltpu.make_async_copy(glo, loc, sem)

    if wait:
        copy(0, 0, pl.multiple_of(seg_ref[0, 3, 0], SEG_ALIGN)).wait()
        return

    def per_expert(e, carry):
        size = seg_ref[0, 0, e]
        src = seg_ref[0, 1, e]
        dst = starts_ref[e] + seg_ref[0, 2, e]
        off = 0
        rows = ROW_TILE
        while rows >= SEG_ALIGN:
            @pl.when((size & rows) != 0)
            def _(off=off, rows=rows):
                copy(src + off, dst + off, rows).start()

            off = off + (size & rows)
            rows //= 2
        return carry

    lax.fori_loop(0, N_EXPERTS, per_expert, 0)


def _dispatch_kernel(starts_ref, sizes_ref, seg_ref, segp_ref, slot_t_ref, xn_ref, xs_ref, sort_ref, zero_ref,
                     sem, zsem):
    i = pl.program_id(0)
    last_step = pl.num_programs(0) - 1
    tm = xn_ref.shape[0]
    tmx = zero_ref.shape[0]
    cap = sort_ref.shape[1]

    def clears(wait):
        def clear(row, rows):
            cp = pltpu.make_async_copy(zero_ref.at[pl.ds(0, rows)],
                                       xs_ref.at[pl.ds(pl.multiple_of(row, SEG_ALIGN), rows)], zsem)
            if wait:
                cp.wait()
            else:
                cp.start()

        def clear_group(e, carry):
            row = starts_ref[e] + sizes_ref[e]
            pad = starts_ref[e + 1] - row
            rows = pl.next_power_of_2(tmx) // 2
            while rows >= SEG_ALIGN:
                @pl.when((pad & rows) != 0)
                def _(row=row, rows=rows):
                    clear(row, rows)

                row = row + (pad & rows)
                rows //= 2
            return carry

        def clear_tail(t, carry):
            clear(starts_ref[N_EXPERTS] + t * tmx, tmx)
            return carry

        lax.fori_loop(0, N_EXPERTS, clear_group, 0)
        lax.fori_loop(0, (xs_ref.shape[0] - starts_ref[N_EXPERTS]) // tmx, clear_tail, 0)

    @pl.when(i == 0)
    def _():
        zero_ref[...] = jnp.zeros_like(zero_ref)
        clears(False)

    slot = i % 2
    slots16 = slot_t_ref[0].astype(I16)
    for r0 in range(0, cap, SORT_CHUNK):
        pos = (lax.broadcasted_iota(I32, (SORT_CHUNK, tm), 0) + r0).astype(I16)
        hit = pos == slots16[0:1, :]
        for k in range(1, TOP_K):
            hit = hit | (pos == slots16[k:k + 1, :])
        sort_ref[slot, r0:r0 + SORT_CHUNK, :] = _pack_halves(_dot(hit.astype(BF16), xn_ref[...]))
    _segment_copies(seg_ref, starts_ref, sort_ref.at[slot], xs_ref, sem.at[slot], True, False)

    @pl.when(i >= 1)
    def _():
        _segment_copies(segp_ref, starts_ref, sort_ref.at[1 - slot], xs_ref, sem.at[1 - slot], True, True)

    @pl.when(i == last_step)
    def _():
        _segment_copies(seg_ref, starts_ref, sort_ref.at[slot], xs_ref, sem.at[slot], True, True)
        clears(True)


def _dispatch(starts, sizes, seg, slot_t, xn, n_rows):
    n, d = xn.shape
    tm = ROW_TILE
    nt = n // tm
    smem = lambda f: pl.BlockSpec((1, SUBLANES, LANES), f, memory_space=pltpu.SMEM)
    any_spec = pl.BlockSpec(memory_space=pl.ANY)
    return pl.pallas_call(
        _dispatch_kernel,
        grid_spec=pltpu.PrefetchScalarGridSpec(
            num_scalar_prefetch=2,
            grid=(nt,),
            in_specs=[smem(lambda i, *_: (i, 0, 0)), smem(lambda i, *_: (jnp.maximum(i - 1, 0), 0, 0)),
                      pl.BlockSpec((1, SUBLANES, tm), lambda i, *_: (i, 0, 0)),
                      pl.BlockSpec((tm, d), lambda i, *_: (i, 0))],
            out_specs=any_spec,
            scratch_shapes=[pltpu.VMEM((2, SORT_ROWS, d // 2), U32), pltpu.VMEM((EXPERT_TILE, d // 2), U32),
                            pltpu.SemaphoreType.DMA((2,)), pltpu.SemaphoreType.DMA],
        ),
        out_shape=jax.ShapeDtypeStruct((n_rows, d // 2), U32),
        compiler_params=_params("arbitrary"),
        name="dispatch",
    )(starts, sizes, seg, seg, slot_t, xn)


def _experts_kernel(te_ref, tb_ref, tv_ref, xs_ref, wgu_ref, bgu_ref, wd_ref, bd_ref, out_ref,
                    wgu_bf, wd_bf):
    i = pl.program_id(0)
    de = wd_ref.shape[1]
    prev = te_ref[jnp.maximum(i - 1, 0)]

    @pl.when(jnp.logical_or(i == 0, te_ref[i] != prev))
    def _():
        wgu_bf[...] = wgu_ref[0].astype(BF16)
        wd_bf[...] = wd_ref[0].astype(BF16)

    @pl.when(tv_ref[i] == 1)
    def _():
        x = _unpack_halves(xs_ref[...])
        acc = jnp.zeros((xs_ref.shape[0], bd_ref.shape[2]), F32)
        for j in range(de // FF_CHUNK):
            cols = slice(j * FF_CHUNK, (j + 1) * FF_CHUNK)
            up_cols = slice(de + j * FF_CHUNK, de + (j + 1) * FF_CHUNK)
            g = _dot(x, wgu_bf[:, cols]) + bgu_ref[0, :, cols]
            u = _dot(x, wgu_bf[:, up_cols]) + bgu_ref[0, :, up_cols]
            g = jnp.minimum(g, SWIGLU_LIMIT)
            u = jnp.clip(u, -SWIGLU_LIMIT, SWIGLU_LIMIT)
            hm = (u + 1.0) * (g * jax.nn.sigmoid(g * SWIGLU_ALPHA))
            acc = acc + _dot(hm.astype(BF16), wd_bf[cols, :])
        out_ref[...] = _pack_halves((acc + bd_ref[0]).astype(BF16).astype(F32))

    @pl.when(tv_ref[i] == 0)
    def _():
        out_ref[...] = jnp.zeros_like(out_ref)


def _experts(tile_e, tile_b, tile_v, xs, w_gate_up, b_gate_up, w_down, b_down):
    n_rows, dw = xs.shape
    tmx = EXPERT_TILE
    ne, d, de2 = w_gate_up.shape
    de = de2 // 2
    return pl.pallas_call(
        _experts_kernel,
        grid_spec=pltpu.PrefetchScalarGridSpec(
            num_scalar_prefetch=3,
            grid=(n_rows // tmx,),
            in_specs=[pl.BlockSpec((tmx, dw), lambda i, te, tb, tv: (tb[i], 0)),
                      pl.BlockSpec((1, d, de2), lambda i, te, tb, tv: (te[i], 0, 0)),
                      pl.BlockSpec((1, 1, de2), lambda i, te, tb, tv: (te[i], 0, 0)),
                      pl.BlockSpec((1, de, d), lambda i, te, tb, tv: (te[i], 0, 0)),
                      pl.BlockSpec((1, 1, d), lambda i, te, tb, tv: (te[i], 0, 0))],
            out_specs=pl.BlockSpec((tmx, dw), lambda i, te, tb, tv: (i, 0)),
            scratch_shapes=[pltpu.VMEM((d, de2), BF16), pltpu.VMEM((de, d), BF16)],
        ),
        out_shape=jax.ShapeDtypeStruct((n_rows, dw), U32),
        compiler_params=_params("arbitrary"),
        name="experts",
    )(tile_e, tile_b, tile_v, xs, w_gate_up, b_gate_up.reshape(ne, 1, de2), w_down, b_down.reshape(ne, 1, d))


def _combine_kernel(starts_ref, seg_ref, segn_ref, h_ref, gate_ref, slot_ref, p_ref, ys_ref, pn_ref, wg_ref,
                    wp_ref, fn_ref, out_ref, ybuf, sem, *, final):
    i = pl.program_id(0)
    nt = pl.num_programs(0)
    tm = h_ref.shape[0]
    cap = ybuf.shape[1]

    @pl.when(i == 0)
    def _():
        ybuf[...] = jnp.zeros_like(ybuf)
        _segment_copies(seg_ref, starts_ref, ybuf.at[0], ys_ref, sem.at[0], False, False)

    slot = i % 2

    @pl.when(i + 1 < nt)
    def _():
        _segment_copies(segn_ref, starts_ref, ybuf.at[1 - slot], ys_ref, sem.at[1 - slot], False, False)

    _segment_copies(seg_ref, starts_ref, ybuf.at[slot], ys_ref, sem.at[slot], False, True)

    gate = gate_ref[...].astype(BF16)
    slot16 = slot_ref[...].astype(I16)
    h2 = h_ref[...]
    for c0 in range(0, cap, SORT_CHUNK):
        pos = (lax.broadcasted_iota(I32, (tm, SORT_CHUNK), 1) + c0).astype(I16)
        weights = jnp.zeros((tm, SORT_CHUNK), BF16)
        for k in range(TOP_K):
            weights = jnp.where(pos == slot16[:, k:k + 1], gate[:, k:k + 1], weights)
        h2 = h2 + _dot(weights, _unpack_halves(ybuf[slot, c0:c0 + SORT_CHUNK, :]))
    hn = _rms(h2, pn_ref[...]).astype(BF16)
    sg = jax.nn.sigmoid(_dot(hn, wg_ref[...]))
    h3 = h2 + sg * _dot(p_ref[...].astype(BF16), wp_ref[...])
    out_ref[...] = _rms(h3, fn_ref[...]) if final else h3


def _combine(starts, seg, h1, gate, slot, p2, ys, ple_norm, w_ple_gate, w_ple_proj, final_norm, final):
    n, d = h1.shape
    tm = ROW_TILE
    nt = n // tm
    wg = w_ple_gate.astype(BF16)
    wp = w_ple_proj.astype(BF16)
    smem = lambda f: pl.BlockSpec((1, SUBLANES, LANES), f, memory_space=pltpu.SMEM)
    full = lambda arr: pl.BlockSpec(arr.shape, lambda i, *_: (0,) * arr.ndim)
    rows = lambda w: pl.BlockSpec((tm, w), lambda i, *_: (i, 0))
    consts = [ple_norm.reshape(1, d), wg, wp, final_norm.reshape(1, d)]
    return pl.pallas_call(
        functools.partial(_combine_kernel, final=final),
        grid_spec=pltpu.PrefetchScalarGridSpec(
            num_scalar_prefetch=1,
            grid=(nt,),
            in_specs=[smem(lambda i, *_: (i, 0, 0)), smem(lambda i, *_: (jnp.minimum(i + 1, nt - 1), 0, 0)),
                      rows(d), rows(TOP_K), rows(TOP_K), rows(p2.shape[1]), pl.BlockSpec(memory_space=pl.ANY)]
                     + [full(c) for c in consts],
            out_specs=rows(d),
            scratch_shapes=[pltpu.VMEM((2, SORT_ROWS, d // 2), U32), pltpu.SemaphoreType.DMA((2,))],
        ),
        out_shape=jax.ShapeDtypeStruct((n, d), F32),
        compiler_params=_params("arbitrary"),
        name="combine",
    )(starts, seg, seg, h1, gate, slot, p2, ys, *consts)


def _route_tables(sizes, n_tiles):
    tmx = EXPERT_TILE
    tile_end = jnp.cumsum((sizes + tmx - 1) // tmx)
    starts = jnp.concatenate([jnp.zeros((1,), I32), tile_end * tmx]).astype(I32)
    n_valid = tile_end[-1]
    t = jnp.arange(n_tiles, dtype=I32)
    tb = jnp.minimum(t, n_valid - 1).astype(I32)
    te = jnp.sum(tile_end[None, :] <= tb[:, None], axis=1).astype(I32)
    tv = (t < n_valid).astype(I32)
    return starts, te, tb, tv


def kernel(x, p, positions, attn_norm, w_in, b_gates, conv_w, conv_b, mlstm_norm, q_norm, w_q_up, kv_norm, w_kv_up, mla_norm, w_out, ffn_norm, w_router, b_router, w_gate_up, b_gate_up, w_down, b_down, ple_norm, w_ple_gate, w_ple_proj, final_norm):
    bsz, s, d = x.shape
    n = bsz * s
    depth = p.shape[0]
    nc = s // MLSTM_CHUNK
    max_rows = n * TOP_K + (n // ROW_TILE) * N_EXPERTS * (SEG_ALIGN - 1) + N_EXPERTS * (EXPERT_TILE - 1)
    n_tiles = max_rows // EXPERT_TILE
    pos2 = positions.reshape(n, 1)
    h = x.reshape(n, d)
    for i in range(depth):
        um, gt, q, k, v = _in_proj(h, pos2, attn_norm[i], w_in[i], b_gates[i], q_norm[i], w_q_up[i],
                                   kv_norm[i], w_kv_up[i])
        ym = _mlstm(um.reshape(bsz, s, -1), gt.reshape(N_GATES, bsz, nc, MLSTM_CHUNK), conv_w[i], conv_b[i],
                    mlstm_norm[i])
        ya = _attention(q.reshape(bsz, s, -1), k.reshape(bsz, s, -1), v.reshape(bsz, s, -1))
        h1, xn, gate, slot, slot_t, seg, sizes = _out_route(ym.reshape(n, -1), ya.reshape(n, -1), h, mla_norm[i],
                                                            w_out[i], ffn_norm[i], w_router[i], b_router[i])
        starts, te, tb, tv = _route_tables(sizes[0, :N_EXPERTS], n_tiles)
        xs = _dispatch(starts, sizes[0, :N_EXPERTS], seg, slot_t, xn, n_tiles * EXPERT_TILE)
        ys = _experts(te, tb, tv, xs, w_gate_up[i], b_gate_up[i], w_down[i], b_down[i])
        h = _combine(starts, seg, h1, gate, slot, p[i].reshape(n, -1), ys, ple_norm[i], w_ple_gate[i],
                     w_ple_proj[i], final_norm, final=(i == depth - 1))
    return h.reshape(bsz, s, d)
```

```python
import functools

import jax
import jax.numpy as jnp
from jax import lax
from jax.experimental import pallas as pl
from jax.experimental.pallas import tpu as pltpu

F32 = jnp.float32
BF16 = jnp.bfloat16
I32 = jnp.int32
U32 = jnp.uint32
I16 = jnp.int16

N_MLSTM_HEADS = 4
MLSTM_HEAD_DIM = 128
D_MLSTM = N_MLSTM_HEADS * MLSTM_HEAD_DIM
MLSTM_CHUNK = 128
N_MLA_HEADS = 4
QK_NOPE_DIM = 128
QK_ROPE_DIM = 64
V_HEAD_DIM = 128
D_MLA = N_MLA_HEADS * V_HEAD_DIM
Q_LORA = 256
KV_LORA = 128
ROPE_THETA = 10000.0
N_EXPERTS = 32
TOP_K = 4
SWIGLU_LIMIT = 7.0
SWIGLU_ALPHA = 1.702
EPS = 1e-6
N_GATES = 4 * N_MLSTM_HEADS
OFF_G = 4 * D_MLSTM
OFF_CQ = OFF_G + N_GATES
OFF_CKV = OFF_CQ + Q_LORA
OFF_KR = OFF_CKV + KV_LORA

LANES = 128
SUBLANES = 8
QK_SLAB = 2 * LANES
VMEM_LIMIT_BYTES = 56 * 1024 * 1024
LOG2_E = 1.4426950408889634

ROW_TILE = 512
Q_TILE = 2048
KV_CHUNK = 1024
EXPERT_TILE = 768
FF_CHUNK = 512
CHUNK_UNROLL = 16
SEG_ALIGN = SUBLANES
ROUTE_TILES_PER_STEP = 2
SORT_CHUNK = 256
SORT_ROWS = ROW_TILE * TOP_K + N_EXPERTS * SEG_ALIGN


def _dot(a, b):
    return jnp.dot(a, b, preferred_element_type=F32)


def _dot_nt(a, b):
    return lax.dot_general(a, b, (((1,), (1,)), ((), ())), preferred_element_type=F32)


def _rms(x, g):
    return x * lax.rsqrt(jnp.mean(x * x, axis=-1, keepdims=True) + EPS) * g


def _log_sigmoid(x):
    return jnp.minimum(x, 0.0) - jnp.log(1.0 + jnp.exp(-jnp.abs(x)))


def _scan_lanes(x, reverse, op, identity):
    n = x.shape[-1]
    lane = lax.broadcasted_iota(I32, x.shape, x.ndim - 1)
    sh = 1
    while sh < n:
        if reverse:
            x = op(x, jnp.where(lane < n - sh, pltpu.roll(x, n - sh, x.ndim - 1), identity))
        else:
            x = op(x, jnp.where(lane >= sh, pltpu.roll(x, sh, x.ndim - 1), identity))
        sh *= 2
    return x


def _cumsum_lanes(x, reverse):
    return _scan_lanes(x, reverse, jnp.add, 0.0)


def _cummax_lanes(x, reverse):
    return _scan_lanes(x, reverse, jnp.maximum, -jnp.inf)


def _pack_halves(x):
    w = x.shape[1] // 2
    lo = lax.shift_right_logical(lax.bitcast_convert_type(x[:, :w], U32), jnp.uint32(16))
    hi = lax.bitcast_convert_type(x[:, w:], U32)
    return lo | hi


def _unpack_halves(words):
    lo = lax.bitcast_convert_type(lax.shift_left(words, jnp.uint32(16)), F32)
    hi = lax.bitcast_convert_type(words & jnp.uint32(0xFFFF0000), F32)
    return jnp.concatenate([lo, hi], axis=1).astype(BF16)


def _params(*sem):
    return pltpu.CompilerParams(dimension_semantics=sem, vmem_limit_bytes=VMEM_LIMIT_BYTES)


def _in_proj_kernel(x_ref, pos_ref, an_ref, wm_ref, wr_ref, bgt_ref, qn_ref, wq_ref, kvn_ref,
                    wk_ref, wv_ref, freq_ref,
                    um_ref, gt_ref, q_ref, k_ref, v_ref):
    a = _rms(x_ref[...], an_ref[...]).astype(BF16)
    um_ref[...] = _dot(a, wm_ref[...])
    rest = _dot(a, wr_ref[...])
    cq = rest[:, :Q_LORA]
    ckv = rest[:, Q_LORA:Q_LORA + KV_LORA]
    kr2 = rest[:, Q_LORA + KV_LORA:Q_LORA + KV_LORA + LANES]
    krs2 = rest[:, Q_LORA + KV_LORA + LANES:Q_LORA + KV_LORA + 2 * LANES]
    gt_ref[...] = rest[:, Q_LORA + KV_LORA + 2 * LANES:].T[:N_GATES, :] + bgt_ref[...]
    ang = freq_ref[...] * pos_ref[0].astype(F32)
    cos_t = jnp.cos(ang)
    sin_t = jnp.sin(ang)
    cos_a = jnp.concatenate([cos_t] * (LANES // cos_t.shape[0]), axis=0).T
    sin_a = jnp.concatenate([-sin_t, sin_t] * (LANES // (2 * sin_t.shape[0])), axis=0).T
    scale = (QK_NOPE_DIM + QK_ROPE_DIM) ** -0.5 * LOG2_E
    lane = lax.broadcasted_iota(I32, cos_a.shape, 1)
    rope_mul = jnp.where(lane < QK_ROPE_DIM, cos_a, sin_a) * scale
    qf = _dot(_rms(cq, qn_ref[...]).astype(BF16), wq_ref[...])
    ckvn = _rms(ckv, kvn_ref[...]).astype(BF16)
    kn = _dot(ckvn, wk_ref[...])
    v_ref[...] = _dot(ckvn, wv_ref[...]).astype(BF16)
    k_rope = (kr2 * cos_a + krs2 * sin_a).astype(BF16)
    for h in range(N_MLA_HEADS):
        o = h * QK_SLAB
        q_ref[:, o:o + LANES] = (qf[:, o:o + LANES] * scale).astype(BF16)
        q_ref[:, o + LANES:o + QK_SLAB] = (qf[:, o + LANES:o + QK_SLAB] * rope_mul).astype(BF16)
        k_ref[:, o:o + LANES] = kn[:, h * LANES:(h + 1) * LANES].astype(BF16)
        k_ref[:, o + LANES:o + QK_SLAB] = k_rope


def _in_proj(x2, pos2, attn_norm, w_in, b_gates, q_norm, w_q_up, kv_norm, w_kv_up):
    n, d = x2.shape
    tm = ROW_TILE
    half = QK_ROPE_DIM // 2
    swap = jnp.concatenate([jnp.arange(half, QK_ROPE_DIM), jnp.arange(0, half)])
    w_kr = w_in[:, OFF_KR:OFF_KR + QK_ROPE_DIM]
    w_krs = w_kr[:, swap]
    wm = w_in[:, :OFF_G].astype(BF16)
    w_g = jnp.pad(w_in[:, OFF_G:OFF_CQ], ((0, 0), (0, LANES - N_GATES)))
    wr = jnp.concatenate([w_in[:, OFF_CQ:OFF_KR], w_kr, w_kr, w_krs, w_krs, w_g], axis=1).astype(BF16)
    bgt = b_gates.reshape(N_GATES, 1)
    wq4 = w_q_up.reshape(Q_LORA, N_MLA_HEADS, QK_NOPE_DIM + QK_ROPE_DIM)
    wq_pe = wq4[:, :, QK_NOPE_DIM:]
    wq = jnp.concatenate([wq4, wq_pe[:, :, swap]], axis=2).reshape(Q_LORA, N_MLA_HEADS * QK_SLAB).astype(BF16)
    wkv4 = w_kv_up.reshape(KV_LORA, N_MLA_HEADS, QK_NOPE_DIM + V_HEAD_DIM)
    wk = wkv4[:, :, :QK_NOPE_DIM].reshape(KV_LORA, N_MLA_HEADS * QK_NOPE_DIM).astype(BF16)
    wv = wkv4[:, :, QK_NOPE_DIM:].reshape(KV_LORA, D_MLA).astype(BF16)
    freqs = ROPE_THETA ** (-jnp.arange(0, QK_ROPE_DIM, 2, dtype=F32) / QK_ROPE_DIM)
    freq_c = freqs.reshape(half, 1)
    full = lambda arr: pl.BlockSpec(arr.shape, lambda i: (0,) * arr.ndim)
    rows = lambda w: pl.BlockSpec((tm, w), lambda i: (i, 0))
    consts = [attn_norm.reshape(1, d), wm, wr, bgt, q_norm.reshape(1, Q_LORA), wq,
              kv_norm.reshape(1, KV_LORA), wk, wv, freq_c]
    return pl.pallas_call(
        _in_proj_kernel,
        grid=(n // tm,),
        in_specs=[rows(d), pl.BlockSpec((1, 1, tm), lambda i: (i, 0, 0))] + [full(c) for c in consts],
        out_specs=[rows(OFF_G), pl.BlockSpec((N_GATES, tm), lambda i: (0, i)),
                   rows(N_MLA_HEADS * QK_SLAB), rows(N_MLA_HEADS * QK_SLAB), rows(D_MLA)],
        out_shape=[jax.ShapeDtypeStruct((n, OFF_G), F32), jax.ShapeDtypeStruct((N_GATES, n), F32),
                   jax.ShapeDtypeStruct((n, N_MLA_HEADS * QK_SLAB), BF16),
                   jax.ShapeDtypeStruct((n, N_MLA_HEADS * QK_SLAB), BF16),
                   jax.ShapeDtypeStruct((n, D_MLA), BF16)],
        compiler_params=_params("parallel"),
        name="in_proj",
    )(x2, pos2.reshape(n // tm, 1, tm), *consts)


def _mlstm_kernel(q_ref, k_ref, v_ref, o_ref, g_ref, cwq_ref, cwk_ref, cbq_ref, cbk_ref, nrm_ref,
                  y_ref,
                  qc_ref, kc_ref, va_ref, cst_ref, ent_ref, b_ref, e_ref, r_ref, ew_ref, mw_ref, bt_ref, mp_ref):
    L = MLSTM_CHUNK
    dh = MLSTM_HEAD_DIM
    nc = q_ref.shape[1] // L
    s_len = q_ref.shape[1]
    h = pl.program_id(1)

    for d in range(2):
        ig = g_ref[2 * d * N_MLSTM_HEADS + h, 0]
        fg = g_ref[(2 * d + 1) * N_MLSTM_HEADS + h, 0]
        b = _cumsum_lanes(_log_sigmoid(fg), reverse=(d == 1))
        btot = b[:, L - 1:L] if d == 0 else b[:, 0:1]
        r = ig - b
        w = btot + r
        mw = jnp.max(w, axis=-1, keepdims=True)
        b_ref[d] = b
        e_ref[d] = b + _cummax_lanes(r, reverse=(d == 1))
        r_ref[d] = r
        ew_ref[d] = jnp.exp(w - mw)
        mw_ref[d] = jnp.broadcast_to(mw, (nc, L))
        bt_ref[d] = jnp.broadcast_to(btot, (nc, L))

    row = lax.broadcasted_iota(I32, (L, dh), 0)

    def conv_silu(ref, cw_ref, cb_ref, c):
        start = pl.multiple_of(c * L, L)
        x = ref[0, pl.ds(start, L), :]
        prev_row = jnp.where(c > 0, ref[0, pl.ds(jnp.maximum(start - 1, 0), 1), :], 0.0)
        next_row = jnp.where(c < nc - 1, ref[0, pl.ds(jnp.minimum(start + L, s_len - 1), 1), :], 0.0)
        x_prev = jnp.where(row == 0, prev_row, pltpu.roll(x, 1, 0))
        x_next = jnp.where(row == L - 1, next_row, pltpu.roll(x, L - 1, 0))
        y = cw_ref[0:1, :] * x_prev + cw_ref[1:2, :] * x + cw_ref[2:3, :] * x_next + cb_ref[...]
        return y * jax.nn.sigmoid(y)

    ones_blk = jnp.ones((L, dh), BF16)

    def pass1(c, carry):
        start = pl.multiple_of(c * L, L)
        qc_ref[pl.ds(start, L), :] = conv_silu(q_ref, cwq_ref, cbq_ref, c).astype(BF16)
        kk = conv_silu(k_ref, cwk_ref, cbk_ref, c) * (dh ** -0.5)
        kc_ref[pl.ds(start, L), :] = kk.astype(BF16)
        va = jnp.concatenate([v_ref[0, pl.ds(start, L), :].astype(BF16), ones_blk], axis=1)
        va_ref[pl.ds(start, L), :] = va
        kt = kk.T
        for d in range(2):
            kw_t = (kt * ew_ref[d, pl.ds(c, 1), :]).astype(BF16)
            cst_ref[d, c] = _dot(kw_t, va)
        return carry

    lax.fori_loop(0, nc, pass1, 0, unroll=CHUNK_UNROLL)

    ent_ref[:, 0] = jnp.zeros((2, dh, 2 * dh), F32)

    def scan(i, carry):
        out = []
        for d in range(2):
            m = carry[d]
            c = i if d == 0 else nc - 1 - i
            mw = mw_ref[d, pl.ds(c, 1), :]
            bt = bt_ref[d, pl.ds(c, 1), :]
            m_new = jnp.maximum(bt + m, mw)
            a = jnp.exp(bt + m - m_new)[:, 0:1]
            cc = jnp.exp(mw - m_new)[:, 0:1]
            ent_ref[d, i + 1] = a * ent_ref[d, i] + cc * cst_ref[d, c]
            mp_ref[d, pl.ds(c, 1), :] = m
            out.append(m_new)
        return tuple(out)

    lax.fori_loop(0, nc, scan, (jnp.zeros((1, L), F32), jnp.zeros((1, L), F32)))

    ti = lax.broadcasted_iota(I32, (L, L), 0)
    si = lax.broadcasted_iota(I32, (L, L), 1)
    masks = (si <= ti, si >= ti)

    def pass3(c, carry):
        start = pl.multiple_of(c * L, L)
        q = qc_ref[pl.ds(start, L), :]
        k = kc_ref[pl.ds(start, L), :]
        va = va_ref[pl.ds(start, L), :]
        qk = _dot_nt(q, k)
        hsum = jnp.zeros((L, dh), F32)
        for d in range(2):
            bmat = jnp.broadcast_to(b_ref[d, pl.ds(c, 1), :], (L, L)).T
            emat = jnp.broadcast_to(e_ref[d, pl.ds(c, 1), :], (L, L)).T
            dmat = jnp.where(masks[d], bmat + r_ref[d, pl.ds(c, 1), :], -jnp.inf)
            inter = bmat + mp_ref[d, pl.ds(c, 1), :]
            m_t = jnp.maximum(inter, emat)
            sc = qk * jnp.exp(dmat - m_t)
            a = jnp.exp(inter - m_t)
            intra = _dot(sc.astype(BF16), va)
            cross = _dot(q, ent_ref[d, c if d == 0 else nc - 1 - c].astype(BF16))
            num = intra[:, :dh] + a * cross[:, :dh]
            den = intra[:, dh:] + a * cross[:, dh:]
            hsum = hsum + num / jnp.maximum(jnp.abs(den), jnp.exp(-m_t))
        hn = _rms(hsum, nrm_ref[...])
        y_ref[0, pl.ds(start, L), :] = hn * jax.nn.sigmoid(o_ref[0, pl.ds(start, L), :])
        return carry

    lax.fori_loop(0, nc, pass3, 0, unroll=CHUNK_UNROLL)


def _mlstm(um3, gt4, conv_w, conv_b, mlstm_norm):
    bsz, s, _ = um3.shape
    H, dh, L = N_MLSTM_HEADS, MLSTM_HEAD_DIM, MLSTM_CHUNK
    nc = s // L
    col = lambda off: pl.BlockSpec((1, s, dh), lambda b, h: (b, 0, off + h))
    vec = lambda rows, off: pl.BlockSpec((rows, dh), lambda b, h: (0, off + h))
    cb = conv_b.reshape(1, 2 * D_MLSTM)
    return pl.pallas_call(
        _mlstm_kernel,
        grid=(bsz, H),
        in_specs=[col(0), col(H), col(2 * H), col(3 * H),
                  pl.BlockSpec((N_GATES, 1, nc, L), lambda b, h: (0, b, 0, 0)),
                  vec(3, 0), vec(3, H), vec(1, 0), vec(1, H), vec(1, 0)],
        out_specs=pl.BlockSpec((1, s, dh), lambda b, h: (b, 0, h)),
        out_shape=jax.ShapeDtypeStruct((bsz, s, D_MLSTM), F32),
        scratch_shapes=[pltpu.VMEM((s, dh), BF16), pltpu.VMEM((s, dh), BF16), pltpu.VMEM((s, 2 * dh), BF16),
                        pltpu.VMEM((2, nc, dh, 2 * dh), F32), pltpu.VMEM((2, nc + 1, dh, 2 * dh), F32)]
                       + [pltpu.VMEM((2, nc, L), F32) for _ in range(7)],
        compiler_params=_params("parallel", "parallel"),
        name="mlstm",
    )(um3, um3, um3, um3, gt4, conv_w, conv_w, cb, cb, mlstm_norm.reshape(1, D_MLSTM))


def _attn_kernel(q_ref, k_ref, v_ref, o_ref):
    tq = q_ref.shape[1]
    q = q_ref[0]
    m = jnp.full((tq, 1), -jnp.inf, F32)
    l = jnp.zeros((tq, 1), F32)
    acc = jnp.zeros((tq, V_HEAD_DIM), F32)
    chunk = min(KV_CHUNK, k_ref.shape[1])
    for c in range(k_ref.shape[1] // chunk):
        keys = slice(c * chunk, (c + 1) * chunk)
        s = _dot_nt(q, k_ref[0, keys, :])
        m_new = jnp.maximum(m, jnp.max(s, axis=-1, keepdims=True))
        alpha = jnp.exp2(m - m_new)
        p = jnp.exp2(s - m_new)
        l = alpha * l + jnp.sum(p, axis=-1, keepdims=True)
        acc = alpha * acc + _dot(p.astype(BF16), v_ref[0, keys, :])
        m = m_new
    o_ref[0] = acc / l


def _attention(q3, k3, v3):
    bsz, s, _ = q3.shape
    tq = min(Q_TILE, s)
    return pl.pallas_call(
        _attn_kernel,
        grid=(bsz, N_MLA_HEADS, s // tq),
        in_specs=[pl.BlockSpec((1, tq, QK_SLAB), lambda b, h, i: (b, i, h)),
                  pl.BlockSpec((1, s, QK_SLAB), lambda b, h, i: (b, 0, h)),
                  pl.BlockSpec((1, s, V_HEAD_DIM), lambda b, h, i: (b, 0, h))],
        out_specs=pl.BlockSpec((1, tq, V_HEAD_DIM), lambda b, h, i: (b, i, h)),
        out_shape=jax.ShapeDtypeStruct((bsz, s, D_MLA), F32),
        compiler_params=_params("parallel", "parallel", "parallel"),
        name="attention",
    )(q3, k3, v3)


def _route_tile(logits, carry):
    tm = logits.shape[1]
    erow = lax.broadcasted_iota(I32, logits.shape, 0)
    work = logits
    vals, hots = [], []
    for k in range(TOP_K):
        mx = jnp.max(work, axis=0, keepdims=True)
        idx = jnp.min(jnp.where(work == mx, erow, N_EXPERTS), axis=0, keepdims=True)
        hot = erow == idx
        work = jnp.where(hot, -jnp.inf, work)
        vals.append(mx)
        hots.append(hot)
    exps = [jnp.exp(v - vals[0]) for v in vals]
    tot = exps[0] + exps[1] + exps[2] + exps[3]
    multi = (hots[0] | hots[1] | hots[2] | hots[3]).astype(BF16)
    ti = lax.broadcasted_iota(I32, (tm, tm), 0)
    tj = lax.broadcasted_iota(I32, (tm, tm), 1)
    local_rank = _dot(multi, (ti < tj).astype(BF16))
    multi_rows = jnp.concatenate([multi, jnp.zeros((LANES - N_EXPERTS, tm), BF16)], axis=0)
    count = _dot_nt(jnp.ones((SUBLANES, tm), BF16), multi_rows)[0:1]
    padded = jnp.ceil(count * (1.0 / SEG_ALIGN)) * SEG_ALIGN
    filled = _cumsum_lanes(padded, reverse=False)
    local_start = filled - padded
    start_col = jnp.broadcast_to(local_start, (SUBLANES, LANES)).T[:N_EXPERTS, 0:1]
    slot_all = local_rank + start_col
    krow = lax.broadcasted_iota(I32, (SUBLANES, tm), 0)
    packed = jnp.zeros((SUBLANES, tm), F32)
    for k in range(TOP_K):
        sk = jnp.sum(jnp.where(hots[k], slot_all, 0.0), axis=0, keepdims=True)
        packed = jnp.where(krow == k, exps[k] / tot, jnp.where(krow == TOP_K + k, sk, packed))
    srow = lax.broadcasted_iota(I32, (SUBLANES, LANES), 0)
    seg = jnp.where(srow == 0, padded, jnp.where(srow == 1, local_start, jnp.where(srow == 2, carry,
                                                                                  filled[:, LANES - 1:LANES])))
    return packed, seg, padded


def _out_route_kernel(ym_ref, ya_ref, x_ref, mn_ref, wom_ref, woa_ref, fn_ref, wr_ref, br_ref,
                      h_ref, xn_ref, gate_ref, slot_ref, slot_t_ref, seg_ref, size_ref,
                      carry_ref):
    i = pl.program_id(0)
    tm = ROW_TILE

    @pl.when(i == 0)
    def _():
        carry_ref[...] = jnp.zeros_like(carry_ref)

    ya = _rms(ya_ref[...], mn_ref[...])
    h1 = x_ref[...] + _dot(ym_ref[...].astype(BF16), wom_ref[...]) + _dot(ya.astype(BF16), woa_ref[...])
    h_ref[...] = h1
    xn = _rms(h1, fn_ref[...])
    xn_hi = xn.astype(BF16)
    xn_ref[...] = xn_hi
    xn_lo = (xn - xn_hi.astype(F32)).astype(BF16)
    both = _dot_nt(wr_ref[...], xn_hi)
    logits = (both[:N_EXPERTS] + both[N_EXPERTS:] + _dot_nt(wr_ref[:N_EXPERTS, :], xn_lo)) + br_ref[...]
    carry = carry_ref[...]
    for s in range(x_ref.shape[0] // tm):
        packed, seg, padded = _route_tile(logits[:, s * tm:(s + 1) * tm], carry)
        krow = lax.broadcasted_iota(I32, packed.shape, 0)
        slot_t_ref[s] = jnp.where(krow < TOP_K, pltpu.roll(packed, TOP_K, 0), 0.0).astype(I32)
        cols = packed.T
        gate_ref[s * tm:(s + 1) * tm, :] = cols[:, :TOP_K]
        slot_ref[s * tm:(s + 1) * tm, :] = cols[:, TOP_K:2 * TOP_K].astype(I32)
        seg_ref[s] = seg.astype(I32)
        carry = carry + padded
    carry_ref[...] = carry
    size_ref[...] = carry.astype(I32)


def _out_route(ym2, ya2, x2, mla_norm, w_out, ffn_norm, w_router, b_router):
    n, d = x2.shape
    sub = min(ROUTE_TILES_PER_STEP, n // ROW_TILE)
    tm = ROW_TILE * sub
    nt = n // ROW_TILE
    wom = w_out[:D_MLSTM].astype(BF16)
    woa = w_out[D_MLSTM:].astype(BF16)
    wr_hi = w_router.T.astype(BF16)
    wr_lo = (w_router.T - wr_hi.astype(F32)).astype(BF16)
    wr = jnp.concatenate([wr_hi, wr_lo], axis=0)
    br = b_router.reshape(N_EXPERTS, 1)
    full = lambda arr: pl.BlockSpec(arr.shape, lambda i: (0,) * arr.ndim)
    rows = lambda w: pl.BlockSpec((tm, w), lambda i: (i, 0))
    consts = [mla_norm.reshape(1, D_MLA), wom, woa, ffn_norm.reshape(1, d), wr, br]
    return pl.pallas_call(
        _out_route_kernel,
        grid=(nt // sub,),
        in_specs=[rows(D_MLSTM), rows(D_MLA), rows(d)] + [full(c) for c in consts],
        out_specs=[rows(d), rows(d), rows(TOP_K), rows(TOP_K),
                   pl.BlockSpec((sub, SUBLANES, ROW_TILE), lambda i: (i, 0, 0)),
                   pl.BlockSpec((sub, SUBLANES, LANES), lambda i: (i, 0, 0)),
                   pl.BlockSpec((1, LANES), lambda i: (0, 0))],
        out_shape=[jax.ShapeDtypeStruct((n, d), F32), jax.ShapeDtypeStruct((n, d), BF16),
                   jax.ShapeDtypeStruct((n, TOP_K), F32), jax.ShapeDtypeStruct((n, TOP_K), I32),
                   jax.ShapeDtypeStruct((nt, SUBLANES, ROW_TILE), I32), jax.ShapeDtypeStruct((nt, SUBLANES, LANES), I32),
                   jax.ShapeDtypeStruct((1, LANES), I32)],
        scratch_shapes=[pltpu.VMEM((1, LANES), F32)],
        compiler_params=_params("arbitrary"),
        name="out_route",
    )(ym2, ya2, x2, *consts)


def _segment_copies(seg_ref, starts_ref, local_ref, global_ref, sem, to_global, wait):
    def copy(src, dst, rows):
        loc = local_ref.at[pl.ds(pl.multiple_of(src, SEG_ALIGN), rows)]
        glo = global_ref.at[pl.ds(pl.multiple_of(dst, SEG_ALIGN), rows)]
        return pltpu.make_async_copy(loc, glo, sem) if to_global else pltpu.make_async_copy(glo, loc, sem)

    if wait:
        copy(0, 0, pl.multiple_of(seg_ref[0, 3, 0], SEG_ALIGN)).wait()
        return

    def per_expert(e, carry):
        size = seg_ref[0, 0, e]
        src = seg_ref[0, 1, e]
        dst = starts_ref[e] + seg_ref[0, 2, e]
        off = 0
        rows = ROW_TILE
        queue = 0
        while rows >= SEG_ALIGN:
            @pl.when((size & rows) != 0)
            def _(off=off, rows=rows, queue=queue):
                copy(src + off, dst + off, rows).start(priority=queue)

            off = off + (size & rows)
            rows //= 2
            queue = 1 - queue
        return carry

    lax.fori_loop(0, N_EXPERTS, per_expert, 0)


def _dispatch_kernel(starts_ref, seg_ref, segp_ref, slot_t_ref, xn_ref, xs_ref, sort_ref, zero_ref, sem, zsem):
    i = pl.program_id(0)
    tm = xn_ref.shape[0]
    tmx = zero_ref.shape[0]
    cap = sort_ref.shape[1]

    @pl.when(i == 0)
    def _():
        zero_ref[...] = jnp.zeros_like(zero_ref)
        n_tail = (xs_ref.shape[0] - starts_ref[N_EXPERTS]) // tmx

        def clear_tile(row, wait):
            cp = pltpu.make_async_copy(zero_ref, xs_ref.at[pl.ds(pl.multiple_of(row, tmx), tmx)], zsem)
            if wait:
                cp.wait()
            else:
                cp.start()

        for wait in (False, True):
            def clear_group(e, carry, wait=wait):
                hi = starts_ref[e + 1]

                @pl.when(hi > starts_ref[e])
                def _():
                    clear_tile(hi - tmx, wait)

                return carry

            def clear_tail(t, carry, wait=wait):
                clear_tile(starts_ref[N_EXPERTS] + t * tmx, wait)
                return carry

            lax.fori_loop(0, N_EXPERTS, clear_group, 0)
            lax.fori_loop(0, n_tail, clear_tail, 0)

    slot = i % 2
    slots16 = slot_t_ref[0].astype(I16)
    for r0 in range(0, cap, SORT_CHUNK):
        pos = (lax.broadcasted_iota(I32, (SORT_CHUNK, tm), 0) + r0).astype(I16)
        hit = pos == slots16[0:1, :]
        for k in range(1, TOP_K):
            hit = hit | (pos == slots16[k:k + 1, :])
        sort_ref[slot, r0:r0 + SORT_CHUNK, :] = _pack_halves(_dot(hit.astype(BF16), xn_ref[...]))
    _segment_copies(seg_ref, starts_ref, sort_ref.at[slot], xs_ref, sem.at[slot], True, False)

    @pl.when(i >= 1)
    def _():
        _segment_copies(segp_ref, starts_ref, sort_ref.at[1 - slot], xs_ref, sem.at[1 - slot], True, True)

    @pl.when(i == pl.num_programs(0) - 1)
    def _():
        _segment_copies(seg_ref, starts_ref, sort_ref.at[slot], xs_ref, sem.at[slot], True, True)


def _dispatch(starts, seg, slot_t, xn, n_rows):
    n, d = xn.shape
    tm = ROW_TILE
    nt = n // tm
    smem = lambda f: pl.BlockSpec((1, SUBLANES, LANES), f, memory_space=pltpu.SMEM)
    any_spec = pl.BlockSpec(memory_space=pl.ANY)
    return pl.pallas_call(
        _dispatch_kernel,
        grid_spec=pltpu.PrefetchScalarGridSpec(
            num_scalar_prefetch=1,
            grid=(nt,),
            in_specs=[smem(lambda i, *_: (i, 0, 0)), smem(lambda i, *_: (jnp.maximum(i - 1, 0), 0, 0)),
                      pl.BlockSpec((1, SUBLANES, tm), lambda i, *_: (i, 0, 0)),
                      pl.BlockSpec((tm, d), lambda i, *_: (i, 0))],
            out_specs=any_spec,
            scratch_shapes=[pltpu.VMEM((2, SORT_ROWS, d // 2), U32), pltpu.VMEM((EXPERT_TILE, d // 2), U32),
                            pltpu.SemaphoreType.DMA((2,)), pltpu.SemaphoreType.DMA],
        ),
        out_shape=jax.ShapeDtypeStruct((n_rows, d // 2), U32),
        compiler_params=_params("arbitrary"),
        name="dispatch",
    )(starts, seg, seg, slot_t, xn)


def _experts_kernel(te_ref, tb_ref, tv_ref, xs_ref, wgu_ref, bgu_ref, wd_ref, bd_ref, out_ref,
                    wgu_bf, wd_bf):
    i = pl.program_id(0)
    de = wd_ref.shape[1]
    prev = te_ref[jnp.maximum(i - 1, 0)]

    @pl.when(jnp.logical_or(i == 0, te_ref[i] != prev))
    def _():
        wgu_bf[...] = wgu_ref[0].astype(BF16)
        wd_bf[...] = wd_ref[0].astype(BF16)

    @pl.when(tv_ref[i] == 1)
    def _():
        x = _unpack_halves(xs_ref[...])
        acc = jnp.zeros((xs_ref.shape[0], bd_ref.shape[2]), F32)
        for j in range(de // FF_CHUNK):
            cols = slice(j * FF_CHUNK, (j + 1) * FF_CHUNK)
            up_cols = slice(de + j * FF_CHUNK, de + (j + 1) * FF_CHUNK)
            g = _dot(x, wgu_bf[:, cols]) + bgu_ref[0, :, cols]
            u = _dot(x, wgu_bf[:, up_cols]) + bgu_ref[0, :, up_cols]
            g = jnp.minimum(g, SWIGLU_LIMIT)
            u = jnp.clip(u, -SWIGLU_LIMIT, SWIGLU_LIMIT)
            hm = (u + 1.0) * (g * jax.nn.sigmoid(g * SWIGLU_ALPHA))
            acc = acc + _dot(hm.astype(BF16), wd_bf[cols, :])
        out_ref[...] = _pack_halves((acc + bd_ref[0]).astype(BF16).astype(F32))

    @pl.when(tv_ref[i] == 0)
    def _():
        out_ref[...] = jnp.zeros_like(out_ref)


def _experts(tile_e, tile_b, tile_v, xs, w_gate_up, b_gate_up, w_down, b_down):
    n_rows, dw = xs.shape
    tmx = EXPERT_TILE
    ne, d, de2 = w_gate_up.shape
    de = de2 // 2
    return pl.pallas_call(
        _experts_kernel,
        grid_spec=pltpu.PrefetchScalarGridSpec(
            num_scalar_prefetch=3,
            grid=(n_rows // tmx,),
            in_specs=[pl.BlockSpec((tmx, dw), lambda i, te, tb, tv: (tb[i], 0)),
                      pl.BlockSpec((1, d, de2), lambda i, te, tb, tv: (te[i], 0, 0)),
                      pl.BlockSpec((1, 1, de2), lambda i, te, tb, tv: (te[i], 0, 0)),
                      pl.BlockSpec((1, de, d), lambda i, te, tb, tv: (te[i], 0, 0)),
                      pl.BlockSpec((1, 1, d), lambda i, te, tb, tv: (te[i], 0, 0))],
            out_specs=pl.BlockSpec((tmx, dw), lambda i, te, tb, tv: (i, 0)),
            scratch_shapes=[pltpu.VMEM((d, de2), BF16), pltpu.VMEM((de, d), BF16)],
        ),
        out_shape=jax.ShapeDtypeStruct((n_rows, dw), U32),
        compiler_params=_params("arbitrary"),
        name="experts",
    )(tile_e, tile_b, tile_v, xs, w_gate_up, b_gate_up.reshape(ne, 1, de2), w_down, b_down.reshape(ne, 1, d))


def _combine_kernel(starts_ref, seg_ref, segn_ref, h_ref, gate_ref, slot_ref, p_ref, ys_ref, pn_ref, wg_ref,
                    wp_ref, fn_ref, out_ref, ybuf, sem, *, final):
    i = pl.program_id(0)
    nt = pl.num_programs(0)
    tm = h_ref.shape[0]
    cap = ybuf.shape[1]

    @pl.when(i == 0)
    def _():
        ybuf[...] = jnp.zeros_like(ybuf)
        _segment_copies(seg_ref, starts_ref, ybuf.at[0], ys_ref, sem.at[0], False, False)

    slot = i % 2

    @pl.when(i + 1 < nt)
    def _():
        _segment_copies(segn_ref, starts_ref, ybuf.at[1 - slot], ys_ref, sem.at[1 - slot], False, False)

    _segment_copies(seg_ref, starts_ref, ybuf.at[slot], ys_ref, sem.at[slot], False, True)

    gate = gate_ref[...].astype(BF16)
    slot16 = slot_ref[...].astype(I16)
    h2 = h_ref[...]
    for c0 in range(0, cap, SORT_CHUNK):
        pos = (lax.broadcasted_iota(I32, (tm, SORT_CHUNK), 1) + c0).astype(I16)
        weights = jnp.zeros((tm, SORT_CHUNK), BF16)
        for k in range(TOP_K):
            weights = jnp.where(pos == slot16[:, k:k + 1], gate[:, k:k + 1], weights)
        h2 = h2 + _dot(weights, _unpack_halves(ybuf[slot, c0:c0 + SORT_CHUNK, :]))
    hn = _rms(h2, pn_ref[...]).astype(BF16)
    sg = jax.nn.sigmoid(_dot(hn, wg_ref[...]))
    h3 = h2 + sg * _dot(p_ref[...].astype(BF16), wp_ref[...])
    out_ref[...] = _rms(h3, fn_ref[...]) if final else h3


def _combine(starts, seg, h1, gate, slot, p2, ys, ple_norm, w_ple_gate, w_ple_proj, final_norm, final):
    n, d = h1.shape
    tm = ROW_TILE
    nt = n // tm
    wg = w_ple_gate.astype(BF16)
    wp = w_ple_proj.astype(BF16)
    smem = lambda f: pl.BlockSpec((1, SUBLANES, LANES), f, memory_space=pltpu.SMEM)
    full = lambda arr: pl.BlockSpec(arr.shape, lambda i, *_: (0,) * arr.ndim)
    rows = lambda w: pl.BlockSpec((tm, w), lambda i, *_: (i, 0))
    consts = [ple_norm.reshape(1, d), wg, wp, final_norm.reshape(1, d)]
    return pl.pallas_call(
        functools.partial(_combine_kernel, final=final),
        grid_spec=pltpu.PrefetchScalarGridSpec(
            num_scalar_prefetch=1,
            grid=(nt,),
            in_specs=[smem(lambda i, *_: (i, 0, 0)), smem(lambda i, *_: (jnp.minimum(i + 1, nt - 1), 0, 0)),
                      rows(d), rows(TOP_K), rows(TOP_K), rows(p2.shape[1]), pl.BlockSpec(memory_space=pl.ANY)]
                     + [full(c) for c in consts],
            out_specs=rows(d),
            scratch_shapes=[pltpu.VMEM((2, SORT_ROWS, d // 2), U32), pltpu.SemaphoreType.DMA((2,))],
        ),
        out_shape=jax.ShapeDtypeStruct((n, d), F32),
        compiler_params=_params("arbitrary"),
        name="combine",
    )(starts, seg, seg, h1, gate, slot, p2, ys, *consts)


def _route_tables(sizes, n_tiles):
    tmx = EXPERT_TILE
    tile_end = jnp.cumsum((sizes + tmx - 1) // tmx)
    starts = jnp.concatenate([jnp.zeros((1,), I32), tile_end * tmx]).astype(I32)
    n_valid = tile_end[-1]
    t = jnp.arange(n_tiles, dtype=I32)
    tb = jnp.minimum(t, n_valid - 1).astype(I32)
    te = jnp.sum(tile_end[None, :] <= tb[:, None], axis=1).astype(I32)
    tv = (t < n_valid).astype(I32)
    return starts, te, tb, tv


def kernel(x, p, positions, attn_norm, w_in, b_gates, conv_w, conv_b, mlstm_norm, q_norm, w_q_up, kv_norm, w_kv_up, mla_norm, w_out, ffn_norm, w_router, b_router, w_gate_up, b_gate_up, w_down, b_down, ple_norm, w_ple_gate, w_ple_proj, final_norm):
    bsz, s, d = x.shape
    n = bsz * s
    depth = p.shape[0]
    nc = s // MLSTM_CHUNK
    max_rows = n * TOP_K + (n // ROW_TILE) * N_EXPERTS * (SEG_ALIGN - 1) + N_EXPERTS * (EXPERT_TILE - 1)
    n_tiles = max_rows // EXPERT_TILE
    pos2 = positions.reshape(n, 1)
    h = x.reshape(n, d)
    for i in range(depth):
        um, gt, q, k, v = _in_proj(h, pos2, attn_norm[i], w_in[i], b_gates[i], q_norm[i], w_q_up[i],
                                   kv_norm[i], w_kv_up[i])
        ym = _mlstm(um.reshape(bsz, s, -1), gt.reshape(N_GATES, bsz, nc, MLSTM_CHUNK), conv_w[i], conv_b[i],
                    mlstm_norm[i])
        ya = _attention(q.reshape(bsz, s, -1), k.reshape(bsz, s, -1), v.reshape(bsz, s, -1))
        h1, xn, gate, slot, slot_t, seg, sizes = _out_route(ym.reshape(n, -1), ya.reshape(n, -1), h, mla_norm[i],
                                                            w_out[i], ffn_norm[i], w_router[i], b_router[i])
        starts, te, tb, tv = _route_tables(sizes[0, :N_EXPERTS], n_tiles)
        xs = _dispatch(starts, seg, slot_t, xn, n_tiles * EXPERT_TILE)
        ys = _experts(te, tb, tv, xs, w_gate_up[i], b_gate_up[i], w_down[i], b_down[i])
        h = _combine(starts, seg, h1, gate, slot, p[i].reshape(n, -1), ys, ple_norm[i], w_ple_gate[i],
                     w_ple_proj[i], final_norm, final=(i == depth - 1))
    return h.reshape(bsz, s, d)
```
